```python
import jax, jax.numpy as jnp
from jax import lax
import numpy as np

D_MODEL = 2048
BATCH = 1
SEQ = 8192
DEPTH = 1
DEC_BATCH = 128
DEC_SEQ = 1
PAST_LEN = 2048
PAGE_SIZE = 128

M_HEADS = 4
M_QK = 128
M_V = 256
M_CHUNK = 64
M_WIDTH = M_HEADS * M_V
A_HEADS = 8
A_HEAD_DIM = 128
A_WIDTH = A_HEADS * A_HEAD_DIM
MOBA_BLOCK = 256
MOBA_TOPK = 3
Q_CHUNK = 32
ROT_DIM = A_HEAD_DIM // 4
ROPE_THETA = 500000.0
N_EXPERTS = 32
TOP_K = 4
D_FF = D_MODEL
SWIGLU_LIMIT = 7.0
SWIGLU_ALPHA = 1.702
PLE_DIM = 256
EPS = 1e-6
NEG = -1e30
SPLIT_SIZES = (M_HEADS * M_QK, M_HEADS * M_QK, M_WIDTH, M_HEADS, M_HEADS, M_WIDTH,
               A_WIDTH, A_WIDTH, A_WIDTH, D_MODEL, D_MODEL)
D_IN = 2 * M_HEADS * M_QK + 2 * M_WIDTH + 2 * M_HEADS + 3 * A_WIDTH + 2 * D_MODEL

kernel_name = 'mlstm_moba_gated_moe_decoder_step'

F32 = jnp.float32


def rmsnorm(x, g):
    xf = x.astype(F32)
    y = xf * lax.rsqrt(jnp.mean(xf * xf, axis=-1, keepdims=True) + EPS) * g.astype(F32)
    return y.astype(x.dtype)


def rope(x, pos):
    half = ROT_DIM // 2
    inv = jnp.power(ROPE_THETA, -jnp.arange(half, dtype=F32) * 2.0 / ROT_DIM)
    ang = pos.astype(F32)[:, None] * inv[None, :]
    cos = jnp.cos(ang)[:, None, :]
    sin = jnp.sin(ang)[:, None, :]
    xr = x[..., :ROT_DIM].astype(F32)
    x1, x2 = xr[..., :half], xr[..., half:]
    rot = jnp.concatenate([x1 * cos - x2 * sin, x2 * cos + x1 * sin], axis=-1)
    return jnp.concatenate([rot.astype(x.dtype), x[..., ROT_DIM:]], axis=-1)


def mixer_inputs(x, pos, g_attn, w_in, b_i, b_f, g_q, g_k):
    B, T, _ = x.shape
    h = rmsnorm(x, g_attn)
    z = h @ w_in
    qm, km, vm, ig, fg, om, qa, ka, va, gm, ga = jnp.split(z, np.cumsum(SPLIT_SIZES)[:-1].tolist(), axis=-1)
    qm = qm.reshape(B, T, M_HEADS, M_QK) * (M_QK ** -0.5)
    km = km.reshape(B, T, M_HEADS, M_QK)
    vm = vm.reshape(B, T, M_HEADS, M_V)
    ig = ig.astype(F32) + b_i.astype(F32)
    lf = jax.nn.log_sigmoid(fg.astype(F32) + b_f.astype(F32))
    qa = rope(rmsnorm(qa.reshape(B, T, A_HEADS, A_HEAD_DIM), g_q), pos)
    ka = rope(rmsnorm(ka.reshape(B, T, A_HEADS, A_HEAD_DIM), g_k), pos)
    va = va.reshape(B, T, A_HEADS, A_HEAD_DIM)
    return qm, km, vm, ig, lf, om, qa, ka, va, gm, ga


def mlstm_chunk(state, inp):
    C, n, m = state
    q, k, v, ig, lf = inp
    q = q.astype(F32)
    k = k.astype(F32)
    v = v.astype(F32)
    L = q.shape[1]
    bcum = jnp.cumsum(lf, axis=1)
    causal = jnp.tril(jnp.ones((L, L), dtype=bool))[None, :, :, None]
    dmat = jnp.where(causal, bcum[:, :, None, :] - bcum[:, None, :, :] + ig[:, None, :, :], NEG)
    a = bcum + m[:, None, :]
    m_t = jnp.maximum(a, jnp.max(dmat, axis=2))
    w_inter = jnp.exp(a - m_t)
    s = jnp.einsum('bthd,bjhd->btjh', q, k) * jnp.exp(dmat - m_t[:, :, None, :])
    num = w_inter[..., None] * jnp.einsum('bhvd,bthd->bthv', C, q) + jnp.einsum('btjh,bjhv->bthv', s, v)
    den = w_inter * jnp.einsum('bhd,bthd->bth', n, q) + jnp.sum(s, axis=2)
    h = num / jnp.maximum(jnp.abs(den), jnp.exp(-m_t))[..., None]
    b_last = bcum[:, -1]
    d_last = b_last[:, None, :] - bcum + ig
    a_last = b_last + m
    m_new = jnp.maximum(a_last, jnp.max(d_last, axis=1))
    w_c = jnp.exp(a_last - m_new)
    w_j = jnp.exp(d_last - m_new[:, None, :])
    C_new = w_c[:, :, None, None] * C + jnp.einsum('bjh,bjhv,bjhd->bhvd', w_j, v, k)
    n_new = w_c[..., None] * n + jnp.einsum('bjh,bjhd->bhd', w_j, k)
    return (C_new, n_new, m_new), h


def mlstm_prompt(q, k, v, ig, lf):
    B, T = q.shape[:2]
    nc = T // M_CHUNK

    def chunks(a):
        return jnp.moveaxis(a.reshape((B, nc, M_CHUNK) + a.shape[2:]), 1, 0)

    init = (jnp.zeros((B, M_HEADS, M_V, M_QK), F32), jnp.zeros((B, M_HEADS, M_QK), F32),
            jnp.zeros((B, M_HEADS), F32))
    state, h = lax.scan(mlstm_chunk, init, (chunks(q), chunks(k), chunks(v), chunks(ig), chunks(lf)))
    return state, jnp.moveaxis(h, 0, 1).reshape(B, T, M_HEADS, M_V)


def block_views(k, v):
    nb = k.shape[0] // MOBA_BLOCK
    kblk = k.reshape(nb, MOBA_BLOCK, A_HEADS, A_HEAD_DIM).transpose(2, 0, 1, 3)
    vblk = v.reshape(nb, MOBA_BLOCK, A_HEADS, A_HEAD_DIM).transpose(2, 0, 1, 3)
    return kblk, vblk, jnp.mean(kblk.astype(F32), axis=2)


def moba_queries(q, pos, kblk, vblk, kmean):
    Q, H, Dh = q.shape
    nb = kblk.shape[1]
    own = pos // MOBA_BLOCK
    sc = jnp.einsum('qhd,hnd->qhn', q.astype(F32), kmean)
    n_sc = max(nb, MOBA_TOPK)
    if n_sc > nb:
        sc = jnp.pad(sc, ((0, 0), (0, 0), (0, n_sc - nb)))
    blk_id = jnp.arange(n_sc)
    sc = jnp.where(blk_id[None, None, :] < own[:, None, None], sc, NEG)
    _, top = lax.top_k(sc, MOBA_TOPK)
    valid = top < own[:, None, None]
    top = jnp.where(valid, top, 0)
    blocks = jnp.concatenate([top, jnp.broadcast_to(own[:, None, None], (Q, H, 1))], axis=-1)
    ok = jnp.concatenate([valid, jnp.ones((Q, H, 1), dtype=bool)], axis=-1)
    hidx = jnp.arange(H)[None, :, None]
    kg = kblk[hidx, blocks]
    vg = vblk[hidx, blocks]
    key_pos = blocks[..., None] * MOBA_BLOCK + jnp.arange(MOBA_BLOCK)
    mask = ok[..., None] & (key_pos <= pos[:, None, None, None])
    s = jnp.einsum('qhd,qhnkd->qhnk', q, kg, preferred_element_type=F32) * (Dh ** -0.5)
    s = jnp.where(mask, s, NEG)
    p = jax.nn.softmax(s.reshape(Q, H, -1), axis=-1).reshape(s.shape)
    o = jnp.einsum('qhnk,qhnkd->qhd', p.astype(vg.dtype), vg, preferred_element_type=F32)
    return o.astype(q.dtype)


def moba_prompt(q, k, v, pos):
    B, T, H, Dh = q.shape
    t_pad = -(-T // MOBA_BLOCK) * MOBA_BLOCK
    pad = ((0, t_pad - T), (0, 0), (0, 0))
    nq = T // Q_CHUNK

    def per_seq(args):
        qb, kb, vb = args
        kblk, vblk, kmean = block_views(jnp.pad(kb, pad), jnp.pad(vb, pad))
        o = lax.map(lambda a: moba_queries(a[0], a[1], kblk, vblk, kmean),
                    (qb.reshape(nq, Q_CHUNK, H, Dh), pos.reshape(nq, Q_CHUNK)))
        return o.reshape(T, H, Dh)

    return lax.map(per_seq, (q, k, v))


def moba_sample(q, k, v, pool_k, pool_v, page_table, pos):
    B, S, H, Dh = q.shape
    past = page_table.shape[1] * pool_k.shape[1]
    T = past + S
    t_pad = -(-T // MOBA_BLOCK) * MOBA_BLOCK
    pad = ((0, t_pad - T), (0, 0), (0, 0))

    def per_seq(args):
        qb, kb, vb, pt = args
        k_all = jnp.concatenate([pool_k[pt].reshape(past, H, Dh), kb.astype(pool_k.dtype)], axis=0)
        v_all = jnp.concatenate([pool_v[pt].reshape(past, H, Dh), vb.astype(pool_v.dtype)], axis=0)
        kblk, vblk, kmean = block_views(jnp.pad(k_all, pad), jnp.pad(v_all, pad))
        return moba_queries(qb.astype(pool_k.dtype), pos, kblk, vblk, kmean)

    return lax.map(per_seq, (q, k, v, page_table))


def moe(h, w_router, b_router, w_up, b_up, w_down, b_down):
    S, D = h.shape
    logits = (h @ w_router).astype(F32) + b_router.astype(F32)
    top_v, top_e = lax.top_k(logits, TOP_K)
    gates = jax.nn.softmax(top_v, axis=-1)
    n_slots = S * TOP_K
    blk = max(8, min(128, n_slots // N_EXPERTS))
    n_blocks = -(-(n_slots + N_EXPERTS * (blk - 1)) // blk)
    e_flat = top_e.reshape(-1)
    tok_flat = jnp.repeat(jnp.arange(S, dtype=jnp.int32), TOP_K)
    w_flat = gates.reshape(-1)
    order = jnp.argsort(e_flat)
    e_sorted = e_flat[order]
    counts = jnp.zeros((N_EXPERTS,), jnp.int32).at[e_flat].add(1)
    starts = jnp.cumsum(counts) - counts
    padded = (counts + blk - 1) // blk * blk
    p_ends = jnp.cumsum(padded)
    p_starts = p_ends - padded
    dest = p_starts[e_sorted] + jnp.arange(n_slots, dtype=jnp.int32) - starts[e_sorted]
    tok_pad = jnp.zeros((n_blocks * blk,), jnp.int32).at[dest].set(tok_flat[order])
    w_pad = jnp.zeros((n_blocks * blk,), F32).at[dest].set(w_flat[order])
    block_e = jnp.minimum(jnp.searchsorted(p_ends, jnp.arange(n_blocks, dtype=jnp.int32) * blk, side='right'),
                          N_EXPERTS - 1)

    def expert_block(args):
        tk, e = args
        u = h[tk] @ w_up[e] + b_up[e]
        gate = jnp.minimum(u[:, :D_FF], SWIGLU_LIMIT)
        lin = jnp.clip(u[:, D_FF:], -SWIGLU_LIMIT, SWIGLU_LIMIT)
        act = gate * jax.nn.sigmoid(SWIGLU_ALPHA * gate) * (lin + 1.0)
        return act @ w_down[e] + b_down[e]

    y = lax.map(expert_block, (tok_pad.reshape(n_blocks, blk), block_e)).reshape(n_blocks * blk, D)
    return jnp.zeros((S, D), F32).at[tok_pad].add(y.astype(F32) * w_pad[:, None])


def block_output(x, p, hm, om, ao, gm, ga, g_mh, w_bm, w_ba, w_out, g_ffn, w_router, b_router,
                 w_up, b_up, w_down, b_down, g_ple, w_pg, w_ple):
    B, T, D = x.shape
    hm = rmsnorm(hm, g_mh).reshape(B, T, M_WIDTH).astype(x.dtype) * jax.nn.sigmoid(om)
    mix = jax.nn.sigmoid(gm) * (hm @ w_bm) + jax.nn.sigmoid(ga) * (ao.reshape(B, T, A_WIDTH) @ w_ba)
    x = x + (mix @ w_out).astype(x.dtype)
    ff = moe(rmsnorm(x, g_ffn).reshape(B * T, D), w_router, b_router, w_up, b_up, w_down, b_down)
    x = x + ff.reshape(B, T, D).astype(x.dtype)
    gate = jax.nn.sigmoid(rmsnorm(x, g_ple) @ w_pg)
    return (x + gate * (p.astype(x.dtype) @ w_ple)).astype(x.dtype)


def setup_inputs(seed: int = 0) -> dict:
    key = jax.random.key(seed)
    ks = jax.random.split(key, 40)
    n_pages = PAST_LEN // PAGE_SIZE
    n_used = DEC_BATCH * n_pages
    n_pool = n_used + max(1, n_used // 4)
    L = DEPTH

    def nrm(k, shape, scale):
        return jax.random.normal(k, shape, F32) * scale

    def gain(k, shape):
        return 1.0 + 0.05 * jax.random.normal(k, shape, F32)

    page_table = jax.random.permutation(ks[7], n_pool)[:n_used].reshape(DEC_BATCH, n_pages).astype(jnp.int32)
    return {
        'x_prompt': nrm(ks[0], (BATCH, SEQ, D_MODEL), 1.0),
        'x_sample': nrm(ks[1], (DEC_BATCH, DEC_SEQ, D_MODEL), 1.0),
        'cache_k': nrm(ks[2], (L, n_pool, PAGE_SIZE, A_HEADS, A_HEAD_DIM), 1.0),
        'cache_v': nrm(ks[3], (L, n_pool, PAGE_SIZE, A_HEADS, A_HEAD_DIM), 1.0),
        'state_mlstm_C': nrm(ks[4], (L, DEC_BATCH, M_HEADS, M_V, M_QK), 0.1),
        'state_mlstm_n': nrm(ks[5], (L, DEC_BATCH, M_HEADS, M_QK), 0.1),
        'state_mlstm_m': nrm(ks[6], (L, DEC_BATCH, M_HEADS), 1.0),
        'page_table': page_table,
        'p_prompt': nrm(ks[8], (L, BATCH, SEQ, PLE_DIM), 1.0),
        'p_sample': nrm(ks[9], (L, DEC_BATCH, DEC_SEQ, PLE_DIM), 1.0),
        'g_attn': gain(ks[10], (L, D_MODEL)),
        'w_in': nrm(ks[11], (L, D_MODEL, D_IN), D_MODEL ** -0.5),
        'b_i': nrm(ks[12], (L, M_HEADS), 0.1),
        'b_f': 3.0 + nrm(ks[13], (L, M_HEADS), 0.1),
        'g_q': gain(ks[14], (L, A_HEAD_DIM)),
        'g_k': gain(ks[15], (L, A_HEAD_DIM)),
        'g_mh': gain(ks[16], (L, M_HEADS, M_V)),
        'w_bm': nrm(ks[17], (L, M_WIDTH, D_MODEL), M_WIDTH ** -0.5),
        'w_ba': nrm(ks[18], (L, A_WIDTH, D_MODEL), A_WIDTH ** -0.5),
        'w_out': nrm(ks[19], (L, D_MODEL, D_MODEL), D_MODEL ** -0.5),
        'g_ffn': gain(ks[20], (L, D_MODEL)),
        'w_router': nrm(ks[21], (L, D_MODEL, N_EXPERTS), D_MODEL ** -0.5),
        'b_router': nrm(ks[22], (L, N_EXPERTS), 0.01),
        'w_up': nrm(ks[23], (L, N_EXPERTS, D_MODEL, 2 * D_FF), D_MODEL ** -0.5),
        'b_up': nrm(ks[24], (L, N_EXPERTS, 2 * D_FF), 0.01),
        'w_down': nrm(ks[25], (L, N_EXPERTS, D_FF, D_MODEL), D_FF ** -0.5),
        'b_down': nrm(ks[26], (L, N_EXPERTS, D_MODEL), 0.01),
        'g_ple': gain(ks[27], (L, D_MODEL)),
        'w_pg': nrm(ks[28], (L, D_MODEL, D_MODEL), D_MODEL ** -0.5),
        'w_ple': nrm(ks[29], (L, PLE_DIM, D_MODEL), PLE_DIM ** -0.5),
    }


def reference(x_prompt, x_sample, cache_k, cache_v, state_mlstm_C, state_mlstm_n, state_mlstm_m,
              page_table, p_prompt, p_sample, g_attn, w_in, b_i, b_f, g_q, g_k, g_mh, w_bm, w_ba,
              w_out, g_ffn, w_router, b_router, w_up, b_up, w_down, b_down, g_ple, w_pg, w_ple):
    pos_p = jnp.arange(x_prompt.shape[1], dtype=jnp.int32)
    past = page_table.shape[1] * cache_k.shape[2]
    pos_s = past + jnp.arange(x_sample.shape[1], dtype=jnp.int32)
    xp, xs = x_prompt, x_sample
    names = ('kp', 'vp', 'Cp', 'np', 'mp', 'ks', 'vs', 'Cs', 'ns', 'ms')
    new = {nm: [] for nm in names}
    for l in range(DEPTH):
        fw = (g_attn[l], w_in[l], b_i[l], b_f[l], g_q[l], g_k[l])
        bw = (g_mh[l], w_bm[l], w_ba[l], w_out[l], g_ffn[l], w_router[l], b_router[l],
              w_up[l], b_up[l], w_down[l], b_down[l], g_ple[l], w_pg[l], w_ple[l])
        qm, km, vm, ig, lf, om, qa, ka, va, gm, ga = mixer_inputs(xp, pos_p, *fw)
        (c_p, n_p, m_p), hm = mlstm_prompt(qm, km, vm, ig, lf)
        ao = moba_prompt(qa, ka, va, pos_p)
        xp = block_output(xp, p_prompt[l], hm, om, ao, gm, ga, *bw)
        new['kp'].append(ka)
        new['vp'].append(va)
        new['Cp'].append(c_p.astype(state_mlstm_C.dtype))
        new['np'].append(n_p.astype(state_mlstm_n.dtype))
        new['mp'].append(m_p.astype(state_mlstm_m.dtype))
        qm, km, vm, ig, lf, om, qa, ka, va, gm, ga = mixer_inputs(xs, pos_s, *fw)
        init = (state_mlstm_C[l].astype(F32), state_mlstm_n[l].astype(F32), state_mlstm_m[l].astype(F32))
        (c_s, n_s, m_s), hm = mlstm_chunk(init, (qm, km, vm, ig, lf))
        ao = moba_sample(qa, ka, va, cache_k[l], cache_v[l], page_table, pos_s).astype(xs.dtype)
        xs = block_output(xs, p_sample[l], hm, om, ao, gm, ga, *bw)
        new['ks'].append(ka)
        new['vs'].append(va)
        new['Cs'].append(c_s.astype(state_mlstm_C.dtype))
        new['ns'].append(n_s.astype(state_mlstm_n.dtype))
        new['ms'].append(m_s.astype(state_mlstm_m.dtype))
    return (xp, xs, jnp.stack(new['kp']), jnp.stack(new['vp']), jnp.stack(new['Cp']), jnp.stack(new['np']),
            jnp.stack(new['mp']), jnp.stack(new['ks']), jnp.stack(new['vs']), jnp.stack(new['Cs']),
            jnp.stack(new['ns']), jnp.stack(new['ms']))
```

```python
import functools

import jax
import jax.numpy as jnp
import numpy as np
from jax import lax
from jax.experimental import pallas as pl
from jax.experimental.pallas import tpu as pltpu

F32 = jnp.float32
BF16 = jnp.bfloat16

M_HEADS = 4
M_QK = 128
M_V = 256
M_CHUNK = 64
M_WIDTH = M_HEADS * M_V
A_HEADS = 8
A_HEAD_DIM = 128
A_WIDTH = A_HEADS * A_HEAD_DIM
MOBA_BLOCK = 256
MOBA_TOPK = 3
ROT_DIM = A_HEAD_DIM // 4
ROPE_THETA = 500000.0
N_EXPERTS = 32
TOP_K = 4
SWIGLU_LIMIT = 7.0
SWIGLU_ALPHA = 1.702
EPS = 1e-6
NEG = -1e30

LANES = 128
SUBLANES = 8
VMEM_LIMIT_BYTES = 56 * 1024 * 1024

MOE_ROWS = 256
MLSTM_CHUNKS_PER_STEP = 2
MLSTM_SAMPLE_SEQS = 8
COMBINE_ROWS = 128


def _pick(n, candidates):
    for c in candidates:
        if n % c == 0:
            return c
    raise ValueError(f"no tile in {candidates} divides {n}")


def _params(*sem):
    return pltpu.CompilerParams(dimension_semantics=sem, vmem_limit_bytes=VMEM_LIMIT_BYTES)


def _dot(a, b):
    return jnp.dot(a, b, preferred_element_type=F32)


def _dot_nt(a, b):
    return lax.dot_general(a, b, (((1,), (1,)), ((), ())), preferred_element_type=F32)


def _dot_tn(a, b):
    return lax.dot_general(a, b, (((0,), (0,)), ((), ())), preferred_element_type=F32)


def _split_bf16(x):
    hi = x.astype(BF16)
    lo = (x - hi.astype(F32)).astype(BF16)
    return hi, lo


def _rms(x, g):
    return x * lax.rsqrt(jnp.mean(x * x, axis=-1, keepdims=True) + EPS) * g


def _log_sigmoid(x):
    return -(jnp.maximum(-x, 0.0) + jnp.log1p(jnp.exp(-jnp.abs(x))))


def _first_argmax(work, lane_f):
    m = jnp.max(work, axis=-1, keepdims=True)
    idx = jnp.min(jnp.where(work == m, lane_f, float(4 * LANES)), axis=-1, keepdims=True)
    return m, idx


def _rmsnorm_kernel(x_ref, g_ref, o_ref):
    o_ref[...] = _rms(x_ref[...], g_ref[...]).astype(o_ref.dtype)


def _rmsnorm_bf16(x, g):
    r, d = x.shape
    tm = _pick(r, (512, 256, 128, 64, 32, 16))
    return pl.pallas_call(
        _rmsnorm_kernel,
        grid=(r // tm,),
        in_specs=[pl.BlockSpec((tm, d), lambda i: (i, 0)), pl.BlockSpec((1, d), lambda i: (0, 0))],
        out_specs=pl.BlockSpec((tm, d), lambda i: (i, 0)),
        out_shape=jax.ShapeDtypeStruct((r, d), BF16),
        compiler_params=_params("parallel"),
        name="rmsnorm_rows",
    )(x, g.reshape(1, d))


def _proj_scale_kernel(h_ref, w_ref, s_ref, o_ref):
    o_ref[...] = (_dot(h_ref[...], w_ref[...]) * s_ref[...]).astype(o_ref.dtype)


def _proj_sigmoid_kernel(h_ref, w_ref, o_ref):
    o_ref[...] = jax.nn.sigmoid(_dot(h_ref[...], w_ref[...])).astype(o_ref.dtype)


def _proj_plain2_kernel(h_ref, w_ref, o32_ref, o16_ref):
    acc = _dot(h_ref[...], w_ref[...])
    o32_ref[...] = acc
    o16_ref[...] = acc.astype(BF16)


def _proj_qk_kernel(h_ref, w_ref, g_ref, c_ref, s1_ref, s2_ref, o32_ref, o16_ref):
    acc = _dot(h_ref[...], w_ref[...])
    g = g_ref[...]
    c, s1, s2 = c_ref[...], s1_ref[...], s2_ref[...]
    for hh in range(acc.shape[1] // A_HEAD_DIM):
        sl = slice(hh * A_HEAD_DIM, (hh + 1) * A_HEAD_DIM)
        y = _rms(acc[:, sl], g)
        up = pltpu.roll(y, A_HEAD_DIM - ROT_DIM // 2, 1)
        dn = pltpu.roll(y, ROT_DIM // 2, 1)
        r = y * c + up * s1 + dn * s2
        o32_ref[:, sl] = r
        o16_ref[:, sl] = r.astype(BF16)


def _proj_gates_kernel(h_ref, wc_ref, wr_ref, bc_ref, br_ref, gc_ref, gr_ref):
    h = h_ref[...]
    zc = _dot(h, wc_ref[...]) + bc_ref[...]
    zr = _dot_nt(wr_ref[...], h) + br_ref[...]
    lane = lax.broadcasted_iota(jnp.int32, zc.shape, 1)
    gc_ref[...] = jnp.where((lane >= M_HEADS) & (lane < 2 * M_HEADS), _log_sigmoid(zc), zc)
    row = lax.broadcasted_iota(jnp.int32, zr.shape, 0)
    gr_ref[...] = jnp.where(row >= M_HEADS, _log_sigmoid(zr), zr)


def _proj_tiles(r, n):
    tm = _pick(r, (1024, 512, 256, 128, 64, 32, 16))
    tn = _pick(n, (1024, 512, 256, 128))
    return tm, tn


def _proj_scale(h, w, colscale, out_dtype):
    r, d = h.shape
    n = w.shape[1]
    tm, tn = _proj_tiles(r, n)
    return pl.pallas_call(
        _proj_scale_kernel,
        grid=(n // tn, r // tm),
        in_specs=[pl.BlockSpec((tm, d), lambda j, i: (i, 0)),
                  pl.BlockSpec((d, tn), lambda j, i: (0, j)),
                  pl.BlockSpec((1, tn), lambda j, i: (0, j))],
        out_specs=pl.BlockSpec((tm, tn), lambda j, i: (i, j)),
        out_shape=jax.ShapeDtypeStruct((r, n), out_dtype),
        compiler_params=_params("parallel", "parallel"),
        name="proj_scale",
    )(h, w, colscale)


def _proj_sigmoid(h, w):
    r, d = h.shape
    n = w.shape[1]
    tm, tn = _proj_tiles(r, n)
    return pl.pallas_call(
        _proj_sigmoid_kernel,
        grid=(n // tn, r // tm),
        in_specs=[pl.BlockSpec((tm, d), lambda j, i: (i, 0)),
                  pl.BlockSpec((d, tn), lambda j, i: (0, j))],
        out_specs=pl.BlockSpec((tm, tn), lambda j, i: (i, j)),
        out_shape=jax.ShapeDtypeStruct((r, n), BF16),
        compiler_params=_params("parallel", "parallel"),
        name="proj_sigmoid",
    )(h, w)


def _proj_plain2(h, w):
    r, d = h.shape
    n = w.shape[1]
    tm, tn = _proj_tiles(r, n)
    return pl.pallas_call(
        _proj_plain2_kernel,
        grid=(n // tn, r // tm),
        in_specs=[pl.BlockSpec((tm, d), lambda j, i: (i, 0)),
                  pl.BlockSpec((d, tn), lambda j, i: (0, j))],
        out_specs=[pl.BlockSpec((tm, tn), lambda j, i: (i, j)),
                   pl.BlockSpec((tm, tn), lambda j, i: (i, j))],
        out_shape=[jax.ShapeDtypeStruct((r, n), F32), jax.ShapeDtypeStruct((r, n), BF16)],
        compiler_params=_params("parallel", "parallel"),
        name="proj_plain",
    )(h, w)


def _proj_qk(h, w, g, rope_c, rope_s1, rope_s2):
    r, d = h.shape
    n = w.shape[1]
    tm, tn = _proj_tiles(r, n)
    hd = A_HEAD_DIM
    return pl.pallas_call(
        _proj_qk_kernel,
        grid=(n // tn, r // tm),
        in_specs=[pl.BlockSpec((tm, d), lambda j, i: (i, 0)),
                  pl.BlockSpec((d, tn), lambda j, i: (0, j)),
                  pl.BlockSpec((1, hd), lambda j, i: (0, 0)),
                  pl.BlockSpec((tm, hd), lambda j, i: (i, 0)),
                  pl.BlockSpec((tm, hd), lambda j, i: (i, 0)),
                  pl.BlockSpec((tm, hd), lambda j, i: (i, 0))],
        out_specs=[pl.BlockSpec((tm, tn), lambda j, i: (i, j)),
                   pl.BlockSpec((tm, tn), lambda j, i: (i, j))],
        out_shape=[jax.ShapeDtypeStruct((r, n), F32), jax.ShapeDtypeStruct((r, n), BF16)],
        compiler_params=_params("parallel", "parallel"),
        name="proj_qk_norm_rope",
    )(h, w, g.reshape(1, hd), rope_c, rope_s1, rope_s2)


def _proj_gates(h, wc, wr, bc, br):
    r, d = h.shape
    tm = _pick(r, (1024, 512, 256, 128)) if r % LANES == 0 else r
    g2 = 2 * M_HEADS
    return pl.pallas_call(
        _proj_gates_kernel,
        grid=(r // tm,),
        in_specs=[pl.BlockSpec((tm, d), lambda i: (i, 0)),
                  pl.BlockSpec((d, LANES), lambda i: (0, 0)),
                  pl.BlockSpec((g2, d), lambda i: (0, 0)),
                  pl.BlockSpec((1, LANES), lambda i: (0, 0)),
                  pl.BlockSpec((g2, 1), lambda i: (0, 0))],
        out_specs=[pl.BlockSpec((tm, LANES), lambda i: (i, 0)),
                   pl.BlockSpec((g2, tm), lambda i: (0, i))],
        out_shape=[jax.ShapeDtypeStruct((r, LANES), F32), jax.ShapeDtypeStruct((g2, r), F32)],
        compiler_params=_params("parallel"),
        name="proj_gates",
    )(h, wc, wr, bc, br)


def _mlstm_prompt_kernel(q_ref, k_ref, v_ref, gc_ref, gr_ref, h_ref, ct_out, n_out, m_out,
                         ct_s, n_s, m_s):
    step = pl.program_id(0)

    @pl.when(step == 0)
    def _():
        ct_s[...] = jnp.zeros_like(ct_s)
        n_s[...] = jnp.zeros_like(n_s)
        m_s[...] = jnp.zeros_like(m_s)

    ln = M_CHUNK
    row = lax.broadcasted_iota(jnp.int32, (ln, ln), 0)
    col = lax.broadcasted_iota(jnp.int32, (ln, ln), 1)
    causal = col <= row
    upto = row <= col
    for cc in range(MLSTM_CHUNKS_PER_STEP):
        rs = slice(cc * ln, (cc + 1) * ln)
        for hh in range(M_HEADS):
            ig_c = gc_ref[rs, hh:hh + 1]
            lf_c = gc_ref[rs, M_HEADS + hh:M_HEADS + hh + 1]
            ig_r = gr_ref[hh:hh + 1, rs]
            lf_r = gr_ref[M_HEADS + hh:M_HEADS + hh + 1, rs]
            bcum_c = jnp.sum(jnp.where(causal, lf_r, 0.0), axis=1, keepdims=True)
            bcum_r = jnp.sum(jnp.where(upto, lf_c, 0.0), axis=0, keepdims=True)
            dmat = jnp.where(causal, bcum_c - bcum_r + ig_r, NEG)
            m_prev = m_s[hh:hh + 1, 0:1]
            a = bcum_c + m_prev
            m_t = jnp.maximum(a, jnp.max(dmat, axis=1, keepdims=True))
            w_inter = jnp.exp(a - m_t)
            decay = jnp.exp(dmat - m_t)
            qh = q_ref[rs, hh * M_QK:(hh + 1) * M_QK]
            kh = k_ref[rs, hh * M_QK:(hh + 1) * M_QK]
            vh = v_ref[rs, hh * M_V:(hh + 1) * M_V]
            s = _dot_nt(qh, kh) * decay
            ct = ct_s[hh]
            n_row = n_s[hh:hh + 1, :]
            num = w_inter * _dot(qh, ct.astype(BF16)) + _dot(s.astype(BF16), vh)
            den = (w_inter * jnp.sum(qh.astype(F32) * n_row, axis=1, keepdims=True)
                   + jnp.sum(s, axis=1, keepdims=True))
            h_ref[rs, hh * M_V:(hh + 1) * M_V] = num / jnp.maximum(jnp.abs(den), jnp.exp(-m_t))
            b_last = bcum_c[ln - 1:ln, :]
            d_last_r = b_last - bcum_r + ig_r
            d_last_c = b_last - bcum_c + ig_c
            a_last = b_last + m_prev
            m_new = jnp.maximum(a_last, jnp.max(d_last_r, axis=1, keepdims=True))
            w_c = jnp.exp(a_last - m_new)
            w_j = jnp.exp(d_last_c - m_new)
            vw = (vh.astype(F32) * w_j).astype(BF16)
            ct_s[hh] = w_c * ct + _dot_tn(kh, vw)
            n_s[hh:hh + 1, :] = w_c * n_row + jnp.sum(kh.astype(F32) * w_j, axis=0, keepdims=True)
            m_s[hh:hh + 1, :] = jnp.broadcast_to(m_new, (1, LANES))

    @pl.when(step == pl.num_programs(0) - 1)
    def _():
        ct_out[...] = ct_s[...]
        n_out[...] = n_s[...]
        m_out[...] = m_s[...]


def _mlstm_prompt(qkv, gcol, grow):
    t = qkv.shape[0]
    rows = M_CHUNK * MLSTM_CHUNKS_PER_STEP
    kq = M_HEADS * M_QK
    return pl.pallas_call(
        _mlstm_prompt_kernel,
        grid=(t // rows,),
        in_specs=[pl.BlockSpec((rows, kq), lambda c: (c, 0)),
                  pl.BlockSpec((rows, kq), lambda c: (c, 1)),
                  pl.BlockSpec((rows, M_WIDTH), lambda c: (c, (2 * kq) // M_WIDTH)),
                  pl.BlockSpec((rows, LANES), lambda c: (c, 0)),
                  pl.BlockSpec((2 * M_HEADS, rows), lambda c: (0, c))],
        out_specs=[pl.BlockSpec((rows, M_WIDTH), lambda c: (c, 0)),
                   pl.BlockSpec((M_HEADS, M_QK, M_V), lambda c: (0, 0, 0)),
                   pl.BlockSpec((SUBLANES, LANES), lambda c: (0, 0)),
                   pl.BlockSpec((SUBLANES, LANES), lambda c: (0, 0))],
        out_shape=[jax.ShapeDtypeStruct((t, M_WIDTH), F32),
                   jax.ShapeDtypeStruct((M_HEADS, M_QK, M_V), F32),
                   jax.ShapeDtypeStruct((SUBLANES, LANES), F32),
                   jax.ShapeDtypeStruct((SUBLANES, LANES), F32)],
        scratch_shapes=[pltpu.VMEM((M_HEADS, M_QK, M_V), F32),
                        pltpu.VMEM((SUBLANES, LANES), F32),
                        pltpu.VMEM((SUBLANES, LANES), F32)],
        compiler_params=_params("arbitrary"),
        name="mlstm_prompt_scan",
    )(qkv, qkv, qkv, gcol, grow)


def _mlstm_sample_kernel(qkv_ref, gc_ref, c_ref, n_ref, m_ref, h_ref, co_ref, no_ref, mo_ref):
    kq = M_HEADS * M_QK
    row8 = lax.broadcasted_iota(jnp.int32, (SUBLANES, 1), 0)
    mrow = lax.broadcasted_iota(jnp.int32, mo_ref.shape, 0)
    mcol = lax.broadcasted_iota(jnp.int32, mo_ref.shape, 1)
    m_all = m_ref[...]
    for s in range(MLSTM_SAMPLE_SEQS):
        for hh in range(M_HEADS):
            q = qkv_ref[s:s + 1, hh * M_QK:(hh + 1) * M_QK]
            k = qkv_ref[s:s + 1, kq + hh * M_QK:kq + (hh + 1) * M_QK]
            v = qkv_ref[s:s + 1, 2 * kq + hh * M_V:2 * kq + (hh + 1) * M_V]
            ig = gc_ref[s:s + 1, hh:hh + 1]
            lf = gc_ref[s:s + 1, M_HEADS + hh:M_HEADS + hh + 1]
            m_prev = m_ref[s:s + 1, hh:hh + 1]
            c = c_ref[s, hh]
            n_row = n_ref[s, hh:hh + 1, :]
            a = lf + m_prev
            m_t = jnp.maximum(a, ig)
            w_c = jnp.exp(a - m_t)
            w_j = jnp.exp(ig - m_t)
            sc = jnp.sum(q * k, axis=1, keepdims=True) * w_j
            q8 = jnp.broadcast_to(q, (SUBLANES, M_QK)).astype(BF16)
            cq = _dot_nt(q8, c.astype(BF16))[0:1, :]
            num = w_c * cq + sc * v
            den = w_c * jnp.sum(n_row * q, axis=1, keepdims=True) + sc
            h_ref[s:s + 1, hh * M_V:(hh + 1) * M_V] = num / jnp.maximum(jnp.abs(den), jnp.exp(-m_t))
            vw8 = jnp.where(row8 == 0, v * w_j, 0.0).astype(BF16)
            k8 = jnp.where(row8 == 0, k, 0.0).astype(BF16)
            co_ref[s, hh] = w_c * c + _dot_tn(vw8, k8)
            no_ref[s, hh:hh + 1, :] = w_c * n_row + w_j * k
            m_all = jnp.where((mrow == s) & (mcol == hh), m_t, m_all)
    mo_ref[...] = m_all


def _mlstm_sample(qkv, gcol, c0, n0, m0):
    ns = qkv.shape[0]
    sb = MLSTM_SAMPLE_SEQS
    wq = qkv.shape[1]
    return pl.pallas_call(
        _mlstm_sample_kernel,
        grid=(ns // sb,),
        in_specs=[pl.BlockSpec((sb, wq), lambda i: (i, 0)),
                  pl.BlockSpec((sb, LANES), lambda i: (i, 0)),
                  pl.BlockSpec((sb, M_HEADS, M_V, M_QK), lambda i: (i, 0, 0, 0)),
                  pl.BlockSpec((sb, M_HEADS, M_QK), lambda i: (i, 0, 0)),
                  pl.BlockSpec((sb, M_HEADS), lambda i: (i, 0))],
        out_specs=[pl.BlockSpec((sb, M_WIDTH), lambda i: (i, 0)),
                   pl.BlockSpec((sb, M_HEADS, M_V, M_QK), lambda i: (i, 0, 0, 0)),
                   pl.BlockSpec((sb, M_HEADS, M_QK), lambda i: (i, 0, 0)),
                   pl.BlockSpec((sb, M_HEADS), lambda i: (i, 0))],
        out_shape=[jax.ShapeDtypeStruct((ns, M_WIDTH), F32),
                   jax.ShapeDtypeStruct(c0.shape, F32),
                   jax.ShapeDtypeStruct(n0.shape, F32),
                   jax.ShapeDtypeStruct(m0.shape, F32)],
        compiler_params=_params("parallel"),
        name="mlstm_sample_step",
    )(qkv, gcol, c0, n0, m0)


def _kmean_kernel(k_ref, o_ref):
    o_ref[0] = jnp.mean(k_ref[...], axis=0, keepdims=True)


def _block_kmean(k32):
    t, w = k32.shape
    nb = t // MOBA_BLOCK
    out = pl.pallas_call(
        _kmean_kernel,
        grid=(nb,),
        in_specs=[pl.BlockSpec((MOBA_BLOCK, w), lambda b: (b, 0))],
        out_specs=pl.BlockSpec((1, 1, w), lambda b: (b, 0, 0)),
        out_shape=jax.ShapeDtypeStruct((nb, 1, w), F32),
        compiler_params=_params("parallel"),
        name="moba_block_kmean",
    )(k32)
    return out.reshape(nb, w)


def _moba_select(sc, own, lane_i, lane_f):
    work = jnp.where(lane_i < own, sc, NEG)
    sel = jnp.zeros(sc.shape, dtype=jnp.bool_)
    for _ in range(MOBA_TOPK):
        _, idx = _first_argmax(work, lane_f)
        hit = lane_f == idx
        sel = sel | hit
        work = jnp.where(hit, -jnp.inf, work)
    return sel & (lane_i < own)


def _moba_prompt_kernel(q_ref, k_ref, v_ref, km_ref, o_ref):
    i = pl.program_id(1)
    bs = MOBA_BLOCK
    scale = A_HEAD_DIM ** -0.5
    q = q_ref[...]
    qh, ql = _split_bf16(q)
    kmh, kml = _split_bf16(km_ref[...])
    sc = _dot_nt(qh, kmh) + (_dot_nt(qh, kml) + _dot_nt(ql, kmh))
    lane_i = lax.broadcasted_iota(jnp.int32, sc.shape, 1)
    lane_f = lane_i.astype(F32)
    sel = _moba_select(sc, i, lane_i, lane_f).astype(F32)

    row = lax.broadcasted_iota(jnp.int32, (bs, bs), 0)
    col = lax.broadcasted_iota(jnp.int32, (bs, bs), 1)
    start = pl.multiple_of(i * bs, bs)
    kd = k_ref[pl.ds(start, bs), :]
    vd = v_ref[pl.ds(start, bs), :]
    s = jnp.where(col <= row, _dot_nt(qh, kd) * scale, NEG)
    m0 = jnp.max(s, axis=1, keepdims=True)
    p = jnp.exp(s - m0)
    l0 = jnp.sum(p, axis=1, keepdims=True)
    acc0 = _dot(p.astype(BF16), vd)

    def body(j, carry):
        m_i, l_i, acc = carry
        off = pl.multiple_of(j * bs, bs)
        kj = k_ref[pl.ds(off, bs), :]
        vj = v_ref[pl.ds(off, bs), :]
        picked = jnp.sum(jnp.where(lane_i == j, sel, 0.0), axis=1, keepdims=True) > 0.0
        sj = jnp.where(picked, _dot_nt(qh, kj) * scale, NEG)
        m_n = jnp.maximum(m_i, jnp.max(sj, axis=1, keepdims=True))
        alpha = jnp.exp(m_i - m_n)
        pj = jnp.exp(sj - m_n)
        l_n = alpha * l_i + jnp.sum(pj, axis=1, keepdims=True)
        acc_n = alpha * acc + _dot(pj.astype(BF16), vj)
        return m_n, l_n, acc_n

    _, l_f, acc_f = lax.fori_loop(0, i, body, (m0, l0, acc0))
    o_ref[...] = (acc_f / l_f).astype(o_ref.dtype)


def _moba_prompt(q32, k16, v16, kmean):
    t = q32.shape[0]
    nb = t // MOBA_BLOCK
    hd = A_HEAD_DIM
    return pl.pallas_call(
        _moba_prompt_kernel,
        grid=(A_HEADS, nb),
        in_specs=[pl.BlockSpec((MOBA_BLOCK, hd), lambda h, i: (i, h)),
                  pl.BlockSpec((t, hd), lambda h, i: (0, h)),
                  pl.BlockSpec((t, hd), lambda h, i: (0, h)),
                  pl.BlockSpec((nb, hd), lambda h, i: (0, h))],
        out_specs=pl.BlockSpec((MOBA_BLOCK, hd), lambda h, i: (i, h)),
        out_shape=jax.ShapeDtypeStruct((t, A_WIDTH), BF16),
        compiler_params=_params("parallel", "arbitrary"),
        name="moba_prompt_attention",
    )(q32, k16, v16, kmean)


def _moba_sample_kernel(n_pages, pages_per_block, pt_ref, q_ref, kn_ref, vn_ref, *refs):
    del pt_ref
    kp = refs[:n_pages]
    vp = refs[n_pages:2 * n_pages]
    o_ref = refs[2 * n_pages]
    nbp = n_pages // pages_per_block
    scale = A_HEAD_DIM ** -0.5
    q = q_ref[0]
    hrow = lax.broadcasted_iota(jnp.int32, (A_HEADS, A_WIDTH), 0)
    hcol = lax.broadcasted_iota(jnp.int32, (A_HEADS, A_WIDTH), 1) // A_HEAD_DIM
    own_head = hrow == hcol
    q_bd = jnp.where(own_head, q, 0.0)
    qb = q_bd.astype(BF16)
    lane_i = lax.broadcasted_iota(jnp.int32, (A_HEADS, LANES), 1)
    lane_f = lane_i.astype(F32)

    sc = jnp.zeros((A_HEADS, LANES), F32)
    for b in range(nbp):
        ksum = jnp.zeros((1, A_WIDTH), F32)
        for pp in range(pages_per_block):
            ksum = ksum + jnp.sum(kp[b * pages_per_block + pp][0], axis=0, keepdims=True)
        kmean = ksum / float(MOBA_BLOCK)
        sc = jnp.where(lane_i == b, jnp.sum(q_bd * kmean, axis=1, keepdims=True), sc)
    sel = _moba_select(sc, nbp, lane_i, lane_f).astype(F32)

    s_self = jnp.sum(q_bd * kn_ref[0], axis=1, keepdims=True) * scale
    s_pages = []
    m = s_self
    for pg in range(n_pages):
        b = pg // pages_per_block
        sp = _dot_nt(qb, kp[pg][0].astype(BF16)) * scale
        sp = jnp.where(sel[:, b:b + 1] > 0.5, sp, NEG)
        s_pages.append(sp)
        m = jnp.maximum(m, jnp.max(sp, axis=1, keepdims=True))
    p_self = jnp.exp(s_self - m)
    l = p_self
    acc = p_self * vn_ref[0]
    for pg in range(n_pages):
        pp_ = jnp.exp(s_pages[pg] - m)
        l = l + jnp.sum(pp_, axis=1, keepdims=True)
        acc = acc + _dot(pp_.astype(BF16), vp[pg][0].astype(BF16))
    o_ref[0] = jnp.sum(jnp.where(own_head, acc / l, 0.0), axis=0, keepdims=True)


def _moba_sample(q32, kn32, vn32, pool_k, pool_v, page_table):
    ns = q32.shape[0]
    n_pool, page, _, _ = pool_k.shape
    n_pages = page_table.shape[1]
    ppb = MOBA_BLOCK // page
    pk = pool_k.reshape(n_pool, page, A_WIDTH)
    pv = pool_v.reshape(n_pool, page, A_WIDTH)
    row_spec = pl.BlockSpec((1, 1, A_WIDTH), lambda s, pt: (s, 0, 0))

    def page_spec(p):
        return pl.BlockSpec((1, page, A_WIDTH), lambda s, pt: (pt[s, p], 0, 0))

    grid_spec = pltpu.PrefetchScalarGridSpec(
        num_scalar_prefetch=1,
        grid=(ns,),
        in_specs=[row_spec, row_spec, row_spec] + [page_spec(p) for p in range(n_pages)] * 2,
        out_specs=row_spec,
    )
    out = pl.pallas_call(
        functools.partial(_moba_sample_kernel, n_pages, ppb),
        grid_spec=grid_spec,
        out_shape=jax.ShapeDtypeStruct((ns, 1, A_WIDTH), F32),
        compiler_params=_params("parallel"),
        name="moba_sample_attention",
    )(page_table, q32.reshape(ns, 1, A_WIDTH), kn32.reshape(ns, 1, A_WIDTH),
      vn32.reshape(ns, 1, A_WIDTH), *([pk] * n_pages), *([pv] * n_pages))
    return out.reshape(ns, A_WIDTH)


def _mix_kernel(hm_ref, om_ref, ao_ref, gm_ref, ga_ref, gmh_ref, wbm_ref, wba_ref, o_ref, hs_ref):
    @pl.when(pl.program_id(1) == 0)
    def _():
        for hh in range(M_HEADS):
            sl = slice(hh * M_V, (hh + 1) * M_V)
            y = _rms(hm_ref[:, sl], gmh_ref[:, sl])
            hs_ref[:, sl] = (y * om_ref[:, sl].astype(F32)).astype(BF16)

    t1 = _dot(hs_ref[...], wbm_ref[...])
    t2 = _dot(ao_ref[...].astype(BF16), wba_ref[...])
    o_ref[...] = (gm_ref[...].astype(F32) * t1 + ga_ref[...].astype(F32) * t2).astype(o_ref.dtype)


def _mix(hm, sg, ao, g_mh, w_bm, w_ba):
    r = hm.shape[0]
    d = w_bm.shape[1]
    tm = _pick(r, (512, 256, 128, 64, 32, 16))
    tn = _pick(d, (512, 256, 128))
    gm0 = M_WIDTH // tn
    ga0 = (M_WIDTH + d) // tn
    return pl.pallas_call(
        _mix_kernel,
        grid=(r // tm, d // tn),
        in_specs=[pl.BlockSpec((tm, M_WIDTH), lambda i, j: (i, 0)),
                  pl.BlockSpec((tm, M_WIDTH), lambda i, j: (i, 0)),
                  pl.BlockSpec((tm, A_WIDTH), lambda i, j: (i, 0)),
                  pl.BlockSpec((tm, tn), lambda i, j: (i, gm0 + j)),
                  pl.BlockSpec((tm, tn), lambda i, j: (i, ga0 + j)),
                  pl.BlockSpec((1, M_WIDTH), lambda i, j: (0, 0)),
                  pl.BlockSpec((M_WIDTH, tn), lambda i, j: (0, j)),
                  pl.BlockSpec((A_WIDTH, tn), lambda i, j: (0, j))],
        out_specs=pl.BlockSpec((tm, tn), lambda i, j: (i, j)),
        out_shape=jax.ShapeDtypeStruct((r, d), BF16),
        scratch_shapes=[pltpu.VMEM((tm, M_WIDTH), BF16)],
        compiler_params=_params("parallel", "arbitrary"),
        name="mixer_merge",
    )(hm, sg, ao, sg, sg, g_mh.reshape(1, M_WIDTH), w_bm, w_ba)


def _resid_router_kernel(x_ref, mix_ref, wout_ref, gffn_ref, wrh_ref, wrl_ref, br_ref,
                         x1_ref, gate_ref, eid_ref):
    x1 = x_ref[...] + _dot(mix_ref[...], wout_ref[...])
    x1_ref[...] = x1
    hh, hl = _split_bf16(_rms(x1, gffn_ref[...]))
    wrh = wrh_ref[...]
    logits = _dot(hh, wrh) + (_dot(hh, wrl_ref[...]) + _dot(hl, wrh)) + br_ref[...]
    lane_i = lax.broadcasted_iota(jnp.int32, logits.shape, 1)
    lane_f = lane_i.astype(F32)
    work = jnp.where(lane_i < N_EXPERTS, logits, -jnp.inf)
    vals, ids = [], []
    for _ in range(TOP_K):
        m, idx = _first_argmax(work, lane_f)
        vals.append(m)
        ids.append(idx)
        work = jnp.where(lane_f == idx, -jnp.inf, work)
    es = [jnp.exp(v - vals[0]) for v in vals]
    den = es[0]
    for e in es[1:]:
        den = den + e
    gate = jnp.zeros(logits.shape, F32)
    eid = jnp.zeros(logits.shape, F32)
    for k in range(TOP_K):
        gate = jnp.where(lane_i == k, es[k] / den, gate)
        eid = jnp.where(lane_i == k, ids[k], eid)
    gate_ref[...] = gate
    eid_ref[...] = eid.astype(jnp.int32)


def _resid_router(x, mix, w_out, g_ffn, wr_hi, wr_lo, b_r, n_total, row0, prev):
    r, d = x.shape
    tm = _pick(r, (256, 128, 64, 32, 16))
    assert row0 % tm == 0
    b0 = row0 // tm
    out_shape = [jax.ShapeDtypeStruct((n_total, d), F32),
                 jax.ShapeDtypeStruct((n_total, LANES), F32),
                 jax.ShapeDtypeStruct((n_total, LANES), jnp.int32)]
    out_specs = [pl.BlockSpec((tm, d), lambda i: (b0 + i, 0)),
                 pl.BlockSpec((tm, LANES), lambda i: (b0 + i, 0)),
                 pl.BlockSpec((tm, LANES), lambda i: (b0 + i, 0))]
    in_specs = [pl.BlockSpec((tm, d), lambda i: (i, 0)),
                pl.BlockSpec((tm, d), lambda i: (i, 0)),
                pl.BlockSpec((d, d), lambda i: (0, 0)),
                pl.BlockSpec((1, d), lambda i: (0, 0)),
                pl.BlockSpec((d, LANES), lambda i: (0, 0)),
                pl.BlockSpec((d, LANES), lambda i: (0, 0)),
                pl.BlockSpec((1, LANES), lambda i: (0, 0))]
    args = [x, mix, w_out, g_ffn.reshape(1, d), wr_hi, wr_lo, b_r]
    kern = _resid_router_kernel
    aliases = {}
    if prev is not None:
        in_specs = in_specs + [pl.BlockSpec(memory_space=pl.ANY)] * 3
        args = args + list(prev)
        aliases = {7: 0, 8: 1, 9: 2}

        def kern(*refs):
            _resid_router_kernel(*refs[:7], *refs[10:])

    return pl.pallas_call(
        kern,
        grid=(r // tm,),
        in_specs=in_specs,
        out_specs=out_specs,
        out_shape=out_shape,
        input_output_aliases=aliases,
        compiler_params=_params("parallel"),
        name="residual_router_topk",
    )(*args)


def _dispatch_kernel(nv_ref, tok_ref, g_ref, x_hbm, o_ref, buf, sem):
    c = pl.program_id(0)
    rows = buf.shape[0]

    def row_copy(r):
        return pltpu.make_async_copy(x_hbm.at[pl.ds(tok_ref[r], 1)], buf.at[pl.ds(r, 1)], sem)

    @pl.when(c < nv_ref[0])
    def _():
        def start(r, carry):
            row_copy(r).start()
            return carry

        def wait(r, carry):
            row_copy(r).wait()
            return carry

        lax.fori_loop(0, rows, start, 0)
        lax.fori_loop(0, rows, wait, 0)
        o_ref[...] = _rms(buf[...], g_ref[...]).astype(o_ref.dtype)


def _dispatch(x1, g_ffn, src_tok, n_valid, n_chunks):
    d = x1.shape[1]
    rows = MOE_ROWS
    grid_spec = pltpu.PrefetchScalarGridSpec(
        num_scalar_prefetch=1,
        grid=(n_chunks,),
        in_specs=[pl.BlockSpec((rows,), lambda c, nv: (jnp.minimum(c, nv[0] - 1),),
                               memory_space=pltpu.SMEM),
                  pl.BlockSpec((1, d), lambda c, nv: (0, 0)),
                  pl.BlockSpec(memory_space=pl.ANY)],
        out_specs=pl.BlockSpec((rows, d), lambda c, nv: (jnp.minimum(c, nv[0] - 1), 0)),
        scratch_shapes=[pltpu.VMEM((rows, d), F32), pltpu.SemaphoreType.DMA(())],
    )
    return pl.pallas_call(
        _dispatch_kernel,
        grid_spec=grid_spec,
        out_shape=jax.ShapeDtypeStruct((n_chunks * rows, d), BF16),
        compiler_params=_params("arbitrary"),
        name="moe_dispatch_gather",
    )(n_valid, src_tok, g_ffn.reshape(1, d), x1)


def _moe_up_kernel(ce_ref, nv_ref, x_ref, wg_ref, wl_ref, bg_ref, bl_ref, o_ref, wg16, wl16):
    c = pl.program_id(1)
    valid = c < nv_ref[0]
    new_expert = (c == 0) | (ce_ref[c] != ce_ref[jnp.maximum(c - 1, 0)])

    @pl.when(valid & new_expert)
    def _():
        wg16[...] = wg_ref[0].astype(BF16)
        wl16[...] = wl_ref[0].astype(BF16)

    @pl.when(valid)
    def _():
        x = x_ref[...]
        gate = jnp.minimum(_dot(x, wg16[...]) + bg_ref[0], SWIGLU_LIMIT)
        lin = jnp.clip(_dot(x, wl16[...]) + bl_ref[0], -SWIGLU_LIMIT, SWIGLU_LIMIT)
        o_ref[...] = (gate * jax.nn.sigmoid(SWIGLU_ALPHA * gate) * (lin + 1.0)).astype(o_ref.dtype)


def _moe_up(xs, w_up, b_up, chunk_expert, n_valid):
    rows = MOE_ROWS
    n_chunks = xs.shape[0] // rows
    e, d, f2 = w_up.shape
    dff = f2 // 2
    tf = _pick(dff, (512, 256, 128))
    nf = dff // tf

    def last(c, nv):
        return jnp.minimum(c, nv[0] - 1)

    grid_spec = pltpu.PrefetchScalarGridSpec(
        num_scalar_prefetch=2,
        grid=(nf, n_chunks),
        in_specs=[pl.BlockSpec((rows, d), lambda j, c, ce, nv: (last(c, nv), 0)),
                  pl.BlockSpec((1, d, tf), lambda j, c, ce, nv: (ce[c], 0, j)),
                  pl.BlockSpec((1, d, tf), lambda j, c, ce, nv: (ce[c], 0, nf + j)),
                  pl.BlockSpec((1, 1, tf), lambda j, c, ce, nv: (ce[c], 0, j)),
                  pl.BlockSpec((1, 1, tf), lambda j, c, ce, nv: (ce[c], 0, nf + j))],
        out_specs=pl.BlockSpec((rows, tf), lambda j, c, ce, nv: (last(c, nv), j)),
        scratch_shapes=[pltpu.VMEM((d, tf), BF16), pltpu.VMEM((d, tf), BF16)],
    )
    b3 = b_up.reshape(e, 1, f2)
    return pl.pallas_call(
        _moe_up_kernel,
        grid_spec=grid_spec,
        out_shape=jax.ShapeDtypeStruct((n_chunks * rows, dff), BF16),
        compiler_params=_params("arbitrary", "arbitrary"),
        name="moe_up_swiglu",
    )(chunk_expert, n_valid, xs, w_up, w_up, b3, b3)


def _moe_down_kernel(ce_ref, nv_ref, a_ref, w_ref, b_ref, o_ref, w16):
    c = pl.program_id(1)
    valid = c < nv_ref[0]
    new_expert = (c == 0) | (ce_ref[c] != ce_ref[jnp.maximum(c - 1, 0)])

    @pl.when(valid & new_expert)
    def _():
        w16[...] = w_ref[0].astype(BF16)

    @pl.when(valid)
    def _():
        o_ref[...] = _dot(a_ref[...], w16[...]) + b_ref[0]


def _moe_down(act, w_down, b_down, chunk_expert, n_valid):
    rows = MOE_ROWS
    n_chunks = act.shape[0] // rows
    e, dff, d = w_down.shape
    tn = _pick(d, (1024, 512, 256, 128))
    nn = d // tn

    def last(c, nv):
        return jnp.minimum(c, nv[0] - 1)

    grid_spec = pltpu.PrefetchScalarGridSpec(
        num_scalar_prefetch=2,
        grid=(nn, n_chunks),
        in_specs=[pl.BlockSpec((rows, dff), lambda j, c, ce, nv: (last(c, nv), 0)),
                  pl.BlockSpec((1, dff, tn), lambda j, c, ce, nv: (ce[c], 0, j)),
                  pl.BlockSpec((1, 1, tn), lambda j, c, ce, nv: (ce[c], 0, j))],
        out_specs=pl.BlockSpec((rows, tn), lambda j, c, ce, nv: (last(c, nv), j)),
        scratch_shapes=[pltpu.VMEM((dff, tn), BF16)],
    )
    return pl.pallas_call(
        _moe_down_kernel,
        grid_spec=grid_spec,
        out_shape=jax.ShapeDtypeStruct((n_chunks * rows, d), F32),
        compiler_params=_params("arbitrary", "arbitrary"),
        name="moe_down",
    )(chunk_expert, n_valid, act, w_down, b_down.reshape(e, 1, d))


def _combine_kernel(dest_ref, x1_ref, gate_ref, gple_ref, y_hbm, x2_ref, h3_ref, buf, sem):
    rows = x1_ref.shape[0]

    def row_copy(r, k):
        return pltpu.make_async_copy(y_hbm.at[pl.ds(dest_ref[r * TOP_K + k], 1)],
                                     buf.at[k, pl.ds(r, 1)], sem)

    def start(r, carry):
        for k in range(TOP_K):
            row_copy(r, k).start()
        return carry

    def wait(r, carry):
        for k in range(TOP_K):
            row_copy(r, k).wait()
        return carry

    lax.fori_loop(0, rows, start, 0)
    lax.fori_loop(0, rows, wait, 0)
    gate = gate_ref[...]
    acc = x1_ref[...]
    for k in range(TOP_K):
        acc = acc + gate[:, k:k + 1] * buf[k]
    x2_ref[...] = acc
    h3_ref[...] = _rms(acc, gple_ref[...]).astype(h3_ref.dtype)


def _combine(x1, gate, dest, y, g_ple):
    n, d = x1.shape
    rows = _pick(n, (COMBINE_ROWS, 64, 32, 16))
    return pl.pallas_call(
        _combine_kernel,
        grid=(n // rows,),
        in_specs=[pl.BlockSpec((rows * TOP_K,), lambda i: (i,), memory_space=pltpu.SMEM),
                  pl.BlockSpec((rows, d), lambda i: (i, 0)),
                  pl.BlockSpec((rows, LANES), lambda i: (i, 0)),
                  pl.BlockSpec((1, d), lambda i: (0, 0)),
                  pl.BlockSpec(memory_space=pl.ANY)],
        out_specs=[pl.BlockSpec((rows, d), lambda i: (i, 0)),
                   pl.BlockSpec((rows, d), lambda i: (i, 0))],
        out_shape=[jax.ShapeDtypeStruct((n, d), F32), jax.ShapeDtypeStruct((n, d), BF16)],
        scratch_shapes=[pltpu.VMEM((TOP_K, rows, d), F32), pltpu.SemaphoreType.DMA(())],
        compiler_params=_params("arbitrary"),
        name="moe_combine_gather",
    )(dest, x1, gate, g_ple.reshape(1, d), y)


def _ple_kernel(x2_ref, h3_ref, p_ref, wpg_ref, wple_ref, o_ref):
    gate = jax.nn.sigmoid(_dot(h3_ref[...], wpg_ref[...]))
    o_ref[...] = x2_ref[...] + gate * _dot(p_ref[...].astype(BF16), wple_ref[...])


def _ple(x2, h3, p, w_pg, w_ple, row0, r):
    d = x2.shape[1]
    pd = p.shape[1]
    tm = _pick(r, (512, 256, 128, 64, 32, 16))
    tn = _pick(d, (1024, 512, 256, 128))
    assert row0 % tm == 0
    b0 = row0 // tm
    return pl.pallas_call(
        _ple_kernel,
        grid=(d // tn, r // tm),
        in_specs=[pl.BlockSpec((tm, tn), lambda j, i: (b0 + i, j)),
                  pl.BlockSpec((tm, d), lambda j, i: (b0 + i, 0)),
                  pl.BlockSpec((tm, pd), lambda j, i: (i, 0)),
                  pl.BlockSpec((d, tn), lambda j, i: (0, j)),
                  pl.BlockSpec((pd, tn), lambda j, i: (0, j))],
        out_specs=pl.BlockSpec((tm, tn), lambda j, i: (i, j)),
        out_shape=jax.ShapeDtypeStruct((r, d), F32),
        compiler_params=_params("parallel", "parallel"),
        name="ple_gate",
    )(x2, h3, p, w_pg, w_ple)


def _rope_tables(pos):
    half = ROT_DIM // 2
    inv = jnp.power(ROPE_THETA, -jnp.arange(half, dtype=F32) * 2.0 / ROT_DIM)
    ang = pos.astype(F32)[:, None] * inv[None, :]
    cos, sin = jnp.cos(ang), jnp.sin(ang)
    r = pos.shape[0]
    rest = A_HEAD_DIM - ROT_DIM
    c = jnp.concatenate([cos, cos, jnp.ones((r, rest), F32)], axis=1)
    s1 = jnp.concatenate([-sin, jnp.zeros((r, half + rest), F32)], axis=1)
    s2 = jnp.concatenate([jnp.zeros((r, half), F32), sin, jnp.zeros((r, rest), F32)], axis=1)
    return c, s1, s2


def _mixer_inputs(x, pos, g_attn, w, b_i, b_f, g_q, g_k, mlstm_dtype):
    d = x.shape[1]
    kq = M_HEADS * M_QK
    o_ig = 2 * kq + M_WIDTH
    o_om = o_ig + 2 * M_HEADS
    o_qa = o_om + M_WIDTH
    o_gm = o_qa + 3 * A_WIDTH
    h = _rmsnorm_bf16(x, g_attn)
    colscale = jnp.concatenate([jnp.full((1, kq), M_QK ** -0.5, F32), jnp.ones((1, kq + M_WIDTH), F32)], axis=1)
    qkv_m = _proj_scale(h, w[:, :o_ig].astype(BF16), colscale, mlstm_dtype)
    wg = w[:, o_ig:o_om]
    wc = jnp.pad(wg, ((0, 0), (0, LANES - 2 * M_HEADS))).astype(BF16)
    bias = jnp.concatenate([b_i, b_f]).astype(F32)
    bc = jnp.pad(bias, (0, LANES - 2 * M_HEADS)).reshape(1, LANES)
    gcol, grow = _proj_gates(h, wc, wg.T.astype(BF16), bc, bias.reshape(2 * M_HEADS, 1))
    sg = _proj_sigmoid(h, jnp.concatenate([w[:, o_om:o_qa], w[:, o_gm:]], axis=1).astype(BF16))
    tabs = _rope_tables(pos)
    q32, _ = _proj_qk(h, w[:, o_qa:o_qa + A_WIDTH].astype(BF16), g_q, *tabs)
    k32, k16 = _proj_qk(h, w[:, o_qa + A_WIDTH:o_qa + 2 * A_WIDTH].astype(BF16), g_k, *tabs)
    v32, v16 = _proj_plain2(h, w[:, o_qa + 2 * A_WIDTH:o_gm].astype(BF16))
    return qkv_m, gcol, grow, sg, q32, k32, k16, v32, v16


def _moe_plan(eid, n_chunks):
    n = eid.shape[0]
    e_flat = eid[:, :TOP_K].reshape(-1)
    onehot = (e_flat[:, None] == jnp.arange(N_EXPERTS, dtype=jnp.int32)[None, :]).astype(jnp.int32)
    before = jnp.cumsum(onehot, axis=0) - onehot
    rank = jnp.sum(before * onehot, axis=1)
    counts = jnp.sum(onehot, axis=0)
    padded = (counts + MOE_ROWS - 1) // MOE_ROWS * MOE_ROWS
    p_ends = jnp.cumsum(padded)
    p_starts = p_ends - padded
    dest = (p_starts[e_flat] + rank).astype(jnp.int32)
    tok = jnp.repeat(jnp.arange(n, dtype=jnp.int32), TOP_K)
    src_tok = jnp.zeros((n_chunks * MOE_ROWS,), jnp.int32).at[dest].set(tok)
    n_valid = (p_ends[-1] // MOE_ROWS).astype(jnp.int32).reshape(1)
    chunk0 = jnp.minimum(jnp.arange(n_chunks, dtype=jnp.int32), n_valid[0] - 1) * MOE_ROWS
    chunk_expert = jnp.minimum(jnp.searchsorted(p_ends, chunk0, side='right'), N_EXPERTS - 1).astype(jnp.int32)
    return dest, src_tok, n_valid, chunk_expert


def kernel(x_prompt, x_sample, cache_k, cache_v, state_mlstm_C, state_mlstm_n, state_mlstm_m, page_table, p_prompt, p_sample, g_attn, w_in, b_i, b_f, g_q, g_k, g_mh, w_bm, w_ba, w_out, g_ffn, w_router, b_router, w_up, b_up, w_down, b_down, g_ple, w_pg, w_ple):
    depth = w_in.shape[0]
    bsz, t, d = x_prompt.shape
    ns, dec_seq, _ = x_sample.shape
    assert depth == 1 and bsz == 1 and dec_seq == 1
    page = cache_k.shape[2]
    past = page_table.shape[1] * page
    assert t % MOBA_BLOCK == 0 and past % MOBA_BLOCK == 0 and MOBA_BLOCK % page == 0
    assert t % (M_CHUNK * MLSTM_CHUNKS_PER_STEP) == 0 and ns % MLSTM_SAMPLE_SEQS == 0
    n = t + ns

    xp = x_prompt[0]
    xs = x_sample[:, 0]
    w = w_in[0]
    fw = (g_attn[0], w, b_i[0], b_f[0], g_q[0], g_k[0])

    qkv_p, gcol_p, grow_p, sg_p, q32_p, k32_p, k16_p, v32_p, v16_p = _mixer_inputs(
        xp, jnp.arange(t, dtype=jnp.int32), *fw, BF16)
    hm_p, ct_p, n_p, m_p = _mlstm_prompt(qkv_p, gcol_p, grow_p)
    ao_p = _moba_prompt(q32_p, k16_p, v16_p, _block_kmean(k32_p))

    qkv_s, gcol_s, _, sg_s, q32_s, k32_s, _, v32_s, _ = _mixer_inputs(
        xs, jnp.full((ns,), past, jnp.int32), *fw, F32)
    hm_s, c_s, n_s, m_s = _mlstm_sample(qkv_s, gcol_s, state_mlstm_C[0].astype(F32),
                                        state_mlstm_n[0].astype(F32), state_mlstm_m[0].astype(F32))
    ao_s = _moba_sample(q32_s, k32_s, v32_s, cache_k[0], cache_v[0], page_table)

    wbm, wba, wout = w_bm[0].astype(BF16), w_ba[0].astype(BF16), w_out[0].astype(BF16)
    mix_p = _mix(hm_p, sg_p, ao_p, g_mh[0], wbm, wba)
    mix_s = _mix(hm_s, sg_s, ao_s, g_mh[0], wbm, wba)
    wr = jnp.pad(w_router[0], ((0, 0), (0, LANES - N_EXPERTS)))
    wr_hi = wr.astype(BF16)
    wr_lo = (wr - wr_hi.astype(F32)).astype(BF16)
    b_r = jnp.pad(b_router[0].astype(F32), (0, LANES - N_EXPERTS)).reshape(1, LANES)
    routed = _resid_router(xp, mix_p, wout, g_ffn[0], wr_hi, wr_lo, b_r, n, 0, None)
    x1, gate, eid = _resid_router(xs, mix_s, wout, g_ffn[0], wr_hi, wr_lo, b_r, n, t, routed)

    n_slots = n * TOP_K
    n_chunks = (n_slots + N_EXPERTS * (MOE_ROWS - 1)) // MOE_ROWS
    dest, src_tok, n_valid, chunk_expert = _moe_plan(eid, n_chunks)
    xs_sorted = _dispatch(x1, g_ffn[0], src_tok, n_valid, n_chunks)
    act = _moe_up(xs_sorted, w_up[0], b_up[0], chunk_expert, n_valid)
    y = _moe_down(act, w_down[0], b_down[0], chunk_expert, n_valid)
    x2, h3 = _combine(x1, gate, dest, y, g_ple[0])

    wpg, wple = w_pg[0].astype(BF16), w_ple[0].astype(BF16)
    y_p = _ple(x2, h3, p_prompt[0, 0], wpg, wple, 0, t)
    y_s = _ple(x2, h3, p_sample[0, :, 0], wpg, wple, t, ns)

    def heads(a):
        return a.reshape(a.shape[0], A_HEADS, A_HEAD_DIM)

    return (y_p[None], y_s[:, None],
            heads(k32_p)[None, None], heads(v32_p)[None, None],
            jnp.swapaxes(ct_p, 1, 2)[None, None].astype(state_mlstm_C.dtype),
            n_p[:M_HEADS][None, None].astype(state_mlstm_n.dtype),
            m_p[:M_HEADS, 0][None, None].astype(state_mlstm_m.dtype),
            heads(k32_s)[None, :, None], heads(v32_s)[None, :, None],
            c_s[None].astype(state_mlstm_C.dtype), n_s[None].astype(state_mlstm_n.dtype),
            m_s[None].astype(state_mlstm_m.dtype))
```

```python
import functools

import jax
import jax.numpy as jnp
import numpy as np
from jax import lax
from jax.experimental import pallas as pl
from jax.experimental.pallas import tpu as pltpu

F32 = jnp.float32
BF16 = jnp.bfloat16

M_HEADS = 4
M_QK = 128
M_V = 256
M_CHUNK = 64
M_WIDTH = M_HEADS * M_V
A_HEADS = 8
A_HEAD_DIM = 128
A_WIDTH = A_HEADS * A_HEAD_DIM
MOBA_BLOCK = 256
MOBA_TOPK = 3
ROT_DIM = A_HEAD_DIM // 4
ROPE_THETA = 500000.0
N_EXPERTS = 32
TOP_K = 4
SWIGLU_LIMIT = 7.0
SWIGLU_ALPHA = 1.702
EPS = 1e-6
NEG = -1e30

LANES = 128
SUBLANES = 8
VMEM_LIMIT_BYTES = 56 * 1024 * 1024

MOE_ROWS = 256
MLSTM_CHUNKS_PER_STEP = 2
MLSTM_SAMPLE_SEQS = 8
COMBINE_ROWS = 128
MOBA_HEADS_PER_STEP = 4


def _pick(n, candidates):
    for c in candidates:
        if n % c == 0:
            return c
    raise ValueError(f"no tile in {candidates} divides {n}")


def _params(*sem):
    return pltpu.CompilerParams(dimension_semantics=sem, vmem_limit_bytes=VMEM_LIMIT_BYTES)


def _dot(a, b):
    return jnp.dot(a, b, preferred_element_type=F32)


def _dot_nt(a, b):
    return lax.dot_general(a, b, (((1,), (1,)), ((), ())), preferred_element_type=F32)


def _dot_tn(a, b):
    return lax.dot_general(a, b, (((0,), (0,)), ((), ())), preferred_element_type=F32)


def _split_bf16(x):
    hi = x.astype(BF16)
    lo = (x - hi.astype(F32)).astype(BF16)
    return hi, lo


def _rms(x, g):
    return x * lax.rsqrt(jnp.mean(x * x, axis=-1, keepdims=True) + EPS) * g


def _log_sigmoid(x):
    return -(jnp.maximum(-x, 0.0) + jnp.log1p(jnp.exp(-jnp.abs(x))))


def _first_argmax(work, lane_f):
    m = jnp.max(work, axis=-1, keepdims=True)
    idx = jnp.min(jnp.where(work == m, lane_f, float(4 * LANES)), axis=-1, keepdims=True)
    return m, idx


def _rmsnorm_kernel(x_ref, g_ref, o_ref):
    o_ref[...] = _rms(x_ref[...], g_ref[...]).astype(o_ref.dtype)


def _rmsnorm_bf16(x, g):
    r, d = x.shape
    tm = _pick(r, (512, 256, 128, 64, 32, 16))
    return pl.pallas_call(
        _rmsnorm_kernel,
        grid=(r // tm,),
        in_specs=[pl.BlockSpec((tm, d), lambda i: (i, 0)), pl.BlockSpec((1, d), lambda i: (0, 0))],
        out_specs=pl.BlockSpec((tm, d), lambda i: (i, 0)),
        out_shape=jax.ShapeDtypeStruct((r, d), BF16),
        compiler_params=_params("parallel"),
        name="rmsnorm_rows",
    )(x, g.reshape(1, d))


def _proj_scale_kernel(h_ref, w_ref, s_ref, o_ref):
    o_ref[...] = (_dot(h_ref[...], w_ref[...]) * s_ref[...]).astype(o_ref.dtype)


def _proj_sigmoid_kernel(h_ref, w_ref, o_ref):
    o_ref[...] = jax.nn.sigmoid(_dot(h_ref[...], w_ref[...])).astype(o_ref.dtype)


def _proj_plain2_kernel(h_ref, w_ref, o32_ref, o16_ref):
    acc = _dot(h_ref[...], w_ref[...])
    o32_ref[...] = acc
    o16_ref[...] = acc.astype(BF16)


def _proj_qk_kernel(h_ref, w_ref, g_ref, c_ref, s1_ref, s2_ref, o32_ref, o16_ref):
    acc = _dot(h_ref[...], w_ref[...])
    g = g_ref[...]
    c, s1, s2 = c_ref[...], s1_ref[...], s2_ref[...]
    for hh in range(acc.shape[1] // A_HEAD_DIM):
        sl = slice(hh * A_HEAD_DIM, (hh + 1) * A_HEAD_DIM)
        y = _rms(acc[:, sl], g)
        up = pltpu.roll(y, A_HEAD_DIM - ROT_DIM // 2, 1)
        dn = pltpu.roll(y, ROT_DIM // 2, 1)
        r = y * c + up * s1 + dn * s2
        o32_ref[:, sl] = r
        o16_ref[:, sl] = r.astype(BF16)


def _proj_gates_kernel(h_ref, wc_ref, wr_ref, bc_ref, br_ref, gc_ref, gr_ref):
    h = h_ref[...]
    zc = _dot(h, wc_ref[...]) + bc_ref[...]
    zr = _dot_nt(wr_ref[...], h) + br_ref[...]
    lane = lax.broadcasted_iota(jnp.int32, zc.shape, 1)
    gc_ref[...] = jnp.where((lane >= M_HEADS) & (lane < 2 * M_HEADS), _log_sigmoid(zc), zc)
    row = lax.broadcasted_iota(jnp.int32, zr.shape, 0)
    gr_ref[...] = jnp.where(row >= M_HEADS, _log_sigmoid(zr), zr)


def _proj_tiles(r, n):
    tm = _pick(r, (1024, 512, 256, 128, 64, 32, 16))
    tn = _pick(n, (1024, 512, 256, 128))
    return tm, tn


def _proj_scale(h, w, colscale, out_dtype):
    r, d = h.shape
    n = w.shape[1]
    tm, tn = _proj_tiles(r, n)
    return pl.pallas_call(
        _proj_scale_kernel,
        grid=(n // tn, r // tm),
        in_specs=[pl.BlockSpec((tm, d), lambda j, i: (i, 0)),
                  pl.BlockSpec((d, tn), lambda j, i: (0, j)),
                  pl.BlockSpec((1, tn), lambda j, i: (0, j))],
        out_specs=pl.BlockSpec((tm, tn), lambda j, i: (i, j)),
        out_shape=jax.ShapeDtypeStruct((r, n), out_dtype),
        compiler_params=_params("parallel", "parallel"),
        name="proj_scale",
    )(h, w, colscale)


def _proj_sigmoid(h, w):
    r, d = h.shape
    n = w.shape[1]
    tm, tn = _proj_tiles(r, n)
    return pl.pallas_call(
        _proj_sigmoid_kernel,
        grid=(n // tn, r // tm),
        in_specs=[pl.BlockSpec((tm, d), lambda j, i: (i, 0)),
                  pl.BlockSpec((d, tn), lambda j, i: (0, j))],
        out_specs=pl.BlockSpec((tm, tn), lambda j, i: (i, j)),
        out_shape=jax.ShapeDtypeStruct((r, n), BF16),
        compiler_params=_params("parallel", "parallel"),
        name="proj_sigmoid",
    )(h, w)


def _proj_plain2(h, w):
    r, d = h.shape
    n = w.shape[1]
    tm, tn = _proj_tiles(r, n)
    return pl.pallas_call(
        _proj_plain2_kernel,
        grid=(n // tn, r // tm),
        in_specs=[pl.BlockSpec((tm, d), lambda j, i: (i, 0)),
                  pl.BlockSpec((d, tn), lambda j, i: (0, j))],
        out_specs=[pl.BlockSpec((tm, tn), lambda j, i: (i, j)),
                   pl.BlockSpec((tm, tn), lambda j, i: (i, j))],
        out_shape=[jax.ShapeDtypeStruct((r, n), F32), jax.ShapeDtypeStruct((r, n), BF16)],
        compiler_params=_params("parallel", "parallel"),
        name="proj_plain",
    )(h, w)


def _proj_qk(h, w, g, rope_c, rope_s1, rope_s2):
    r, d = h.shape
    n = w.shape[1]
    tm, tn = _proj_tiles(r, n)
    hd = A_HEAD_DIM
    return pl.pallas_call(
        _proj_qk_kernel,
        grid=(n // tn, r // tm),
        in_specs=[pl.BlockSpec((tm, d), lambda j, i: (i, 0)),
                  pl.BlockSpec((d, tn), lambda j, i: (0, j)),
                  pl.BlockSpec((1, hd), lambda j, i: (0, 0)),
                  pl.BlockSpec((tm, hd), lambda j, i: (i, 0)),
                  pl.BlockSpec((tm, hd), lambda j, i: (i, 0)),
                  pl.BlockSpec((tm, hd), lambda j, i: (i, 0))],
        out_specs=[pl.BlockSpec((tm, tn), lambda j, i: (i, j)),
                   pl.BlockSpec((tm, tn), lambda j, i: (i, j))],
        out_shape=[jax.ShapeDtypeStruct((r, n), F32), jax.ShapeDtypeStruct((r, n), BF16)],
        compiler_params=_params("parallel", "parallel"),
        name="proj_qk_norm_rope",
    )(h, w, g.reshape(1, hd), rope_c, rope_s1, rope_s2)


def _proj_gates(h, wc, wr, bc, br):
    r, d = h.shape
    tm = _pick(r, (1024, 512, 256, 128)) if r % LANES == 0 else r
    g2 = 2 * M_HEADS
    return pl.pallas_call(
        _proj_gates_kernel,
        grid=(r // tm,),
        in_specs=[pl.BlockSpec((tm, d), lambda i: (i, 0)),
                  pl.BlockSpec((d, LANES), lambda i: (0, 0)),
                  pl.BlockSpec((g2, d), lambda i: (0, 0)),
                  pl.BlockSpec((1, LANES), lambda i: (0, 0)),
                  pl.BlockSpec((g2, 1), lambda i: (0, 0))],
        out_specs=[pl.BlockSpec((tm, LANES), lambda i: (i, 0)),
                   pl.BlockSpec((g2, tm), lambda i: (0, i))],
        out_shape=[jax.ShapeDtypeStruct((r, LANES), F32), jax.ShapeDtypeStruct((g2, r), F32)],
        compiler_params=_params("parallel"),
        name="proj_gates",
    )(h, wc, wr, bc, br)


def _mlstm_prompt_kernel(q_ref, k_ref, v_ref, gc_ref, gr_ref, h_ref, ct_out, n_out, m_out,
                         ct_s, n_s, m_s):
    step = pl.program_id(0)

    @pl.when(step == 0)
    def _():
        ct_s[...] = jnp.zeros_like(ct_s)
        n_s[...] = jnp.zeros_like(n_s)
        m_s[...] = jnp.zeros_like(m_s)

    ln = M_CHUNK
    row = lax.broadcasted_iota(jnp.int32, (ln, ln), 0)
    col = lax.broadcasted_iota(jnp.int32, (ln, ln), 1)
    causal = col <= row
    upto = row <= col
    for cc in range(MLSTM_CHUNKS_PER_STEP):
        rs = slice(cc * ln, (cc + 1) * ln)
        for hh in range(M_HEADS):
            ig_c = gc_ref[rs, hh:hh + 1]
            lf_c = gc_ref[rs, M_HEADS + hh:M_HEADS + hh + 1]
            ig_r = gr_ref[hh:hh + 1, rs]
            lf_r = gr_ref[M_HEADS + hh:M_HEADS + hh + 1, rs]
            bcum_c = jnp.sum(jnp.where(causal, lf_r, 0.0), axis=1, keepdims=True)
            bcum_r = jnp.sum(jnp.where(upto, lf_c, 0.0), axis=0, keepdims=True)
            dmat = jnp.where(causal, bcum_c - bcum_r + ig_r, NEG)
            m_prev = m_s[hh:hh + 1, 0:1]
            a = bcum_c + m_prev
            m_t = jnp.maximum(a, jnp.max(dmat, axis=1, keepdims=True))
            w_inter = jnp.exp(a - m_t)
            decay = jnp.exp(dmat - m_t)
            qh = q_ref[rs, hh * M_QK:(hh + 1) * M_QK]
            kh = k_ref[rs, hh * M_QK:(hh + 1) * M_QK]
            vh = v_ref[rs, hh * M_V:(hh + 1) * M_V]
            s = _dot_nt(qh, kh) * decay
            ct = ct_s[hh]
            n_row = n_s[hh:hh + 1, :]
            num = w_inter * _dot(qh, ct.astype(BF16)) + _dot(s.astype(BF16), vh)
            den = (w_inter * jnp.sum(qh.astype(F32) * n_row, axis=1, keepdims=True)
                   + jnp.sum(s, axis=1, keepdims=True))
            h_ref[rs, hh * M_V:(hh + 1) * M_V] = num / jnp.maximum(jnp.abs(den), jnp.exp(-m_t))
            b_last = bcum_c[ln - 1:ln, :]
            d_last_r = b_last - bcum_r + ig_r
            d_last_c = b_last - bcum_c + ig_c
            a_last = b_last + m_prev
            m_new = jnp.maximum(a_last, jnp.max(d_last_r, axis=1, keepdims=True))
            w_c = jnp.exp(a_last - m_new)
            w_j = jnp.exp(d_last_c - m_new)
            vw = (vh.astype(F32) * w_j).astype(BF16)
            ct_s[hh] = w_c * ct + _dot_tn(kh, vw)
            n_s[hh:hh + 1, :] = w_c * n_row + jnp.sum(kh.astype(F32) * w_j, axis=0, keepdims=True)
            m_s[hh:hh + 1, :] = jnp.broadcast_to(m_new, (1, LANES))

    @pl.when(step == pl.num_programs(0) - 1)
    def _():
        ct_out[...] = ct_s[...]
        n_out[...] = n_s[...]
        m_out[...] = m_s[...]


def _mlstm_prompt(qkv, gcol, grow):
    t = qkv.shape[0]
    rows = M_CHUNK * MLSTM_CHUNKS_PER_STEP
    kq = M_HEADS * M_QK
    return pl.pallas_call(
        _mlstm_prompt_kernel,
        grid=(t // rows,),
        in_specs=[pl.BlockSpec((rows, kq), lambda c: (c, 0)),
                  pl.BlockSpec((rows, kq), lambda c: (c, 1)),
                  pl.BlockSpec((rows, M_WIDTH), lambda c: (c, (2 * kq) // M_WIDTH)),
                  pl.BlockSpec((rows, LANES), lambda c: (c, 0)),
                  pl.BlockSpec((2 * M_HEADS, rows), lambda c: (0, c))],
        out_specs=[pl.BlockSpec((rows, M_WIDTH), lambda c: (c, 0)),
                   pl.BlockSpec((M_HEADS, M_QK, M_V), lambda c: (0, 0, 0)),
                   pl.BlockSpec((SUBLANES, LANES), lambda c: (0, 0)),
                   pl.BlockSpec((SUBLANES, LANES), lambda c: (0, 0))],
        out_shape=[jax.ShapeDtypeStruct((t, M_WIDTH), F32),
                   jax.ShapeDtypeStruct((M_HEADS, M_QK, M_V), F32),
                   jax.ShapeDtypeStruct((SUBLANES, LANES), F32),
                   jax.ShapeDtypeStruct((SUBLANES, LANES), F32)],
        scratch_shapes=[pltpu.VMEM((M_HEADS, M_QK, M_V), F32),
                        pltpu.VMEM((SUBLANES, LANES), F32),
                        pltpu.VMEM((SUBLANES, LANES), F32)],
        compiler_params=_params("arbitrary"),
        name="mlstm_prompt_scan",
    )(qkv, qkv, qkv, gcol, grow)


def _mlstm_sample_kernel(qkv_ref, gc_ref, c_ref, n_ref, m_ref, h_ref, co_ref, no_ref, mo_ref):
    kq = M_HEADS * M_QK
    row8 = lax.broadcasted_iota(jnp.int32, (SUBLANES, 1), 0)
    mrow = lax.broadcasted_iota(jnp.int32, mo_ref.shape, 0)
    mcol = lax.broadcasted_iota(jnp.int32, mo_ref.shape, 1)
    m_all = m_ref[...]
    for s in range(MLSTM_SAMPLE_SEQS):
        for hh in range(M_HEADS):
            q = qkv_ref[s:s + 1, hh * M_QK:(hh + 1) * M_QK]
            k = qkv_ref[s:s + 1, kq + hh * M_QK:kq + (hh + 1) * M_QK]
            v = qkv_ref[s:s + 1, 2 * kq + hh * M_V:2 * kq + (hh + 1) * M_V]
            ig = gc_ref[s:s + 1, hh:hh + 1]
            lf = gc_ref[s:s + 1, M_HEADS + hh:M_HEADS + hh + 1]
            m_prev = m_ref[s:s + 1, hh:hh + 1]
            c = c_ref[s, hh]
            n_row = n_ref[s, hh:hh + 1, :]
            a = lf + m_prev
            m_t = jnp.maximum(a, ig)
            w_c = jnp.exp(a - m_t)
            w_j = jnp.exp(ig - m_t)
            sc = jnp.sum(q * k, axis=1, keepdims=True) * w_j
            q8 = jnp.broadcast_to(q, (SUBLANES, M_QK)).astype(BF16)
            cq = _dot_nt(q8, c.astype(BF16))[0:1, :]
            num = w_c * cq + sc * v
            den = w_c * jnp.sum(n_row * q, axis=1, keepdims=True) + sc
            h_ref[s:s + 1, hh * M_V:(hh + 1) * M_V] = num / jnp.maximum(jnp.abs(den), jnp.exp(-m_t))
            vw8 = jnp.where(row8 == 0, v * w_j, 0.0).astype(BF16)
            k8 = jnp.where(row8 == 0, k, 0.0).astype(BF16)
            co_ref[s, hh] = w_c * c + _dot_tn(vw8, k8)
            no_ref[s, hh:hh + 1, :] = w_c * n_row + w_j * k
            m_all = jnp.where((mrow == s) & (mcol == hh), m_t, m_all)
    mo_ref[...] = m_all


def _mlstm_sample(qkv, gcol, c0, n0, m0):
    ns = qkv.shape[0]
    sb = MLSTM_SAMPLE_SEQS
    wq = qkv.shape[1]
    return pl.pallas_call(
        _mlstm_sample_kernel,
        grid=(ns // sb,),
        in_specs=[pl.BlockSpec((sb, wq), lambda i: (i, 0)),
                  pl.BlockSpec((sb, LANES), lambda i: (i, 0)),
                  pl.BlockSpec((sb, M_HEADS, M_V, M_QK), lambda i: (i, 0, 0, 0)),
                  pl.BlockSpec((sb, M_HEADS, M_QK), lambda i: (i, 0, 0)),
                  pl.BlockSpec((sb, M_HEADS), lambda i: (i, 0))],
        out_specs=[pl.BlockSpec((sb, M_WIDTH), lambda i: (i, 0)),
                   pl.BlockSpec((sb, M_HEADS, M_V, M_QK), lambda i: (i, 0, 0, 0)),
                   pl.BlockSpec((sb, M_HEADS, M_QK), lambda i: (i, 0, 0)),
                   pl.BlockSpec((sb, M_HEADS), lambda i: (i, 0))],
        out_shape=[jax.ShapeDtypeStruct((ns, M_WIDTH), F32),
                   jax.ShapeDtypeStruct(c0.shape, F32),
                   jax.ShapeDtypeStruct(n0.shape, F32),
                   jax.ShapeDtypeStruct(m0.shape, F32)],
        compiler_params=_params("parallel"),
        name="mlstm_sample_step",
    )(qkv, gcol, c0, n0, m0)


def _kmean_kernel(k_ref, o_ref):
    o_ref[0] = jnp.mean(k_ref[...], axis=0, keepdims=True)


def _block_kmean(k32):
    t, w = k32.shape
    nb = t // MOBA_BLOCK
    out = pl.pallas_call(
        _kmean_kernel,
        grid=(nb,),
        in_specs=[pl.BlockSpec((MOBA_BLOCK, w), lambda b: (b, 0))],
        out_specs=pl.BlockSpec((1, 1, w), lambda b: (b, 0, 0)),
        out_shape=jax.ShapeDtypeStruct((nb, 1, w), F32),
        compiler_params=_params("parallel"),
        name="moba_block_kmean",
    )(k32)
    return out.reshape(nb, w)


def _moba_select(sc, own, lane_i, lane_f):
    work = jnp.where(lane_i < own, sc, NEG)
    sel = jnp.zeros(sc.shape, dtype=jnp.bool_)
    for _ in range(MOBA_TOPK):
        _, idx = _first_argmax(work, lane_f)
        hit = lane_f == idx
        sel = sel | hit
        work = jnp.where(hit, -jnp.inf, work)
    return sel & (lane_i < own)


def _moba_prompt_kernel(q_ref, k_ref, v_ref, e_ref, km_ref, o_ref):
    i = pl.program_id(1)
    bs = MOBA_BLOCK
    pw = 2 * bs
    hd = A_HEAD_DIM
    heads = range(MOBA_HEADS_PER_STEP)
    ones = jnp.ones((pw, hd), BF16)
    lane_i = lax.broadcasted_iota(jnp.int32, (bs, LANES), 1)
    lane_f = lane_i.astype(F32)

    def head_slice(hh):
        return slice(hh * hd, (hh + 1) * hd)

    q_aug = []
    for hh in heads:
        q = q_ref[:, head_slice(hh)]
        qh, ql = _split_bf16(q)
        kmh, kml = _split_bf16(km_ref[:, head_slice(hh)])
        sc = _dot_nt(qh, kmh) + (_dot_nt(qh, kml) + _dot_nt(ql, kmh))
        sel = _moba_select(sc, i, lane_i, lane_f)
        bias = jnp.where(sel | (lane_i >= i), 0.0, NEG)
        q_aug.append(jnp.concatenate([(q * (hd ** -0.5)).astype(BF16), bias.astype(BF16)], axis=1))

    def pair_scores(p, hh, onehot):
        off = pl.multiple_of(p * pw, pw)
        k_aug = jnp.concatenate([k_ref[pl.ds(off, pw), head_slice(hh)], onehot], axis=1)
        v_aug = jnp.concatenate([v_ref[pl.ds(off, pw), head_slice(hh)], ones], axis=1)
        return _dot_nt(q_aug[hh], k_aug), v_aug

    p_own = i // 2
    row = lax.broadcasted_iota(jnp.int32, (bs, pw), 0)
    col = lax.broadcasted_iota(jnp.int32, (bs, pw), 1)
    causal = col + (p_own * pw - i * bs) <= row
    onehot_own = e_ref[pl.ds(pl.multiple_of(p_own * pw, pw), pw), :]
    init = []
    for hh in heads:
        s, v_aug = pair_scores(p_own, hh, onehot_own)
        s = jnp.where(causal, s, NEG)
        m0 = jnp.max(s, axis=1, keepdims=True)
        init += [m0, _dot(jnp.exp(s - m0).astype(BF16), v_aug)]

    def body(p, carry):
        onehot = e_ref[pl.ds(pl.multiple_of(p * pw, pw), pw), :]
        out = []
        for hh in heads:
            m_i, acc = carry[2 * hh], carry[2 * hh + 1]
            sj, vj = pair_scores(p, hh, onehot)
            m_n = jnp.maximum(m_i, jnp.max(sj, axis=1, keepdims=True))
            out += [m_n, jnp.exp(m_i - m_n) * acc + _dot(jnp.exp(sj - m_n).astype(BF16), vj)]
        return tuple(out)

    final = lax.fori_loop(0, p_own, body, tuple(init))
    for hh in heads:
        acc = final[2 * hh + 1]
        o_ref[:, head_slice(hh)] = (acc[:, :hd] / acc[:, hd:]).astype(o_ref.dtype)


def _moba_prompt(q32, k16, v16, kmean):
    t = q32.shape[0]
    nb = t // MOBA_BLOCK
    hd = MOBA_HEADS_PER_STEP * A_HEAD_DIM
    assert nb <= LANES and nb % 2 == 0 and A_HEADS % MOBA_HEADS_PER_STEP == 0
    block_onehot = (jnp.arange(t, dtype=jnp.int32)[:, None] // MOBA_BLOCK
                    == jnp.arange(LANES, dtype=jnp.int32)[None, :]).astype(BF16)
    kmean_pad = jnp.pad(kmean, ((0, LANES - nb), (0, 0)))
    return pl.pallas_call(
        _moba_prompt_kernel,
        grid=(A_HEADS // MOBA_HEADS_PER_STEP, nb),
        in_specs=[pl.BlockSpec((MOBA_BLOCK, hd), lambda h, i: (i, h)),
                  pl.BlockSpec((t, hd), lambda h, i: (0, h)),
                  pl.BlockSpec((t, hd), lambda h, i: (0, h)),
                  pl.BlockSpec((t, LANES), lambda h, i: (0, 0)),
                  pl.BlockSpec((LANES, hd), lambda h, i: (0, h))],
        out_specs=pl.BlockSpec((MOBA_BLOCK, hd), lambda h, i: (i, h)),
        out_shape=jax.ShapeDtypeStruct((t, A_WIDTH), BF16),
        compiler_params=_params("parallel", "arbitrary"),
        name="moba_prompt_attention",
    )(q32, k16, v16, block_onehot, kmean_pad)


def _moba_sample_kernel(n_pages, pages_per_block, pt_ref, q_ref, kn_ref, vn_ref, *refs):
    del pt_ref
    kp = refs[:n_pages]
    vp = refs[n_pages:2 * n_pages]
    o_ref = refs[2 * n_pages]
    nbp = n_pages // pages_per_block
    hd = A_HEAD_DIM
    page = kp[0].shape[2] // A_HEADS
    q = q_ref[0]
    lane_i = lax.broadcasted_iota(jnp.int32, (A_HEADS, LANES), 1)

    sc = jnp.zeros((A_HEADS, LANES), F32)
    for b in range(nbp):
        ksum = jnp.zeros((A_HEADS, hd), F32)
        for pp in range(pages_per_block):
            rows = kp[b * pages_per_block + pp][0, 0]
            ksum = ksum + jnp.sum(rows.reshape(page, A_HEADS, hd), axis=0)
        kmean = ksum / float(MOBA_BLOCK)
        sc = jnp.where(lane_i == b, jnp.sum(q * kmean, axis=1, keepdims=True), sc)
    sel = _moba_select(sc, nbp, lane_i, lane_i.astype(F32)).astype(F32)

    qs = q * (hd ** -0.5)
    s_self = jnp.sum(qs * kn_ref[0], axis=1, keepdims=True)
    outs = []
    for hh in range(A_HEADS):
        q_rep = jnp.broadcast_to(qs[hh:hh + 1, :], (LANES, hd)).astype(BF16)
        m = jnp.broadcast_to(s_self[hh:hh + 1, :], (1, LANES))
        l = jnp.ones((1, LANES), F32)
        acc = vn_ref[0, hh:hh + 1, :]
        for pg in range(n_pages):
            b = pg // pages_per_block
            kh = kp[pg][0, 0, pl.ds(hh, page, stride=A_HEADS), :]
            vh = vp[pg][0, 0, pl.ds(hh, page, stride=A_HEADS), :]
            s = _dot_nt(kh.astype(BF16), q_rep)
            s = jnp.where(sel[hh:hh + 1, b:b + 1] > 0.5, s, NEG)
            m_n = jnp.maximum(m, jnp.max(s, axis=0, keepdims=True))
            alpha = jnp.exp(m - m_n)
            p = jnp.exp(s - m_n)
            l = alpha * l + jnp.sum(p, axis=0, keepdims=True)
            acc = alpha * acc + jnp.sum(p * vh, axis=0, keepdims=True)
            m = m_n
        outs.append(acc / l)
    o_ref[0] = jnp.concatenate(outs, axis=0)


def _moba_sample(q32, kn32, vn32, cache_k, cache_v, page_table):
    ns = q32.shape[0]
    page = cache_k.shape[2]
    n_pages = page_table.shape[1]
    ppb = MOBA_BLOCK // page
    head_spec = pl.BlockSpec((1, A_HEADS, A_HEAD_DIM), lambda s, pt: (s, 0, 0))

    def page_spec(p):
        return pl.BlockSpec((1, 1, page * A_HEADS, A_HEAD_DIM), lambda s, pt: (0, pt[s, p], 0, 0))

    def rows(cache):
        return cache.reshape(cache.shape[0], cache.shape[1], page * A_HEADS, A_HEAD_DIM)

    def heads(a):
        return a.reshape(ns, A_HEADS, A_HEAD_DIM)

    grid_spec = pltpu.PrefetchScalarGridSpec(
        num_scalar_prefetch=1,
        grid=(ns,),
        in_specs=[head_spec, head_spec, head_spec] + [page_spec(p) for p in range(n_pages)] * 2,
        out_specs=head_spec,
    )
    out = pl.pallas_call(
        functools.partial(_moba_sample_kernel, n_pages, ppb),
        grid_spec=grid_spec,
        out_shape=jax.ShapeDtypeStruct((ns, A_HEADS, A_HEAD_DIM), F32),
        compiler_params=_params("parallel"),
        name="moba_sample_attention",
    )(page_table, heads(q32), heads(kn32), heads(vn32), *([rows(cache_k)] * n_pages), *([rows(cache_v)] * n_pages))
    return out.reshape(ns, A_WIDTH)


def _mix_kernel(hm_ref, om_ref, ao_ref, gm_ref, ga_ref, gmh_ref, wbm_ref, wba_ref, o_ref, hs_ref):
    @pl.when(pl.program_id(1) == 0)
    def _():
        for hh in range(M_HEADS):
            sl = slice(hh * M_V, (hh + 1) * M_V)
            y = _rms(hm_ref[:, sl], gmh_ref[:, sl])
            hs_ref[:, sl] = (y * om_ref[:, sl].astype(F32)).astype(BF16)

    t1 = _dot(hs_ref[...], wbm_ref[...])
    t2 = _dot(ao_ref[...].astype(BF16), wba_ref[...])
    o_ref[...] = (gm_ref[...].astype(F32) * t1 + ga_ref[...].astype(F32) * t2).astype(o_ref.dtype)


def _mix(hm, sg, ao, g_mh, w_bm, w_ba):
    r = hm.shape[0]
    d = w_bm.shape[1]
    tm = _pick(r, (512, 256, 128, 64, 32, 16))
    tn = _pick(d, (512, 256, 128))
    gm0 = M_WIDTH // tn
    ga0 = (M_WIDTH + d) // tn
    return pl.pallas_call(
        _mix_kernel,
        grid=(r // tm, d // tn),
        in_specs=[pl.BlockSpec((tm, M_WIDTH), lambda i, j: (i, 0)),
                  pl.BlockSpec((tm, M_WIDTH), lambda i, j: (i, 0)),
                  pl.BlockSpec((tm, A_WIDTH), lambda i, j: (i, 0)),
                  pl.BlockSpec((tm, tn), lambda i, j: (i, gm0 + j)),
                  pl.BlockSpec((tm, tn), lambda i, j: (i, ga0 + j)),
                  pl.BlockSpec((1, M_WIDTH), lambda i, j: (0, 0)),
                  pl.BlockSpec((M_WIDTH, tn), lambda i, j: (0, j)),
                  pl.BlockSpec((A_WIDTH, tn), lambda i, j: (0, j))],
        out_specs=pl.BlockSpec((tm, tn), lambda i, j: (i, j)),
        out_shape=jax.ShapeDtypeStruct((r, d), BF16),
        scratch_shapes=[pltpu.VMEM((tm, M_WIDTH), BF16)],
        compiler_params=_params("parallel", "arbitrary"),
        name="mixer_merge",
    )(hm, sg, ao, sg, sg, g_mh.reshape(1, M_WIDTH), w_bm, w_ba)


def _resid_router_kernel(x_ref, mix_ref, wout_ref, gffn_ref, wrh_ref, wrl_ref, br_ref,
                         x1_ref, gate_ref, eid_ref):
    x1 = x_ref[...] + _dot(mix_ref[...], wout_ref[...])
    x1_ref[...] = x1
    hh, hl = _split_bf16(_rms(x1, gffn_ref[...]))
    wrh = wrh_ref[...]
    logits = _dot(hh, wrh) + (_dot(hh, wrl_ref[...]) + _dot(hl, wrh)) + br_ref[...]
    lane_i = lax.broadcasted_iota(jnp.int32, logits.shape, 1)
    lane_f = lane_i.astype(F32)
    work = jnp.where(lane_i < N_EXPERTS, logits, -jnp.inf)
    vals, ids = [], []
    for _ in range(TOP_K):
        m, idx = _first_argmax(work, lane_f)
        vals.append(m)
        ids.append(idx)
        work = jnp.where(lane_f == idx, -jnp.inf, work)
    es = [jnp.exp(v - vals[0]) for v in vals]
    den = es[0]
    for e in es[1:]:
        den = den + e
    gate = jnp.zeros(logits.shape, F32)
    eid = jnp.zeros(logits.shape, F32)
    for k in range(TOP_K):
        gate = jnp.where(lane_i == k, es[k] / den, gate)
        eid = jnp.where(lane_i == k, ids[k], eid)
    gate_ref[...] = gate
    eid_ref[...] = eid.astype(jnp.int32)


def _resid_router(x, mix, w_out, g_ffn, wr_hi, wr_lo, b_r, n_total, row0, prev):
    r, d = x.shape
    tm = _pick(r, (256, 128, 64, 32, 16))
    assert row0 % tm == 0
    b0 = row0 // tm
    out_shape = [jax.ShapeDtypeStruct((n_total, d), F32),
                 jax.ShapeDtypeStruct((n_total, LANES), F32),
                 jax.ShapeDtypeStruct((n_total, LANES), jnp.int32)]
    out_specs = [pl.BlockSpec((tm, d), lambda i: (b0 + i, 0)),
                 pl.BlockSpec((tm, LANES), lambda i: (b0 + i, 0)),
                 pl.BlockSpec((tm, LANES), lambda i: (b0 + i, 0))]
    in_specs = [pl.BlockSpec((tm, d), lambda i: (i, 0)),
                pl.BlockSpec((tm, d), lambda i: (i, 0)),
                pl.BlockSpec((d, d), lambda i: (0, 0)),
                pl.BlockSpec((1, d), lambda i: (0, 0)),
                pl.BlockSpec((d, LANES), lambda i: (0, 0)),
                pl.BlockSpec((d, LANES), lambda i: (0, 0)),
                pl.BlockSpec((1, LANES), lambda i: (0, 0))]
    args = [x, mix, w_out, g_ffn.reshape(1, d), wr_hi, wr_lo, b_r]
    kern = _resid_router_kernel
    aliases = {}
    if prev is not None:
        in_specs = in_specs + [pl.BlockSpec(memory_space=pl.ANY)] * 3
        args = args + list(prev)
        aliases = {7: 0, 8: 1, 9: 2}

        def kern(*refs):
            _resid_router_kernel(*refs[:7], *refs[10:])

    return pl.pallas_call(
        kern,
        grid=(r // tm,),
        in_specs=in_specs,
        out_specs=out_specs,
        out_shape=out_shape,
        input_output_aliases=aliases,
        compiler_params=_params("parallel"),
        name="residual_router_topk",
    )(*args)


def _dispatch_kernel(nv_ref, tok_ref, g_ref, x_hbm, o_ref, buf, sem):
    c = pl.program_id(0)
    rows = buf.shape[0]

    def row_copy(r):
        return pltpu.make_async_copy(x_hbm.at[pl.ds(tok_ref[r], 1)], buf.at[pl.ds(r, 1)], sem)

    @pl.when(c < nv_ref[0])
    def _():
        def start(r, carry):
            row_copy(r).start()
            return carry

        def wait(r, carry):
            row_copy(r).wait()
            return carry

        lax.fori_loop(0, rows, start, 0)
        lax.fori_loop(0, rows, wait, 0)
        o_ref[...] = _rms(buf[...], g_ref[...]).astype(o_ref.dtype)


def _dispatch(x1, g_ffn, src_tok, n_valid, n_chunks):
    d = x1.shape[1]
    rows = MOE_ROWS
    grid_spec = pltpu.PrefetchScalarGridSpec(
        num_scalar_prefetch=1,
        grid=(n_chunks,),
        in_specs=[pl.BlockSpec((rows,), lambda c, nv: (jnp.minimum(c, nv[0] - 1),),
                               memory_space=pltpu.SMEM),
                  pl.BlockSpec((1, d), lambda c, nv: (0, 0)),
                  pl.BlockSpec(memory_space=pl.ANY)],
        out_specs=pl.BlockSpec((rows, d), lambda c, nv: (jnp.minimum(c, nv[0] - 1), 0)),
        scratch_shapes=[pltpu.VMEM((rows, d), F32), pltpu.SemaphoreType.DMA(())],
    )
    return pl.pallas_call(
        _dispatch_kernel,
        grid_spec=grid_spec,
        out_shape=jax.ShapeDtypeStruct((n_chunks * rows, d), BF16),
        compiler_params=_params("arbitrary"),
        name="moe_dispatch_gather",
    )(n_valid, src_tok, g_ffn.reshape(1, d), x1)


def _moe_up_kernel(ce_ref, nv_ref, x_ref, wg_ref, wl_ref, bg_ref, bl_ref, o_ref, wg16, wl16):
    c = pl.program_id(1)
    valid = c < nv_ref[0]
    new_expert = (c == 0) | (ce_ref[c] != ce_ref[jnp.maximum(c - 1, 0)])

    @pl.when(valid & new_expert)
    def _():
        wg16[...] = wg_ref[0].astype(BF16)
        wl16[...] = wl_ref[0].astype(BF16)

    @pl.when(valid)
    def _():
        x = x_ref[...]
        gate = jnp.minimum(_dot(x, wg16[...]) + bg_ref[0], SWIGLU_LIMIT)
        lin = jnp.clip(_dot(x, wl16[...]) + bl_ref[0], -SWIGLU_LIMIT, SWIGLU_LIMIT)
        o_ref[...] = (gate * jax.nn.sigmoid(SWIGLU_ALPHA * gate) * (lin + 1.0)).astype(o_ref.dtype)


def _moe_up(xs, w_up, b_up, chunk_expert, n_valid):
    rows = MOE_ROWS
    n_chunks = xs.shape[0] // rows
    e, d, f2 = w_up.shape
    dff = f2 // 2
    tf = _pick(dff, (1024, 512, 256, 128))
    nf = dff // tf

    def last(c, nv):
        return jnp.minimum(c, nv[0] - 1)

    grid_spec = pltpu.PrefetchScalarGridSpec(
        num_scalar_prefetch=2,
        grid=(nf, n_chunks),
        in_specs=[pl.BlockSpec((rows, d), lambda j, c, ce, nv: (last(c, nv), 0)),
                  pl.BlockSpec((1, d, tf), lambda j, c, ce, nv: (ce[c], 0, j)),
                  pl.BlockSpec((1, d, tf), lambda j, c, ce, nv: (ce[c], 0, nf + j)),
                  pl.BlockSpec((1, 1, tf), lambda j, c, ce, nv: (ce[c], 0, j)),
                  pl.BlockSpec((1, 1, tf), lambda j, c, ce, nv: (ce[c], 0, nf + j))],
        out_specs=pl.BlockSpec((rows, tf), lambda j, c, ce, nv: (last(c, nv), j)),
        scratch_shapes=[pltpu.VMEM((d, tf), BF16), pltpu.VMEM((d, tf), BF16)],
    )
    b3 = b_up.reshape(e, 1, f2)
    return pl.pallas_call(
        _moe_up_kernel,
        grid_spec=grid_spec,
        out_shape=jax.ShapeDtypeStruct((n_chunks * rows, dff), BF16),
        compiler_params=_params("arbitrary", "arbitrary"),
        name="moe_up_swiglu",
    )(chunk_expert, n_valid, xs, w_up, w_up, b3, b3)


def _moe_down_kernel(ce_ref, nv_ref, a_ref, w_ref, b_ref, o_ref, w16):
    c = pl.program_id(1)
    valid = c < nv_ref[0]
    new_expert = (c == 0) | (ce_ref[c] != ce_ref[jnp.maximum(c - 1, 0)])

    @pl.when(valid & new_expert)
    def _():
        w16[...] = w_ref[0].astype(BF16)

    @pl.when(valid)
    def _():
        o_ref[...] = _dot(a_ref[...], w16[...]) + b_ref[0]


def _moe_down(act, w_down, b_down, chunk_expert, n_valid):
    rows = MOE_ROWS
    n_chunks = act.shape[0] // rows
    e, dff, d = w_down.shape
    tn = _pick(d, (2048, 1024, 512, 256, 128))
    nn = d // tn

    def last(c, nv):
        return jnp.minimum(c, nv[0] - 1)

    grid_spec = pltpu.PrefetchScalarGridSpec(
        num_scalar_prefetch=2,
        grid=(nn, n_chunks),
        in_specs=[pl.BlockSpec((rows, dff), lambda j, c, ce, nv: (last(c, nv), 0)),
                  pl.BlockSpec((1, dff, tn), lambda j, c, ce, nv: (ce[c], 0, j)),
                  pl.BlockSpec((1, 1, tn), lambda j, c, ce, nv: (ce[c], 0, j))],
        out_specs=pl.BlockSpec((rows, tn), lambda j, c, ce, nv: (last(c, nv), j)),
        scratch_shapes=[pltpu.VMEM((dff, tn), BF16)],
    )
    return pl.pallas_call(
        _moe_down_kernel,
        grid_spec=grid_spec,
        out_shape=jax.ShapeDtypeStruct((n_chunks * rows, d), F32),
        compiler_params=_params("arbitrary", "arbitrary"),
        name="moe_down",
    )(chunk_expert, n_valid, act, w_down, b_down.reshape(e, 1, d))


def _combine_kernel(dest_ref, x1_ref, gate_ref, gple_ref, y_hbm, x2_ref, h3_ref, buf, sem):
    rows = x1_ref.shape[0]

    def row_copy(r, k):
        return pltpu.make_async_copy(y_hbm.at[pl.ds(dest_ref[r * TOP_K + k], 1)],
                                     buf.at[k, pl.ds(r, 1)], sem)

    def start(r, carry):
        for k in range(TOP_K):
            row_copy(r, k).start()
        return carry

    def wait(r, carry):
        for k in range(TOP_K):
            row_copy(r, k).wait()
        return carry

    lax.fori_loop(0, rows, start, 0)
    lax.fori_loop(0, rows, wait, 0)
    gate = gate_ref[...]
    acc = x1_ref[...]
    for k in range(TOP_K):
        acc = acc + gate[:, k:k + 1] * buf[k]
    x2_ref[...] = acc
    h3_ref[...] = _rms(acc, gple_ref[...]).astype(h3_ref.dtype)


def _combine(x1, gate, dest, y, g_ple):
    n, d = x1.shape
    rows = _pick(n, (COMBINE_ROWS, 64, 32, 16))
    return pl.pallas_call(
        _combine_kernel,
        grid=(n // rows,),
        in_specs=[pl.BlockSpec((rows * TOP_K,), lambda i: (i,), memory_space=pltpu.SMEM),
                  pl.BlockSpec((rows, d), lambda i: (i, 0)),
                  pl.BlockSpec((rows, LANES), lambda i: (i, 0)),
                  pl.BlockSpec((1, d), lambda i: (0, 0)),
                  pl.BlockSpec(memory_space=pl.ANY)],
        out_specs=[pl.BlockSpec((rows, d), lambda i: (i, 0)),
                   pl.BlockSpec((rows, d), lambda i: (i, 0))],
        out_shape=[jax.ShapeDtypeStruct((n, d), F32), jax.ShapeDtypeStruct((n, d), BF16)],
        scratch_shapes=[pltpu.VMEM((TOP_K, rows, d), F32), pltpu.SemaphoreType.DMA(())],
        compiler_params=_params("arbitrary"),
        name="moe_combine_gather",
    )(dest, x1, gate, g_ple.reshape(1, d), y)


def _ple_kernel(x2_ref, h3_ref, p_ref, wpg_ref, wple_ref, o_ref):
    gate = jax.nn.sigmoid(_dot(h3_ref[...], wpg_ref[...]))
    o_ref[...] = x2_ref[...] + gate * _dot(p_ref[...].astype(BF16), wple_ref[...])


def _ple(x2, h3, p, w_pg, w_ple, row0, r):
    d = x2.shape[1]
    pd = p.shape[1]
    tm = _pick(r, (512, 256, 128, 64, 32, 16))
    tn = _pick(d, (1024, 512, 256, 128))
    assert row0 % tm == 0
    b0 = row0 // tm
    return pl.pallas_call(
        _ple_kernel,
        grid=(d // tn, r // tm),
        in_specs=[pl.BlockSpec((tm, tn), lambda j, i: (b0 + i, j)),
                  pl.BlockSpec((tm, d), lambda j, i: (b0 + i, 0)),
                  pl.BlockSpec((tm, pd), lambda j, i: (i, 0)),
                  pl.BlockSpec((d, tn), lambda j, i: (0, j)),
                  pl.BlockSpec((pd, tn), lambda j, i: (0, j))],
        out_specs=pl.BlockSpec((tm, tn), lambda j, i: (i, j)),
        out_shape=jax.ShapeDtypeStruct((r, d), F32),
        compiler_params=_params("parallel", "parallel"),
        name="ple_gate",
    )(x2, h3, p, w_pg, w_ple)


def _rope_tables(pos):
    half = ROT_DIM // 2
    inv = jnp.power(ROPE_THETA, -jnp.arange(half, dtype=F32) * 2.0 / ROT_DIM)
    ang = pos.astype(F32)[:, None] * inv[None, :]
    cos, sin = jnp.cos(ang), jnp.sin(ang)
    r = pos.shape[0]
    rest = A_HEAD_DIM - ROT_DIM
    c = jnp.concatenate([cos, cos, jnp.ones((r, rest), F32)], axis=1)
    s1 = jnp.concatenate([-sin, jnp.zeros((r, half + rest), F32)], axis=1)
    s2 = jnp.concatenate([jnp.zeros((r, half), F32), sin, jnp.zeros((r, rest), F32)], axis=1)
    return c, s1, s2


def _mixer_inputs(x, pos, g_attn, w, b_i, b_f, g_q, g_k, mlstm_dtype):
    d = x.shape[1]
    kq = M_HEADS * M_QK
    o_ig = 2 * kq + M_WIDTH
    o_om = o_ig + 2 * M_HEADS
    o_qa = o_om + M_WIDTH
    o_gm = o_qa + 3 * A_WIDTH
    h = _rmsnorm_bf16(x, g_attn)
    colscale = jnp.concatenate([jnp.full((1, kq), M_QK ** -0.5, F32), jnp.ones((1, kq + M_WIDTH), F32)], axis=1)
    qkv_m = _proj_scale(h, w[:, :o_ig].astype(BF16), colscale, mlstm_dtype)
    wg = w[:, o_ig:o_om]
    wc = jnp.pad(wg, ((0, 0), (0, LANES - 2 * M_HEADS))).astype(BF16)
    bias = jnp.concatenate([b_i, b_f]).astype(F32)
    bc = jnp.pad(bias, (0, LANES - 2 * M_HEADS)).reshape(1, LANES)
    gcol, grow = _proj_gates(h, wc, wg.T.astype(BF16), bc, bias.reshape(2 * M_HEADS, 1))
    sg = _proj_sigmoid(h, jnp.concatenate([w[:, o_om:o_qa], w[:, o_gm:]], axis=1).astype(BF16))
    tabs = _rope_tables(pos)
    q32, _ = _proj_qk(h, w[:, o_qa:o_qa + A_WIDTH].astype(BF16), g_q, *tabs)
    k32, k16 = _proj_qk(h, w[:, o_qa + A_WIDTH:o_qa + 2 * A_WIDTH].astype(BF16), g_k, *tabs)
    v32, v16 = _proj_plain2(h, w[:, o_qa + 2 * A_WIDTH:o_gm].astype(BF16))
    return qkv_m, gcol, grow, sg, q32, k32, k16, v32, v16


def _moe_plan(eid, n_chunks):
    n = eid.shape[0]
    e_flat = eid[:, :TOP_K].reshape(-1)
    onehot = (e_flat[:, None] == jnp.arange(N_EXPERTS, dtype=jnp.int32)[None, :]).astype(jnp.int32)
    before = jnp.cumsum(onehot, axis=0) - onehot
    rank = jnp.sum(before * onehot, axis=1)
    counts = jnp.sum(onehot, axis=0)
    padded = (counts + MOE_ROWS - 1) // MOE_ROWS * MOE_ROWS
    p_ends = jnp.cumsum(padded)
    p_starts = p_ends - padded
    dest = (p_starts[e_flat] + rank).astype(jnp.int32)
    tok = jnp.repeat(jnp.arange(n, dtype=jnp.int32), TOP_K)
    src_tok = jnp.zeros((n_chunks * MOE_ROWS,), jnp.int32).at[dest].set(tok)
    n_valid = (p_ends[-1] // MOE_ROWS).astype(jnp.int32).reshape(1)
    chunk0 = jnp.minimum(jnp.arange(n_chunks, dtype=jnp.int32), n_valid[0] - 1) * MOE_ROWS
    chunk_expert = jnp.minimum(jnp.sum((p_ends[None, :] <= chunk0[:, None]).astype(jnp.int32), axis=1),
                               N_EXPERTS - 1)
    return dest, src_tok, n_valid, chunk_expert


def kernel(x_prompt, x_sample, cache_k, cache_v, state_mlstm_C, state_mlstm_n, state_mlstm_m, page_table, p_prompt, p_sample, g_attn, w_in, b_i, b_f, g_q, g_k, g_mh, w_bm, w_ba, w_out, g_ffn, w_router, b_router, w_up, b_up, w_down, b_down, g_ple, w_pg, w_ple):
    depth = w_in.shape[0]
    bsz, t, d = x_prompt.shape
    ns, dec_seq, _ = x_sample.shape
    assert depth == 1 and bsz == 1 and dec_seq == 1
    page = cache_k.shape[2]
    past = page_table.shape[1] * page
    assert t % MOBA_BLOCK == 0 and past % MOBA_BLOCK == 0 and MOBA_BLOCK % page == 0
    assert t % (M_CHUNK * MLSTM_CHUNKS_PER_STEP) == 0 and ns % MLSTM_SAMPLE_SEQS == 0
    n = t + ns

    xp = x_prompt[0]
    xs = x_sample[:, 0]
    w = w_in[0]
    fw = (g_attn[0], w, b_i[0], b_f[0], g_q[0], g_k[0])

    qkv_p, gcol_p, grow_p, sg_p, q32_p, k32_p, k16_p, v32_p, v16_p = _mixer_inputs(
        xp, jnp.arange(t, dtype=jnp.int32), *fw, BF16)
    hm_p, ct_p, n_p, m_p = _mlstm_prompt(qkv_p, gcol_p, grow_p)
    ao_p = _moba_prompt(q32_p, k16_p, v16_p, _block_kmean(k32_p))

    qkv_s, gcol_s, _, sg_s, q32_s, k32_s, _, v32_s, _ = _mixer_inputs(
        xs, jnp.full((ns,), past, jnp.int32), *fw, F32)
    hm_s, c_s, n_s, m_s = _mlstm_sample(qkv_s, gcol_s, state_mlstm_C[0].astype(F32),
                                        state_mlstm_n[0].astype(F32), state_mlstm_m[0].astype(F32))
    ao_s = _moba_sample(q32_s, k32_s, v32_s, cache_k, cache_v, page_table)

    wbm, wba, wout = w_bm[0].astype(BF16), w_ba[0].astype(BF16), w_out[0].astype(BF16)
    mix_p = _mix(hm_p, sg_p, ao_p, g_mh[0], wbm, wba)
    mix_s = _mix(hm_s, sg_s, ao_s, g_mh[0], wbm, wba)
    wr = jnp.pad(w_router[0], ((0, 0), (0, LANES - N_EXPERTS)))
    wr_hi = wr.astype(BF16)
    wr_lo = (wr - wr_hi.astype(F32)).astype(BF16)
    b_r = jnp.pad(b_router[0].astype(F32), (0, LANES - N_EXPERTS)).reshape(1, LANES)
    routed = _resid_router(xp, mix_p, wout, g_ffn[0], wr_hi, wr_lo, b_r, n, 0, None)
    x1, gate, eid = _resid_router(xs, mix_s, wout, g_ffn[0], wr_hi, wr_lo, b_r, n, t, routed)

    n_slots = n * TOP_K
    n_chunks = (n_slots + N_EXPERTS * (MOE_ROWS - 1)) // MOE_ROWS
    dest, src_tok, n_valid, chunk_expert = _moe_plan(eid, n_chunks)
    xs_sorted = _dispatch(x1, g_ffn[0], src_tok, n_valid, n_chunks)
    act = _moe_up(xs_sorted, w_up[0], b_up[0], chunk_expert, n_valid)
    y = _moe_down(act, w_down[0], b_down[0], chunk_expert, n_valid)
    x2, h3 = _combine(x1, gate, dest, y, g_ple[0])

    wpg, wple = w_pg[0].astype(BF16), w_ple[0].astype(BF16)
    y_p = _ple(x2, h3, p_prompt[0, 0], wpg, wple, 0, t)
    y_s = _ple(x2, h3, p_sample[0, :, 0], wpg, wple, t, ns)

    def heads(a):
        return a.reshape(a.shape[0], A_HEADS, A_HEAD_DIM)

    return (y_p[None], y_s[:, None],
            heads(k32_p)[None, None], heads(v32_p)[None, None],
            jnp.swapaxes(ct_p, 1, 2)[None, None].astype(state_mlstm_C.dtype),
            n_p[:M_HEADS][None, None].astype(state_mlstm_n.dtype),
            m_p[:M_HEADS, 0][None, None].astype(state_mlstm_m.dtype),
            heads(k32_s)[None, :, None], heads(v32_s)[None, :, None],
            c_s[None].astype(state_mlstm_C.dtype), n_s[None].astype(state_mlstm_n.dtype),
            m_s[None].astype(state_mlstm_m.dtype))
```

```python
import functools

import jax
import jax.numpy as jnp
import numpy as np
from jax import lax
from jax.experimental import pallas as pl
from jax.experimental.pallas import tpu as pltpu

F32 = jnp.float32
BF16 = jnp.bfloat16

M_HEADS = 4
M_QK = 128
M_V = 256
M_CHUNK = 64
M_WIDTH = M_HEADS * M_V
A_HEADS = 8
A_HEAD_DIM = 128
A_WIDTH = A_HEADS * A_HEAD_DIM
MOBA_BLOCK = 256
MOBA_TOPK = 3
ROT_DIM = A_HEAD_DIM // 4
ROPE_THETA = 500000.0
N_EXPERTS = 32
TOP_K = 4
SWIGLU_LIMIT = 7.0
SWIGLU_ALPHA = 1.702
EPS = 1e-6
NEG = -1e30

LANES = 128
SUBLANES = 8
VMEM_LIMIT_BYTES = 56 * 1024 * 1024

MOE_ROWS = 256
MLSTM_CHUNKS_PER_STEP = 2
MLSTM_SAMPLE_SEQS = 8
COMBINE_ROWS = 128
MOBA_HEADS_PER_STEP = 4
GATHER_UNROLL = 8


def _pick(n, candidates):
    for c in candidates:
        if n % c == 0:
            return c
    raise ValueError(f"no tile in {candidates} divides {n}")


def _params(*sem):
    return pltpu.CompilerParams(dimension_semantics=sem, vmem_limit_bytes=VMEM_LIMIT_BYTES)


def _dot(a, b):
    return jnp.dot(a, b, preferred_element_type=F32)


def _dot_nt(a, b):
    return lax.dot_general(a, b, (((1,), (1,)), ((), ())), preferred_element_type=F32)


def _dot_tn(a, b):
    return lax.dot_general(a, b, (((0,), (0,)), ((), ())), preferred_element_type=F32)


def _split_bf16(x):
    hi = x.astype(BF16)
    lo = (x - hi.astype(F32)).astype(BF16)
    return hi, lo


def _rms(x, g):
    return x * lax.rsqrt(jnp.mean(x * x, axis=-1, keepdims=True) + EPS) * g


def _log_sigmoid(x):
    return -(jnp.maximum(-x, 0.0) + jnp.log1p(jnp.exp(-jnp.abs(x))))


def _first_argmax(work, lane_f):
    m = jnp.max(work, axis=-1, keepdims=True)
    idx = jnp.min(jnp.where(work == m, lane_f, float(4 * LANES)), axis=-1, keepdims=True)
    return m, idx


def _rmsnorm_kernel(x_ref, g_ref, o_ref):
    o_ref[...] = _rms(x_ref[...], g_ref[...]).astype(o_ref.dtype)


def _rmsnorm_bf16(x, g):
    r, d = x.shape
    tm = _pick(r, (512, 256, 128, 64, 32, 16))
    return pl.pallas_call(
        _rmsnorm_kernel,
        grid=(r // tm,),
        in_specs=[pl.BlockSpec((tm, d), lambda i: (i, 0)), pl.BlockSpec((1, d), lambda i: (0, 0))],
        out_specs=pl.BlockSpec((tm, d), lambda i: (i, 0)),
        out_shape=jax.ShapeDtypeStruct((r, d), BF16),
        compiler_params=_params("parallel"),
        name="rmsnorm_rows",
    )(x, g.reshape(1, d))


def _proj_scale_kernel(h_ref, w_ref, s_ref, o_ref):
    o_ref[...] = (_dot(h_ref[...], w_ref[...]) * s_ref[...]).astype(o_ref.dtype)


def _proj_sigmoid_kernel(h_ref, w_ref, o_ref):
    o_ref[...] = jax.nn.sigmoid(_dot(h_ref[...], w_ref[...])).astype(o_ref.dtype)


def _proj_plain2_kernel(h_ref, w_ref, o32_ref, o16_ref):
    acc = _dot(h_ref[...], w_ref[...])
    o32_ref[...] = acc
    o16_ref[...] = acc.astype(BF16)


def _proj_qk_kernel(h_ref, w_ref, g_ref, c_ref, s1_ref, s2_ref, o32_ref, o16_ref):
    acc = _dot(h_ref[...], w_ref[...])
    g = g_ref[...]
    c, s1, s2 = c_ref[...], s1_ref[...], s2_ref[...]
    for hh in range(acc.shape[1] // A_HEAD_DIM):
        sl = slice(hh * A_HEAD_DIM, (hh + 1) * A_HEAD_DIM)
        y = _rms(acc[:, sl], g)
        up = pltpu.roll(y, A_HEAD_DIM - ROT_DIM // 2, 1)
        dn = pltpu.roll(y, ROT_DIM // 2, 1)
        r = y * c + up * s1 + dn * s2
        o32_ref[:, sl] = r
        o16_ref[:, sl] = r.astype(BF16)


def _proj_gates_kernel(h_ref, wc_ref, wr_ref, bc_ref, br_ref, gc_ref, gr_ref):
    h = h_ref[...]
    zc = _dot(h, wc_ref[...]) + bc_ref[...]
    zr = _dot_nt(wr_ref[...], h) + br_ref[...]
    lane = lax.broadcasted_iota(jnp.int32, zc.shape, 1)
    gc_ref[...] = jnp.where((lane >= M_HEADS) & (lane < 2 * M_HEADS), _log_sigmoid(zc), zc)
    row = lax.broadcasted_iota(jnp.int32, zr.shape, 0)
    gr_ref[...] = jnp.where(row >= M_HEADS, _log_sigmoid(zr), zr)


def _proj_tiles(r, n):
    tm = _pick(r, (1024, 512, 256, 128, 64, 32, 16))
    tn = _pick(n, (1024, 512, 256, 128))
    return tm, tn


def _proj_scale(h, w, colscale, out_dtype):
    r, d = h.shape
    n = w.shape[1]
    tm, tn = _proj_tiles(r, n)
    return pl.pallas_call(
        _proj_scale_kernel,
        grid=(n // tn, r // tm),
        in_specs=[pl.BlockSpec((tm, d), lambda j, i: (i, 0)),
                  pl.BlockSpec((d, tn), lambda j, i: (0, j)),
                  pl.BlockSpec((1, tn), lambda j, i: (0, j))],
        out_specs=pl.BlockSpec((tm, tn), lambda j, i: (i, j)),
        out_shape=jax.ShapeDtypeStruct((r, n), out_dtype),
        compiler_params=_params("parallel", "parallel"),
        name="proj_scale",
    )(h, w, colscale)


def _proj_sigmoid(h, w):
    r, d = h.shape
    n = w.shape[1]
    tm, tn = _proj_tiles(r, n)
    return pl.pallas_call(
        _proj_sigmoid_kernel,
        grid=(n // tn, r // tm),
        in_specs=[pl.BlockSpec((tm, d), lambda j, i: (i, 0)),
                  pl.BlockSpec((d, tn), lambda j, i: (0, j))],
        out_specs=pl.BlockSpec((tm, tn), lambda j, i: (i, j)),
        out_shape=jax.ShapeDtypeStruct((r, n), BF16),
        compiler_params=_params("parallel", "parallel"),
        name="proj_sigmoid",
    )(h, w)


def _proj_plain2(h, w):
    r, d = h.shape
    n = w.shape[1]
    tm, tn = _proj_tiles(r, n)
    return pl.pallas_call(
        _proj_plain2_kernel,
        grid=(n // tn, r // tm),
        in_specs=[pl.BlockSpec((tm, d), lambda j, i: (i, 0)),
                  pl.BlockSpec((d, tn), lambda j, i: (0, j))],
        out_specs=[pl.BlockSpec((tm, tn), lambda j, i: (i, j)),
                   pl.BlockSpec((tm, tn), lambda j, i: (i, j))],
        out_shape=[jax.ShapeDtypeStruct((r, n), F32), jax.ShapeDtypeStruct((r, n), BF16)],
        compiler_params=_params("parallel", "parallel"),
        name="proj_plain",
    )(h, w)


def _proj_qk(h, w, g, rope_c, rope_s1, rope_s2):
    r, d = h.shape
    n = w.shape[1]
    tm, tn = _proj_tiles(r, n)
    hd = A_HEAD_DIM
    return pl.pallas_call(
        _proj_qk_kernel,
        grid=(n // tn, r // tm),
        in_specs=[pl.BlockSpec((tm, d), lambda j, i: (i, 0)),
                  pl.BlockSpec((d, tn), lambda j, i: (0, j)),
                  pl.BlockSpec((1, hd), lambda j, i: (0, 0)),
                  pl.BlockSpec((tm, hd), lambda j, i: (i, 0)),
                  pl.BlockSpec((tm, hd), lambda j, i: (i, 0)),
                  pl.BlockSpec((tm, hd), lambda j, i: (i, 0))],
        out_specs=[pl.BlockSpec((tm, tn), lambda j, i: (i, j)),
                   pl.BlockSpec((tm, tn), lambda j, i: (i, j))],
        out_shape=[jax.ShapeDtypeStruct((r, n), F32), jax.ShapeDtypeStruct((r, n), BF16)],
        compiler_params=_params("parallel", "parallel"),
        name="proj_qk_norm_rope",
    )(h, w, g.reshape(1, hd), rope_c, rope_s1, rope_s2)


def _proj_gates(h, wc, wr, bc, br):
    r, d = h.shape
    tm = _pick(r, (1024, 512, 256, 128)) if r % LANES == 0 else r
    g2 = 2 * M_HEADS
    return pl.pallas_call(
        _proj_gates_kernel,
        grid=(r // tm,),
        in_specs=[pl.BlockSpec((tm, d), lambda i: (i, 0)),
                  pl.BlockSpec((d, LANES), lambda i: (0, 0)),
                  pl.BlockSpec((g2, d), lambda i: (0, 0)),
                  pl.BlockSpec((1, LANES), lambda i: (0, 0)),
                  pl.BlockSpec((g2, 1), lambda i: (0, 0))],
        out_specs=[pl.BlockSpec((tm, LANES), lambda i: (i, 0)),
                   pl.BlockSpec((g2, tm), lambda i: (0, i))],
        out_shape=[jax.ShapeDtypeStruct((r, LANES), F32), jax.ShapeDtypeStruct((g2, r), F32)],
        compiler_params=_params("parallel"),
        name="proj_gates",
    )(h, wc, wr, bc, br)


def _mlstm_prompt_kernel(q_ref, k_ref, v_ref, gc_ref, gr_ref, h_ref, ct_out, n_out, m_out,
                         ct_s, n_s, m_s):
    step = pl.program_id(0)

    @pl.when(step == 0)
    def _():
        ct_s[...] = jnp.zeros_like(ct_s)
        n_s[...] = jnp.zeros_like(n_s)
        m_s[...] = jnp.zeros_like(m_s)

    ln = M_CHUNK
    row = lax.broadcasted_iota(jnp.int32, (ln, ln), 0)
    col = lax.broadcasted_iota(jnp.int32, (ln, ln), 1)
    causal = col <= row
    upto = row <= col
    for cc in range(MLSTM_CHUNKS_PER_STEP):
        rs = slice(cc * ln, (cc + 1) * ln)
        for hh in range(M_HEADS):
            ig_c = gc_ref[rs, hh:hh + 1]
            lf_c = gc_ref[rs, M_HEADS + hh:M_HEADS + hh + 1]
            ig_r = gr_ref[hh:hh + 1, rs]
            lf_r = gr_ref[M_HEADS + hh:M_HEADS + hh + 1, rs]
            bcum_c = jnp.sum(jnp.where(causal, lf_r, 0.0), axis=1, keepdims=True)
            bcum_r = jnp.sum(jnp.where(upto, lf_c, 0.0), axis=0, keepdims=True)
            dmat = jnp.where(causal, bcum_c - bcum_r + ig_r, NEG)
            m_prev = m_s[hh:hh + 1, 0:1]
            a = bcum_c + m_prev
            m_t = jnp.maximum(a, jnp.max(dmat, axis=1, keepdims=True))
            w_inter = jnp.exp(a - m_t)
            decay = jnp.exp(dmat - m_t)
            qh = q_ref[rs, hh * M_QK:(hh + 1) * M_QK]
            kh = k_ref[rs, hh * M_QK:(hh + 1) * M_QK]
            vh = v_ref[rs, hh * M_V:(hh + 1) * M_V]
            s = _dot_nt(qh, kh) * decay
            ct = ct_s[hh]
            n_row = n_s[hh:hh + 1, :]
            num = w_inter * _dot(qh, ct.astype(BF16)) + _dot(s.astype(BF16), vh)
            den = (w_inter * jnp.sum(qh.astype(F32) * n_row, axis=1, keepdims=True)
                   + jnp.sum(s, axis=1, keepdims=True))
            h_ref[rs, hh * M_V:(hh + 1) * M_V] = num / jnp.maximum(jnp.abs(den), jnp.exp(-m_t))
            b_last = bcum_c[ln - 1:ln, :]
            d_last_r = b_last - bcum_r + ig_r
            d_last_c = b_last - bcum_c + ig_c
            a_last = b_last + m_prev
            m_new = jnp.maximum(a_last, jnp.max(d_last_r, axis=1, keepdims=True))
            w_c = jnp.exp(a_last - m_new)
            w_j = jnp.exp(d_last_c - m_new)
            vw = (vh.astype(F32) * w_j).astype(BF16)
            ct_s[hh] = w_c * ct + _dot_tn(kh, vw)
            n_s[hh:hh + 1, :] = w_c * n_row + jnp.sum(kh.astype(F32) * w_j, axis=0, keepdims=True)
            m_s[hh:hh + 1, :] = jnp.broadcast_to(m_new, (1, LANES))

    @pl.when(step == pl.num_programs(0) - 1)
    def _():
        ct_out[...] = ct_s[...]
        n_out[...] = n_s[...]
        m_out[...] = m_s[...]


def _mlstm_prompt(qkv, gcol, grow):
    t = qkv.shape[0]
    rows = M_CHUNK * MLSTM_CHUNKS_PER_STEP
    kq = M_HEADS * M_QK
    return pl.pallas_call(
        _mlstm_prompt_kernel,
        grid=(t // rows,),
        in_specs=[pl.BlockSpec((rows, kq), lambda c: (c, 0)),
                  pl.BlockSpec((rows, kq), lambda c: (c, 1)),
                  pl.BlockSpec((rows, M_WIDTH), lambda c: (c, (2 * kq) // M_WIDTH)),
                  pl.BlockSpec((rows, LANES), lambda c: (c, 0)),
                  pl.BlockSpec((2 * M_HEADS, rows), lambda c: (0, c))],
        out_specs=[pl.BlockSpec((rows, M_WIDTH), lambda c: (c, 0)),
                   pl.BlockSpec((M_HEADS, M_QK, M_V), lambda c: (0, 0, 0)),
                   pl.BlockSpec((SUBLANES, LANES), lambda c: (0, 0)),
                   pl.BlockSpec((SUBLANES, LANES), lambda c: (0, 0))],
        out_shape=[jax.ShapeDtypeStruct((t, M_WIDTH), F32),
                   jax.ShapeDtypeStruct((M_HEADS, M_QK, M_V), F32),
                   jax.ShapeDtypeStruct((SUBLANES, LANES), F32),
                   jax.ShapeDtypeStruct((SUBLANES, LANES), F32)],
        scratch_shapes=[pltpu.VMEM((M_HEADS, M_QK, M_V), F32),
                        pltpu.VMEM((SUBLANES, LANES), F32),
                        pltpu.VMEM((SUBLANES, LANES), F32)],
        compiler_params=_params("arbitrary"),
        name="mlstm_prompt_scan",
    )(qkv, qkv, qkv, gcol, grow)


def _mlstm_sample_kernel(qkv_ref, gc_ref, c_ref, n_ref, m_ref, h_ref, co_ref, no_ref, mo_ref):
    kq = M_HEADS * M_QK
    row8 = lax.broadcasted_iota(jnp.int32, (SUBLANES, 1), 0)
    mrow = lax.broadcasted_iota(jnp.int32, mo_ref.shape, 0)
    mcol = lax.broadcasted_iota(jnp.int32, mo_ref.shape, 1)
    m_all = m_ref[...]
    for s in range(MLSTM_SAMPLE_SEQS):
        for hh in range(M_HEADS):
            q = qkv_ref[s:s + 1, hh * M_QK:(hh + 1) * M_QK]
            k = qkv_ref[s:s + 1, kq + hh * M_QK:kq + (hh + 1) * M_QK]
            v = qkv_ref[s:s + 1, 2 * kq + hh * M_V:2 * kq + (hh + 1) * M_V]
            ig = gc_ref[s:s + 1, hh:hh + 1]
            lf = gc_ref[s:s + 1, M_HEADS + hh:M_HEADS + hh + 1]
            m_prev = m_ref[s:s + 1, hh:hh + 1]
            c = c_ref[s, hh]
            n_row = n_ref[s, hh:hh + 1, :]
            a = lf + m_prev
            m_t = jnp.maximum(a, ig)
            w_c = jnp.exp(a - m_t)
            w_j = jnp.exp(ig - m_t)
            sc = jnp.sum(q * k, axis=1, keepdims=True) * w_j
            q8 = jnp.broadcast_to(q, (SUBLANES, M_QK)).astype(BF16)
            cq = _dot_nt(q8, c.astype(BF16))[0:1, :]
            num = w_c * cq + sc * v
            den = w_c * jnp.sum(n_row * q, axis=1, keepdims=True) + sc
            h_ref[s:s + 1, hh * M_V:(hh + 1) * M_V] = num / jnp.maximum(jnp.abs(den), jnp.exp(-m_t))
            vw8 = jnp.where(row8 == 0, v * w_j, 0.0).astype(BF16)
            k8 = jnp.where(row8 == 0, k, 0.0).astype(BF16)
            co_ref[s, hh] = w_c * c + _dot_tn(vw8, k8)
            no_ref[s, hh:hh + 1, :] = w_c * n_row + w_j * k
            m_all = jnp.where((mrow == s) & (mcol == hh), m_t, m_all)
    mo_ref[...] = m_all


def _mlstm_sample(qkv, gcol, c0, n0, m0):
    ns = qkv.shape[0]
    sb = MLSTM_SAMPLE_SEQS
    wq = qkv.shape[1]
    return pl.pallas_call(
        _mlstm_sample_kernel,
        grid=(ns // sb,),
        in_specs=[pl.BlockSpec((sb, wq), lambda i: (i, 0)),
                  pl.BlockSpec((sb, LANES), lambda i: (i, 0)),
                  pl.BlockSpec((sb, M_HEADS, M_V, M_QK), lambda i: (i, 0, 0, 0)),
                  pl.BlockSpec((sb, M_HEADS, M_QK), lambda i: (i, 0, 0)),
                  pl.BlockSpec((sb, M_HEADS), lambda i: (i, 0))],
        out_specs=[pl.BlockSpec((sb, M_WIDTH), lambda i: (i, 0)),
                   pl.BlockSpec((sb, M_HEADS, M_V, M_QK), lambda i: (i, 0, 0, 0)),
                   pl.BlockSpec((sb, M_HEADS, M_QK), lambda i: (i, 0, 0)),
                   pl.BlockSpec((sb, M_HEADS), lambda i: (i, 0))],
        out_shape=[jax.ShapeDtypeStruct((ns, M_WIDTH), F32),
                   jax.ShapeDtypeStruct(c0.shape, F32),
                   jax.ShapeDtypeStruct(n0.shape, F32),
                   jax.ShapeDtypeStruct(m0.shape, F32)],
        compiler_params=_params("parallel"),
        name="mlstm_sample_step",
    )(qkv, gcol, c0, n0, m0)


def _kmean_kernel(k_ref, o_ref):
    o_ref[0] = jnp.mean(k_ref[...], axis=0, keepdims=True)


def _block_kmean(k32):
    t, w = k32.shape
    nb = t // MOBA_BLOCK
    out = pl.pallas_call(
        _kmean_kernel,
        grid=(nb,),
        in_specs=[pl.BlockSpec((MOBA_BLOCK, w), lambda b: (b, 0))],
        out_specs=pl.BlockSpec((1, 1, w), lambda b: (b, 0, 0)),
        out_shape=jax.ShapeDtypeStruct((nb, 1, w), F32),
        compiler_params=_params("parallel"),
        name="moba_block_kmean",
    )(k32)
    return out.reshape(nb, w)


def _moba_select(sc, own, lane_i, lane_f):
    work = jnp.where(lane_i < own, sc, NEG)
    sel = jnp.zeros(sc.shape, dtype=jnp.bool_)
    for _ in range(MOBA_TOPK):
        _, idx = _first_argmax(work, lane_f)
        hit = lane_f == idx
        sel = sel | hit
        work = jnp.where(hit, -jnp.inf, work)
    return sel & (lane_i < own)


def _moba_prompt_kernel(q_ref, k_ref, v_ref, e_ref, km_ref, o_ref):
    i = pl.program_id(1)
    bs = MOBA_BLOCK
    pw = 2 * bs
    hd = A_HEAD_DIM
    heads = range(MOBA_HEADS_PER_STEP)
    ones = jnp.ones((pw, hd), BF16)
    lane_i = lax.broadcasted_iota(jnp.int32, (bs, LANES), 1)
    lane_f = lane_i.astype(F32)

    def head_slice(hh):
        return slice(hh * hd, (hh + 1) * hd)

    q_aug = []
    for hh in heads:
        q = q_ref[:, head_slice(hh)]
        qh, ql = _split_bf16(q)
        kmh, kml = _split_bf16(km_ref[:, head_slice(hh)])
        sc = _dot_nt(qh, kmh) + (_dot_nt(qh, kml) + _dot_nt(ql, kmh))
        sel = _moba_select(sc, i, lane_i, lane_f)
        bias = jnp.where(sel | (lane_i >= i), 0.0, NEG)
        q_aug.append(jnp.concatenate([(q * (hd ** -0.5)).astype(BF16), bias.astype(BF16)], axis=1))

    def pair_scores(p, hh, onehot):
        off = pl.multiple_of(p * pw, pw)
        k_aug = jnp.concatenate([k_ref[pl.ds(off, pw), head_slice(hh)], onehot], axis=1)
        v_aug = jnp.concatenate([v_ref[pl.ds(off, pw), head_slice(hh)], ones], axis=1)
        return _dot_nt(q_aug[hh], k_aug), v_aug

    p_own = i // 2
    row = lax.broadcasted_iota(jnp.int32, (bs, pw), 0)
    col = lax.broadcasted_iota(jnp.int32, (bs, pw), 1)
    causal = col + (p_own * pw - i * bs) <= row
    onehot_own = e_ref[pl.ds(pl.multiple_of(p_own * pw, pw), pw), :]
    init = []
    for hh in heads:
        s, v_aug = pair_scores(p_own, hh, onehot_own)
        s = jnp.where(causal, s, NEG)
        m0 = jnp.max(s, axis=1, keepdims=True)
        init += [m0, _dot(jnp.exp(s - m0).astype(BF16), v_aug)]

    def body(p, carry):
        onehot = e_ref[pl.ds(pl.multiple_of(p * pw, pw), pw), :]
        out = []
        for hh in heads:
            m_i, acc = carry[2 * hh], carry[2 * hh + 1]
            sj, vj = pair_scores(p, hh, onehot)
            m_n = jnp.maximum(m_i, jnp.max(sj, axis=1, keepdims=True))
            out += [m_n, jnp.exp(m_i - m_n) * acc + _dot(jnp.exp(sj - m_n).astype(BF16), vj)]
        return tuple(out)

    final = lax.fori_loop(0, p_own, body, tuple(init))
    for hh in heads:
        acc = final[2 * hh + 1]
        o_ref[:, head_slice(hh)] = (acc[:, :hd] / acc[:, hd:]).astype(o_ref.dtype)


def _moba_prompt(q32, k16, v16, kmean):
    t = q32.shape[0]
    nb = t // MOBA_BLOCK
    hd = MOBA_HEADS_PER_STEP * A_HEAD_DIM
    assert nb <= LANES and nb % 2 == 0 and A_HEADS % MOBA_HEADS_PER_STEP == 0
    block_onehot = (jnp.arange(t, dtype=jnp.int32)[:, None] // MOBA_BLOCK
                    == jnp.arange(LANES, dtype=jnp.int32)[None, :]).astype(BF16)
    kmean_pad = jnp.pad(kmean, ((0, LANES - nb), (0, 0)))
    return pl.pallas_call(
        _moba_prompt_kernel,
        grid=(A_HEADS // MOBA_HEADS_PER_STEP, nb),
        in_specs=[pl.BlockSpec((MOBA_BLOCK, hd), lambda h, i: (i, h)),
                  pl.BlockSpec((t, hd), lambda h, i: (0, h)),
                  pl.BlockSpec((t, hd), lambda h, i: (0, h)),
                  pl.BlockSpec((t, LANES), lambda h, i: (0, 0)),
                  pl.BlockSpec((LANES, hd), lambda h, i: (0, h))],
        out_specs=pl.BlockSpec((MOBA_BLOCK, hd), lambda h, i: (i, h)),
        out_shape=jax.ShapeDtypeStruct((t, A_WIDTH), BF16),
        compiler_params=_params("parallel", "arbitrary"),
        name="moba_prompt_attention",
    )(q32, k16, v16, block_onehot, kmean_pad)


def _moba_sample_kernel(n_pages, pages_per_block, pt_ref, q_ref, kn_ref, vn_ref, *refs):
    del pt_ref
    kp = refs[:n_pages]
    vp = refs[n_pages:2 * n_pages]
    o_ref = refs[2 * n_pages]
    nbp = n_pages // pages_per_block
    hd = A_HEAD_DIM
    page = kp[0].shape[2] // A_HEADS
    q = q_ref[0]
    lane_i = lax.broadcasted_iota(jnp.int32, (A_HEADS, LANES), 1)

    sc = jnp.zeros((A_HEADS, LANES), F32)
    for b in range(nbp):
        ksum = jnp.zeros((A_HEADS, hd), F32)
        for pp in range(pages_per_block):
            rows = kp[b * pages_per_block + pp][0, 0]
            ksum = ksum + jnp.sum(rows.reshape(page, A_HEADS, hd), axis=0)
        kmean = ksum / float(MOBA_BLOCK)
        sc = jnp.where(lane_i == b, jnp.sum(q * kmean, axis=1, keepdims=True), sc)
    sel = _moba_select(sc, nbp, lane_i, lane_i.astype(F32)).astype(F32)

    qs = q * (hd ** -0.5)
    s_self = jnp.sum(qs * kn_ref[0], axis=1, keepdims=True)
    q_rep = [jnp.broadcast_to(qs[hh:hh + 1, :], (LANES, hd)).astype(BF16) for hh in range(A_HEADS)]
    m = [jnp.broadcast_to(s_self[hh:hh + 1, :], (1, LANES)) for hh in range(A_HEADS)]
    l = [jnp.ones((1, LANES), F32) for _ in range(A_HEADS)]
    acc = [vn_ref[0, hh:hh + 1, :] for hh in range(A_HEADS)]
    for pg in range(n_pages):
        b = pg // pages_per_block
        for hh in range(A_HEADS):
            kh = kp[pg][0, 0, pl.ds(hh, page, stride=A_HEADS), :]
            vh = vp[pg][0, 0, pl.ds(hh, page, stride=A_HEADS), :]
            s = _dot_nt(kh.astype(BF16), q_rep[hh])
            s = jnp.where(sel[hh:hh + 1, b:b + 1] > 0.5, s, NEG)
            m_n = jnp.maximum(m[hh], jnp.max(s, axis=0, keepdims=True))
            alpha = jnp.exp(m[hh] - m_n)
            p = jnp.exp(s - m_n)
            l[hh] = alpha * l[hh] + jnp.sum(p, axis=0, keepdims=True)
            acc[hh] = alpha * acc[hh] + jnp.sum(p * vh, axis=0, keepdims=True)
            m[hh] = m_n
    outs = [acc[hh] / l[hh] for hh in range(A_HEADS)]
    o_ref[0] = jnp.concatenate(outs, axis=0)


def _moba_sample(q32, kn32, vn32, cache_k, cache_v, page_table):
    ns = q32.shape[0]
    page = cache_k.shape[2]
    n_pages = page_table.shape[1]
    ppb = MOBA_BLOCK // page
    head_spec = pl.BlockSpec((1, A_HEADS, A_HEAD_DIM), lambda s, pt: (s, 0, 0))

    def page_spec(p):
        return pl.BlockSpec((1, 1, page * A_HEADS, A_HEAD_DIM), lambda s, pt: (0, pt[s, p], 0, 0))

    def rows(cache):
        return cache.reshape(cache.shape[0], cache.shape[1], page * A_HEADS, A_HEAD_DIM)

    def heads(a):
        return a.reshape(ns, A_HEADS, A_HEAD_DIM)

    grid_spec = pltpu.PrefetchScalarGridSpec(
        num_scalar_prefetch=1,
        grid=(ns,),
        in_specs=[head_spec, head_spec, head_spec] + [page_spec(p) for p in range(n_pages)] * 2,
        out_specs=head_spec,
    )
    out = pl.pallas_call(
        functools.partial(_moba_sample_kernel, n_pages, ppb),
        grid_spec=grid_spec,
        out_shape=jax.ShapeDtypeStruct((ns, A_HEADS, A_HEAD_DIM), F32),
        compiler_params=_params("parallel"),
        name="moba_sample_attention",
    )(page_table, heads(q32), heads(kn32), heads(vn32), *([rows(cache_k)] * n_pages), *([rows(cache_v)] * n_pages))
    return out.reshape(ns, A_WIDTH)


def _mix_kernel(hm_ref, om_ref, ao_ref, gm_ref, ga_ref, gmh_ref, wbm_ref, wba_ref, o_ref, hs_ref):
    @pl.when(pl.program_id(1) == 0)
    def _():
        for hh in range(M_HEADS):
            sl = slice(hh * M_V, (hh + 1) * M_V)
            y = _rms(hm_ref[:, sl], gmh_ref[:, sl])
            hs_ref[:, sl] = (y * om_ref[:, sl].astype(F32)).astype(BF16)

    t1 = _dot(hs_ref[...], wbm_ref[...])
    t2 = _dot(ao_ref[...].astype(BF16), wba_ref[...])
    o_ref[...] = (gm_ref[...].astype(F32) * t1 + ga_ref[...].astype(F32) * t2).astype(o_ref.dtype)


def _mix(hm, sg, ao, g_mh, w_bm, w_ba):
    r = hm.shape[0]
    d = w_bm.shape[1]
    tm = _pick(r, (512, 256, 128, 64, 32, 16))
    tn = _pick(d, (512, 256, 128))
    gm0 = M_WIDTH // tn
    ga0 = (M_WIDTH + d) // tn
    return pl.pallas_call(
        _mix_kernel,
        grid=(r // tm, d // tn),
        in_specs=[pl.BlockSpec((tm, M_WIDTH), lambda i, j: (i, 0)),
                  pl.BlockSpec((tm, M_WIDTH), lambda i, j: (i, 0)),
                  pl.BlockSpec((tm, A_WIDTH), lambda i, j: (i, 0)),
                  pl.BlockSpec((tm, tn), lambda i, j: (i, gm0 + j)),
                  pl.BlockSpec((tm, tn), lambda i, j: (i, ga0 + j)),
                  pl.BlockSpec((1, M_WIDTH), lambda i, j: (0, 0)),
                  pl.BlockSpec((M_WIDTH, tn), lambda i, j: (0, j)),
                  pl.BlockSpec((A_WIDTH, tn), lambda i, j: (0, j))],
        out_specs=pl.BlockSpec((tm, tn), lambda i, j: (i, j)),
        out_shape=jax.ShapeDtypeStruct((r, d), BF16),
        scratch_shapes=[pltpu.VMEM((tm, M_WIDTH), BF16)],
        compiler_params=_params("parallel", "arbitrary"),
        name="mixer_merge",
    )(hm, sg, ao, sg, sg, g_mh.reshape(1, M_WIDTH), w_bm, w_ba)


def _store_row_tiles(o_ref, x, rows):
    s = x.shape[1] // LANES
    for c in range(s):
        o_ref[pl.ds(c, rows, stride=s), :] = x[:, c * LANES:(c + 1) * LANES]


def _resid_router_kernel(x_ref, mix_ref, wout_ref, gffn_ref, wrh_ref, wrl_ref, br_ref,
                         x1_ref, x1t_ref, gate_ref, eid_ref):
    x1 = x_ref[...] + _dot(mix_ref[...], wout_ref[...])
    x1_ref[...] = x1
    _store_row_tiles(x1t_ref, x1, x1.shape[0])
    hh, hl = _split_bf16(_rms(x1, gffn_ref[...]))
    wrh = wrh_ref[...]
    logits = _dot(hh, wrh) + (_dot(hh, wrl_ref[...]) + _dot(hl, wrh)) + br_ref[...]
    lane_i = lax.broadcasted_iota(jnp.int32, logits.shape, 1)
    lane_f = lane_i.astype(F32)
    work = jnp.where(lane_i < N_EXPERTS, logits, -jnp.inf)
    vals, ids = [], []
    for _ in range(TOP_K):
        m, idx = _first_argmax(work, lane_f)
        vals.append(m)
        ids.append(idx)
        work = jnp.where(lane_f == idx, -jnp.inf, work)
    es = [jnp.exp(v - vals[0]) for v in vals]
    den = es[0]
    for e in es[1:]:
        den = den + e
    gate = jnp.zeros(logits.shape, F32)
    eid = jnp.zeros(logits.shape, F32)
    for k in range(TOP_K):
        gate = jnp.where(lane_i == k, es[k] / den, gate)
        eid = jnp.where(lane_i == k, ids[k], eid)
    gate_ref[...] = gate
    eid_ref[...] = eid.astype(jnp.int32)


def _resid_router(x, mix, w_out, g_ffn, wr_hi, wr_lo, b_r, n_total, row0, prev):
    r, d = x.shape
    s = d // LANES
    tm = _pick(r, (256, 128, 64, 32, 16))
    assert row0 % tm == 0
    b0 = row0 // tm
    out_shape = [jax.ShapeDtypeStruct((n_total, d), F32),
                 jax.ShapeDtypeStruct((n_total * s, LANES), F32),
                 jax.ShapeDtypeStruct((n_total, LANES), F32),
                 jax.ShapeDtypeStruct((n_total, LANES), jnp.int32)]
    out_specs = [pl.BlockSpec((tm, d), lambda i: (b0 + i, 0)),
                 pl.BlockSpec((tm * s, LANES), lambda i: (b0 + i, 0)),
                 pl.BlockSpec((tm, LANES), lambda i: (b0 + i, 0)),
                 pl.BlockSpec((tm, LANES), lambda i: (b0 + i, 0))]
    in_specs = [pl.BlockSpec((tm, d), lambda i: (i, 0)),
                pl.BlockSpec((tm, d), lambda i: (i, 0)),
                pl.BlockSpec((d, d), lambda i: (0, 0)),
                pl.BlockSpec((1, d), lambda i: (0, 0)),
                pl.BlockSpec((d, LANES), lambda i: (0, 0)),
                pl.BlockSpec((d, LANES), lambda i: (0, 0)),
                pl.BlockSpec((1, LANES), lambda i: (0, 0))]
    args = [x, mix, w_out, g_ffn.reshape(1, d), wr_hi, wr_lo, b_r]
    n_in = len(args)
    kern = _resid_router_kernel
    aliases = {}
    if prev is not None:
        in_specs = in_specs + [pl.BlockSpec(memory_space=pl.ANY)] * len(prev)
        args = args + list(prev)
        aliases = {n_in + k: k for k in range(len(prev))}

        def kern(*refs):
            _resid_router_kernel(*refs[:n_in], *refs[n_in + len(prev):])

    return pl.pallas_call(
        kern,
        grid=(r // tm,),
        in_specs=in_specs,
        out_specs=out_specs,
        out_shape=out_shape,
        input_output_aliases=aliases,
        compiler_params=_params("parallel"),
        name="residual_router_topk",
    )(*args)


def _gathered_rows(buf):
    return jnp.concatenate([buf[c] for c in range(buf.shape[0])], axis=1)


def _dispatch_kernel(nv_ref, tok_ref, g_ref, xt_hbm, o_ref, buf, sem):
    c = pl.program_id(0)
    s, rows, _ = buf.shape

    @pl.when(c < nv_ref[0])
    def _():
        def start(r, carry):
            pltpu.make_async_copy(xt_hbm.at[pl.ds(tok_ref[r] * s, s)], buf.at[:, r], sem).start()
            return carry

        lax.fori_loop(0, rows, start, 0, unroll=GATHER_UNROLL)
        pltpu.make_async_copy(buf, buf, sem).wait()
        o_ref[...] = _rms(_gathered_rows(buf), g_ref[...]).astype(o_ref.dtype)


def _dispatch(x1t, g_ffn, src_tok, n_valid, n_chunks):
    d = g_ffn.shape[0]
    s = d // LANES
    rows = MOE_ROWS
    grid_spec = pltpu.PrefetchScalarGridSpec(
        num_scalar_prefetch=1,
        grid=(n_chunks,),
        in_specs=[pl.BlockSpec((rows,), lambda c, nv: (jnp.minimum(c, nv[0] - 1),),
                               memory_space=pltpu.SMEM),
                  pl.BlockSpec((1, d), lambda c, nv: (0, 0)),
                  pl.BlockSpec(memory_space=pl.ANY)],
        out_specs=pl.BlockSpec((rows, d), lambda c, nv: (jnp.minimum(c, nv[0] - 1), 0)),
        scratch_shapes=[pltpu.VMEM((s, rows, LANES), F32), pltpu.SemaphoreType.DMA(())],
    )
    return pl.pallas_call(
        _dispatch_kernel,
        grid_spec=grid_spec,
        out_shape=jax.ShapeDtypeStruct((n_chunks * rows, d), BF16),
        compiler_params=_params("arbitrary"),
        name="moe_dispatch_gather",
    )(n_valid, src_tok, g_ffn.reshape(1, d), x1t)


def _expert_tile_schedule(ce_ref, nx_ref, nv_ref, c, j, n_pass):
    first = (c == 0) | (ce_ref[c] != ce_ref[jnp.maximum(c - 1, 0)])
    more_in_pass = nx_ref[c] >= 0
    has_next = more_in_pass | (j + 1 < n_pass)
    next_e = jnp.where(more_in_pass, nx_ref[c], ce_ref[0])
    next_j = jnp.where(more_in_pass, j, j + 1)
    return (c < nv_ref[0]) & first, has_next, next_j, next_e


def _moe_up_kernel(ce_ref, nx_ref, nv_ref, x_ref, bg_ref, bl_ref, w_hbm, o_ref, raw, w16, sem):
    j = pl.program_id(0)
    c = pl.program_id(1)
    tf = raw.shape[2]
    n_pass = pl.num_programs(0)
    dff = n_pass * tf

    def fetch(jj, e, k):
        col = pl.multiple_of(k * dff + jj * tf, LANES)
        return pltpu.make_async_copy(w_hbm.at[e, :, pl.ds(col, tf)], raw.at[k], sem.at[k])

    first, has_next, next_j, next_e = _expert_tile_schedule(ce_ref, nx_ref, nv_ref, c, j, n_pass)

    @pl.when(first & (j == 0) & (c == 0))
    def _():
        for k in range(2):
            fetch(0, ce_ref[0], k).start()

    @pl.when(first)
    def _():
        for k in range(2):
            fetch(j, ce_ref[c], k).wait()
            w16[k] = raw[k].astype(BF16)

    @pl.when(first & has_next)
    def _():
        for k in range(2):
            fetch(next_j, next_e, k).start()

    @pl.when(c < nv_ref[0])
    def _():
        x = x_ref[...]
        gate = jnp.minimum(_dot(x, w16[0]) + bg_ref[0], SWIGLU_LIMIT)
        lin = jnp.clip(_dot(x, w16[1]) + bl_ref[0], -SWIGLU_LIMIT, SWIGLU_LIMIT)
        o_ref[...] = (gate * jax.nn.sigmoid(SWIGLU_ALPHA * gate) * (lin + 1.0)).astype(o_ref.dtype)


def _moe_up(xs, w_up, b_up, chunk_expert, next_expert, n_valid):
    rows = MOE_ROWS
    n_chunks = xs.shape[0] // rows
    e, d, f2 = w_up.shape
    dff = f2 // 2
    tf = _pick(dff, (1024, 512, 256, 128))
    nf = dff // tf

    def last(c, nv):
        return jnp.minimum(c, nv[0] - 1)

    grid_spec = pltpu.PrefetchScalarGridSpec(
        num_scalar_prefetch=3,
        grid=(nf, n_chunks),
        in_specs=[pl.BlockSpec((rows, d), lambda j, c, ce, nx, nv: (last(c, nv), 0)),
                  pl.BlockSpec((1, 1, tf), lambda j, c, ce, nx, nv: (ce[c], 0, j)),
                  pl.BlockSpec((1, 1, tf), lambda j, c, ce, nx, nv: (ce[c], 0, nf + j)),
                  pl.BlockSpec(memory_space=pl.ANY)],
        out_specs=pl.BlockSpec((rows, tf), lambda j, c, ce, nx, nv: (last(c, nv), j)),
        scratch_shapes=[pltpu.VMEM((2, d, tf), F32), pltpu.VMEM((2, d, tf), BF16),
                        pltpu.SemaphoreType.DMA((2,))],
    )
    b3 = b_up.reshape(e, 1, f2)
    return pl.pallas_call(
        _moe_up_kernel,
        grid_spec=grid_spec,
        out_shape=jax.ShapeDtypeStruct((n_chunks * rows, dff), BF16),
        compiler_params=_params("arbitrary", "arbitrary"),
        name="moe_up_swiglu",
    )(chunk_expert, next_expert, n_valid, xs, b3, b3, w_up)


def _moe_down_kernel(ce_ref, nx_ref, nv_ref, a_ref, b_ref, w_hbm, o_ref, raw, w16, sem):
    c = pl.program_id(0)

    def fetch(e):
        return pltpu.make_async_copy(w_hbm.at[e], raw, sem)

    first, has_next, _, next_e = _expert_tile_schedule(ce_ref, nx_ref, nv_ref, c, 0, 1)

    @pl.when(first & (c == 0))
    def _():
        fetch(ce_ref[0]).start()

    @pl.when(first)
    def _():
        fetch(ce_ref[c]).wait()
        w16[...] = raw[...].astype(BF16)

    @pl.when(first & has_next)
    def _():
        fetch(next_e).start()

    @pl.when(c < nv_ref[0])
    def _():
        y = _dot(a_ref[...], w16[...]) + b_ref[0]
        _store_row_tiles(o_ref, y, y.shape[0])


def _moe_down(act, w_down, b_down, chunk_expert, next_expert, n_valid):
    rows = MOE_ROWS
    n_chunks = act.shape[0] // rows
    e, dff, d = w_down.shape
    s = d // LANES

    def last(c, nv):
        return jnp.minimum(c, nv[0] - 1)

    grid_spec = pltpu.PrefetchScalarGridSpec(
        num_scalar_prefetch=3,
        grid=(n_chunks,),
        in_specs=[pl.BlockSpec((rows, dff), lambda c, ce, nx, nv: (last(c, nv), 0)),
                  pl.BlockSpec((1, 1, d), lambda c, ce, nx, nv: (ce[c], 0, 0)),
                  pl.BlockSpec(memory_space=pl.ANY)],
        out_specs=pl.BlockSpec((rows * s, LANES), lambda c, ce, nx, nv: (last(c, nv), 0)),
        scratch_shapes=[pltpu.VMEM((dff, d), F32), pltpu.VMEM((dff, d), BF16),
                        pltpu.SemaphoreType.DMA(())],
    )
    return pl.pallas_call(
        _moe_down_kernel,
        grid_spec=grid_spec,
        out_shape=jax.ShapeDtypeStruct((n_chunks * rows * s, LANES), F32),
        compiler_params=_params("arbitrary"),
        name="moe_down",
    )(chunk_expert, next_expert, n_valid, act, b_down.reshape(e, 1, d), w_down)


def _combine_kernel(dest_ref, x1_ref, gate_ref, gple_ref, yt_hbm, x2_ref, h3_ref, buf, sem):
    _, s, rows, _ = buf.shape

    def start(r, carry):
        for k in range(TOP_K):
            pltpu.make_async_copy(yt_hbm.at[pl.ds(dest_ref[r * TOP_K + k] * s, s)], buf.at[k, :, r], sem).start()
        return carry

    lax.fori_loop(0, rows, start, 0, unroll=GATHER_UNROLL // TOP_K)
    pltpu.make_async_copy(buf, buf, sem).wait()
    gate = gate_ref[...]
    acc = x1_ref[...]
    for k in range(TOP_K):
        acc = acc + gate[:, k:k + 1] * _gathered_rows(buf[k])
    x2_ref[...] = acc
    h3_ref[...] = _rms(acc, gple_ref[...]).astype(h3_ref.dtype)


def _combine(x1, gate, dest, yt, g_ple):
    n, d = x1.shape
    s = d // LANES
    rows = _pick(n, (COMBINE_ROWS, 64, 32, 16))
    return pl.pallas_call(
        _combine_kernel,
        grid=(n // rows,),
        in_specs=[pl.BlockSpec((rows * TOP_K,), lambda i: (i,), memory_space=pltpu.SMEM),
                  pl.BlockSpec((rows, d), lambda i: (i, 0)),
                  pl.BlockSpec((rows, LANES), lambda i: (i, 0)),
                  pl.BlockSpec((1, d), lambda i: (0, 0)),
                  pl.BlockSpec(memory_space=pl.ANY)],
        out_specs=[pl.BlockSpec((rows, d), lambda i: (i, 0)),
                   pl.BlockSpec((rows, d), lambda i: (i, 0))],
        out_shape=[jax.ShapeDtypeStruct((n, d), F32), jax.ShapeDtypeStruct((n, d), BF16)],
        scratch_shapes=[pltpu.VMEM((TOP_K, s, rows, LANES), F32), pltpu.SemaphoreType.DMA(())],
        compiler_params=_params("arbitrary"),
        name="moe_combine_gather",
    )(dest, x1, gate, g_ple.reshape(1, d), yt)


def _ple_kernel(x2_ref, h3_ref, p_ref, wpg_ref, wple_ref, o_ref):
    gate = jax.nn.sigmoid(_dot(h3_ref[...], wpg_ref[...]))
    o_ref[...] = x2_ref[...] + gate * _dot(p_ref[...].astype(BF16), wple_ref[...])


def _ple(x2, h3, p, w_pg, w_ple, row0, r):
    d = x2.shape[1]
    pd = p.shape[1]
    tm = _pick(r, (512, 256, 128, 64, 32, 16))
    tn = _pick(d, (1024, 512, 256, 128))
    assert row0 % tm == 0
    b0 = row0 // tm
    return pl.pallas_call(
        _ple_kernel,
        grid=(d // tn, r // tm),
        in_specs=[pl.BlockSpec((tm, tn), lambda j, i: (b0 + i, j)),
                  pl.BlockSpec((tm, d), lambda j, i: (b0 + i, 0)),
                  pl.BlockSpec((tm, pd), lambda j, i: (i, 0)),
                  pl.BlockSpec((d, tn), lambda j, i: (0, j)),
                  pl.BlockSpec((pd, tn), lambda j, i: (0, j))],
        out_specs=pl.BlockSpec((tm, tn), lambda j, i: (i, j)),
        out_shape=jax.ShapeDtypeStruct((r, d), F32),
        compiler_params=_params("parallel", "parallel"),
        name="ple_gate",
    )(x2, h3, p, w_pg, w_ple)


def _rope_tables(pos):
    half = ROT_DIM // 2
    inv = jnp.power(ROPE_THETA, -jnp.arange(half, dtype=F32) * 2.0 / ROT_DIM)
    ang = pos.astype(F32)[:, None] * inv[None, :]
    cos, sin = jnp.cos(ang), jnp.sin(ang)
    r = pos.shape[0]
    rest = A_HEAD_DIM - ROT_DIM
    c = jnp.concatenate([cos, cos, jnp.ones((r, rest), F32)], axis=1)
    s1 = jnp.concatenate([-sin, jnp.zeros((r, half + rest), F32)], axis=1)
    s2 = jnp.concatenate([jnp.zeros((r, half), F32), sin, jnp.zeros((r, rest), F32)], axis=1)
    return c, s1, s2


def _mixer_inputs(x, pos, g_attn, w, b_i, b_f, g_q, g_k, mlstm_dtype):
    d = x.shape[1]
    kq = M_HEADS * M_QK
    o_ig = 2 * kq + M_WIDTH
    o_om = o_ig + 2 * M_HEADS
    o_qa = o_om + M_WIDTH
    o_gm = o_qa + 3 * A_WIDTH
    h = _rmsnorm_bf16(x, g_attn)
    colscale = jnp.concatenate([jnp.full((1, kq), M_QK ** -0.5, F32), jnp.ones((1, kq + M_WIDTH), F32)], axis=1)
    qkv_m = _proj_scale(h, w[:, :o_ig].astype(BF16), colscale, mlstm_dtype)
    wg = w[:, o_ig:o_om]
    wc = jnp.pad(wg, ((0, 0), (0, LANES - 2 * M_HEADS))).astype(BF16)
    bias = jnp.concatenate([b_i, b_f]).astype(F32)
    bc = jnp.pad(bias, (0, LANES - 2 * M_HEADS)).reshape(1, LANES)
    gcol, grow = _proj_gates(h, wc, wg.T.astype(BF16), bc, bias.reshape(2 * M_HEADS, 1))
    sg = _proj_sigmoid(h, jnp.concatenate([w[:, o_om:o_qa], w[:, o_gm:]], axis=1).astype(BF16))
    tabs = _rope_tables(pos)
    q32, _ = _proj_qk(h, w[:, o_qa:o_qa + A_WIDTH].astype(BF16), g_q, *tabs)
    k32, k16 = _proj_qk(h, w[:, o_qa + A_WIDTH:o_qa + 2 * A_WIDTH].astype(BF16), g_k, *tabs)
    v32, v16 = _proj_plain2(h, w[:, o_qa + 2 * A_WIDTH:o_gm].astype(BF16))
    return qkv_m, gcol, grow, sg, q32, k32, k16, v32, v16


def _moe_plan(eid, n_chunks):
    n = eid.shape[0]
    e_flat = eid[:, :TOP_K].reshape(-1)
    onehot = (e_flat[:, None] == jnp.arange(N_EXPERTS, dtype=jnp.int32)[None, :]).astype(jnp.int32)
    before = jnp.cumsum(onehot, axis=0) - onehot
    rank = jnp.sum(before * onehot, axis=1)
    counts = jnp.sum(onehot, axis=0)
    padded = (counts + MOE_ROWS - 1) // MOE_ROWS * MOE_ROWS
    p_ends = jnp.cumsum(padded)
    p_starts = p_ends - padded
    dest = (p_starts[e_flat] + rank).astype(jnp.int32)
    tok = jnp.repeat(jnp.arange(n, dtype=jnp.int32), TOP_K)
    src_tok = jnp.zeros((n_chunks * MOE_ROWS,), jnp.int32).at[dest].set(tok)
    n_valid = (p_ends[-1] // MOE_ROWS).astype(jnp.int32).reshape(1)
    chunk0 = jnp.minimum(jnp.arange(n_chunks, dtype=jnp.int32), n_valid[0] - 1) * MOE_ROWS
    chunk_expert = jnp.minimum(jnp.sum((p_ends[None, :] <= chunk0[:, None]).astype(jnp.int32), axis=1),
                               N_EXPERTS - 1)
    ids = jnp.arange(N_EXPERTS, dtype=jnp.int32)
    later = (ids[None, :] > chunk_expert[:, None]) & (counts[None, :] > 0)
    next_expert = jnp.min(jnp.where(later, ids[None, :], N_EXPERTS), axis=1)
    next_expert = jnp.where(next_expert < N_EXPERTS, next_expert, -1).astype(jnp.int32)
    return dest, src_tok, n_valid, chunk_expert, next_expert


def kernel(x_prompt, x_sample, cache_k, cache_v, state_mlstm_C, state_mlstm_n, state_mlstm_m, page_table, p_prompt, p_sample, g_attn, w_in, b_i, b_f, g_q, g_k, g_mh, w_bm, w_ba, w_out, g_ffn, w_router, b_router, w_up, b_up, w_down, b_down, g_ple, w_pg, w_ple):
    depth = w_in.shape[0]
    bsz, t, d = x_prompt.shape
    ns, dec_seq, _ = x_sample.shape
    assert depth == 1 and bsz == 1 and dec_seq == 1
    page = cache_k.shape[2]
    past = page_table.shape[1] * page
    assert t % MOBA_BLOCK == 0 and past % MOBA_BLOCK == 0 and MOBA_BLOCK % page == 0
    assert t % (M_CHUNK * MLSTM_CHUNKS_PER_STEP) == 0 and ns % MLSTM_SAMPLE_SEQS == 0
    n = t + ns

    xp = x_prompt[0]
    xs = x_sample[:, 0]
    w = w_in[0]
    fw = (g_attn[0], w, b_i[0], b_f[0], g_q[0], g_k[0])

    qkv_p, gcol_p, grow_p, sg_p, q32_p, k32_p, k16_p, v32_p, v16_p = _mixer_inputs(
        xp, jnp.arange(t, dtype=jnp.int32), *fw, BF16)
    hm_p, ct_p, n_p, m_p = _mlstm_prompt(qkv_p, gcol_p, grow_p)
    ao_p = _moba_prompt(q32_p, k16_p, v16_p, _block_kmean(k32_p))

    qkv_s, gcol_s, _, sg_s, q32_s, k32_s, _, v32_s, _ = _mixer_inputs(
        xs, jnp.full((ns,), past, jnp.int32), *fw, F32)
    hm_s, c_s, n_s, m_s = _mlstm_sample(qkv_s, gcol_s, state_mlstm_C[0].astype(F32),
                                        state_mlstm_n[0].astype(F32), state_mlstm_m[0].astype(F32))
    ao_s = _moba_sample(q32_s, k32_s, v32_s, cache_k, cache_v, page_table)

    wbm, wba, wout = w_bm[0].astype(BF16), w_ba[0].astype(BF16), w_out[0].astype(BF16)
    mix_p = _mix(hm_p, sg_p, ao_p, g_mh[0], wbm, wba)
    mix_s = _mix(hm_s, sg_s, ao_s, g_mh[0], wbm, wba)
    wr = jnp.pad(w_router[0], ((0, 0), (0, LANES - N_EXPERTS)))
    wr_hi = wr.astype(BF16)
    wr_lo = (wr - wr_hi.astype(F32)).astype(BF16)
    b_r = jnp.pad(b_router[0].astype(F32), (0, LANES - N_EXPERTS)).reshape(1, LANES)
    routed = _resid_router(xp, mix_p, wout, g_ffn[0], wr_hi, wr_lo, b_r, n, 0, None)
    x1, x1t, gate, eid = _resid_router(xs, mix_s, wout, g_ffn[0], wr_hi, wr_lo, b_r, n, t, routed)

    n_slots = n * TOP_K
    n_chunks = (n_slots + N_EXPERTS * (MOE_ROWS - 1)) // MOE_ROWS
    dest, src_tok, n_valid, chunk_expert, next_expert = _moe_plan(eid, n_chunks)
    xs_sorted = _dispatch(x1t, g_ffn[0], src_tok, n_valid, n_chunks)
    act = _moe_up(xs_sorted, w_up[0], b_up[0], chunk_expert, next_expert, n_valid)
    yt = _moe_down(act, w_down[0], b_down[0], chunk_expert, next_expert, n_valid)
    x2, h3 = _combine(x1, gate, dest, yt, g_ple[0])

    wpg, wple = w_pg[0].astype(BF16), w_ple[0].astype(BF16)
    y_p = _ple(x2, h3, p_prompt[0, 0], wpg, wple, 0, t)
    y_s = _ple(x2, h3, p_sample[0, :, 0], wpg, wple, t, ns)

    def heads(a):
        return a.reshape(a.shape[0], A_HEADS, A_HEAD_DIM)

    return (y_p[None], y_s[:, None],
            heads(k32_p)[None, None], heads(v32_p)[None, None],
            jnp.swapaxes(ct_p, 1, 2)[None, None].astype(state_mlstm_C.dtype),
            n_p[:M_HEADS][None, None].astype(state_mlstm_n.dtype),
            m_p[:M_HEADS, 0][None, None].astype(state_mlstm_m.dtype),
            heads(k32_s)[None, :, None], heads(v32_s)[None, :, None],
            c_s[None].astype(state_mlstm_C.dtype), n_s[None].astype(state_mlstm_n.dtype),
            m_s[None].astype(state_mlstm_m.dtype))
```

```python
import functools

import jax
import jax.numpy as jnp
import numpy as np
from jax import lax
from jax.experimental import pallas as pl
from jax.experimental.pallas import tpu as pltpu

F32 = jnp.float32
BF16 = jnp.bfloat16

M_HEADS = 4
M_QK = 128
M_V = 256
M_CHUNK = 64
M_WIDTH = M_HEADS * M_V
A_HEADS = 8
A_HEAD_DIM = 128
A_WIDTH = A_HEADS * A_HEAD_DIM
MOBA_BLOCK = 256
MOBA_TOPK = 3
ROT_DIM = A_HEAD_DIM // 4
ROPE_THETA = 500000.0
N_EXPERTS = 32
TOP_K = 4
SWIGLU_LIMIT = 7.0
SWIGLU_ALPHA = 1.702
EPS = 1e-6
NEG = -1e30

LANES = 128
SUBLANES = 8
VMEM_LIMIT_BYTES = 56 * 1024 * 1024

MOE_ROWS = 256
MLSTM_CHUNKS_PER_STEP = 2
MLSTM_SAMPLE_SEQS = 8
COMBINE_ROWS = 128
MOBA_HEADS_PER_STEP = 4
GATHER_UNROLL = 8


def _pick(n, candidates):
    for c in candidates:
        if n % c == 0:
            return c
    raise ValueError(f"no tile in {candidates} divides {n}")


def _params(*sem):
    return pltpu.CompilerParams(dimension_semantics=sem, vmem_limit_bytes=VMEM_LIMIT_BYTES)


def _dot(a, b):
    return jnp.dot(a, b, preferred_element_type=F32)


def _dot_nt(a, b):
    return lax.dot_general(a, b, (((1,), (1,)), ((), ())), preferred_element_type=F32)


def _dot_tn(a, b):
    return lax.dot_general(a, b, (((0,), (0,)), ((), ())), preferred_element_type=F32)


def _split_bf16(x):
    hi = x.astype(BF16)
    lo = (x - hi.astype(F32)).astype(BF16)
    return hi, lo


def _rms(x, g):
    return x * lax.rsqrt(jnp.mean(x * x, axis=-1, keepdims=True) + EPS) * g


def _log_sigmoid(x):
    return -(jnp.maximum(-x, 0.0) + jnp.log1p(jnp.exp(-jnp.abs(x))))


def _first_argmax(work, lane_f):
    m = jnp.max(work, axis=-1, keepdims=True)
    idx = jnp.min(jnp.where(work == m, lane_f, float(4 * LANES)), axis=-1, keepdims=True)
    return m, idx


def _rmsnorm_kernel(x_ref, g_ref, o_ref):
    o_ref[...] = _rms(x_ref[...], g_ref[...]).astype(o_ref.dtype)


def _rmsnorm_bf16(x, g):
    r, d = x.shape
    tm = _pick(r, (512, 256, 128, 64, 32, 16))
    return pl.pallas_call(
        _rmsnorm_kernel,
        grid=(r // tm,),
        in_specs=[pl.BlockSpec((tm, d), lambda i: (i, 0)), pl.BlockSpec((1, d), lambda i: (0, 0))],
        out_specs=pl.BlockSpec((tm, d), lambda i: (i, 0)),
        out_shape=jax.ShapeDtypeStruct((r, d), BF16),
        compiler_params=_params("parallel"),
        name="rmsnorm_rows",
    )(x, g.reshape(1, d))


def _proj_scale_kernel(h_ref, w_ref, s_ref, o_ref):
    o_ref[...] = (_dot(h_ref[...], w_ref[...]) * s_ref[...]).astype(o_ref.dtype)


def _proj_sigmoid_kernel(h_ref, w_ref, o_ref):
    o_ref[...] = jax.nn.sigmoid(_dot(h_ref[...], w_ref[...])).astype(o_ref.dtype)


def _proj_plain2_kernel(h_ref, w_ref, o32_ref, o16_ref):
    acc = _dot(h_ref[...], w_ref[...])
    o32_ref[...] = acc
    o16_ref[...] = acc.astype(BF16)


def _proj_qk_kernel(h_ref, w_ref, g_ref, c_ref, s1_ref, s2_ref, o32_ref, o16_ref):
    acc = _dot(h_ref[...], w_ref[...])
    g = g_ref[...]
    c, s1, s2 = c_ref[...], s1_ref[...], s2_ref[...]
    for hh in range(acc.shape[1] // A_HEAD_DIM):
        sl = slice(hh * A_HEAD_DIM, (hh + 1) * A_HEAD_DIM)
        y = _rms(acc[:, sl], g)
        up = pltpu.roll(y, A_HEAD_DIM - ROT_DIM // 2, 1)
        dn = pltpu.roll(y, ROT_DIM // 2, 1)
        r = y * c + up * s1 + dn * s2
        o32_ref[:, sl] = r
        o16_ref[:, sl] = r.astype(BF16)


def _proj_gates_kernel(h_ref, wc_ref, wr_ref, bc_ref, br_ref, gc_ref, gr_ref):
    h = h_ref[...]
    zc = _dot(h, wc_ref[...]) + bc_ref[...]
    zr = _dot_nt(wr_ref[...], h) + br_ref[...]
    lane = lax.broadcasted_iota(jnp.int32, zc.shape, 1)
    gc_ref[...] = jnp.where((lane >= M_HEADS) & (lane < 2 * M_HEADS), _log_sigmoid(zc), zc)
    row = lax.broadcasted_iota(jnp.int32, zr.shape, 0)
    gr_ref[...] = jnp.where(row >= M_HEADS, _log_sigmoid(zr), zr)


def _proj_tiles(r, n):
    tm = _pick(r, (1024, 512, 256, 128, 64, 32, 16))
    tn = _pick(n, (1024, 512, 256, 128))
    return tm, tn


def _proj_scale(h, w, colscale, out_dtype):
    r, d = h.shape
    n = w.shape[1]
    tm, tn = _proj_tiles(r, n)
    return pl.pallas_call(
        _proj_scale_kernel,
        grid=(n // tn, r // tm),
        in_specs=[pl.BlockSpec((tm, d), lambda j, i: (i, 0)),
                  pl.BlockSpec((d, tn), lambda j, i: (0, j)),
                  pl.BlockSpec((1, tn), lambda j, i: (0, j))],
        out_specs=pl.BlockSpec((tm, tn), lambda j, i: (i, j)),
        out_shape=jax.ShapeDtypeStruct((r, n), out_dtype),
        compiler_params=_params("parallel", "parallel"),
        name="proj_scale",
    )(h, w, colscale)


def _proj_sigmoid(h, w):
    r, d = h.shape
    n = w.shape[1]
    tm, tn = _proj_tiles(r, n)
    return pl.pallas_call(
        _proj_sigmoid_kernel,
        grid=(n // tn, r // tm),
        in_specs=[pl.BlockSpec((tm, d), lambda j, i: (i, 0)),
                  pl.BlockSpec((d, tn), lambda j, i: (0, j))],
        out_specs=pl.BlockSpec((tm, tn), lambda j, i: (i, j)),
        out_shape=jax.ShapeDtypeStruct((r, n), BF16),
        compiler_params=_params("parallel", "parallel"),
        name="proj_sigmoid",
    )(h, w)


def _proj_plain2(h, w):
    r, d = h.shape
    n = w.shape[1]
    tm, tn = _proj_tiles(r, n)
    return pl.pallas_call(
        _proj_plain2_kernel,
        grid=(n // tn, r // tm),
        in_specs=[pl.BlockSpec((tm, d), lambda j, i: (i, 0)),
                  pl.BlockSpec((d, tn), lambda j, i: (0, j))],
        out_specs=[pl.BlockSpec((tm, tn), lambda j, i: (i, j)),
                   pl.BlockSpec((tm, tn), lambda j, i: (i, j))],
        out_shape=[jax.ShapeDtypeStruct((r, n), F32), jax.ShapeDtypeStruct((r, n), BF16)],
        compiler_params=_params("parallel", "parallel"),
        name="proj_plain",
    )(h, w)


def _proj_qk(h, w, g, rope_c, rope_s1, rope_s2):
    r, d = h.shape
    n = w.shape[1]
    tm, tn = _proj_tiles(r, n)
    hd = A_HEAD_DIM
    return pl.pallas_call(
        _proj_qk_kernel,
        grid=(n // tn, r // tm),
        in_specs=[pl.BlockSpec((tm, d), lambda j, i: (i, 0)),
                  pl.BlockSpec((d, tn), lambda j, i: (0, j)),
                  pl.BlockSpec((1, hd), lambda j, i: (0, 0)),
                  pl.BlockSpec((tm, hd), lambda j, i: (i, 0)),
                  pl.BlockSpec((tm, hd), lambda j, i: (i, 0)),
                  pl.BlockSpec((tm, hd), lambda j, i: (i, 0))],
        out_specs=[pl.BlockSpec((tm, tn), lambda j, i: (i, j)),
                   pl.BlockSpec((tm, tn), lambda j, i: (i, j))],
        out_shape=[jax.ShapeDtypeStruct((r, n), F32), jax.ShapeDtypeStruct((r, n), BF16)],
        compiler_params=_params("parallel", "parallel"),
        name="proj_qk_norm_rope",
    )(h, w, g.reshape(1, hd), rope_c, rope_s1, rope_s2)


def _proj_gates(h, wc, wr, bc, br):
    r, d = h.shape
    tm = _pick(r, (1024, 512, 256, 128)) if r % LANES == 0 else r
    g2 = 2 * M_HEADS
    return pl.pallas_call(
        _proj_gates_kernel,
        grid=(r // tm,),
        in_specs=[pl.BlockSpec((tm, d), lambda i: (i, 0)),
                  pl.BlockSpec((d, LANES), lambda i: (0, 0)),
                  pl.BlockSpec((g2, d), lambda i: (0, 0)),
                  pl.BlockSpec((1, LANES), lambda i: (0, 0)),
                  pl.BlockSpec((g2, 1), lambda i: (0, 0))],
        out_specs=[pl.BlockSpec((tm, LANES), lambda i: (i, 0)),
                   pl.BlockSpec((g2, tm), lambda i: (0, i))],
        out_shape=[jax.ShapeDtypeStruct((r, LANES), F32), jax.ShapeDtypeStruct((g2, r), F32)],
        compiler_params=_params("parallel"),
        name="proj_gates",
    )(h, wc, wr, bc, br)


def _mlstm_prompt_kernel(q_ref, k_ref, v_ref, gc_ref, gr_ref, h_ref, ct_out, n_out, m_out,
                         ct_s, n_s, m_s):
    step = pl.program_id(0)

    @pl.when(step == 0)
    def _():
        ct_s[...] = jnp.zeros_like(ct_s)
        n_s[...] = jnp.zeros_like(n_s)
        m_s[...] = jnp.zeros_like(m_s)

    ln = M_CHUNK
    row = lax.broadcasted_iota(jnp.int32, (ln, ln), 0)
    col = lax.broadcasted_iota(jnp.int32, (ln, ln), 1)
    causal = col <= row
    upto = row <= col
    for cc in range(MLSTM_CHUNKS_PER_STEP):
        rs = slice(cc * ln, (cc + 1) * ln)
        for hh in range(M_HEADS):
            ig_c = gc_ref[rs, hh:hh + 1]
            lf_c = gc_ref[rs, M_HEADS + hh:M_HEADS + hh + 1]
            ig_r = gr_ref[hh:hh + 1, rs]
            lf_r = gr_ref[M_HEADS + hh:M_HEADS + hh + 1, rs]
            bcum_c = jnp.sum(jnp.where(causal, lf_r, 0.0), axis=1, keepdims=True)
            bcum_r = jnp.sum(jnp.where(upto, lf_c, 0.0), axis=0, keepdims=True)
            dmat = jnp.where(causal, bcum_c - bcum_r + ig_r, NEG)
            m_prev = m_s[hh:hh + 1, 0:1]
            a = bcum_c + m_prev
            m_t = jnp.maximum(a, jnp.max(dmat, axis=1, keepdims=True))
            w_inter = jnp.exp(a - m_t)
            decay = jnp.exp(dmat - m_t)
            qh = q_ref[rs, hh * M_QK:(hh + 1) * M_QK]
            kh = k_ref[rs, hh * M_QK:(hh + 1) * M_QK]
            vh = v_ref[rs, hh * M_V:(hh + 1) * M_V]
            s = _dot_nt(qh, kh) * decay
            ct = ct_s[hh]
            n_row = n_s[hh:hh + 1, :]
            num = w_inter * _dot(qh, ct.astype(BF16)) + _dot(s.astype(BF16), vh)
            den = (w_inter * jnp.sum(qh.astype(F32) * n_row, axis=1, keepdims=True)
                   + jnp.sum(s, axis=1, keepdims=True))
            h_ref[rs, hh * M_V:(hh + 1) * M_V] = num / jnp.maximum(jnp.abs(den), jnp.exp(-m_t))
            b_last = bcum_c[ln - 1:ln, :]
            d_last_r = b_last - bcum_r + ig_r
            d_last_c = b_last - bcum_c + ig_c
            a_last = b_last + m_prev
            m_new = jnp.maximum(a_last, jnp.max(d_last_r, axis=1, keepdims=True))
            w_c = jnp.exp(a_last - m_new)
            w_j = jnp.exp(d_last_c - m_new)
            vw = (vh.astype(F32) * w_j).astype(BF16)
            ct_s[hh] = w_c * ct + _dot_tn(kh, vw)
            n_s[hh:hh + 1, :] = w_c * n_row + jnp.sum(kh.astype(F32) * w_j, axis=0, keepdims=True)
            m_s[hh:hh + 1, :] = jnp.broadcast_to(m_new, (1, LANES))

    @pl.when(step == pl.num_programs(0) - 1)
    def _():
        ct_out[...] = ct_s[...]
        n_out[...] = n_s[...]
        m_out[...] = m_s[...]


def _mlstm_prompt(qkv, gcol, grow):
    t = qkv.shape[0]
    rows = M_CHUNK * MLSTM_CHUNKS_PER_STEP
    kq = M_HEADS * M_QK
    return pl.pallas_call(
        _mlstm_prompt_kernel,
        grid=(t // rows,),
        in_specs=[pl.BlockSpec((rows, kq), lambda c: (c, 0)),
                  pl.BlockSpec((rows, kq), lambda c: (c, 1)),
                  pl.BlockSpec((rows, M_WIDTH), lambda c: (c, (2 * kq) // M_WIDTH)),
                  pl.BlockSpec((rows, LANES), lambda c: (c, 0)),
                  pl.BlockSpec((2 * M_HEADS, rows), lambda c: (0, c))],
        out_specs=[pl.BlockSpec((rows, M_WIDTH), lambda c: (c, 0)),
                   pl.BlockSpec((M_HEADS, M_QK, M_V), lambda c: (0, 0, 0)),
                   pl.BlockSpec((SUBLANES, LANES), lambda c: (0, 0)),
                   pl.BlockSpec((SUBLANES, LANES), lambda c: (0, 0))],
        out_shape=[jax.ShapeDtypeStruct((t, M_WIDTH), F32),
                   jax.ShapeDtypeStruct((M_HEADS, M_QK, M_V), F32),
                   jax.ShapeDtypeStruct((SUBLANES, LANES), F32),
                   jax.ShapeDtypeStruct((SUBLANES, LANES), F32)],
        scratch_shapes=[pltpu.VMEM((M_HEADS, M_QK, M_V), F32),
                        pltpu.VMEM((SUBLANES, LANES), F32),
                        pltpu.VMEM((SUBLANES, LANES), F32)],
        compiler_params=_params("arbitrary"),
        name="mlstm_prompt_scan",
    )(qkv, qkv, qkv, gcol, grow)


def _mlstm_sample_kernel(qkv_ref, gc_ref, c_ref, n_ref, m_ref, h_ref, co_ref, no_ref, mo_ref):
    kq = M_HEADS * M_QK
    row8 = lax.broadcasted_iota(jnp.int32, (SUBLANES, 1), 0)
    mrow = lax.broadcasted_iota(jnp.int32, mo_ref.shape, 0)
    mcol = lax.broadcasted_iota(jnp.int32, mo_ref.shape, 1)
    m_all = m_ref[...]
    for s in range(MLSTM_SAMPLE_SEQS):
        for hh in range(M_HEADS):
            q = qkv_ref[s:s + 1, hh * M_QK:(hh + 1) * M_QK]
            k = qkv_ref[s:s + 1, kq + hh * M_QK:kq + (hh + 1) * M_QK]
            v = qkv_ref[s:s + 1, 2 * kq + hh * M_V:2 * kq + (hh + 1) * M_V]
            ig = gc_ref[s:s + 1, hh:hh + 1]
            lf = gc_ref[s:s + 1, M_HEADS + hh:M_HEADS + hh + 1]
            m_prev = m_ref[s:s + 1, hh:hh + 1]
            c = c_ref[s, hh]
            n_row = n_ref[s, hh:hh + 1, :]
            a = lf + m_prev
            m_t = jnp.maximum(a, ig)
            w_c = jnp.exp(a - m_t)
            w_j = jnp.exp(ig - m_t)
            sc = jnp.sum(q * k, axis=1, keepdims=True) * w_j
            q8 = jnp.broadcast_to(q, (SUBLANES, M_QK)).astype(BF16)
            cq = _dot_nt(q8, c.astype(BF16))[0:1, :]
            num = w_c * cq + sc * v
            den = w_c * jnp.sum(n_row * q, axis=1, keepdims=True) + sc
            h_ref[s:s + 1, hh * M_V:(hh + 1) * M_V] = num / jnp.maximum(jnp.abs(den), jnp.exp(-m_t))
            vw8 = jnp.where(row8 == 0, v * w_j, 0.0).astype(BF16)
            k8 = jnp.where(row8 == 0, k, 0.0).astype(BF16)
            co_ref[s, hh] = w_c * c + _dot_tn(vw8, k8)
            no_ref[s, hh:hh + 1, :] = w_c * n_row + w_j * k
            m_all = jnp.where((mrow == s) & (mcol == hh), m_t, m_all)
    mo_ref[...] = m_all


def _mlstm_sample(qkv, gcol, c0, n0, m0):
    ns = qkv.shape[0]
    sb = MLSTM_SAMPLE_SEQS
    wq = qkv.shape[1]
    return pl.pallas_call(
        _mlstm_sample_kernel,
        grid=(ns // sb,),
        in_specs=[pl.BlockSpec((sb, wq), lambda i: (i, 0)),
                  pl.BlockSpec((sb, LANES), lambda i: (i, 0)),
                  pl.BlockSpec((sb, M_HEADS, M_V, M_QK), lambda i: (i, 0, 0, 0)),
                  pl.BlockSpec((sb, M_HEADS, M_QK), lambda i: (i, 0, 0)),
                  pl.BlockSpec((sb, M_HEADS), lambda i: (i, 0))],
        out_specs=[pl.BlockSpec((sb, M_WIDTH), lambda i: (i, 0)),
                   pl.BlockSpec((sb, M_HEADS, M_V, M_QK), lambda i: (i, 0, 0, 0)),
                   pl.BlockSpec((sb, M_HEADS, M_QK), lambda i: (i, 0, 0)),
                   pl.BlockSpec((sb, M_HEADS), lambda i: (i, 0))],
        out_shape=[jax.ShapeDtypeStruct((ns, M_WIDTH), F32),
                   jax.ShapeDtypeStruct(c0.shape, F32),
                   jax.ShapeDtypeStruct(n0.shape, F32),
                   jax.ShapeDtypeStruct(m0.shape, F32)],
        compiler_params=_params("parallel"),
        name="mlstm_sample_step",
    )(qkv, gcol, c0, n0, m0)


def _kmean_kernel(k_ref, o_ref):
    o_ref[0] = jnp.mean(k_ref[...], axis=0, keepdims=True)


def _block_kmean(k32):
    t, w = k32.shape
    nb = t // MOBA_BLOCK
    out = pl.pallas_call(
        _kmean_kernel,
        grid=(nb,),
        in_specs=[pl.BlockSpec((MOBA_BLOCK, w), lambda b: (b, 0))],
        out_specs=pl.BlockSpec((1, 1, w), lambda b: (b, 0, 0)),
        out_shape=jax.ShapeDtypeStruct((nb, 1, w), F32),
        compiler_params=_params("parallel"),
        name="moba_block_kmean",
    )(k32)
    return out.reshape(nb, w)


def _moba_select(sc, own, lane_i, lane_f):
    work = jnp.where(lane_i < own, sc, NEG)
    sel = jnp.zeros(sc.shape, dtype=jnp.bool_)
    for _ in range(MOBA_TOPK):
        _, idx = _first_argmax(work, lane_f)
        hit = lane_f == idx
        sel = sel | hit
        work = jnp.where(hit, -jnp.inf, work)
    return sel & (lane_i < own)


def _moba_prompt_kernel(q_ref, k_ref, v_ref, e_ref, km_ref, o_ref):
    i = pl.program_id(1)
    bs = MOBA_BLOCK
    pw = 2 * bs
    hd = A_HEAD_DIM
    heads = range(MOBA_HEADS_PER_STEP)
    ones = jnp.ones((pw, hd), BF16)
    lane_i = lax.broadcasted_iota(jnp.int32, (bs, LANES), 1)
    lane_f = lane_i.astype(F32)

    def head_slice(hh):
        return slice(hh * hd, (hh + 1) * hd)

    q_aug = []
    for hh in heads:
        q = q_ref[:, head_slice(hh)]
        qh, ql = _split_bf16(q)
        kmh, kml = _split_bf16(km_ref[:, head_slice(hh)])
        sc = _dot_nt(qh, kmh) + (_dot_nt(qh, kml) + _dot_nt(ql, kmh))
        sel = _moba_select(sc, i, lane_i, lane_f)
        bias = jnp.where(sel | (lane_i >= i), 0.0, NEG)
        q_aug.append(jnp.concatenate([(q * (hd ** -0.5)).astype(BF16), bias.astype(BF16)], axis=1))

    def pair_scores(p, hh, onehot):
        off = pl.multiple_of(p * pw, pw)
        k_aug = jnp.concatenate([k_ref[pl.ds(off, pw), head_slice(hh)], onehot], axis=1)
        v_aug = jnp.concatenate([v_ref[pl.ds(off, pw), head_slice(hh)], ones], axis=1)
        return _dot_nt(q_aug[hh], k_aug), v_aug

    p_own = i // 2
    row = lax.broadcasted_iota(jnp.int32, (bs, pw), 0)
    col = lax.broadcasted_iota(jnp.int32, (bs, pw), 1)
    causal = col + (p_own * pw - i * bs) <= row
    onehot_own = e_ref[pl.ds(pl.multiple_of(p_own * pw, pw), pw), :]
    init = []
    for hh in heads:
        s, v_aug = pair_scores(p_own, hh, onehot_own)
        s = jnp.where(causal, s, NEG)
        m0 = jnp.max(s, axis=1, keepdims=True)
        init += [m0, _dot(jnp.exp(s - m0).astype(BF16), v_aug)]

    def body(p, carry):
        onehot = e_ref[pl.ds(pl.multiple_of(p * pw, pw), pw), :]
        out = []
        for hh in heads:
            m_i, acc = carry[2 * hh], carry[2 * hh + 1]
            sj, vj = pair_scores(p, hh, onehot)
            m_n = jnp.maximum(m_i, jnp.max(sj, axis=1, keepdims=True))
            out += [m_n, jnp.exp(m_i - m_n) * acc + _dot(jnp.exp(sj - m_n).astype(BF16), vj)]
        return tuple(out)

    final = lax.fori_loop(0, p_own, body, tuple(init))
    for hh in heads:
        acc = final[2 * hh + 1]
        o_ref[:, head_slice(hh)] = (acc[:, :hd] / acc[:, hd:]).astype(o_ref.dtype)


def _moba_prompt(q32, k16, v16, kmean):
    t = q32.shape[0]
    nb = t // MOBA_BLOCK
    hd = MOBA_HEADS_PER_STEP * A_HEAD_DIM
    assert nb <= LANES and nb % 2 == 0 and A_HEADS % MOBA_HEADS_PER_STEP == 0
    block_onehot = (jnp.arange(t, dtype=jnp.int32)[:, None] // MOBA_BLOCK
                    == jnp.arange(LANES, dtype=jnp.int32)[None, :]).astype(BF16)
    kmean_pad = jnp.pad(kmean, ((0, LANES - nb), (0, 0)))
    return pl.pallas_call(
        _moba_prompt_kernel,
        grid=(A_HEADS // MOBA_HEADS_PER_STEP, nb),
        in_specs=[pl.BlockSpec((MOBA_BLOCK, hd), lambda h, i: (i, h)),
                  pl.BlockSpec((t, hd), lambda h, i: (0, h)),
                  pl.BlockSpec((t, hd), lambda h, i: (0, h)),
                  pl.BlockSpec((t, LANES), lambda h, i: (0, 0)),
                  pl.BlockSpec((LANES, hd), lambda h, i: (0, h))],
        out_specs=pl.BlockSpec((MOBA_BLOCK, hd), lambda h, i: (i, h)),
        out_shape=jax.ShapeDtypeStruct((t, A_WIDTH), BF16),
        compiler_params=_params("parallel", "arbitrary"),
        name="moba_prompt_attention",
    )(q32, k16, v16, block_onehot, kmean_pad)


def _moba_sample_kernel(n_pages, pages_per_block, pt_ref, q_ref, kn_ref, vn_ref, *refs):
    del pt_ref
    kp = refs[:n_pages]
    vp = refs[n_pages:2 * n_pages]
    o_ref = refs[2 * n_pages]
    nbp = n_pages // pages_per_block
    hd = A_HEAD_DIM
    rows = kp[0].shape[2]
    page = rows // A_HEADS
    q = q_ref[0]
    lane_i = lax.broadcasted_iota(jnp.int32, (A_HEADS, LANES), 1)

    sc = jnp.zeros((A_HEADS, LANES), F32)
    k16 = []
    for b in range(nbp):
        ksum = jnp.zeros((A_HEADS, hd), F32)
        for pp in range(pages_per_block):
            kf = kp[b * pages_per_block + pp][0, 0]
            ksum = ksum + jnp.sum(kf.reshape(page, A_HEADS, hd), axis=0)
            k16.append(kf.astype(BF16))
        kmean = ksum / float(MOBA_BLOCK)
        sc = jnp.where(lane_i == b, jnp.sum(q * kmean, axis=1, keepdims=True), sc)
    sel = _moba_select(sc, nbp, lane_i, lane_i.astype(F32)).astype(F32)

    qs = q * (hd ** -0.5)
    own = jnp.sum(qs * kn_ref[0], axis=1, keepdims=True)
    s = _dot_nt(qs.astype(BF16), jnp.concatenate(k16, axis=0))
    head_row = lax.broadcasted_iota(jnp.int32, (A_HEADS, rows), 0)
    row_head = lax.broadcasted_iota(jnp.int32, (A_HEADS, rows), 1) % A_HEADS
    mine = head_row == row_head
    s = jnp.concatenate(
        [jnp.where(mine & (sel[:, pg // pages_per_block:pg // pages_per_block + 1] > 0.5),
                   s[:, pg * rows:(pg + 1) * rows], NEG) for pg in range(n_pages)], axis=1)
    m = jnp.maximum(own, jnp.max(s, axis=1, keepdims=True))
    p = jnp.exp(s - m)
    p_own = jnp.exp(own - m)
    l = jnp.sum(p, axis=1, keepdims=True) + p_own
    v16 = jnp.concatenate([vp[pg][0, 0].astype(BF16) for pg in range(n_pages)], axis=0)
    o_ref[0] = (_dot(p.astype(BF16), v16) + p_own * vn_ref[0]) / l


def _moba_sample(q32, kn32, vn32, cache_k, cache_v, page_table):
    ns = q32.shape[0]
    page = cache_k.shape[2]
    n_pages = page_table.shape[1]
    ppb = MOBA_BLOCK // page
    head_spec = pl.BlockSpec((1, A_HEADS, A_HEAD_DIM), lambda s, pt: (s, 0, 0))

    def page_spec(p):
        return pl.BlockSpec((1, 1, page * A_HEADS, A_HEAD_DIM), lambda s, pt: (0, pt[s, p], 0, 0))

    def rows(cache):
        return cache.reshape(cache.shape[0], cache.shape[1], page * A_HEADS, A_HEAD_DIM)

    def heads(a):
        return a.reshape(ns, A_HEADS, A_HEAD_DIM)

    grid_spec = pltpu.PrefetchScalarGridSpec(
        num_scalar_prefetch=1,
        grid=(ns,),
        in_specs=[head_spec, head_spec, head_spec] + [page_spec(p) for p in range(n_pages)] * 2,
        out_specs=head_spec,
    )
    out = pl.pallas_call(
        functools.partial(_moba_sample_kernel, n_pages, ppb),
        grid_spec=grid_spec,
        out_shape=jax.ShapeDtypeStruct((ns, A_HEADS, A_HEAD_DIM), F32),
        compiler_params=_params("parallel"),
        name="moba_sample_attention",
    )(page_table, heads(q32), heads(kn32), heads(vn32), *([rows(cache_k)] * n_pages), *([rows(cache_v)] * n_pages))
    return out.reshape(ns, A_WIDTH)


def _mix_kernel(hm_ref, om_ref, ao_ref, gm_ref, ga_ref, gmh_ref, wbm_ref, wba_ref, o_ref, hs_ref):
    @pl.when(pl.program_id(1) == 0)
    def _():
        for hh in range(M_HEADS):
            sl = slice(hh * M_V, (hh + 1) * M_V)
            y = _rms(hm_ref[:, sl], gmh_ref[:, sl])
            hs_ref[:, sl] = (y * om_ref[:, sl].astype(F32)).astype(BF16)

    t1 = _dot(hs_ref[...], wbm_ref[...])
    t2 = _dot(ao_ref[...].astype(BF16), wba_ref[...])
    o_ref[...] = (gm_ref[...].astype(F32) * t1 + ga_ref[...].astype(F32) * t2).astype(o_ref.dtype)


def _mix(hm, sg, ao, g_mh, w_bm, w_ba):
    r = hm.shape[0]
    d = w_bm.shape[1]
    tm = _pick(r, (512, 256, 128, 64, 32, 16))
    tn = _pick(d, (512, 256, 128))
    gm0 = M_WIDTH // tn
    ga0 = (M_WIDTH + d) // tn
    return pl.pallas_call(
        _mix_kernel,
        grid=(r // tm, d // tn),
        in_specs=[pl.BlockSpec((tm, M_WIDTH), lambda i, j: (i, 0)),
                  pl.BlockSpec((tm, M_WIDTH), lambda i, j: (i, 0)),
                  pl.BlockSpec((tm, A_WIDTH), lambda i, j: (i, 0)),
                  pl.BlockSpec((tm, tn), lambda i, j: (i, gm0 + j)),
                  pl.BlockSpec((tm, tn), lambda i, j: (i, ga0 + j)),
                  pl.BlockSpec((1, M_WIDTH), lambda i, j: (0, 0)),
                  pl.BlockSpec((M_WIDTH, tn), lambda i, j: (0, j)),
                  pl.BlockSpec((A_WIDTH, tn), lambda i, j: (0, j))],
        out_specs=pl.BlockSpec((tm, tn), lambda i, j: (i, j)),
        out_shape=jax.ShapeDtypeStruct((r, d), BF16),
        scratch_shapes=[pltpu.VMEM((tm, M_WIDTH), BF16)],
        compiler_params=_params("parallel", "arbitrary"),
        name="mixer_merge",
    )(hm, sg, ao, sg, sg, g_mh.reshape(1, M_WIDTH), w_bm, w_ba)


def _store_row_tiles(o_ref, x, rows):
    s = x.shape[1] // LANES
    for c in range(s):
        o_ref[pl.ds(c, rows, stride=s), :] = x[:, c * LANES:(c + 1) * LANES]


def _resid_router_kernel(x_ref, mix_ref, wout_ref, gffn_ref, wrh_ref, wrl_ref, br_ref,
                         x1_ref, x1t_ref, gate_ref, eid_ref):
    x1 = x_ref[...] + _dot(mix_ref[...], wout_ref[...])
    x1_ref[...] = x1
    _store_row_tiles(x1t_ref, x1, x1.shape[0])
    hh, hl = _split_bf16(_rms(x1, gffn_ref[...]))
    wrh = wrh_ref[...]
    logits = _dot(hh, wrh) + (_dot(hh, wrl_ref[...]) + _dot(hl, wrh)) + br_ref[...]
    lane_i = lax.broadcasted_iota(jnp.int32, logits.shape, 1)
    lane_f = lane_i.astype(F32)
    work = jnp.where(lane_i < N_EXPERTS, logits, -jnp.inf)
    vals, ids = [], []
    for _ in range(TOP_K):
        m, idx = _first_argmax(work, lane_f)
        vals.append(m)
        ids.append(idx)
        work = jnp.where(lane_f == idx, -jnp.inf, work)
    es = [jnp.exp(v - vals[0]) for v in vals]
    den = es[0]
    for e in es[1:]:
        den = den + e
    gate = jnp.zeros(logits.shape, F32)
    eid = jnp.zeros(logits.shape, F32)
    for k in range(TOP_K):
        gate = jnp.where(lane_i == k, es[k] / den, gate)
        eid = jnp.where(lane_i == k, ids[k], eid)
    gate_ref[...] = gate
    eid_ref[...] = eid.astype(jnp.int32)


def _resid_router(x, mix, w_out, g_ffn, wr_hi, wr_lo, b_r, n_total, row0, prev):
    r, d = x.shape
    s = d // LANES
    tm = _pick(r, (256, 128, 64, 32, 16))
    assert row0 % tm == 0
    b0 = row0 // tm
    out_shape = [jax.ShapeDtypeStruct((n_total, d), F32),
                 jax.ShapeDtypeStruct((n_total * s, LANES), F32),
                 jax.ShapeDtypeStruct((n_total, LANES), F32),
                 jax.ShapeDtypeStruct((n_total, LANES), jnp.int32)]
    out_specs = [pl.BlockSpec((tm, d), lambda i: (b0 + i, 0)),
                 pl.BlockSpec((tm * s, LANES), lambda i: (b0 + i, 0)),
                 pl.BlockSpec((tm, LANES), lambda i: (b0 + i, 0)),
                 pl.BlockSpec((tm, LANES), lambda i: (b0 + i, 0))]
    in_specs = [pl.BlockSpec((tm, d), lambda i: (i, 0)),
                pl.BlockSpec((tm, d), lambda i: (i, 0)),
                pl.BlockSpec((d, d), lambda i: (0, 0)),
                pl.BlockSpec((1, d), lambda i: (0, 0)),
                pl.BlockSpec((d, LANES), lambda i: (0, 0)),
                pl.BlockSpec((d, LANES), lambda i: (0, 0)),
                pl.BlockSpec((1, LANES), lambda i: (0, 0))]
    args = [x, mix, w_out, g_ffn.reshape(1, d), wr_hi, wr_lo, b_r]
    n_in = len(args)
    kern = _resid_router_kernel
    aliases = {}
    if prev is not None:
        in_specs = in_specs + [pl.BlockSpec(memory_space=pl.ANY)] * len(prev)
        args = args + list(prev)
        aliases = {n_in + k: k for k in range(len(prev))}

        def kern(*refs):
            _resid_router_kernel(*refs[:n_in], *refs[n_in + len(prev):])

    return pl.pallas_call(
        kern,
        grid=(r // tm,),
        in_specs=in_specs,
        out_specs=out_specs,
        out_shape=out_shape,
        input_output_aliases=aliases,
        compiler_params=_params("parallel"),
        name="residual_router_topk",
    )(*args)


def _row_pitch(s):
    return s + 1 if s % 2 == 0 else s


def _gathered_rows(buf, rows, s):
    return jnp.concatenate([buf[pl.ds(c, rows, stride=_row_pitch(s)), :] for c in range(s)], axis=1)


def _dispatch_kernel(nv_ref, tok_ref, g_ref, xt_hbm, o_ref, buf, sem):
    c = pl.program_id(0)
    rows = o_ref.shape[0]
    s = o_ref.shape[1] // LANES
    pitch = _row_pitch(s)

    @pl.when(c < nv_ref[0])
    def _():
        def start(r, carry):
            pltpu.make_async_copy(xt_hbm.at[pl.ds(tok_ref[r] * s, s)], buf.at[pl.ds(r * pitch, s)], sem).start()
            return carry

        lax.fori_loop(0, rows, start, 0, unroll=GATHER_UNROLL)
        landed = buf.at[pl.ds(0, rows * s)]
        pltpu.make_async_copy(landed, landed, sem).wait()
        o_ref[...] = _rms(_gathered_rows(buf, rows, s), g_ref[...]).astype(o_ref.dtype)


def _dispatch(x1t, g_ffn, src_tok, n_valid, n_chunks):
    d = g_ffn.shape[0]
    s = d // LANES
    rows = MOE_ROWS
    grid_spec = pltpu.PrefetchScalarGridSpec(
        num_scalar_prefetch=1,
        grid=(n_chunks,),
        in_specs=[pl.BlockSpec((rows,), lambda c, nv: (jnp.minimum(c, nv[0] - 1),),
                               memory_space=pltpu.SMEM),
                  pl.BlockSpec((1, d), lambda c, nv: (0, 0)),
                  pl.BlockSpec(memory_space=pl.ANY)],
        out_specs=pl.BlockSpec((rows, d), lambda c, nv: (jnp.minimum(c, nv[0] - 1), 0)),
        scratch_shapes=[pltpu.VMEM((rows * _row_pitch(s), LANES), F32), pltpu.SemaphoreType.DMA(())],
    )
    return pl.pallas_call(
        _dispatch_kernel,
        grid_spec=grid_spec,
        out_shape=jax.ShapeDtypeStruct((n_chunks * rows, d), BF16),
        compiler_params=_params("arbitrary"),
        name="moe_dispatch_gather",
    )(n_valid, src_tok, g_ffn.reshape(1, d), x1t)


def _expert_tile_schedule(ce_ref, nx_ref, nv_ref, c, j, n_pass):
    first = (c == 0) | (ce_ref[c] != ce_ref[jnp.maximum(c - 1, 0)])
    more_in_pass = nx_ref[c] >= 0
    has_next = more_in_pass | (j + 1 < n_pass)
    next_e = jnp.where(more_in_pass, nx_ref[c], ce_ref[0])
    next_j = jnp.where(more_in_pass, j, j + 1)
    return (c < nv_ref[0]) & first, has_next, next_j, next_e


def _moe_up_kernel(ce_ref, nx_ref, nv_ref, x_ref, bg_ref, bl_ref, w_hbm, o_ref, raw, w16, sem):
    j = pl.program_id(0)
    c = pl.program_id(1)
    tf = raw.shape[2]
    n_pass = pl.num_programs(0)
    dff = n_pass * tf

    def fetch(jj, e, k):
        col = pl.multiple_of(k * dff + jj * tf, LANES)
        return pltpu.make_async_copy(w_hbm.at[e, :, pl.ds(col, tf)], raw.at[k], sem.at[k])

    first, has_next, next_j, next_e = _expert_tile_schedule(ce_ref, nx_ref, nv_ref, c, j, n_pass)

    @pl.when(first & (j == 0) & (c == 0))
    def _():
        for k in range(2):
            fetch(0, ce_ref[0], k).start()

    @pl.when(first)
    def _():
        for k in range(2):
            fetch(j, ce_ref[c], k).wait()
            w16[k] = raw[k].astype(BF16)

    @pl.when(first & has_next)
    def _():
        for k in range(2):
            fetch(next_j, next_e, k).start()

    @pl.when(c < nv_ref[0])
    def _():
        x = x_ref[...]
        gate = jnp.minimum(_dot(x, w16[0]) + bg_ref[0], SWIGLU_LIMIT)
        lin = jnp.clip(_dot(x, w16[1]) + bl_ref[0], -SWIGLU_LIMIT, SWIGLU_LIMIT)
        o_ref[...] = (gate * jax.nn.sigmoid(SWIGLU_ALPHA * gate) * (lin + 1.0)).astype(o_ref.dtype)


def _moe_up(xs, w_up, b_up, chunk_expert, next_expert, n_valid):
    rows = MOE_ROWS
    n_chunks = xs.shape[0] // rows
    e, d, f2 = w_up.shape
    dff = f2 // 2
    tf = _pick(dff, (1024, 512, 256, 128))
    nf = dff // tf

    def last(c, nv):
        return jnp.minimum(c, nv[0] - 1)

    grid_spec = pltpu.PrefetchScalarGridSpec(
        num_scalar_prefetch=3,
        grid=(nf, n_chunks),
        in_specs=[pl.BlockSpec((rows, d), lambda j, c, ce, nx, nv: (last(c, nv), 0)),
                  pl.BlockSpec((1, 1, tf), lambda j, c, ce, nx, nv: (ce[c], 0, j)),
                  pl.BlockSpec((1, 1, tf), lambda j, c, ce, nx, nv: (ce[c], 0, nf + j)),
                  pl.BlockSpec(memory_space=pl.ANY)],
        out_specs=pl.BlockSpec((rows, tf), lambda j, c, ce, nx, nv: (last(c, nv), j)),
        scratch_shapes=[pltpu.VMEM((2, d, tf), F32), pltpu.VMEM((2, d, tf), BF16),
                        pltpu.SemaphoreType.DMA((2,))],
    )
    b3 = b_up.reshape(e, 1, f2)
    return pl.pallas_call(
        _moe_up_kernel,
        grid_spec=grid_spec,
        out_shape=jax.ShapeDtypeStruct((n_chunks * rows, dff), BF16),
        compiler_params=_params("arbitrary", "arbitrary"),
        name="moe_up_swiglu",
    )(chunk_expert, next_expert, n_valid, xs, b3, b3, w_up)


def _moe_down_kernel(ce_ref, nx_ref, nv_ref, a_ref, b_ref, w_hbm, o_ref, raw, w16, sem):
    c = pl.program_id(0)

    def fetch(e):
        return pltpu.make_async_copy(w_hbm.at[e], raw, sem)

    first, has_next, _, next_e = _expert_tile_schedule(ce_ref, nx_ref, nv_ref, c, 0, 1)

    @pl.when(first & (c == 0))
    def _():
        fetch(ce_ref[0]).start()

    @pl.when(first)
    def _():
        fetch(ce_ref[c]).wait()
        w16[...] = raw[...].astype(BF16)

    @pl.when(first & has_next)
    def _():
        fetch(next_e).start()

    @pl.when(c < nv_ref[0])
    def _():
        y = _dot(a_ref[...], w16[...]) + b_ref[0]
        _store_row_tiles(o_ref, y, y.shape[0])


def _moe_down(act, w_down, b_down, chunk_expert, next_expert, n_valid):
    rows = MOE_ROWS
    n_chunks = act.shape[0] // rows
    e, dff, d = w_down.shape
    s = d // LANES

    def last(c, nv):
        return jnp.minimum(c, nv[0] - 1)

    grid_spec = pltpu.PrefetchScalarGridSpec(
        num_scalar_prefetch=3,
        grid=(n_chunks,),
        in_specs=[pl.BlockSpec((rows, dff), lambda c, ce, nx, nv: (last(c, nv), 0)),
                  pl.BlockSpec((1, 1, d), lambda c, ce, nx, nv: (ce[c], 0, 0)),
                  pl.BlockSpec(memory_space=pl.ANY)],
        out_specs=pl.BlockSpec((rows * s, LANES), lambda c, ce, nx, nv: (last(c, nv), 0)),
        scratch_shapes=[pltpu.VMEM((dff, d), F32), pltpu.VMEM((dff, d), BF16),
                        pltpu.SemaphoreType.DMA(())],
    )
    return pl.pallas_call(
        _moe_down_kernel,
        grid_spec=grid_spec,
        out_shape=jax.ShapeDtypeStruct((n_chunks * rows * s, LANES), F32),
        compiler_params=_params("arbitrary"),
        name="moe_down",
    )(chunk_expert, next_expert, n_valid, act, b_down.reshape(e, 1, d), w_down)


def _combine_kernel(dest_ref, x1_ref, gate_ref, gple_ref, yt_hbm, x2_ref, h3_ref, buf, sem):
    rows = x1_ref.shape[0]
    s = x1_ref.shape[1] // LANES
    pitch = _row_pitch(s)

    def start(r, carry):
        for k in range(TOP_K):
            pltpu.make_async_copy(yt_hbm.at[pl.ds(dest_ref[r * TOP_K + k] * s, s)],
                                  buf.at[k, pl.ds(r * pitch, s)], sem).start()
        return carry

    lax.fori_loop(0, rows, start, 0, unroll=GATHER_UNROLL // TOP_K)
    landed = buf.at[:, pl.ds(0, rows * s)]
    pltpu.make_async_copy(landed, landed, sem).wait()
    gate = gate_ref[...]
    acc = x1_ref[...]
    for k in range(TOP_K):
        acc = acc + gate[:, k:k + 1] * _gathered_rows(buf.at[k], rows, s)
    x2_ref[...] = acc
    h3_ref[...] = _rms(acc, gple_ref[...]).astype(h3_ref.dtype)


def _combine(x1, gate, dest, yt, g_ple):
    n, d = x1.shape
    s = d // LANES
    rows = _pick(n, (COMBINE_ROWS, 64, 32, 16))
    return pl.pallas_call(
        _combine_kernel,
        grid=(n // rows,),
        in_specs=[pl.BlockSpec((rows * TOP_K,), lambda i: (i,), memory_space=pltpu.SMEM),
                  pl.BlockSpec((rows, d), lambda i: (i, 0)),
                  pl.BlockSpec((rows, LANES), lambda i: (i, 0)),
                  pl.BlockSpec((1, d), lambda i: (0, 0)),
                  pl.BlockSpec(memory_space=pl.ANY)],
        out_specs=[pl.BlockSpec((rows, d), lambda i: (i, 0)),
                   pl.BlockSpec((rows, d), lambda i: (i, 0))],
        out_shape=[jax.ShapeDtypeStruct((n, d), F32), jax.ShapeDtypeStruct((n, d), BF16)],
        scratch_shapes=[pltpu.VMEM((TOP_K, rows * _row_pitch(s), LANES), F32), pltpu.SemaphoreType.DMA(())],
        compiler_params=_params("arbitrary"),
        name="moe_combine_gather",
    )(dest, x1, gate, g_ple.reshape(1, d), yt)


def _ple_kernel(x2_ref, h3_ref, p_ref, wpg_ref, wple_ref, o_ref):
    gate = jax.nn.sigmoid(_dot(h3_ref[...], wpg_ref[...]))
    o_ref[...] = x2_ref[...] + gate * _dot(p_ref[...].astype(BF16), wple_ref[...])


def _ple(x2, h3, p, w_pg, w_ple, row0, r):
    d = x2.shape[1]
    pd = p.shape[1]
    tm = _pick(r, (512, 256, 128, 64, 32, 16))
    tn = _pick(d, (1024, 512, 256, 128))
    assert row0 % tm == 0
    b0 = row0 // tm
    return pl.pallas_call(
        _ple_kernel,
        grid=(d // tn, r // tm),
        in_specs=[pl.BlockSpec((tm, tn), lambda j, i: (b0 + i, j)),
                  pl.BlockSpec((tm, d), lambda j, i: (b0 + i, 0)),
                  pl.BlockSpec((tm, pd), lambda j, i: (i, 0)),
                  pl.BlockSpec((d, tn), lambda j, i: (0, j)),
                  pl.BlockSpec((pd, tn), lambda j, i: (0, j))],
        out_specs=pl.BlockSpec((tm, tn), lambda j, i: (i, j)),
        out_shape=jax.ShapeDtypeStruct((r, d), F32),
        compiler_params=_params("parallel", "parallel"),
        name="ple_gate",
    )(x2, h3, p, w_pg, w_ple)


def _rope_tables(pos):
    half = ROT_DIM // 2
    inv = jnp.power(ROPE_THETA, -jnp.arange(half, dtype=F32) * 2.0 / ROT_DIM)
    ang = pos.astype(F32)[:, None] * inv[None, :]
    cos, sin = jnp.cos(ang), jnp.sin(ang)
    r = pos.shape[0]
    rest = A_HEAD_DIM - ROT_DIM
    c = jnp.concatenate([cos, cos, jnp.ones((r, rest), F32)], axis=1)
    s1 = jnp.concatenate([-sin, jnp.zeros((r, half + rest), F32)], axis=1)
    s2 = jnp.concatenate([jnp.zeros((r, half), F32), sin, jnp.zeros((r, rest), F32)], axis=1)
    return c, s1, s2


def _mixer_inputs(x, pos, g_attn, w, b_i, b_f, g_q, g_k, mlstm_dtype):
    d = x.shape[1]
    kq = M_HEADS * M_QK
    o_ig = 2 * kq + M_WIDTH
    o_om = o_ig + 2 * M_HEADS
    o_qa = o_om + M_WIDTH
    o_gm = o_qa + 3 * A_WIDTH
    h = _rmsnorm_bf16(x, g_attn)
    colscale = jnp.concatenate([jnp.full((1, kq), M_QK ** -0.5, F32), jnp.ones((1, kq + M_WIDTH), F32)], axis=1)
    qkv_m = _proj_scale(h, w[:, :o_ig].astype(BF16), colscale, mlstm_dtype)
    wg = w[:, o_ig:o_om]
    wc = jnp.pad(wg, ((0, 0), (0, LANES - 2 * M_HEADS))).astype(BF16)
    bias = jnp.concatenate([b_i, b_f]).astype(F32)
    bc = jnp.pad(bias, (0, LANES - 2 * M_HEADS)).reshape(1, LANES)
    gcol, grow = _proj_gates(h, wc, wg.T.astype(BF16), bc, bias.reshape(2 * M_HEADS, 1))
    sg = _proj_sigmoid(h, jnp.concatenate([w[:, o_om:o_qa], w[:, o_gm:]], axis=1).astype(BF16))
    tabs = _rope_tables(pos)
    q32, _ = _proj_qk(h, w[:, o_qa:o_qa + A_WIDTH].astype(BF16), g_q, *tabs)
    k32, k16 = _proj_qk(h, w[:, o_qa + A_WIDTH:o_qa + 2 * A_WIDTH].astype(BF16), g_k, *tabs)
    v32, v16 = _proj_plain2(h, w[:, o_qa + 2 * A_WIDTH:o_gm].astype(BF16))
    return qkv_m, gcol, grow, sg, q32, k32, k16, v32, v16


def _moe_plan(eid, n_chunks):
    n = eid.shape[0]
    e_flat = eid[:, :TOP_K].reshape(-1)
    onehot = (e_flat[:, None] == jnp.arange(N_EXPERTS, dtype=jnp.int32)[None, :]).astype(jnp.int32)
    before = jnp.cumsum(onehot, axis=0) - onehot
    rank = jnp.sum(before * onehot, axis=1)
    counts = jnp.sum(onehot, axis=0)
    padded = (counts + MOE_ROWS - 1) // MOE_ROWS * MOE_ROWS
    p_ends = jnp.cumsum(padded)
    p_starts = p_ends - padded
    dest = (p_starts[e_flat] + rank).astype(jnp.int32)
    tok = jnp.repeat(jnp.arange(n, dtype=jnp.int32), TOP_K)
    src_tok = jnp.zeros((n_chunks * MOE_ROWS,), jnp.int32).at[dest].set(tok)
    n_valid = (p_ends[-1] // MOE_ROWS).astype(jnp.int32).reshape(1)
    chunk0 = jnp.minimum(jnp.arange(n_chunks, dtype=jnp.int32), n_valid[0] - 1) * MOE_ROWS
    chunk_expert = jnp.minimum(jnp.sum((p_ends[None, :] <= chunk0[:, None]).astype(jnp.int32), axis=1),
                               N_EXPERTS - 1)
    ids = jnp.arange(N_EXPERTS, dtype=jnp.int32)
    later = (ids[None, :] > chunk_expert[:, None]) & (counts[None, :] > 0)
    next_expert = jnp.min(jnp.where(later, ids[None, :], N_EXPERTS), axis=1)
    next_expert = jnp.where(next_expert < N_EXPERTS, next_expert, -1).astype(jnp.int32)
    return dest, src_tok, n_valid, chunk_expert, next_expert


def kernel(x_prompt, x_sample, cache_k, cache_v, state_mlstm_C, state_mlstm_n, state_mlstm_m, page_table, p_prompt, p_sample, g_attn, w_in, b_i, b_f, g_q, g_k, g_mh, w_bm, w_ba, w_out, g_ffn, w_router, b_router, w_up, b_up, w_down, b_down, g_ple, w_pg, w_ple):
    depth = w_in.shape[0]
    bsz, t, d = x_prompt.shape
    ns, dec_seq, _ = x_sample.shape
    assert depth == 1 and bsz == 1 and dec_seq == 1
    page = cache_k.shape[2]
    past = page_table.shape[1] * page
    assert t % MOBA_BLOCK == 0 and past % MOBA_BLOCK == 0 and MOBA_BLOCK % page == 0
    assert t % (M_CHUNK * MLSTM_CHUNKS_PER_STEP) == 0 and ns % MLSTM_SAMPLE_SEQS == 0
    n = t + ns

    xp = x_prompt[0]
    xs = x_sample[:, 0]
    w = w_in[0]
    fw = (g_attn[0], w, b_i[0], b_f[0], g_q[0], g_k[0])

    qkv_p, gcol_p, grow_p, sg_p, q32_p, k32_p, k16_p, v32_p, v16_p = _mixer_inputs(
        xp, jnp.arange(t, dtype=jnp.int32), *fw, BF16)
    hm_p, ct_p, n_p, m_p = _mlstm_prompt(qkv_p, gcol_p, grow_p)
    ao_p = _moba_prompt(q32_p, k16_p, v16_p, _block_kmean(k32_p))

    qkv_s, gcol_s, _, sg_s, q32_s, k32_s, _, v32_s, _ = _mixer_inputs(
        xs, jnp.full((ns,), past, jnp.int32), *fw, F32)
    hm_s, c_s, n_s, m_s = _mlstm_sample(qkv_s, gcol_s, state_mlstm_C[0].astype(F32),
                                        state_mlstm_n[0].astype(F32), state_mlstm_m[0].astype(F32))
    ao_s = _moba_sample(q32_s, k32_s, v32_s, cache_k, cache_v, page_table)

    wbm, wba, wout = w_bm[0].astype(BF16), w_ba[0].astype(BF16), w_out[0].astype(BF16)
    mix_p = _mix(hm_p, sg_p, ao_p, g_mh[0], wbm, wba)
    mix_s = _mix(hm_s, sg_s, ao_s, g_mh[0], wbm, wba)
    wr = jnp.pad(w_router[0], ((0, 0), (0, LANES - N_EXPERTS)))
    wr_hi = wr.astype(BF16)
    wr_lo = (wr - wr_hi.astype(F32)).astype(BF16)
    b_r = jnp.pad(b_router[0].astype(F32), (0, LANES - N_EXPERTS)).reshape(1, LANES)
    routed = _resid_router(xp, mix_p, wout, g_ffn[0], wr_hi, wr_lo, b_r, n, 0, None)
    x1, x1t, gate, eid = _resid_router(xs, mix_s, wout, g_ffn[0], wr_hi, wr_lo, b_r, n, t, routed)

    n_slots = n * TOP_K
    n_chunks = (n_slots + N_EXPERTS * (MOE_ROWS - 1)) // MOE_ROWS
    dest, src_tok, n_valid, chunk_expert, next_expert = _moe_plan(eid, n_chunks)
    xs_sorted = _dispatch(x1t, g_ffn[0], src_tok, n_valid, n_chunks)
    act = _moe_up(xs_sorted, w_up[0], b_up[0], chunk_expert, next_expert, n_valid)
    yt = _moe_down(act, w_down[0], b_down[0], chunk_expert, next_expert, n_valid)
    x2, h3 = _combine(x1, gate, dest, yt, g_ple[0])

    wpg, wple = w_pg[0].astype(BF16), w_ple[0].astype(BF16)
    y_p = _ple(x2, h3, p_prompt[0, 0], wpg, wple, 0, t)
    y_s = _ple(x2, h3, p_sample[0, :, 0], wpg, wple, t, ns)

    def heads(a):
        return a.reshape(a.shape[0], A_HEADS, A_HEAD_DIM)

    return (y_p[None], y_s[:, None],
            heads(k32_p)[None, None], heads(v32_p)[None, None],
            jnp.swapaxes(ct_p, 1, 2)[None, None].astype(state_mlstm_C.dtype),
            n_p[:M_HEADS][None, None].astype(state_mlstm_n.dtype),
            m_p[:M_HEADS, 0][None, None].astype(state_mlstm_m.dtype),
            heads(k32_s)[None, :, None], heads(v32_s)[None, :, None],
            c_s[None].astype(state_mlstm_C.dtype), n_s[None].astype(state_mlstm_n.dtype),
            m_s[None].astype(state_mlstm_m.dtype))
```

```python
import functools

import jax
import jax.numpy as jnp
import numpy as np
from jax import lax
from jax.experimental import pallas as pl
from jax.experimental.pallas import tpu as pltpu

F32 = jnp.float32
BF16 = jnp.bfloat16

M_HEADS = 4
M_QK = 128
M_V = 256
M_CHUNK = 64
M_WIDTH = M_HEADS * M_V
A_HEADS = 8
A_HEAD_DIM = 128
A_WIDTH = A_HEADS * A_HEAD_DIM
MOBA_BLOCK = 256
MOBA_TOPK = 3
ROT_DIM = A_HEAD_DIM // 4
ROPE_THETA = 500000.0
N_EXPERTS = 32
TOP_K = 4
SWIGLU_LIMIT = 7.0
SWIGLU_ALPHA = 1.702
EPS = 1e-6
NEG = -1e30

LANES = 128
SUBLANES = 8
VMEM_LIMIT_BYTES = 56 * 1024 * 1024

MOE_ROWS = 256
MLSTM_CHUNKS_PER_STEP = 2
MLSTM_SAMPLE_SEQS = 8
COMBINE_ROWS = 128
MOBA_HEADS_PER_STEP = 4
GATHER_UNROLL = 8


def _pick(n, candidates):
    for c in candidates:
        if n % c == 0:
            return c
    raise ValueError(f"no tile in {candidates} divides {n}")


def _params(*sem):
    return pltpu.CompilerParams(dimension_semantics=sem, vmem_limit_bytes=VMEM_LIMIT_BYTES)


def _dot(a, b):
    return jnp.dot(a, b, preferred_element_type=F32)


def _dot_nt(a, b):
    return lax.dot_general(a, b, (((1,), (1,)), ((), ())), preferred_element_type=F32)


def _dot_tn(a, b):
    return lax.dot_general(a, b, (((0,), (0,)), ((), ())), preferred_element_type=F32)


def _split_bf16(x):
    hi = x.astype(BF16)
    lo = (x - hi.astype(F32)).astype(BF16)
    return hi, lo


def _rms(x, g):
    return x * lax.rsqrt(jnp.mean(x * x, axis=-1, keepdims=True) + EPS) * g


def _log_sigmoid(x):
    return -(jnp.maximum(-x, 0.0) + jnp.log1p(jnp.exp(-jnp.abs(x))))


def _first_argmax(work, lane_f):
    m = jnp.max(work, axis=-1, keepdims=True)
    idx = jnp.min(jnp.where(work == m, lane_f, float(4 * LANES)), axis=-1, keepdims=True)
    return m, idx


def _rmsnorm_kernel(x_ref, g_ref, o_ref):
    o_ref[...] = _rms(x_ref[...], g_ref[...]).astype(o_ref.dtype)


def _rmsnorm_bf16(x, g):
    r, d = x.shape
    tm = _pick(r, (512, 256, 128, 64, 32, 16))
    return pl.pallas_call(
        _rmsnorm_kernel,
        grid=(r // tm,),
        in_specs=[pl.BlockSpec((tm, d), lambda i: (i, 0)), pl.BlockSpec((1, d), lambda i: (0, 0))],
        out_specs=pl.BlockSpec((tm, d), lambda i: (i, 0)),
        out_shape=jax.ShapeDtypeStruct((r, d), BF16),
        compiler_params=_params("parallel"),
        name="rmsnorm_rows",
    )(x, g.reshape(1, d))


def _proj_scale_kernel(h_ref, w_ref, s_ref, o_ref):
    o_ref[...] = (_dot(h_ref[...], w_ref[...]) * s_ref[...]).astype(o_ref.dtype)


def _proj_sigmoid_kernel(h_ref, w_ref, o_ref):
    o_ref[...] = jax.nn.sigmoid(_dot(h_ref[...], w_ref[...])).astype(o_ref.dtype)


def _proj_plain2_kernel(h_ref, w_ref, o32_ref, o16_ref):
    acc = _dot(h_ref[...], w_ref[...])
    o32_ref[...] = acc
    o16_ref[...] = acc.astype(BF16)


def _proj_qk_kernel(h_ref, w_ref, g_ref, c_ref, s1_ref, s2_ref, o32_ref, o16_ref):
    acc = _dot(h_ref[...], w_ref[...])
    g = g_ref[...]
    c, s1, s2 = c_ref[...], s1_ref[...], s2_ref[...]
    for hh in range(acc.shape[1] // A_HEAD_DIM):
        sl = slice(hh * A_HEAD_DIM, (hh + 1) * A_HEAD_DIM)
        y = _rms(acc[:, sl], g)
        up = pltpu.roll(y, A_HEAD_DIM - ROT_DIM // 2, 1)
        dn = pltpu.roll(y, ROT_DIM // 2, 1)
        r = y * c + up * s1 + dn * s2
        o32_ref[:, sl] = r
        o16_ref[:, sl] = r.astype(BF16)


def _proj_gates_kernel(h_ref, wc_ref, wr_ref, bc_ref, br_ref, gc_ref, gr_ref):
    h = h_ref[...]
    zc = _dot(h, wc_ref[...]) + bc_ref[...]
    zr = _dot_nt(wr_ref[...], h) + br_ref[...]
    lane = lax.broadcasted_iota(jnp.int32, zc.shape, 1)
    gc_ref[...] = jnp.where((lane >= M_HEADS) & (lane < 2 * M_HEADS), _log_sigmoid(zc), zc)
    row = lax.broadcasted_iota(jnp.int32, zr.shape, 0)
    gr_ref[...] = jnp.where(row >= M_HEADS, _log_sigmoid(zr), zr)


def _proj_tiles(r, n):
    tm = _pick(r, (1024, 512, 256, 128, 64, 32, 16))
    tn = _pick(n, (1024, 512, 256, 128))
    return tm, tn


def _proj_scale(h, w, colscale, out_dtype):
    r, d = h.shape
    n = w.shape[1]
    tm, tn = _proj_tiles(r, n)
    return pl.pallas_call(
        _proj_scale_kernel,
        grid=(n // tn, r // tm),
        in_specs=[pl.BlockSpec((tm, d), lambda j, i: (i, 0)),
                  pl.BlockSpec((d, tn), lambda j, i: (0, j)),
                  pl.BlockSpec((1, tn), lambda j, i: (0, j))],
        out_specs=pl.BlockSpec((tm, tn), lambda j, i: (i, j)),
        out_shape=jax.ShapeDtypeStruct((r, n), out_dtype),
        compiler_params=_params("parallel", "parallel"),
        name="proj_scale",
    )(h, w, colscale)


def _proj_sigmoid(h, w):
    r, d = h.shape
    n = w.shape[1]
    tm, tn = _proj_tiles(r, n)
    return pl.pallas_call(
        _proj_sigmoid_kernel,
        grid=(n // tn, r // tm),
        in_specs=[pl.BlockSpec((tm, d), lambda j, i: (i, 0)),
                  pl.BlockSpec((d, tn), lambda j, i: (0, j))],
        out_specs=pl.BlockSpec((tm, tn), lambda j, i: (i, j)),
        out_shape=jax.ShapeDtypeStruct((r, n), BF16),
        compiler_params=_params("parallel", "parallel"),
        name="proj_sigmoid",
    )(h, w)


def _proj_plain2(h, w):
    r, d = h.shape
    n = w.shape[1]
    tm, tn = _proj_tiles(r, n)
    return pl.pallas_call(
        _proj_plain2_kernel,
        grid=(n // tn, r // tm),
        in_specs=[pl.BlockSpec((tm, d), lambda j, i: (i, 0)),
                  pl.BlockSpec((d, tn), lambda j, i: (0, j))],
        out_specs=[pl.BlockSpec((tm, tn), lambda j, i: (i, j)),
                   pl.BlockSpec((tm, tn), lambda j, i: (i, j))],
        out_shape=[jax.ShapeDtypeStruct((r, n), F32), jax.ShapeDtypeStruct((r, n), BF16)],
        compiler_params=_params("parallel", "parallel"),
        name="proj_plain",
    )(h, w)


def _proj_qk(h, w, g, rope_c, rope_s1, rope_s2):
    r, d = h.shape
    n = w.shape[1]
    tm, tn = _proj_tiles(r, n)
    hd = A_HEAD_DIM
    return pl.pallas_call(
        _proj_qk_kernel,
        grid=(n // tn, r // tm),
        in_specs=[pl.BlockSpec((tm, d), lambda j, i: (i, 0)),
                  pl.BlockSpec((d, tn), lambda j, i: (0, j)),
                  pl.BlockSpec((1, hd), lambda j, i: (0, 0)),
                  pl.BlockSpec((tm, hd), lambda j, i: (i, 0)),
                  pl.BlockSpec((tm, hd), lambda j, i: (i, 0)),
                  pl.BlockSpec((tm, hd), lambda j, i: (i, 0))],
        out_specs=[pl.BlockSpec((tm, tn), lambda j, i: (i, j)),
                   pl.BlockSpec((tm, tn), lambda j, i: (i, j))],
        out_shape=[jax.ShapeDtypeStruct((r, n), F32), jax.ShapeDtypeStruct((r, n), BF16)],
        compiler_params=_params("parallel", "parallel"),
        name="proj_qk_norm_rope",
    )(h, w, g.reshape(1, hd), rope_c, rope_s1, rope_s2)


def _proj_gates(h, wc, wr, bc, br):
    r, d = h.shape
    tm = _pick(r, (1024, 512, 256, 128)) if r % LANES == 0 else r
    g2 = 2 * M_HEADS
    return pl.pallas_call(
        _proj_gates_kernel,
        grid=(r // tm,),
        in_specs=[pl.BlockSpec((tm, d), lambda i: (i, 0)),
                  pl.BlockSpec((d, LANES), lambda i: (0, 0)),
                  pl.BlockSpec((g2, d), lambda i: (0, 0)),
                  pl.BlockSpec((1, LANES), lambda i: (0, 0)),
                  pl.BlockSpec((g2, 1), lambda i: (0, 0))],
        out_specs=[pl.BlockSpec((tm, LANES), lambda i: (i, 0)),
                   pl.BlockSpec((g2, tm), lambda i: (0, i))],
        out_shape=[jax.ShapeDtypeStruct((r, LANES), F32), jax.ShapeDtypeStruct((g2, r), F32)],
        compiler_params=_params("parallel"),
        name="proj_gates",
    )(h, wc, wr, bc, br)


def _mlstm_prompt_kernel(q_ref, k_ref, v_ref, gc_ref, gr_ref, h_ref, ct_out, n_out, m_out,
                         ct_s, n_s, m_s):
    step = pl.program_id(0)

    @pl.when(step == 0)
    def _():
        ct_s[...] = jnp.zeros_like(ct_s)
        n_s[...] = jnp.zeros_like(n_s)
        m_s[...] = jnp.zeros_like(m_s)

    ln = M_CHUNK
    row = lax.broadcasted_iota(jnp.int32, (ln, ln), 0)
    col = lax.broadcasted_iota(jnp.int32, (ln, ln), 1)
    causal = col <= row
    upto = row <= col
    for cc in range(MLSTM_CHUNKS_PER_STEP):
        rs = slice(cc * ln, (cc + 1) * ln)
        for hh in range(M_HEADS):
            ig_c = gc_ref[rs, hh:hh + 1]
            lf_c = gc_ref[rs, M_HEADS + hh:M_HEADS + hh + 1]
            ig_r = gr_ref[hh:hh + 1, rs]
            lf_r = gr_ref[M_HEADS + hh:M_HEADS + hh + 1, rs]
            bcum_c = jnp.sum(jnp.where(causal, lf_r, 0.0), axis=1, keepdims=True)
            bcum_r = jnp.sum(jnp.where(upto, lf_c, 0.0), axis=0, keepdims=True)
            dmat = jnp.where(causal, bcum_c - bcum_r + ig_r, NEG)
            m_prev = m_s[hh:hh + 1, 0:1]
            a = bcum_c + m_prev
            m_t = jnp.maximum(a, jnp.max(dmat, axis=1, keepdims=True))
            w_inter = jnp.exp(a - m_t)
            decay = jnp.exp(dmat - m_t)
            qh = q_ref[rs, hh * M_QK:(hh + 1) * M_QK]
            kh = k_ref[rs, hh * M_QK:(hh + 1) * M_QK]
            vh = v_ref[rs, hh * M_V:(hh + 1) * M_V]
            s = _dot_nt(qh, kh) * decay
            ct = ct_s[hh]
            n_row = n_s[hh:hh + 1, :]
            num = w_inter * _dot(qh, ct.astype(BF16)) + _dot(s.astype(BF16), vh)
            den = (w_inter * jnp.sum(qh.astype(F32) * n_row, axis=1, keepdims=True)
                   + jnp.sum(s, axis=1, keepdims=True))
            h_ref[rs, hh * M_V:(hh + 1) * M_V] = num / jnp.maximum(jnp.abs(den), jnp.exp(-m_t))
            b_last = bcum_c[ln - 1:ln, :]
            d_last_r = b_last - bcum_r + ig_r
            d_last_c = b_last - bcum_c + ig_c
            a_last = b_last + m_prev
            m_new = jnp.maximum(a_last, jnp.max(d_last_r, axis=1, keepdims=True))
            w_c = jnp.exp(a_last - m_new)
            w_j = jnp.exp(d_last_c - m_new)
            vw = (vh.astype(F32) * w_j).astype(BF16)
            ct_s[hh] = w_c * ct + _dot_tn(kh, vw)
            n_s[hh:hh + 1, :] = w_c * n_row + jnp.sum(kh.astype(F32) * w_j, axis=0, keepdims=True)
            m_s[hh:hh + 1, :] = jnp.broadcast_to(m_new, (1, LANES))

    @pl.when(step == pl.num_programs(0) - 1)
    def _():
        ct_out[...] = ct_s[...]
        n_out[...] = n_s[...]
        m_out[...] = m_s[...]


def _mlstm_prompt(qkv, gcol, grow):
    t = qkv.shape[0]
    rows = M_CHUNK * MLSTM_CHUNKS_PER_STEP
    kq = M_HEADS * M_QK
    return pl.pallas_call(
        _mlstm_prompt_kernel,
        grid=(t // rows,),
        in_specs=[pl.BlockSpec((rows, kq), lambda c: (c, 0)),
                  pl.BlockSpec((rows, kq), lambda c: (c, 1)),
                  pl.BlockSpec((rows, M_WIDTH), lambda c: (c, (2 * kq) // M_WIDTH)),
                  pl.BlockSpec((rows, LANES), lambda c: (c, 0)),
                  pl.BlockSpec((2 * M_HEADS, rows), lambda c: (0, c))],
        out_specs=[pl.BlockSpec((rows, M_WIDTH), lambda c: (c, 0)),
                   pl.BlockSpec((M_HEADS, M_QK, M_V), lambda c: (0, 0, 0)),
                   pl.BlockSpec((SUBLANES, LANES), lambda c: (0, 0)),
                   pl.BlockSpec((SUBLANES, LANES), lambda c: (0, 0))],
        out_shape=[jax.ShapeDtypeStruct((t, M_WIDTH), F32),
                   jax.ShapeDtypeStruct((M_HEADS, M_QK, M_V), F32),
                   jax.ShapeDtypeStruct((SUBLANES, LANES), F32),
                   jax.ShapeDtypeStruct((SUBLANES, LANES), F32)],
        scratch_shapes=[pltpu.VMEM((M_HEADS, M_QK, M_V), F32),
                        pltpu.VMEM((SUBLANES, LANES), F32),
                        pltpu.VMEM((SUBLANES, LANES), F32)],
        compiler_params=_params("arbitrary"),
        name="mlstm_prompt_scan",
    )(qkv, qkv, qkv, gcol, grow)


def _mlstm_sample_kernel(qkv_ref, gc_ref, c_ref, n_ref, m_ref, h_ref, co_ref, no_ref, mo_ref):
    kq = M_HEADS * M_QK
    row8 = lax.broadcasted_iota(jnp.int32, (SUBLANES, 1), 0)
    mrow = lax.broadcasted_iota(jnp.int32, mo_ref.shape, 0)
    mcol = lax.broadcasted_iota(jnp.int32, mo_ref.shape, 1)
    m_all = m_ref[...]
    for s in range(MLSTM_SAMPLE_SEQS):
        for hh in range(M_HEADS):
            q = qkv_ref[s:s + 1, hh * M_QK:(hh + 1) * M_QK]
            k = qkv_ref[s:s + 1, kq + hh * M_QK:kq + (hh + 1) * M_QK]
            v = qkv_ref[s:s + 1, 2 * kq + hh * M_V:2 * kq + (hh + 1) * M_V]
            ig = gc_ref[s:s + 1, hh:hh + 1]
            lf = gc_ref[s:s + 1, M_HEADS + hh:M_HEADS + hh + 1]
            m_prev = m_ref[s:s + 1, hh:hh + 1]
            c = c_ref[s, hh]
            n_row = n_ref[s, hh:hh + 1, :]
            a = lf + m_prev
            m_t = jnp.maximum(a, ig)
            w_c = jnp.exp(a - m_t)
            w_j = jnp.exp(ig - m_t)
            sc = jnp.sum(q * k, axis=1, keepdims=True) * w_j
            q8 = jnp.broadcast_to(q, (SUBLANES, M_QK)).astype(BF16)
            cq = _dot_nt(q8, c.astype(BF16))[0:1, :]
            num = w_c * cq + sc * v
            den = w_c * jnp.sum(n_row * q, axis=1, keepdims=True) + sc
            h_ref[s:s + 1, hh * M_V:(hh + 1) * M_V] = num / jnp.maximum(jnp.abs(den), jnp.exp(-m_t))
            vw8 = jnp.where(row8 == 0, v * w_j, 0.0).astype(BF16)
            k8 = jnp.where(row8 == 0, k, 0.0).astype(BF16)
            co_ref[s, hh] = w_c * c + _dot_tn(vw8, k8)
            no_ref[s, hh:hh + 1, :] = w_c * n_row + w_j * k
            m_all = jnp.where((mrow == s) & (mcol == hh), m_t, m_all)
    mo_ref[...] = m_all


def _mlstm_sample(qkv, gcol, c0, n0, m0):
    ns = qkv.shape[0]
    sb = MLSTM_SAMPLE_SEQS
    wq = qkv.shape[1]
    return pl.pallas_call(
        _mlstm_sample_kernel,
        grid=(ns // sb,),
        in_specs=[pl.BlockSpec((sb, wq), lambda i: (i, 0)),
                  pl.BlockSpec((sb, LANES), lambda i: (i, 0)),
                  pl.BlockSpec((sb, M_HEADS, M_V, M_QK), lambda i: (i, 0, 0, 0)),
                  pl.BlockSpec((sb, M_HEADS, M_QK), lambda i: (i, 0, 0)),
                  pl.BlockSpec((sb, M_HEADS), lambda i: (i, 0))],
        out_specs=[pl.BlockSpec((sb, M_WIDTH), lambda i: (i, 0)),
                   pl.BlockSpec((sb, M_HEADS, M_V, M_QK), lambda i: (i, 0, 0, 0)),
                   pl.BlockSpec((sb, M_HEADS, M_QK), lambda i: (i, 0, 0)),
                   pl.BlockSpec((sb, M_HEADS), lambda i: (i, 0))],
        out_shape=[jax.ShapeDtypeStruct((ns, M_WIDTH), F32),
                   jax.ShapeDtypeStruct(c0.shape, F32),
                   jax.ShapeDtypeStruct(n0.shape, F32),
                   jax.ShapeDtypeStruct(m0.shape, F32)],
        compiler_params=_params("parallel"),
        name="mlstm_sample_step",
    )(qkv, gcol, c0, n0, m0)


def _kmean_kernel(k_ref, o_ref):
    o_ref[0] = jnp.mean(k_ref[...], axis=0, keepdims=True)


def _block_kmean(k32):
    t, w = k32.shape
    nb = t // MOBA_BLOCK
    out = pl.pallas_call(
        _kmean_kernel,
        grid=(nb,),
        in_specs=[pl.BlockSpec((MOBA_BLOCK, w), lambda b: (b, 0))],
        out_specs=pl.BlockSpec((1, 1, w), lambda b: (b, 0, 0)),
        out_shape=jax.ShapeDtypeStruct((nb, 1, w), F32),
        compiler_params=_params("parallel"),
        name="moba_block_kmean",
    )(k32)
    return out.reshape(nb, w)


def _moba_select(sc, own, lane_i, lane_f):
    work = jnp.where(lane_i < own, sc, NEG)
    sel = jnp.zeros(sc.shape, dtype=jnp.bool_)
    for _ in range(MOBA_TOPK):
        _, idx = _first_argmax(work, lane_f)
        hit = lane_f == idx
        sel = sel | hit
        work = jnp.where(hit, -jnp.inf, work)
    return sel & (lane_i < own)


def _moba_prompt_kernel(q_ref, k_ref, v_ref, e_ref, km_ref, o_ref):
    i = pl.program_id(1)
    bs = MOBA_BLOCK
    pw = 2 * bs
    hd = A_HEAD_DIM
    heads = range(MOBA_HEADS_PER_STEP)
    ones = jnp.ones((pw, hd), BF16)
    lane_i = lax.broadcasted_iota(jnp.int32, (bs, LANES), 1)
    lane_f = lane_i.astype(F32)

    def head_slice(hh):
        return slice(hh * hd, (hh + 1) * hd)

    q_aug = []
    for hh in heads:
        q = q_ref[:, head_slice(hh)]
        qh, ql = _split_bf16(q)
        kmh, kml = _split_bf16(km_ref[:, head_slice(hh)])
        sc = _dot_nt(qh, kmh) + (_dot_nt(qh, kml) + _dot_nt(ql, kmh))
        sel = _moba_select(sc, i, lane_i, lane_f)
        bias = jnp.where(sel | (lane_i >= i), 0.0, NEG)
        q_aug.append(jnp.concatenate([(q * (hd ** -0.5)).astype(BF16), bias.astype(BF16)], axis=1))

    def pair_scores(p, hh, onehot):
        off = pl.multiple_of(p * pw, pw)
        k_aug = jnp.concatenate([k_ref[pl.ds(off, pw), head_slice(hh)], onehot], axis=1)
        v_aug = jnp.concatenate([v_ref[pl.ds(off, pw), head_slice(hh)], ones], axis=1)
        return _dot_nt(q_aug[hh], k_aug), v_aug

    p_own = i // 2
    row = lax.broadcasted_iota(jnp.int32, (bs, pw), 0)
    col = lax.broadcasted_iota(jnp.int32, (bs, pw), 1)
    causal = col + (p_own * pw - i * bs) <= row
    onehot_own = e_ref[pl.ds(pl.multiple_of(p_own * pw, pw), pw), :]
    init = []
    for hh in heads:
        s, v_aug = pair_scores(p_own, hh, onehot_own)
        s = jnp.where(causal, s, NEG)
        m0 = jnp.max(s, axis=1, keepdims=True)
        init += [m0, _dot(jnp.exp(s - m0).astype(BF16), v_aug)]

    def body(p, carry):
        onehot = e_ref[pl.ds(pl.multiple_of(p * pw, pw), pw), :]
        out = []
        for hh in heads:
            m_i, acc = carry[2 * hh], carry[2 * hh + 1]
            sj, vj = pair_scores(p, hh, onehot)
            m_n = jnp.maximum(m_i, jnp.max(sj, axis=1, keepdims=True))
            out += [m_n, jnp.exp(m_i - m_n) * acc + _dot(jnp.exp(sj - m_n).astype(BF16), vj)]
        return tuple(out)

    final = lax.fori_loop(0, p_own, body, tuple(init))
    for hh in heads:
        acc = final[2 * hh + 1]
        o_ref[:, head_slice(hh)] = (acc[:, :hd] / acc[:, hd:]).astype(o_ref.dtype)


def _moba_prompt(q32, k16, v16, kmean):
    t = q32.shape[0]
    nb = t // MOBA_BLOCK
    hd = MOBA_HEADS_PER_STEP * A_HEAD_DIM
    assert nb <= LANES and nb % 2 == 0 and A_HEADS % MOBA_HEADS_PER_STEP == 0
    block_onehot = (jnp.arange(t, dtype=jnp.int32)[:, None] // MOBA_BLOCK
                    == jnp.arange(LANES, dtype=jnp.int32)[None, :]).astype(BF16)
    kmean_pad = jnp.pad(kmean, ((0, LANES - nb), (0, 0)))
    return pl.pallas_call(
        _moba_prompt_kernel,
        grid=(A_HEADS // MOBA_HEADS_PER_STEP, nb),
        in_specs=[pl.BlockSpec((MOBA_BLOCK, hd), lambda h, i: (i, h)),
                  pl.BlockSpec((t, hd), lambda h, i: (0, h)),
                  pl.BlockSpec((t, hd), lambda h, i: (0, h)),
                  pl.BlockSpec((t, LANES), lambda h, i: (0, 0)),
                  pl.BlockSpec((LANES, hd), lambda h, i: (0, h))],
        out_specs=pl.BlockSpec((MOBA_BLOCK, hd), lambda h, i: (i, h)),
        out_shape=jax.ShapeDtypeStruct((t, A_WIDTH), BF16),
        compiler_params=_params("parallel", "arbitrary"),
        name="moba_prompt_attention",
    )(q32, k16, v16, block_onehot, kmean_pad)


def _moba_sample_kernel(n_pages, pages_per_block, pt_ref, q_ref, kn_ref, vn_ref, *refs):
    del pt_ref
    kp = refs[:n_pages]
    vp = refs[n_pages:2 * n_pages]
    o_ref = refs[2 * n_pages]
    nbp = n_pages // pages_per_block
    hd = A_HEAD_DIM
    rows = kp[0].shape[2]
    page = rows // A_HEADS
    q = q_ref[0]
    lane_i = lax.broadcasted_iota(jnp.int32, (A_HEADS, LANES), 1)

    sc = jnp.zeros((A_HEADS, LANES), F32)
    k16 = []
    for b in range(nbp):
        ksum = jnp.zeros((A_HEADS, hd), F32)
        for pp in range(pages_per_block):
            kf = kp[b * pages_per_block + pp][0, 0]
            ksum = ksum + jnp.sum(kf.reshape(page, A_HEADS, hd), axis=0)
            k16.append(kf.astype(BF16))
        kmean = ksum / float(MOBA_BLOCK)
        sc = jnp.where(lane_i == b, jnp.sum(q * kmean, axis=1, keepdims=True), sc)
    sel = _moba_select(sc, nbp, lane_i, lane_i.astype(F32)).astype(F32)

    qs = q * (hd ** -0.5)
    own = jnp.sum(qs * kn_ref[0], axis=1, keepdims=True)
    s = _dot_nt(qs.astype(BF16), jnp.concatenate(k16, axis=0))
    head_row = lax.broadcasted_iota(jnp.int32, (A_HEADS, rows), 0)
    row_head = lax.broadcasted_iota(jnp.int32, (A_HEADS, rows), 1) % A_HEADS
    mine = head_row == row_head
    s = jnp.concatenate(
        [jnp.where(mine & (sel[:, pg // pages_per_block:pg // pages_per_block + 1] > 0.5),
                   s[:, pg * rows:(pg + 1) * rows], NEG) for pg in range(n_pages)], axis=1)
    m = jnp.maximum(own, jnp.max(s, axis=1, keepdims=True))
    p = jnp.exp(s - m)
    p_own = jnp.exp(own - m)
    l = jnp.sum(p, axis=1, keepdims=True) + p_own
    v16 = jnp.concatenate([vp[pg][0, 0].astype(BF16) for pg in range(n_pages)], axis=0)
    o_ref[0] = (_dot(p.astype(BF16), v16) + p_own * vn_ref[0]) / l


def _moba_sample(q32, kn32, vn32, cache_k, cache_v, page_table):
    ns = q32.shape[0]
    page = cache_k.shape[2]
    n_pages = page_table.shape[1]
    ppb = MOBA_BLOCK // page
    head_spec = pl.BlockSpec((1, A_HEADS, A_HEAD_DIM), lambda s, pt: (s, 0, 0))

    def page_spec(p):
        return pl.BlockSpec((1, 1, page * A_HEADS, A_HEAD_DIM), lambda s, pt: (0, pt[s, p], 0, 0))

    def rows(cache):
        return cache.reshape(cache.shape[0], cache.shape[1], page * A_HEADS, A_HEAD_DIM)

    def heads(a):
        return a.reshape(ns, A_HEADS, A_HEAD_DIM)

    grid_spec = pltpu.PrefetchScalarGridSpec(
        num_scalar_prefetch=1,
        grid=(ns,),
        in_specs=[head_spec, head_spec, head_spec] + [page_spec(p) for p in range(n_pages)] * 2,
        out_specs=head_spec,
    )
    out = pl.pallas_call(
        functools.partial(_moba_sample_kernel, n_pages, ppb),
        grid_spec=grid_spec,
        out_shape=jax.ShapeDtypeStruct((ns, A_HEADS, A_HEAD_DIM), F32),
        compiler_params=_params("parallel"),
        name="moba_sample_attention",
    )(page_table, heads(q32), heads(kn32), heads(vn32), *([rows(cache_k)] * n_pages), *([rows(cache_v)] * n_pages))
    return out.reshape(ns, A_WIDTH)


def _mix_kernel(hm_ref, om_ref, ao_ref, gm_ref, ga_ref, gmh_ref, wbm_ref, wba_ref, o_ref, hs_ref):
    @pl.when(pl.program_id(1) == 0)
    def _():
        for hh in range(M_HEADS):
            sl = slice(hh * M_V, (hh + 1) * M_V)
            y = _rms(hm_ref[:, sl], gmh_ref[:, sl])
            hs_ref[:, sl] = (y * om_ref[:, sl].astype(F32)).astype(BF16)

    t1 = _dot(hs_ref[...], wbm_ref[...])
    t2 = _dot(ao_ref[...].astype(BF16), wba_ref[...])
    o_ref[...] = (gm_ref[...].astype(F32) * t1 + ga_ref[...].astype(F32) * t2).astype(o_ref.dtype)


def _mix(hm, sg, ao, g_mh, w_bm, w_ba):
    r = hm.shape[0]
    d = w_bm.shape[1]
    tm = _pick(r, (512, 256, 128, 64, 32, 16))
    tn = _pick(d, (512, 256, 128))
    gm0 = M_WIDTH // tn
    ga0 = (M_WIDTH + d) // tn
    return pl.pallas_call(
        _mix_kernel,
        grid=(r // tm, d // tn),
        in_specs=[pl.BlockSpec((tm, M_WIDTH), lambda i, j: (i, 0)),
                  pl.BlockSpec((tm, M_WIDTH), lambda i, j: (i, 0)),
                  pl.BlockSpec((tm, A_WIDTH), lambda i, j: (i, 0)),
                  pl.BlockSpec((tm, tn), lambda i, j: (i, gm0 + j)),
                  pl.BlockSpec((tm, tn), lambda i, j: (i, ga0 + j)),
                  pl.BlockSpec((1, M_WIDTH), lambda i, j: (0, 0)),
                  pl.BlockSpec((M_WIDTH, tn), lambda i, j: (0, j)),
                  pl.BlockSpec((A_WIDTH, tn), lambda i, j: (0, j))],
        out_specs=pl.BlockSpec((tm, tn), lambda i, j: (i, j)),
        out_shape=jax.ShapeDtypeStruct((r, d), BF16),
        scratch_shapes=[pltpu.VMEM((tm, M_WIDTH), BF16)],
        compiler_params=_params("parallel", "arbitrary"),
        name="mixer_merge",
    )(hm, sg, ao, sg, sg, g_mh.reshape(1, M_WIDTH), w_bm, w_ba)


def _store_row_tiles(o_ref, x, rows):
    s = x.shape[1] // LANES
    for c in range(s):
        o_ref[pl.ds(c, rows, stride=s), :] = x[:, c * LANES:(c + 1) * LANES]


def _resid_router_kernel(x_ref, mix_ref, wout_ref, gffn_ref, wrh_ref, wrl_ref, br_ref,
                         x1_ref, x1t_ref, gate_ref, eid_ref):
    x1 = x_ref[...] + _dot(mix_ref[...], wout_ref[...])
    x1_ref[...] = x1
    _store_row_tiles(x1t_ref, x1, x1.shape[0])
    hh, hl = _split_bf16(_rms(x1, gffn_ref[...]))
    wrh = wrh_ref[...]
    logits = _dot(hh, wrh) + (_dot(hh, wrl_ref[...]) + _dot(hl, wrh)) + br_ref[...]
    lane_i = lax.broadcasted_iota(jnp.int32, logits.shape, 1)
    lane_f = lane_i.astype(F32)
    work = jnp.where(lane_i < N_EXPERTS, logits, -jnp.inf)
    vals, ids = [], []
    for _ in range(TOP_K):
        m, idx = _first_argmax(work, lane_f)
        vals.append(m)
        ids.append(idx)
        work = jnp.where(lane_f == idx, -jnp.inf, work)
    es = [jnp.exp(v - vals[0]) for v in vals]
    den = es[0]
    for e in es[1:]:
        den = den + e
    gate = jnp.zeros(logits.shape, F32)
    eid = jnp.zeros(logits.shape, F32)
    for k in range(TOP_K):
        gate = jnp.where(lane_i == k, es[k] / den, gate)
        eid = jnp.where(lane_i == k, ids[k], eid)
    gate_ref[...] = gate
    eid_ref[...] = eid.astype(jnp.int32)


def _resid_router(x, mix, w_out, g_ffn, wr_hi, wr_lo, b_r, n_total, row0, prev):
    r, d = x.shape
    s = d // LANES
    tm = _pick(r, (256, 128, 64, 32, 16))
    assert row0 % tm == 0
    b0 = row0 // tm
    out_shape = [jax.ShapeDtypeStruct((n_total, d), F32),
                 jax.ShapeDtypeStruct((n_total * s, LANES), F32),
                 jax.ShapeDtypeStruct((n_total, LANES), F32),
                 jax.ShapeDtypeStruct((n_total, LANES), jnp.int32)]
    out_specs = [pl.BlockSpec((tm, d), lambda i: (b0 + i, 0)),
                 pl.BlockSpec((tm * s, LANES), lambda i: (b0 + i, 0)),
                 pl.BlockSpec((tm, LANES), lambda i: (b0 + i, 0)),
                 pl.BlockSpec((tm, LANES), lambda i: (b0 + i, 0))]
    in_specs = [pl.BlockSpec((tm, d), lambda i: (i, 0)),
                pl.BlockSpec((tm, d), lambda i: (i, 0)),
                pl.BlockSpec((d, d), lambda i: (0, 0)),
                pl.BlockSpec((1, d), lambda i: (0, 0)),
                pl.BlockSpec((d, LANES), lambda i: (0, 0)),
                pl.BlockSpec((d, LANES), lambda i: (0, 0)),
                pl.BlockSpec((1, LANES), lambda i: (0, 0))]
    args = [x, mix, w_out, g_ffn.reshape(1, d), wr_hi, wr_lo, b_r]
    n_in = len(args)
    kern = _resid_router_kernel
    aliases = {}
    if prev is not None:
        in_specs = in_specs + [pl.BlockSpec(memory_space=pl.ANY)] * len(prev)
        args = args + list(prev)
        aliases = {n_in + k: k for k in range(len(prev))}

        def kern(*refs):
            _resid_router_kernel(*refs[:n_in], *refs[n_in + len(prev):])

    return pl.pallas_call(
        kern,
        grid=(r // tm,),
        in_specs=in_specs,
        out_specs=out_specs,
        out_shape=out_shape,
        input_output_aliases=aliases,
        compiler_params=_params("parallel"),
        name="residual_router_topk",
    )(*args)


def _row_pitch(s):
    return s + 1 if s % 2 == 0 else s


def _gathered_rows(buf, rows, s):
    return jnp.concatenate([buf[pl.ds(c, rows, stride=_row_pitch(s)), :] for c in range(s)], axis=1)


def _start_row_gathers(src_hbm, idx_ref, idx_base, idx_step, dst, rows, s, sem):
    pitch = _row_pitch(s)

    def trip(i, carry):
        for u in range(GATHER_UNROLL):
            r = i * GATHER_UNROLL + u
            idx = idx_ref[idx_base + r * idx_step]
            pltpu.make_async_copy(src_hbm.at[pl.ds(idx * s, s)], dst.at[pl.ds(r * pitch, s)],
                                  sem).start(priority=u % 2)
        return carry

    lax.fori_loop(0, rows // GATHER_UNROLL, trip, 0)


def _wait_row_gathers(dst, rows, s, sem):
    landed = dst.at[pl.ds(0, rows * s)]
    pltpu.make_async_copy(landed, landed, sem).wait()


def _dispatch_kernel(nv_ref, tok_ref, tok_next_ref, g_ref, xt_hbm, o_ref, buf, sem):
    c = pl.program_id(0)
    rows = o_ref.shape[0]
    s = o_ref.shape[1] // LANES
    slot = c % 2

    @pl.when(c == 0)
    def _():
        _start_row_gathers(xt_hbm, tok_ref, 0, 1, buf.at[0], rows, s, sem.at[0])

    @pl.when(c + 1 < nv_ref[0])
    def _():
        _start_row_gathers(xt_hbm, tok_next_ref, 0, 1, buf.at[1 - slot], rows, s, sem.at[1 - slot])

    @pl.when(c < nv_ref[0])
    def _():
        _wait_row_gathers(buf.at[slot], rows, s, sem.at[slot])
        o_ref[...] = _rms(_gathered_rows(buf.at[slot], rows, s), g_ref[...]).astype(o_ref.dtype)


def _dispatch(x1t, g_ffn, src_tok, n_valid, n_chunks):
    d = g_ffn.shape[0]
    s = d // LANES
    rows = MOE_ROWS
    assert rows % GATHER_UNROLL == 0
    grid_spec = pltpu.PrefetchScalarGridSpec(
        num_scalar_prefetch=1,
        grid=(n_chunks,),
        in_specs=[pl.BlockSpec((rows,), lambda c, nv: (jnp.minimum(c, nv[0] - 1),),
                               memory_space=pltpu.SMEM),
                  pl.BlockSpec((rows,), lambda c, nv: (jnp.minimum(c + 1, nv[0] - 1),),
                               memory_space=pltpu.SMEM),
                  pl.BlockSpec((1, d), lambda c, nv: (0, 0)),
                  pl.BlockSpec(memory_space=pl.ANY)],
        out_specs=pl.BlockSpec((rows, d), lambda c, nv: (jnp.minimum(c, nv[0] - 1), 0)),
        scratch_shapes=[pltpu.VMEM((2, rows * _row_pitch(s), LANES), F32), pltpu.SemaphoreType.DMA((2,))],
    )
    return pl.pallas_call(
        _dispatch_kernel,
        grid_spec=grid_spec,
        out_shape=jax.ShapeDtypeStruct((n_chunks * rows, d), BF16),
        compiler_params=_params("arbitrary"),
        name="moe_dispatch_gather",
    )(n_valid, src_tok, src_tok, g_ffn.reshape(1, d), x1t)


def _expert_tile_schedule(ce_ref, nx_ref, nv_ref, c, j, n_pass):
    first = (c == 0) | (ce_ref[c] != ce_ref[jnp.maximum(c - 1, 0)])
    more_in_pass = nx_ref[c] >= 0
    has_next = more_in_pass | (j + 1 < n_pass)
    next_e = jnp.where(more_in_pass, nx_ref[c], ce_ref[0])
    next_j = jnp.where(more_in_pass, j, j + 1)
    return (c < nv_ref[0]) & first, has_next, next_j, next_e


def _moe_up_kernel(ce_ref, nx_ref, nv_ref, x_ref, bg_ref, bl_ref, w_hbm, o_ref, raw, w16, sem):
    j = pl.program_id(0)
    c = pl.program_id(1)
    tf = raw.shape[2]
    n_pass = pl.num_programs(0)
    dff = n_pass * tf

    def fetch(jj, e, k):
        col = pl.multiple_of(k * dff + jj * tf, LANES)
        return pltpu.make_async_copy(w_hbm.at[e, :, pl.ds(col, tf)], raw.at[k], sem.at[k])

    first, has_next, next_j, next_e = _expert_tile_schedule(ce_ref, nx_ref, nv_ref, c, j, n_pass)

    @pl.when(first & (j == 0) & (c == 0))
    def _():
        for k in range(2):
            fetch(0, ce_ref[0], k).start()

    @pl.when(first)
    def _():
        for k in range(2):
            fetch(j, ce_ref[c], k).wait()
            w16[k] = raw[k].astype(BF16)

    @pl.when(first & has_next)
    def _():
        for k in range(2):
            fetch(next_j, next_e, k).start()

    @pl.when(c < nv_ref[0])
    def _():
        x = x_ref[...]
        gate = jnp.minimum(_dot(x, w16[0]) + bg_ref[0], SWIGLU_LIMIT)
        lin = jnp.clip(_dot(x, w16[1]) + bl_ref[0], -SWIGLU_LIMIT, SWIGLU_LIMIT)
        o_ref[...] = (gate * jax.nn.sigmoid(SWIGLU_ALPHA * gate) * (lin + 1.0)).astype(o_ref.dtype)


def _moe_up(xs, w_up, b_up, chunk_expert, next_expert, n_valid):
    rows = MOE_ROWS
    n_chunks = xs.shape[0] // rows
    e, d, f2 = w_up.shape
    dff = f2 // 2
    tf = _pick(dff, (1024, 512, 256, 128))
    nf = dff // tf

    def last(c, nv):
        return jnp.minimum(c, nv[0] - 1)

    grid_spec = pltpu.PrefetchScalarGridSpec(
        num_scalar_prefetch=3,
        grid=(nf, n_chunks),
        in_specs=[pl.BlockSpec((rows, d), lambda j, c, ce, nx, nv: (last(c, nv), 0)),
                  pl.BlockSpec((1, 1, tf), lambda j, c, ce, nx, nv: (ce[c], 0, j)),
                  pl.BlockSpec((1, 1, tf), lambda j, c, ce, nx, nv: (ce[c], 0, nf + j)),
                  pl.BlockSpec(memory_space=pl.ANY)],
        out_specs=pl.BlockSpec((rows, tf), lambda j, c, ce, nx, nv: (last(c, nv), j)),
        scratch_shapes=[pltpu.VMEM((2, d, tf), F32), pltpu.VMEM((2, d, tf), BF16),
                        pltpu.SemaphoreType.DMA((2,))],
    )
    b3 = b_up.reshape(e, 1, f2)
    return pl.pallas_call(
        _moe_up_kernel,
        grid_spec=grid_spec,
        out_shape=jax.ShapeDtypeStruct((n_chunks * rows, dff), BF16),
        compiler_params=_params("arbitrary", "arbitrary"),
        name="moe_up_swiglu",
    )(chunk_expert, next_expert, n_valid, xs, b3, b3, w_up)


def _moe_down_kernel(ce_ref, nx_ref, nv_ref, a_ref, b_ref, w_hbm, o_ref, raw, w16, sem):
    c = pl.program_id(0)

    def fetch(e):
        return pltpu.make_async_copy(w_hbm.at[e], raw, sem)

    first, has_next, _, next_e = _expert_tile_schedule(ce_ref, nx_ref, nv_ref, c, 0, 1)

    @pl.when(first & (c == 0))
    def _():
        fetch(ce_ref[0]).start()

    @pl.when(first)
    def _():
        fetch(ce_ref[c]).wait()
        w16[...] = raw[...].astype(BF16)

    @pl.when(first & has_next)
    def _():
        fetch(next_e).start()

    @pl.when(c < nv_ref[0])
    def _():
        y = _dot(a_ref[...], w16[...]) + b_ref[0]
        _store_row_tiles(o_ref, y, y.shape[0])


def _moe_down(act, w_down, b_down, chunk_expert, next_expert, n_valid):
    rows = MOE_ROWS
    n_chunks = act.shape[0] // rows
    e, dff, d = w_down.shape
    s = d // LANES

    def last(c, nv):
        return jnp.minimum(c, nv[0] - 1)

    grid_spec = pltpu.PrefetchScalarGridSpec(
        num_scalar_prefetch=3,
        grid=(n_chunks,),
        in_specs=[pl.BlockSpec((rows, dff), lambda c, ce, nx, nv: (last(c, nv), 0)),
                  pl.BlockSpec((1, 1, d), lambda c, ce, nx, nv: (ce[c], 0, 0)),
                  pl.BlockSpec(memory_space=pl.ANY)],
        out_specs=pl.BlockSpec((rows * s, LANES), lambda c, ce, nx, nv: (last(c, nv), 0)),
        scratch_shapes=[pltpu.VMEM((dff, d), F32), pltpu.VMEM((dff, d), BF16),
                        pltpu.SemaphoreType.DMA(())],
    )
    return pl.pallas_call(
        _moe_down_kernel,
        grid_spec=grid_spec,
        out_shape=jax.ShapeDtypeStruct((n_chunks * rows * s, LANES), F32),
        compiler_params=_params("arbitrary"),
        name="moe_down",
    )(chunk_expert, next_expert, n_valid, act, b_down.reshape(e, 1, d), w_down)


def _combine_kernel(dest_ref, dest_next_ref, x1_ref, gate_ref, gple_ref, yt_hbm, x2_ref, h3_ref, buf, sem):
    i = pl.program_id(0)
    rows = x1_ref.shape[0]
    s = x1_ref.shape[1] // LANES
    slot = i % 2

    def start(idx_ref, to):
        for k in range(TOP_K):
            _start_row_gathers(yt_hbm, idx_ref, k, TOP_K, buf.at[to, k], rows, s, sem.at[to])

    @pl.when(i == 0)
    def _():
        start(dest_ref, 0)

    @pl.when(i + 1 < pl.num_programs(0))
    def _():
        start(dest_next_ref, 1 - slot)

    for k in range(TOP_K):
        _wait_row_gathers(buf.at[slot, k], rows, s, sem.at[slot])
    gate = gate_ref[...]
    acc = x1_ref[...]
    for k in range(TOP_K):
        acc = acc + gate[:, k:k + 1] * _gathered_rows(buf.at[slot, k], rows, s)
    x2_ref[...] = acc
    h3_ref[...] = _rms(acc, gple_ref[...]).astype(h3_ref.dtype)


def _combine(x1, gate, dest, yt, g_ple):
    n, d = x1.shape
    s = d // LANES
    rows = _pick(n, (COMBINE_ROWS, 64, 32, 16))
    steps = n // rows
    assert rows % GATHER_UNROLL == 0
    return pl.pallas_call(
        _combine_kernel,
        grid=(steps,),
        in_specs=[pl.BlockSpec((rows * TOP_K,), lambda i: (i,), memory_space=pltpu.SMEM),
                  pl.BlockSpec((rows * TOP_K,), lambda i: (jnp.minimum(i + 1, steps - 1),),
                               memory_space=pltpu.SMEM),
                  pl.BlockSpec((rows, d), lambda i: (i, 0)),
                  pl.BlockSpec((rows, LANES), lambda i: (i, 0)),
                  pl.BlockSpec((1, d), lambda i: (0, 0)),
                  pl.BlockSpec(memory_space=pl.ANY)],
        out_specs=[pl.BlockSpec((rows, d), lambda i: (i, 0)),
                   pl.BlockSpec((rows, d), lambda i: (i, 0))],
        out_shape=[jax.ShapeDtypeStruct((n, d), F32), jax.ShapeDtypeStruct((n, d), BF16)],
        scratch_shapes=[pltpu.VMEM((2, TOP_K, rows * _row_pitch(s), LANES), F32),
                        pltpu.SemaphoreType.DMA((2,))],
        compiler_params=_params("arbitrary"),
        name="moe_combine_gather",
    )(dest, dest, x1, gate, g_ple.reshape(1, d), yt)


def _ple_kernel(x2_ref, h3_ref, p_ref, wpg_ref, wple_ref, o_ref):
    gate = jax.nn.sigmoid(_dot(h3_ref[...], wpg_ref[...]))
    o_ref[...] = x2_ref[...] + gate * _dot(p_ref[...].astype(BF16), wple_ref[...])


def _ple(x2, h3, p, w_pg, w_ple, row0, r):
    d = x2.shape[1]
    pd = p.shape[1]
    tm = _pick(r, (512, 256, 128, 64, 32, 16))
    tn = _pick(d, (1024, 512, 256, 128))
    assert row0 % tm == 0
    b0 = row0 // tm
    return pl.pallas_call(
        _ple_kernel,
        grid=(d // tn, r // tm),
        in_specs=[pl.BlockSpec((tm, tn), lambda j, i: (b0 + i, j)),
                  pl.BlockSpec((tm, d), lambda j, i: (b0 + i, 0)),
                  pl.BlockSpec((tm, pd), lambda j, i: (i, 0)),
                  pl.BlockSpec((d, tn), lambda j, i: (0, j)),
                  pl.BlockSpec((pd, tn), lambda j, i: (0, j))],
        out_specs=pl.BlockSpec((tm, tn), lambda j, i: (i, j)),
        out_shape=jax.ShapeDtypeStruct((r, d), F32),
        compiler_params=_params("parallel", "parallel"),
        name="ple_gate",
    )(x2, h3, p, w_pg, w_ple)


def _rope_tables(pos):
    half = ROT_DIM // 2
    inv = jnp.power(ROPE_THETA, -jnp.arange(half, dtype=F32) * 2.0 / ROT_DIM)
    ang = pos.astype(F32)[:, None] * inv[None, :]
    cos, sin = jnp.cos(ang), jnp.sin(ang)
    r = pos.shape[0]
    rest = A_HEAD_DIM - ROT_DIM
    c = jnp.concatenate([cos, cos, jnp.ones((r, rest), F32)], axis=1)
    s1 = jnp.concatenate([-sin, jnp.zeros((r, half + rest), F32)], axis=1)
    s2 = jnp.concatenate([jnp.zeros((r, half), F32), sin, jnp.zeros((r, rest), F32)], axis=1)
    return c, s1, s2


def _mixer_inputs(x, pos, g_attn, w, b_i, b_f, g_q, g_k, mlstm_dtype):
    d = x.shape[1]
    kq = M_HEADS * M_QK
    o_ig = 2 * kq + M_WIDTH
    o_om = o_ig + 2 * M_HEADS
    o_qa = o_om + M_WIDTH
    o_gm = o_qa + 3 * A_WIDTH
    h = _rmsnorm_bf16(x, g_attn)
    colscale = jnp.concatenate([jnp.full((1, kq), M_QK ** -0.5, F32), jnp.ones((1, kq + M_WIDTH), F32)], axis=1)
    qkv_m = _proj_scale(h, w[:, :o_ig].astype(BF16), colscale, mlstm_dtype)
    wg = w[:, o_ig:o_om]
    wc = jnp.pad(wg, ((0, 0), (0, LANES - 2 * M_HEADS))).astype(BF16)
    bias = jnp.concatenate([b_i, b_f]).astype(F32)
    bc = jnp.pad(bias, (0, LANES - 2 * M_HEADS)).reshape(1, LANES)
    gcol, grow = _proj_gates(h, wc, wg.T.astype(BF16), bc, bias.reshape(2 * M_HEADS, 1))
    sg = _proj_sigmoid(h, jnp.concatenate([w[:, o_om:o_qa], w[:, o_gm:]], axis=1).astype(BF16))
    tabs = _rope_tables(pos)
    q32, _ = _proj_qk(h, w[:, o_qa:o_qa + A_WIDTH].astype(BF16), g_q, *tabs)
    k32, k16 = _proj_qk(h, w[:, o_qa + A_WIDTH:o_qa + 2 * A_WIDTH].astype(BF16), g_k, *tabs)
    v32, v16 = _proj_plain2(h, w[:, o_qa + 2 * A_WIDTH:o_gm].astype(BF16))
    return qkv_m, gcol, grow, sg, q32, k32, k16, v32, v16


def _moe_plan(eid, n_chunks):
    n = eid.shape[0]
    e_flat = eid[:, :TOP_K].reshape(-1)
    onehot = (e_flat[:, None] == jnp.arange(N_EXPERTS, dtype=jnp.int32)[None, :]).astype(jnp.int32)
    before = jnp.cumsum(onehot, axis=0) - onehot
    rank = jnp.sum(before * onehot, axis=1)
    counts = jnp.sum(onehot, axis=0)
    padded = (counts + MOE_ROWS - 1) // MOE_ROWS * MOE_ROWS
    p_ends = jnp.cumsum(padded)
    p_starts = p_ends - padded
    dest = (p_starts[e_flat] + rank).astype(jnp.int32)
    tok = jnp.repeat(jnp.arange(n, dtype=jnp.int32), TOP_K)
    src_tok = jnp.zeros((n_chunks * MOE_ROWS,), jnp.int32).at[dest].set(tok)
    n_valid = (p_ends[-1] // MOE_ROWS).astype(jnp.int32).reshape(1)
    chunk0 = jnp.minimum(jnp.arange(n_chunks, dtype=jnp.int32), n_valid[0] - 1) * MOE_ROWS
    chunk_expert = jnp.minimum(jnp.sum((p_ends[None, :] <= chunk0[:, None]).astype(jnp.int32), axis=1),
                               N_EXPERTS - 1)
    ids = jnp.arange(N_EXPERTS, dtype=jnp.int32)
    later = (ids[None, :] > chunk_expert[:, None]) & (counts[None, :] > 0)
    next_expert = jnp.min(jnp.where(later, ids[None, :], N_EXPERTS), axis=1)
    next_expert = jnp.where(next_expert < N_EXPERTS, next_expert, -1).astype(jnp.int32)
    return dest, src_tok, n_valid, chunk_expert, next_expert


def kernel(x_prompt, x_sample, cache_k, cache_v, state_mlstm_C, state_mlstm_n, state_mlstm_m, page_table, p_prompt, p_sample, g_attn, w_in, b_i, b_f, g_q, g_k, g_mh, w_bm, w_ba, w_out, g_ffn, w_router, b_router, w_up, b_up, w_down, b_down, g_ple, w_pg, w_ple):
    depth = w_in.shape[0]
    bsz, t, d = x_prompt.shape
    ns, dec_seq, _ = x_sample.shape
    assert depth == 1 and bsz == 1 and dec_seq == 1
    page = cache_k.shape[2]
    past = page_table.shape[1] * page
    assert t % MOBA_BLOCK == 0 and past % MOBA_BLOCK == 0 and MOBA_BLOCK % page == 0
    assert t % (M_CHUNK * MLSTM_CHUNKS_PER_STEP) == 0 and ns % MLSTM_SAMPLE_SEQS == 0
    n = t + ns

    xp = x_prompt[0]
    xs = x_sample[:, 0]
    w = w_in[0]
    fw = (g_attn[0], w, b_i[0], b_f[0], g_q[0], g_k[0])

    qkv_p, gcol_p, grow_p, sg_p, q32_p, k32_p, k16_p, v32_p, v16_p = _mixer_inputs(
        xp, jnp.arange(t, dtype=jnp.int32), *fw, BF16)
    hm_p, ct_p, n_p, m_p = _mlstm_prompt(qkv_p, gcol_p, grow_p)
    ao_p = _moba_prompt(q32_p, k16_p, v16_p, _block_kmean(k32_p))

    qkv_s, gcol_s, _, sg_s, q32_s, k32_s, _, v32_s, _ = _mixer_inputs(
        xs, jnp.full((ns,), past, jnp.int32), *fw, F32)
    hm_s, c_s, n_s, m_s = _mlstm_sample(qkv_s, gcol_s, state_mlstm_C[0].astype(F32),
                                        state_mlstm_n[0].astype(F32), state_mlstm_m[0].astype(F32))
    ao_s = _moba_sample(q32_s, k32_s, v32_s, cache_k, cache_v, page_table)

    wbm, wba, wout = w_bm[0].astype(BF16), w_ba[0].astype(BF16), w_out[0].astype(BF16)
    mix_p = _mix(hm_p, sg_p, ao_p, g_mh[0], wbm, wba)
    mix_s = _mix(hm_s, sg_s, ao_s, g_mh[0], wbm, wba)
    wr = jnp.pad(w_router[0], ((0, 0), (0, LANES - N_EXPERTS)))
    wr_hi = wr.astype(BF16)
    wr_lo = (wr - wr_hi.astype(F32)).astype(BF16)
    b_r = jnp.pad(b_router[0].astype(F32), (0, LANES - N_EXPERTS)).reshape(1, LANES)
    routed = _resid_router(xp, mix_p, wout, g_ffn[0], wr_hi, wr_lo, b_r, n, 0, None)
    x1, x1t, gate, eid = _resid_router(xs, mix_s, wout, g_ffn[0], wr_hi, wr_lo, b_r, n, t, routed)

    n_slots = n * TOP_K
    n_chunks = (n_slots + N_EXPERTS * (MOE_ROWS - 1)) // MOE_ROWS
    dest, src_tok, n_valid, chunk_expert, next_expert = _moe_plan(eid, n_chunks)
    xs_sorted = _dispatch(x1t, g_ffn[0], src_tok, n_valid, n_chunks)
    act = _moe_up(xs_sorted, w_up[0], b_up[0], chunk_expert, next_expert, n_valid)
    yt = _moe_down(act, w_down[0], b_down[0], chunk_expert, next_expert, n_valid)
    x2, h3 = _combine(x1, gate, dest, yt, g_ple[0])

    wpg, wple = w_pg[0].astype(BF16), w_ple[0].astype(BF16)
    y_p = _ple(x2, h3, p_prompt[0, 0], wpg, wple, 0, t)
    y_s = _ple(x2, h3, p_sample[0, :, 0], wpg, wple, t, ns)

    def heads(a):
        return a.reshape(a.shape[0], A_HEADS, A_HEAD_DIM)

    return (y_p[None], y_s[:, None],
            heads(k32_p)[None, None], heads(v32_p)[None, None],
            jnp.swapaxes(ct_p, 1, 2)[None, None].astype(state_mlstm_C.dtype),
            n_p[:M_HEADS][None, None].astype(state_mlstm_n.dtype),
            m_p[:M_HEADS, 0][None, None].astype(state_mlstm_m.dtype),
            heads(k32_s)[None, :, None], heads(v32_s)[None, :, None],
            c_s[None].astype(state_mlstm_C.dtype), n_s[None].astype(state_mlstm_n.dtype),
            m_s[None].astype(state_mlstm_m.dtype))
```

```python
import functools

import jax
import jax.numpy as jnp
import numpy as np
from jax import lax
from jax.experimental import pallas as pl
from jax.experimental.pallas import tpu as pltpu

F32 = jnp.float32
BF16 = jnp.bfloat16

M_HEADS = 4
M_QK = 128
M_V = 256
M_CHUNK = 64
M_WIDTH = M_HEADS * M_V
A_HEADS = 8
A_HEAD_DIM = 128
A_WIDTH = A_HEADS * A_HEAD_DIM
MOBA_BLOCK = 256
MOBA_TOPK = 3
ROT_DIM = A_HEAD_DIM // 4
ROPE_THETA = 500000.0
N_EXPERTS = 32
TOP_K = 4
SWIGLU_LIMIT = 7.0
SWIGLU_ALPHA = 1.702
EPS = 1e-6
NEG = -1e30

LANES = 128
SUBLANES = 8
VMEM_LIMIT_BYTES = 56 * 1024 * 1024

MOE_ROWS = 256
MLSTM_CHUNKS_PER_STEP = 2
MLSTM_SAMPLE_SEQS = 8
COMBINE_ROWS = 128
MOBA_HEADS_PER_STEP = 4
GATHER_UNROLL = 8


def _pick(n, candidates):
    for c in candidates:
        if n % c == 0:
            return c
    raise ValueError(f"no tile in {candidates} divides {n}")


def _params(*sem):
    return pltpu.CompilerParams(dimension_semantics=sem, vmem_limit_bytes=VMEM_LIMIT_BYTES)


def _dot(a, b):
    return jnp.dot(a, b, preferred_element_type=F32)


def _dot_nt(a, b):
    return lax.dot_general(a, b, (((1,), (1,)), ((), ())), preferred_element_type=F32)


def _dot_tn(a, b):
    return lax.dot_general(a, b, (((0,), (0,)), ((), ())), preferred_element_type=F32)


def _split_bf16(x):
    hi = x.astype(BF16)
    lo = (x - hi.astype(F32)).astype(BF16)
    return hi, lo


def _rms(x, g):
    return x * lax.rsqrt(jnp.mean(x * x, axis=-1, keepdims=True) + EPS) * g


def _log_sigmoid(x):
    return -(jnp.maximum(-x, 0.0) + jnp.log1p(jnp.exp(-jnp.abs(x))))


def _first_argmax(work, lane_f):
    m = jnp.max(work, axis=-1, keepdims=True)
    idx = jnp.min(jnp.where(work == m, lane_f, float(4 * LANES)), axis=-1, keepdims=True)
    return m, idx


def _rmsnorm_kernel(x_ref, g_ref, o_ref):
    o_ref[...] = _rms(x_ref[...], g_ref[...]).astype(o_ref.dtype)


def _rmsnorm_bf16(x, g):
    r, d = x.shape
    tm = _pick(r, (512, 256, 128, 64, 32, 16))
    return pl.pallas_call(
        _rmsnorm_kernel,
        grid=(r // tm,),
        in_specs=[pl.BlockSpec((tm, d), lambda i: (i, 0)), pl.BlockSpec((1, d), lambda i: (0, 0))],
        out_specs=pl.BlockSpec((tm, d), lambda i: (i, 0)),
        out_shape=jax.ShapeDtypeStruct((r, d), BF16),
        compiler_params=_params("parallel"),
        name="rmsnorm_rows",
    )(x, g.reshape(1, d))


def _proj_scale_kernel(h_ref, w_ref, s_ref, o_ref):
    o_ref[...] = (_dot(h_ref[...], w_ref[...]) * s_ref[...]).astype(o_ref.dtype)


def _proj_sigmoid_kernel(h_ref, w_ref, o_ref):
    o_ref[...] = jax.nn.sigmoid(_dot(h_ref[...], w_ref[...])).astype(o_ref.dtype)


def _proj_plain2_kernel(h_ref, w_ref, o32_ref, o16_ref):
    acc = _dot(h_ref[...], w_ref[...])
    o32_ref[...] = acc
    o16_ref[...] = acc.astype(BF16)


def _proj_qk_kernel(h_ref, w_ref, g_ref, c_ref, s1_ref, s2_ref, o32_ref, o16_ref):
    acc = _dot(h_ref[...], w_ref[...])
    g = g_ref[...]
    c, s1, s2 = c_ref[...], s1_ref[...], s2_ref[...]
    for hh in range(acc.shape[1] // A_HEAD_DIM):
        sl = slice(hh * A_HEAD_DIM, (hh + 1) * A_HEAD_DIM)
        y = _rms(acc[:, sl], g)
        up = pltpu.roll(y, A_HEAD_DIM - ROT_DIM // 2, 1)
        dn = pltpu.roll(y, ROT_DIM // 2, 1)
        r = y * c + up * s1 + dn * s2
        o32_ref[:, sl] = r
        o16_ref[:, sl] = r.astype(BF16)


def _proj_gates_kernel(h_ref, wc_ref, wr_ref, bc_ref, br_ref, gc_ref, gr_ref):
    h = h_ref[...]
    zc = _dot(h, wc_ref[...]) + bc_ref[...]
    zr = _dot_nt(wr_ref[...], h) + br_ref[...]
    lane = lax.broadcasted_iota(jnp.int32, zc.shape, 1)
    gc_ref[...] = jnp.where((lane >= M_HEADS) & (lane < 2 * M_HEADS), _log_sigmoid(zc), zc)
    row = lax.broadcasted_iota(jnp.int32, zr.shape, 0)
    gr_ref[...] = jnp.where(row >= M_HEADS, _log_sigmoid(zr), zr)


def _proj_tiles(r, n):
    tm = _pick(r, (1024, 512, 256, 128, 64, 32, 16))
    tn = _pick(n, (1024, 512, 256, 128))
    return tm, tn


def _proj_scale(h, w, colscale, out_dtype):
    r, d = h.shape
    n = w.shape[1]
    tm, tn = _proj_tiles(r, n)
    return pl.pallas_call(
        _proj_scale_kernel,
        grid=(n // tn, r // tm),
        in_specs=[pl.BlockSpec((tm, d), lambda j, i: (i, 0)),
                  pl.BlockSpec((d, tn), lambda j, i: (0, j)),
                  pl.BlockSpec((1, tn), lambda j, i: (0, j))],
        out_specs=pl.BlockSpec((tm, tn), lambda j, i: (i, j)),
        out_shape=jax.ShapeDtypeStruct((r, n), out_dtype),
        compiler_params=_params("parallel", "parallel"),
        name="proj_scale",
    )(h, w, colscale)


def _proj_sigmoid(h, w):
    r, d = h.shape
    n = w.shape[1]
    tm, tn = _proj_tiles(r, n)
    return pl.pallas_call(
        _proj_sigmoid_kernel,
        grid=(n // tn, r // tm),
        in_specs=[pl.BlockSpec((tm, d), lambda j, i: (i, 0)),
                  pl.BlockSpec((d, tn), lambda j, i: (0, j))],
        out_specs=pl.BlockSpec((tm, tn), lambda j, i: (i, j)),
        out_shape=jax.ShapeDtypeStruct((r, n), BF16),
        compiler_params=_params("parallel", "parallel"),
        name="proj_sigmoid",
    )(h, w)


def _proj_plain2(h, w):
    r, d = h.shape
    n = w.shape[1]
    tm, tn = _proj_tiles(r, n)
    return pl.pallas_call(
        _proj_plain2_kernel,
        grid=(n // tn, r // tm),
        in_specs=[pl.BlockSpec((tm, d), lambda j, i: (i, 0)),
                  pl.BlockSpec((d, tn), lambda j, i: (0, j))],
        out_specs=[pl.BlockSpec((tm, tn), lambda j, i: (i, j)),
                   pl.BlockSpec((tm, tn), lambda j, i: (i, j))],
        out_shape=[jax.ShapeDtypeStruct((r, n), F32), jax.ShapeDtypeStruct((r, n), BF16)],
        compiler_params=_params("parallel", "parallel"),
        name="proj_plain",
    )(h, w)


def _proj_qk(h, w, g, rope_c, rope_s1, rope_s2):
    r, d = h.shape
    n = w.shape[1]
    tm, tn = _proj_tiles(r, n)
    hd = A_HEAD_DIM
    return pl.pallas_call(
        _proj_qk_kernel,
        grid=(n // tn, r // tm),
        in_specs=[pl.BlockSpec((tm, d), lambda j, i: (i, 0)),
                  pl.BlockSpec((d, tn), lambda j, i: (0, j)),
                  pl.BlockSpec((1, hd), lambda j, i: (0, 0)),
                  pl.BlockSpec((tm, hd), lambda j, i: (i, 0)),
                  pl.BlockSpec((tm, hd), lambda j, i: (i, 0)),
                  pl.BlockSpec((tm, hd), lambda j, i: (i, 0))],
        out_specs=[pl.BlockSpec((tm, tn), lambda j, i: (i, j)),
                   pl.BlockSpec((tm, tn), lambda j, i: (i, j))],
        out_shape=[jax.ShapeDtypeStruct((r, n), F32), jax.ShapeDtypeStruct((r, n), BF16)],
        compiler_params=_params("parallel", "parallel"),
        name="proj_qk_norm_rope",
    )(h, w, g.reshape(1, hd), rope_c, rope_s1, rope_s2)


def _proj_gates(h, wc, wr, bc, br):
    r, d = h.shape
    tm = _pick(r, (1024, 512, 256, 128)) if r % LANES == 0 else r
    g2 = 2 * M_HEADS
    return pl.pallas_call(
        _proj_gates_kernel,
        grid=(r // tm,),
        in_specs=[pl.BlockSpec((tm, d), lambda i: (i, 0)),
                  pl.BlockSpec((d, LANES), lambda i: (0, 0)),
                  pl.BlockSpec((g2, d), lambda i: (0, 0)),
                  pl.BlockSpec((1, LANES), lambda i: (0, 0)),
                  pl.BlockSpec((g2, 1), lambda i: (0, 0))],
        out_specs=[pl.BlockSpec((tm, LANES), lambda i: (i, 0)),
                   pl.BlockSpec((g2, tm), lambda i: (0, i))],
        out_shape=[jax.ShapeDtypeStruct((r, LANES), F32), jax.ShapeDtypeStruct((g2, r), F32)],
        compiler_params=_params("parallel"),
        name="proj_gates",
    )(h, wc, wr, bc, br)


def _mlstm_prompt_kernel(q_ref, k_ref, v_ref, gc_ref, gr_ref, h_ref, ct_out, n_out, m_out,
                         ct_s, n_s, m_s):
    step = pl.program_id(0)

    @pl.when(step == 0)
    def _():
        ct_s[...] = jnp.zeros_like(ct_s)
        n_s[...] = jnp.zeros_like(n_s)
        m_s[...] = jnp.zeros_like(m_s)

    ln = M_CHUNK
    row = lax.broadcasted_iota(jnp.int32, (ln, ln), 0)
    col = lax.broadcasted_iota(jnp.int32, (ln, ln), 1)
    causal = col <= row
    upto = row <= col
    pre = {}
    for cc in range(MLSTM_CHUNKS_PER_STEP):
        rs = slice(cc * ln, (cc + 1) * ln)
        for hh in range(M_HEADS):
            ig_c = gc_ref[rs, hh:hh + 1]
            lf_c = gc_ref[rs, M_HEADS + hh:M_HEADS + hh + 1]
            ig_r = gr_ref[hh:hh + 1, rs]
            lf_r = gr_ref[M_HEADS + hh:M_HEADS + hh + 1, rs]
            bcum_c = jnp.sum(jnp.where(causal, lf_r, 0.0), axis=1, keepdims=True)
            bcum_r = jnp.sum(jnp.where(upto, lf_c, 0.0), axis=0, keepdims=True)
            dmat = jnp.where(causal, bcum_c - bcum_r + ig_r, NEG)
            qh = q_ref[rs, hh * M_QK:(hh + 1) * M_QK]
            kh = k_ref[rs, hh * M_QK:(hh + 1) * M_QK]
            b_last = bcum_c[ln - 1:ln, :]
            pre[cc, hh] = dict(
                bcum_c=bcum_c, dmat=dmat, dmax=jnp.max(dmat, axis=1, keepdims=True),
                qk=_dot_nt(qh, kh), b_last=b_last,
                d_last_c=b_last - bcum_c + ig_c,
                d_last_max=jnp.max(b_last - bcum_r + ig_r, axis=1, keepdims=True))

    for hh in range(M_HEADS):
        ct = ct_s[hh]
        n_row = n_s[hh:hh + 1, :]
        m_prev = m_s[hh:hh + 1, 0:1]
        for cc in range(MLSTM_CHUNKS_PER_STEP):
            rs = slice(cc * ln, (cc + 1) * ln)
            u = pre[cc, hh]
            qh = q_ref[rs, hh * M_QK:(hh + 1) * M_QK]
            kh = k_ref[rs, hh * M_QK:(hh + 1) * M_QK]
            vh = v_ref[rs, hh * M_V:(hh + 1) * M_V]
            a = u["bcum_c"] + m_prev
            m_t = jnp.maximum(a, u["dmax"])
            w_inter = jnp.exp(a - m_t)
            s = u["qk"] * jnp.exp(u["dmat"] - m_t)
            num = w_inter * _dot(qh, ct.astype(BF16)) + _dot(s.astype(BF16), vh)
            den = (w_inter * jnp.sum(qh.astype(F32) * n_row, axis=1, keepdims=True)
                   + jnp.sum(s, axis=1, keepdims=True))
            h_ref[rs, hh * M_V:(hh + 1) * M_V] = num / jnp.maximum(jnp.abs(den), jnp.exp(-m_t))
            a_last = u["b_last"] + m_prev
            m_new = jnp.maximum(a_last, u["d_last_max"])
            w_c = jnp.exp(a_last - m_new)
            w_j = jnp.exp(u["d_last_c"] - m_new)
            vw = (vh.astype(F32) * w_j).astype(BF16)
            ct = w_c * ct + _dot_tn(kh, vw)
            n_row = w_c * n_row + jnp.sum(kh.astype(F32) * w_j, axis=0, keepdims=True)
            m_prev = m_new
        ct_s[hh] = ct
        n_s[hh:hh + 1, :] = n_row
        m_s[hh:hh + 1, :] = jnp.broadcast_to(m_prev, (1, LANES))

    @pl.when(step == pl.num_programs(0) - 1)
    def _():
        ct_out[...] = ct_s[...]
        n_out[...] = n_s[...]
        m_out[...] = m_s[...]


def _mlstm_prompt(qkv, gcol, grow):
    t = qkv.shape[0]
    rows = M_CHUNK * MLSTM_CHUNKS_PER_STEP
    kq = M_HEADS * M_QK
    return pl.pallas_call(
        _mlstm_prompt_kernel,
        grid=(t // rows,),
        in_specs=[pl.BlockSpec((rows, kq), lambda c: (c, 0)),
                  pl.BlockSpec((rows, kq), lambda c: (c, 1)),
                  pl.BlockSpec((rows, M_WIDTH), lambda c: (c, (2 * kq) // M_WIDTH)),
                  pl.BlockSpec((rows, LANES), lambda c: (c, 0)),
                  pl.BlockSpec((2 * M_HEADS, rows), lambda c: (0, c))],
        out_specs=[pl.BlockSpec((rows, M_WIDTH), lambda c: (c, 0)),
                   pl.BlockSpec((M_HEADS, M_QK, M_V), lambda c: (0, 0, 0)),
                   pl.BlockSpec((SUBLANES, LANES), lambda c: (0, 0)),
                   pl.BlockSpec((SUBLANES, LANES), lambda c: (0, 0))],
        out_shape=[jax.ShapeDtypeStruct((t, M_WIDTH), F32),
                   jax.ShapeDtypeStruct((M_HEADS, M_QK, M_V), F32),
                   jax.ShapeDtypeStruct((SUBLANES, LANES), F32),
                   jax.ShapeDtypeStruct((SUBLANES, LANES), F32)],
        scratch_shapes=[pltpu.VMEM((M_HEADS, M_QK, M_V), F32),
                        pltpu.VMEM((SUBLANES, LANES), F32),
                        pltpu.VMEM((SUBLANES, LANES), F32)],
        compiler_params=_params("arbitrary"),
        name="mlstm_prompt_scan",
    )(qkv, qkv, qkv, gcol, grow)


def _mlstm_sample_kernel(qkv_ref, gc_ref, c_ref, n_ref, m_ref, h_ref, co_ref, no_ref, mo_ref):
    kq = M_HEADS * M_QK
    row8 = lax.broadcasted_iota(jnp.int32, (SUBLANES, 1), 0)
    mrow = lax.broadcasted_iota(jnp.int32, mo_ref.shape, 0)
    mcol = lax.broadcasted_iota(jnp.int32, mo_ref.shape, 1)
    m_all = m_ref[...]
    for s in range(MLSTM_SAMPLE_SEQS):
        for hh in range(M_HEADS):
            q = qkv_ref[s:s + 1, hh * M_QK:(hh + 1) * M_QK]
            k = qkv_ref[s:s + 1, kq + hh * M_QK:kq + (hh + 1) * M_QK]
            v = qkv_ref[s:s + 1, 2 * kq + hh * M_V:2 * kq + (hh + 1) * M_V]
            ig = gc_ref[s:s + 1, hh:hh + 1]
            lf = gc_ref[s:s + 1, M_HEADS + hh:M_HEADS + hh + 1]
            m_prev = m_ref[s:s + 1, hh:hh + 1]
            c = c_ref[s, hh]
            n_row = n_ref[s, hh:hh + 1, :]
            a = lf + m_prev
            m_t = jnp.maximum(a, ig)
            w_c = jnp.exp(a - m_t)
            w_j = jnp.exp(ig - m_t)
            sc = jnp.sum(q * k, axis=1, keepdims=True) * w_j
            q8 = jnp.broadcast_to(q, (SUBLANES, M_QK)).astype(BF16)
            cq = _dot_nt(q8, c.astype(BF16))[0:1, :]
            num = w_c * cq + sc * v
            den = w_c * jnp.sum(n_row * q, axis=1, keepdims=True) + sc
            h_ref[s:s + 1, hh * M_V:(hh + 1) * M_V] = num / jnp.maximum(jnp.abs(den), jnp.exp(-m_t))
            vw8 = jnp.where(row8 == 0, v * w_j, 0.0).astype(BF16)
            k8 = jnp.where(row8 == 0, k, 0.0).astype(BF16)
            co_ref[s, hh] = w_c * c + _dot_tn(vw8, k8)
            no_ref[s, hh:hh + 1, :] = w_c * n_row + w_j * k
            m_all = jnp.where((mrow == s) & (mcol == hh), m_t, m_all)
    mo_ref[...] = m_all


def _mlstm_sample(qkv, gcol, c0, n0, m0):
    ns = qkv.shape[0]
    sb = MLSTM_SAMPLE_SEQS
    wq = qkv.shape[1]
    return pl.pallas_call(
        _mlstm_sample_kernel,
        grid=(ns // sb,),
        in_specs=[pl.BlockSpec((sb, wq), lambda i: (i, 0)),
                  pl.BlockSpec((sb, LANES), lambda i: (i, 0)),
                  pl.BlockSpec((sb, M_HEADS, M_V, M_QK), lambda i: (i, 0, 0, 0)),
                  pl.BlockSpec((sb, M_HEADS, M_QK), lambda i: (i, 0, 0)),
                  pl.BlockSpec((sb, M_HEADS), lambda i: (i, 0))],
        out_specs=[pl.BlockSpec((sb, M_WIDTH), lambda i: (i, 0)),
                   pl.BlockSpec((sb, M_HEADS, M_V, M_QK), lambda i: (i, 0, 0, 0)),
                   pl.BlockSpec((sb, M_HEADS, M_QK), lambda i: (i, 0, 0)),
                   pl.BlockSpec((sb, M_HEADS), lambda i: (i, 0))],
        out_shape=[jax.ShapeDtypeStruct((ns, M_WIDTH), F32),
                   jax.ShapeDtypeStruct(c0.shape, F32),
                   jax.ShapeDtypeStruct(n0.shape, F32),
                   jax.ShapeDtypeStruct(m0.shape, F32)],
        compiler_params=_params("parallel"),
        name="mlstm_sample_step",
    )(qkv, gcol, c0, n0, m0)


def _kmean_kernel(k_ref, o_ref):
    o_ref[0] = jnp.mean(k_ref[...], axis=0, keepdims=True)


def _block_kmean(k32):
    t, w = k32.shape
    nb = t // MOBA_BLOCK
    out = pl.pallas_call(
        _kmean_kernel,
        grid=(nb,),
        in_specs=[pl.BlockSpec((MOBA_BLOCK, w), lambda b: (b, 0))],
        out_specs=pl.BlockSpec((1, 1, w), lambda b: (b, 0, 0)),
        out_shape=jax.ShapeDtypeStruct((nb, 1, w), F32),
        compiler_params=_params("parallel"),
        name="moba_block_kmean",
    )(k32)
    return out.reshape(nb, w)


def _moba_select(sc, own, lane_i, lane_f):
    work = jnp.where(lane_i < own, sc, NEG)
    sel = jnp.zeros(sc.shape, dtype=jnp.bool_)
    for _ in range(MOBA_TOPK):
        _, idx = _first_argmax(work, lane_f)
        hit = lane_f == idx
        sel = sel | hit
        work = jnp.where(hit, -jnp.inf, work)
    return sel & (lane_i < own)


def _moba_prompt_kernel(q_ref, k_ref, v_ref, e_ref, km_ref, o_ref):
    i = pl.program_id(1)
    bs = MOBA_BLOCK
    pw = 2 * bs
    hd = A_HEAD_DIM
    heads = range(MOBA_HEADS_PER_STEP)
    ones = jnp.ones((pw, hd), BF16)
    lane_i = lax.broadcasted_iota(jnp.int32, (bs, LANES), 1)
    lane_f = lane_i.astype(F32)

    def head_slice(hh):
        return slice(hh * hd, (hh + 1) * hd)

    q_aug = []
    for hh in heads:
        q = q_ref[:, head_slice(hh)]
        qh, ql = _split_bf16(q)
        kmh, kml = _split_bf16(km_ref[:, head_slice(hh)])
        sc = _dot_nt(qh, kmh) + (_dot_nt(qh, kml) + _dot_nt(ql, kmh))
        sel = _moba_select(sc, i, lane_i, lane_f)
        bias = jnp.where(sel | (lane_i >= i), 0.0, NEG)
        q_aug.append(jnp.concatenate([(q * (hd ** -0.5)).astype(BF16), bias.astype(BF16)], axis=1))

    def pair_scores(p, hh, onehot):
        off = pl.multiple_of(p * pw, pw)
        k_aug = jnp.concatenate([k_ref[pl.ds(off, pw), head_slice(hh)], onehot], axis=1)
        v_aug = jnp.concatenate([v_ref[pl.ds(off, pw), head_slice(hh)], ones], axis=1)
        return _dot_nt(q_aug[hh], k_aug), v_aug

    p_own = i // 2
    row = lax.broadcasted_iota(jnp.int32, (bs, pw), 0)
    col = lax.broadcasted_iota(jnp.int32, (bs, pw), 1)
    causal = col + (p_own * pw - i * bs) <= row
    onehot_own = e_ref[pl.ds(pl.multiple_of(p_own * pw, pw), pw), :]
    init = []
    for hh in heads:
        s, v_aug = pair_scores(p_own, hh, onehot_own)
        s = jnp.where(causal, s, NEG)
        m0 = jnp.max(s, axis=1, keepdims=True)
        init += [m0, _dot(jnp.exp(s - m0).astype(BF16), v_aug)]

    def body(p, carry):
        onehot = e_ref[pl.ds(pl.multiple_of(p * pw, pw), pw), :]
        out = []
        for hh in heads:
            m_i, acc = carry[2 * hh], carry[2 * hh + 1]
            sj, vj = pair_scores(p, hh, onehot)
            m_n = jnp.maximum(m_i, jnp.max(sj, axis=1, keepdims=True))
            out += [m_n, jnp.exp(m_i - m_n) * acc + _dot(jnp.exp(sj - m_n).astype(BF16), vj)]
        return tuple(out)

    final = lax.fori_loop(0, p_own, body, tuple(init))
    for hh in heads:
        acc = final[2 * hh + 1]
        o_ref[:, head_slice(hh)] = (acc[:, :hd] / acc[:, hd:]).astype(o_ref.dtype)


def _moba_prompt(q32, k16, v16, kmean):
    t = q32.shape[0]
    nb = t // MOBA_BLOCK
    hd = MOBA_HEADS_PER_STEP * A_HEAD_DIM
    assert nb <= LANES and nb % 2 == 0 and A_HEADS % MOBA_HEADS_PER_STEP == 0
    block_onehot = (jnp.arange(t, dtype=jnp.int32)[:, None] // MOBA_BLOCK
                    == jnp.arange(LANES, dtype=jnp.int32)[None, :]).astype(BF16)
    kmean_pad = jnp.pad(kmean, ((0, LANES - nb), (0, 0)))
    return pl.pallas_call(
        _moba_prompt_kernel,
        grid=(A_HEADS // MOBA_HEADS_PER_STEP, nb),
        in_specs=[pl.BlockSpec((MOBA_BLOCK, hd), lambda h, i: (i, h)),
                  pl.BlockSpec((t, hd), lambda h, i: (0, h)),
                  pl.BlockSpec((t, hd), lambda h, i: (0, h)),
                  pl.BlockSpec((t, LANES), lambda h, i: (0, 0)),
                  pl.BlockSpec((LANES, hd), lambda h, i: (0, h))],
        out_specs=pl.BlockSpec((MOBA_BLOCK, hd), lambda h, i: (i, h)),
        out_shape=jax.ShapeDtypeStruct((t, A_WIDTH), BF16),
        compiler_params=_params("parallel", "arbitrary"),
        name="moba_prompt_attention",
    )(q32, k16, v16, block_onehot, kmean_pad)


def _moba_sample_kernel(n_pages, pages_per_block, pt_ref, q_ref, kn_ref, vn_ref, *refs):
    del pt_ref
    kp = refs[:n_pages]
    vp = refs[n_pages:2 * n_pages]
    o_ref = refs[2 * n_pages]
    nbp = n_pages // pages_per_block
    hd = A_HEAD_DIM
    rows = kp[0].shape[2]
    page = rows // A_HEADS
    q = q_ref[0]
    lane_i = lax.broadcasted_iota(jnp.int32, (A_HEADS, LANES), 1)

    sc = jnp.zeros((A_HEADS, LANES), F32)
    k16 = []
    for b in range(nbp):
        ksum = jnp.zeros((A_HEADS, hd), F32)
        for pp in range(pages_per_block):
            kf = kp[b * pages_per_block + pp][0, 0]
            ksum = ksum + jnp.sum(kf.reshape(page, A_HEADS, hd), axis=0)
            k16.append(kf.astype(BF16))
        kmean = ksum / float(MOBA_BLOCK)
        sc = jnp.where(lane_i == b, jnp.sum(q * kmean, axis=1, keepdims=True), sc)
    sel = _moba_select(sc, nbp, lane_i, lane_i.astype(F32)).astype(F32)

    qs = q * (hd ** -0.5)
    own = jnp.sum(qs * kn_ref[0], axis=1, keepdims=True)
    s = _dot_nt(qs.astype(BF16), jnp.concatenate(k16, axis=0))
    head_row = lax.broadcasted_iota(jnp.int32, (A_HEADS, rows), 0)
    row_head = lax.broadcasted_iota(jnp.int32, (A_HEADS, rows), 1) % A_HEADS
    mine = head_row == row_head
    s = jnp.concatenate(
        [jnp.where(mine & (sel[:, pg // pages_per_block:pg // pages_per_block + 1] > 0.5),
                   s[:, pg * rows:(pg + 1) * rows], NEG) for pg in range(n_pages)], axis=1)
    m = jnp.maximum(own, jnp.max(s, axis=1, keepdims=True))
    p = jnp.exp(s - m)
    p_own = jnp.exp(own - m)
    l = jnp.sum(p, axis=1, keepdims=True) + p_own
    v16 = jnp.concatenate([vp[pg][0, 0].astype(BF16) for pg in range(n_pages)], axis=0)
    o_ref[0] = (_dot(p.astype(BF16), v16) + p_own * vn_ref[0]) / l


def _moba_sample(q32, kn32, vn32, cache_k, cache_v, page_table):
    ns = q32.shape[0]
    page = cache_k.shape[2]
    n_pages = page_table.shape[1]
    ppb = MOBA_BLOCK // page
    head_spec = pl.BlockSpec((1, A_HEADS, A_HEAD_DIM), lambda s, pt: (s, 0, 0))

    def page_spec(p):
        return pl.BlockSpec((1, 1, page * A_HEADS, A_HEAD_DIM), lambda s, pt: (0, pt[s, p], 0, 0))

    def rows(cache):
        return cache.reshape(cache.shape[0], cache.shape[1], page * A_HEADS, A_HEAD_DIM)

    def heads(a):
        return a.reshape(ns, A_HEADS, A_HEAD_DIM)

    grid_spec = pltpu.PrefetchScalarGridSpec(
        num_scalar_prefetch=1,
        grid=(ns,),
        in_specs=[head_spec, head_spec, head_spec] + [page_spec(p) for p in range(n_pages)] * 2,
        out_specs=head_spec,
    )
    out = pl.pallas_call(
        functools.partial(_moba_sample_kernel, n_pages, ppb),
        grid_spec=grid_spec,
        out_shape=jax.ShapeDtypeStruct((ns, A_HEADS, A_HEAD_DIM), F32),
        compiler_params=_params("parallel"),
        name="moba_sample_attention",
    )(page_table, heads(q32), heads(kn32), heads(vn32), *([rows(cache_k)] * n_pages), *([rows(cache_v)] * n_pages))
    return out.reshape(ns, A_WIDTH)


def _mix_kernel(hm_ref, om_ref, ao_ref, gm_ref, ga_ref, gmh_ref, wbm_ref, wba_ref, o_ref, hs_ref):
    @pl.when(pl.program_id(1) == 0)
    def _():
        for hh in range(M_HEADS):
            sl = slice(hh * M_V, (hh + 1) * M_V)
            y = _rms(hm_ref[:, sl], gmh_ref[:, sl])
            hs_ref[:, sl] = (y * om_ref[:, sl].astype(F32)).astype(BF16)

    t1 = _dot(hs_ref[...], wbm_ref[...])
    t2 = _dot(ao_ref[...].astype(BF16), wba_ref[...])
    o_ref[...] = (gm_ref[...].astype(F32) * t1 + ga_ref[...].astype(F32) * t2).astype(o_ref.dtype)


def _mix(hm, sg, ao, g_mh, w_bm, w_ba):
    r = hm.shape[0]
    d = w_bm.shape[1]
    tm = _pick(r, (512, 256, 128, 64, 32, 16))
    tn = _pick(d, (512, 256, 128))
    gm0 = M_WIDTH // tn
    ga0 = (M_WIDTH + d) // tn
    return pl.pallas_call(
        _mix_kernel,
        grid=(r // tm, d // tn),
        in_specs=[pl.BlockSpec((tm, M_WIDTH), lambda i, j: (i, 0)),
                  pl.BlockSpec((tm, M_WIDTH), lambda i, j: (i, 0)),
                  pl.BlockSpec((tm, A_WIDTH), lambda i, j: (i, 0)),
                  pl.BlockSpec((tm, tn), lambda i, j: (i, gm0 + j)),
                  pl.BlockSpec((tm, tn), lambda i, j: (i, ga0 + j)),
                  pl.BlockSpec((1, M_WIDTH), lambda i, j: (0, 0)),
                  pl.BlockSpec((M_WIDTH, tn), lambda i, j: (0, j)),
                  pl.BlockSpec((A_WIDTH, tn), lambda i, j: (0, j))],
        out_specs=pl.BlockSpec((tm, tn), lambda i, j: (i, j)),
        out_shape=jax.ShapeDtypeStruct((r, d), BF16),
        scratch_shapes=[pltpu.VMEM((tm, M_WIDTH), BF16)],
        compiler_params=_params("parallel", "arbitrary"),
        name="mixer_merge",
    )(hm, sg, ao, sg, sg, g_mh.reshape(1, M_WIDTH), w_bm, w_ba)


def _store_row_tiles(o_ref, x, rows):
    s = x.shape[1] // LANES
    for c in range(s):
        o_ref[pl.ds(c, rows, stride=s), :] = x[:, c * LANES:(c + 1) * LANES]


def _resid_router_kernel(x_ref, mix_ref, wout_ref, gffn_ref, wrh_ref, wrl_ref, br_ref,
                         x1_ref, x1t_ref, gate_ref, eid_ref):
    x1 = x_ref[...] + _dot(mix_ref[...], wout_ref[...])
    x1_ref[...] = x1
    _store_row_tiles(x1t_ref, x1, x1.shape[0])
    hh, hl = _split_bf16(_rms(x1, gffn_ref[...]))
    wrh = wrh_ref[...]
    logits = _dot(hh, wrh) + (_dot(hh, wrl_ref[...]) + _dot(hl, wrh)) + br_ref[...]
    lane_i = lax.broadcasted_iota(jnp.int32, logits.shape, 1)
    lane_f = lane_i.astype(F32)
    work = jnp.where(lane_i < N_EXPERTS, logits, -jnp.inf)
    vals, ids = [], []
    for _ in range(TOP_K):
        m, idx = _first_argmax(work, lane_f)
        vals.append(m)
        ids.append(idx)
        work = jnp.where(lane_f == idx, -jnp.inf, work)
    es = [jnp.exp(v - vals[0]) for v in vals]
    den = es[0]
    for e in es[1:]:
        den = den + e
    gate = jnp.zeros(logits.shape, F32)
    eid = jnp.zeros(logits.shape, F32)
    for k in range(TOP_K):
        gate = jnp.where(lane_i == k, es[k] / den, gate)
        eid = jnp.where(lane_i == k, ids[k], eid)
    gate_ref[...] = gate
    eid_ref[...] = eid.astype(jnp.int32)


def _resid_router(x, mix, w_out, g_ffn, wr_hi, wr_lo, b_r, n_total, row0, prev):
    r, d = x.shape
    s = d // LANES
    tm = _pick(r, (256, 128, 64, 32, 16))
    assert row0 % tm == 0
    b0 = row0 // tm
    out_shape = [jax.ShapeDtypeStruct((n_total, d), F32),
                 jax.ShapeDtypeStruct((n_total * s, LANES), F32),
                 jax.ShapeDtypeStruct((n_total, LANES), F32),
                 jax.ShapeDtypeStruct((n_total, LANES), jnp.int32)]
    out_specs = [pl.BlockSpec((tm, d), lambda i: (b0 + i, 0)),
                 pl.BlockSpec((tm * s, LANES), lambda i: (b0 + i, 0)),
                 pl.BlockSpec((tm, LANES), lambda i: (b0 + i, 0)),
                 pl.BlockSpec((tm, LANES), lambda i: (b0 + i, 0))]
    in_specs = [pl.BlockSpec((tm, d), lambda i: (i, 0)),
                pl.BlockSpec((tm, d), lambda i: (i, 0)),
                pl.BlockSpec((d, d), lambda i: (0, 0)),
                pl.BlockSpec((1, d), lambda i: (0, 0)),
                pl.BlockSpec((d, LANES), lambda i: (0, 0)),
                pl.BlockSpec((d, LANES), lambda i: (0, 0)),
                pl.BlockSpec((1, LANES), lambda i: (0, 0))]
    args = [x, mix, w_out, g_ffn.reshape(1, d), wr_hi, wr_lo, b_r]
    n_in = len(args)
    kern = _resid_router_kernel
    aliases = {}
    if prev is not None:
        in_specs = in_specs + [pl.BlockSpec(memory_space=pl.ANY)] * len(prev)
        args = args + list(prev)
        aliases = {n_in + k: k for k in range(len(prev))}

        def kern(*refs):
            _resid_router_kernel(*refs[:n_in], *refs[n_in + len(prev):])

    return pl.pallas_call(
        kern,
        grid=(r // tm,),
        in_specs=in_specs,
        out_specs=out_specs,
        out_shape=out_shape,
        input_output_aliases=aliases,
        compiler_params=_params("parallel"),
        name="residual_router_topk",
    )(*args)


def _row_pitch(s):
    return s + 1 if s % 2 == 0 else s


def _gathered_rows(buf, rows, s):
    return jnp.concatenate([buf[pl.ds(c, rows, stride=_row_pitch(s)), :] for c in range(s)], axis=1)


def _start_row_gathers(src_hbm, idx_ref, idx_base, idx_step, dst, rows, s, sem):
    pitch = _row_pitch(s)
    trips = rows // GATHER_UNROLL

    def trip(i, carry):
        for u in range(GATHER_UNROLL):
            r = u * trips + i
            idx = idx_ref[idx_base + r * idx_step]
            pltpu.make_async_copy(src_hbm.at[pl.ds(idx * s, s)], dst.at[pl.ds(r * pitch, s)],
                                  sem).start(priority=u % 2)
        return carry

    lax.fori_loop(0, trips, trip, 0)


def _wait_row_gathers(dst, rows, s, sem):
    landed = dst.at[pl.ds(0, rows * s)]
    pltpu.make_async_copy(landed, landed, sem).wait()


def _dispatch_kernel(nv_ref, tok_ref, tok_next_ref, g_ref, xt_hbm, o_ref, buf, sem):
    c = pl.program_id(0)
    rows = o_ref.shape[0]
    s = o_ref.shape[1] // LANES
    slot = c % 2

    @pl.when(c == 0)
    def _():
        _start_row_gathers(xt_hbm, tok_ref, 0, 1, buf.at[0], rows, s, sem.at[0])

    @pl.when(c + 1 < nv_ref[0])
    def _():
        _start_row_gathers(xt_hbm, tok_next_ref, 0, 1, buf.at[1 - slot], rows, s, sem.at[1 - slot])

    @pl.when(c < nv_ref[0])
    def _():
        _wait_row_gathers(buf.at[slot], rows, s, sem.at[slot])
        o_ref[...] = _rms(_gathered_rows(buf.at[slot], rows, s), g_ref[...]).astype(o_ref.dtype)


def _dispatch(x1t, g_ffn, src_tok, n_valid, n_chunks):
    d = g_ffn.shape[0]
    s = d // LANES
    rows = MOE_ROWS
    assert rows % GATHER_UNROLL == 0
    grid_spec = pltpu.PrefetchScalarGridSpec(
        num_scalar_prefetch=1,
        grid=(n_chunks,),
        in_specs=[pl.BlockSpec((rows,), lambda c, nv: (jnp.minimum(c, nv[0] - 1),),
                               memory_space=pltpu.SMEM),
                  pl.BlockSpec((rows,), lambda c, nv: (jnp.minimum(c + 1, nv[0] - 1),),
                               memory_space=pltpu.SMEM),
                  pl.BlockSpec((1, d), lambda c, nv: (0, 0)),
                  pl.BlockSpec(memory_space=pl.ANY)],
        out_specs=pl.BlockSpec((rows, d), lambda c, nv: (jnp.minimum(c, nv[0] - 1), 0)),
        scratch_shapes=[pltpu.VMEM((2, rows * _row_pitch(s), LANES), F32), pltpu.SemaphoreType.DMA((2,))],
    )
    return pl.pallas_call(
        _dispatch_kernel,
        grid_spec=grid_spec,
        out_shape=jax.ShapeDtypeStruct((n_chunks * rows, d), BF16),
        compiler_params=_params("arbitrary"),
        name="moe_dispatch_gather",
    )(n_valid, src_tok, src_tok, g_ffn.reshape(1, d), x1t)


def _expert_tile_schedule(ce_ref, nx_ref, nv_ref, c, j, n_pass):
    first = (c == 0) | (ce_ref[c] != ce_ref[jnp.maximum(c - 1, 0)])
    more_in_pass = nx_ref[c] >= 0
    has_next = more_in_pass | (j + 1 < n_pass)
    next_e = jnp.where(more_in_pass, nx_ref[c], ce_ref[0])
    next_j = jnp.where(more_in_pass, j, j + 1)
    return (c < nv_ref[0]) & first, has_next, next_j, next_e


def _moe_up_kernel(ce_ref, nx_ref, nv_ref, x_ref, bg_ref, bl_ref, w_hbm, o_ref, raw, w16, sem):
    j = pl.program_id(0)
    c = pl.program_id(1)
    tf = raw.shape[2]
    n_pass = pl.num_programs(0)
    dff = n_pass * tf

    def fetch(jj, e, k):
        col = pl.multiple_of(k * dff + jj * tf, LANES)
        return pltpu.make_async_copy(w_hbm.at[e, :, pl.ds(col, tf)], raw.at[k], sem.at[k])

    first, has_next, next_j, next_e = _expert_tile_schedule(ce_ref, nx_ref, nv_ref, c, j, n_pass)

    @pl.when(first & (j == 0) & (c == 0))
    def _():
        for k in range(2):
            fetch(0, ce_ref[0], k).start()

    @pl.when(first)
    def _():
        for k in range(2):
            fetch(j, ce_ref[c], k).wait()
            w16[k] = raw[k].astype(BF16)

    @pl.when(first & has_next)
    def _():
        for k in range(2):
            fetch(next_j, next_e, k).start()

    @pl.when(c < nv_ref[0])
    def _():
        x = x_ref[...]
        gate = jnp.minimum(_dot(x, w16[0]) + bg_ref[0], SWIGLU_LIMIT)
        lin = jnp.clip(_dot(x, w16[1]) + bl_ref[0], -SWIGLU_LIMIT, SWIGLU_LIMIT)
        o_ref[...] = (gate * jax.nn.sigmoid(SWIGLU_ALPHA * gate) * (lin + 1.0)).astype(o_ref.dtype)


def _moe_up(xs, w_up, b_up, chunk_expert, next_expert, n_valid):
    rows = MOE_ROWS
    n_chunks = xs.shape[0] // rows
    e, d, f2 = w_up.shape
    dff = f2 // 2
    tf = _pick(dff, (1024, 512, 256, 128))
    nf = dff // tf

    def last(c, nv):
        return jnp.minimum(c, nv[0] - 1)

    grid_spec = pltpu.PrefetchScalarGridSpec(
        num_scalar_prefetch=3,
        grid=(nf, n_chunks),
        in_specs=[pl.BlockSpec((rows, d), lambda j, c, ce, nx, nv: (last(c, nv), 0)),
                  pl.BlockSpec((1, 1, tf), lambda j, c, ce, nx, nv: (ce[c], 0, j)),
                  pl.BlockSpec((1, 1, tf), lambda j, c, ce, nx, nv: (ce[c], 0, nf + j)),
                  pl.BlockSpec(memory_space=pl.ANY)],
        out_specs=pl.BlockSpec((rows, tf), lambda j, c, ce, nx, nv: (last(c, nv), j)),
        scratch_shapes=[pltpu.VMEM((2, d, tf), F32), pltpu.VMEM((2, d, tf), BF16),
                        pltpu.SemaphoreType.DMA((2,))],
    )
    b3 = b_up.reshape(e, 1, f2)
    return pl.pallas_call(
        _moe_up_kernel,
        grid_spec=grid_spec,
        out_shape=jax.ShapeDtypeStruct((n_chunks * rows, dff), BF16),
        compiler_params=_params("arbitrary", "arbitrary"),
        name="moe_up_swiglu",
    )(chunk_expert, next_expert, n_valid, xs, b3, b3, w_up)


def _moe_down_kernel(ce_ref, nx_ref, nv_ref, a_ref, b_ref, w_hbm, o_ref, raw, w16, sem):
    c = pl.program_id(0)

    def fetch(e):
        return pltpu.make_async_copy(w_hbm.at[e], raw, sem)

    first, has_next, _, next_e = _expert_tile_schedule(ce_ref, nx_ref, nv_ref, c, 0, 1)

    @pl.when(first & (c == 0))
    def _():
        fetch(ce_ref[0]).start()

    @pl.when(first)
    def _():
        fetch(ce_ref[c]).wait()
        w16[...] = raw[...].astype(BF16)

    @pl.when(first & has_next)
    def _():
        fetch(next_e).start()

    @pl.when(c < nv_ref[0])
    def _():
        y = _dot(a_ref[...], w16[...]) + b_ref[0]
        _store_row_tiles(o_ref, y, y.shape[0])


def _moe_down(act, w_down, b_down, chunk_expert, next_expert, n_valid):
    rows = MOE_ROWS
    n_chunks = act.shape[0] // rows
    e, dff, d = w_down.shape
    s = d // LANES

    def last(c, nv):
        return jnp.minimum(c, nv[0] - 1)

    grid_spec = pltpu.PrefetchScalarGridSpec(
        num_scalar_prefetch=3,
        grid=(n_chunks,),
        in_specs=[pl.BlockSpec((rows, dff), lambda c, ce, nx, nv: (last(c, nv), 0)),
                  pl.BlockSpec((1, 1, d), lambda c, ce, nx, nv: (ce[c], 0, 0)),
                  pl.BlockSpec(memory_space=pl.ANY)],
        out_specs=pl.BlockSpec((rows * s, LANES), lambda c, ce, nx, nv: (last(c, nv), 0)),
        scratch_shapes=[pltpu.VMEM((dff, d), F32), pltpu.VMEM((dff, d), BF16),
                        pltpu.SemaphoreType.DMA(())],
    )
    return pl.pallas_call(
        _moe_down_kernel,
        grid_spec=grid_spec,
        out_shape=jax.ShapeDtypeStruct((n_chunks * rows * s, LANES), F32),
        compiler_params=_params("arbitrary"),
        name="moe_down",
    )(chunk_expert, next_expert, n_valid, act, b_down.reshape(e, 1, d), w_down)


def _combine_kernel(dest_ref, dest_next_ref, x1_ref, gate_ref, gple_ref, yt_hbm, x2_ref, h3_ref, buf, sem):
    i = pl.program_id(0)
    rows = x1_ref.shape[0]
    s = x1_ref.shape[1] // LANES
    slot = i % 2

    def start(idx_ref, to):
        for k in range(TOP_K):
            _start_row_gathers(yt_hbm, idx_ref, k, TOP_K, buf.at[to, k], rows, s, sem.at[to])

    @pl.when(i == 0)
    def _():
        start(dest_ref, 0)

    @pl.when(i + 1 < pl.num_programs(0))
    def _():
        start(dest_next_ref, 1 - slot)

    for k in range(TOP_K):
        _wait_row_gathers(buf.at[slot, k], rows, s, sem.at[slot])
    gate = gate_ref[...]
    acc = x1_ref[...]
    for k in range(TOP_K):
        acc = acc + gate[:, k:k + 1] * _gathered_rows(buf.at[slot, k], rows, s)
    x2_ref[...] = acc
    h3_ref[...] = _rms(acc, gple_ref[...]).astype(h3_ref.dtype)


def _combine(x1, gate, dest, yt, g_ple):
    n, d = x1.shape
    s = d // LANES
    rows = _pick(n, (COMBINE_ROWS, 64, 32, 16))
    steps = n // rows
    assert rows % GATHER_UNROLL == 0
    return pl.pallas_call(
        _combine_kernel,
        grid=(steps,),
        in_specs=[pl.BlockSpec((rows * TOP_K,), lambda i: (i,), memory_space=pltpu.SMEM),
                  pl.BlockSpec((rows * TOP_K,), lambda i: (jnp.minimum(i + 1, steps - 1),),
                               memory_space=pltpu.SMEM),
                  pl.BlockSpec((rows, d), lambda i: (i, 0)),
                  pl.BlockSpec((rows, LANES), lambda i: (i, 0)),
                  pl.BlockSpec((1, d), lambda i: (0, 0)),
                  pl.BlockSpec(memory_space=pl.ANY)],
        out_specs=[pl.BlockSpec((rows, d), lambda i: (i, 0)),
                   pl.BlockSpec((rows, d), lambda i: (i, 0))],
        out_shape=[jax.ShapeDtypeStruct((n, d), F32), jax.ShapeDtypeStruct((n, d), BF16)],
        scratch_shapes=[pltpu.VMEM((2, TOP_K, rows * _row_pitch(s), LANES), F32),
                        pltpu.SemaphoreType.DMA((2,))],
        compiler_params=_params("arbitrary"),
        name="moe_combine_gather",
    )(dest, dest, x1, gate, g_ple.reshape(1, d), yt)


def _ple_kernel(x2_ref, h3_ref, p_ref, wpg_ref, wple_ref, o_ref):
    gate = jax.nn.sigmoid(_dot(h3_ref[...], wpg_ref[...]))
    o_ref[...] = x2_ref[...] + gate * _dot(p_ref[...].astype(BF16), wple_ref[...])


def _ple(x2, h3, p, w_pg, w_ple, row0, r):
    d = x2.shape[1]
    pd = p.shape[1]
    tm = _pick(r, (512, 256, 128, 64, 32, 16))
    tn = _pick(d, (1024, 512, 256, 128))
    assert row0 % tm == 0
    b0 = row0 // tm
    return pl.pallas_call(
        _ple_kernel,
        grid=(d // tn, r // tm),
        in_specs=[pl.BlockSpec((tm, tn), lambda j, i: (b0 + i, j)),
                  pl.BlockSpec((tm, d), lambda j, i: (b0 + i, 0)),
                  pl.BlockSpec((tm, pd), lambda j, i: (i, 0)),
                  pl.BlockSpec((d, tn), lambda j, i: (0, j)),
                  pl.BlockSpec((pd, tn), lambda j, i: (0, j))],
        out_specs=pl.BlockSpec((tm, tn), lambda j, i: (i, j)),
        out_shape=jax.ShapeDtypeStruct((r, d), F32),
        compiler_params=_params("parallel", "parallel"),
        name="ple_gate",
    )(x2, h3, p, w_pg, w_ple)


def _rope_tables(pos):
    half = ROT_DIM // 2
    inv = jnp.power(ROPE_THETA, -jnp.arange(half, dtype=F32) * 2.0 / ROT_DIM)
    ang = pos.astype(F32)[:, None] * inv[None, :]
    cos, sin = jnp.cos(ang), jnp.sin(ang)
    r = pos.shape[0]
    rest = A_HEAD_DIM - ROT_DIM
    c = jnp.concatenate([cos, cos, jnp.ones((r, rest), F32)], axis=1)
    s1 = jnp.concatenate([-sin, jnp.zeros((r, half + rest), F32)], axis=1)
    s2 = jnp.concatenate([jnp.zeros((r, half), F32), sin, jnp.zeros((r, rest), F32)], axis=1)
    return c, s1, s2


def _mixer_inputs(x, pos, g_attn, w, b_i, b_f, g_q, g_k, mlstm_dtype):
    d = x.shape[1]
    kq = M_HEADS * M_QK
    o_ig = 2 * kq + M_WIDTH
    o_om = o_ig + 2 * M_HEADS
    o_qa = o_om + M_WIDTH
    o_gm = o_qa + 3 * A_WIDTH
    h = _rmsnorm_bf16(x, g_attn)
    colscale = jnp.concatenate([jnp.full((1, kq), M_QK ** -0.5, F32), jnp.ones((1, kq + M_WIDTH), F32)], axis=1)
    qkv_m = _proj_scale(h, w[:, :o_ig].astype(BF16), colscale, mlstm_dtype)
    wg = w[:, o_ig:o_om]
    wc = jnp.pad(wg, ((0, 0), (0, LANES - 2 * M_HEADS))).astype(BF16)
    bias = jnp.concatenate([b_i, b_f]).astype(F32)
    bc = jnp.pad(bias, (0, LANES - 2 * M_HEADS)).reshape(1, LANES)
    gcol, grow = _proj_gates(h, wc, wg.T.astype(BF16), bc, bias.reshape(2 * M_HEADS, 1))
    sg = _proj_sigmoid(h, jnp.concatenate([w[:, o_om:o_qa], w[:, o_gm:]], axis=1).astype(BF16))
    tabs = _rope_tables(pos)
    q32, _ = _proj_qk(h, w[:, o_qa:o_qa + A_WIDTH].astype(BF16), g_q, *tabs)
    k32, k16 = _proj_qk(h, w[:, o_qa + A_WIDTH:o_qa + 2 * A_WIDTH].astype(BF16), g_k, *tabs)
    v32, v16 = _proj_plain2(h, w[:, o_qa + 2 * A_WIDTH:o_gm].astype(BF16))
    return qkv_m, gcol, grow, sg, q32, k32, k16, v32, v16


def _moe_plan(eid, n_chunks):
    n = eid.shape[0]
    e_flat = eid[:, :TOP_K].reshape(-1)
    onehot = (e_flat[:, None] == jnp.arange(N_EXPERTS, dtype=jnp.int32)[None, :]).astype(jnp.int32)
    before = jnp.cumsum(onehot, axis=0) - onehot
    rank = jnp.sum(before * onehot, axis=1)
    counts = jnp.sum(onehot, axis=0)
    padded = (counts + MOE_ROWS - 1) // MOE_ROWS * MOE_ROWS
    p_ends = jnp.cumsum(padded)
    p_starts = p_ends - padded
    dest = (p_starts[e_flat] + rank).astype(jnp.int32)
    tok = jnp.repeat(jnp.arange(n, dtype=jnp.int32), TOP_K)
    src_tok = jnp.zeros((n_chunks * MOE_ROWS,), jnp.int32).at[dest].set(tok)
    n_valid = (p_ends[-1] // MOE_ROWS).astype(jnp.int32).reshape(1)
    chunk0 = jnp.minimum(jnp.arange(n_chunks, dtype=jnp.int32), n_valid[0] - 1) * MOE_ROWS
    chunk_expert = jnp.minimum(jnp.sum((p_ends[None, :] <= chunk0[:, None]).astype(jnp.int32), axis=1),
                               N_EXPERTS - 1)
    ids = jnp.arange(N_EXPERTS, dtype=jnp.int32)
    later = (ids[None, :] > chunk_expert[:, None]) & (counts[None, :] > 0)
    next_expert = jnp.min(jnp.where(later, ids[None, :], N_EXPERTS), axis=1)
    next_expert = jnp.where(next_expert < N_EXPERTS, next_expert, -1).astype(jnp.int32)
    return dest, src_tok, n_valid, chunk_expert, next_expert


def kernel(x_prompt, x_sample, cache_k, cache_v, state_mlstm_C, state_mlstm_n, state_mlstm_m, page_table, p_prompt, p_sample, g_attn, w_in, b_i, b_f, g_q, g_k, g_mh, w_bm, w_ba, w_out, g_ffn, w_router, b_router, w_up, b_up, w_down, b_down, g_ple, w_pg, w_ple):
    depth = w_in.shape[0]
    bsz, t, d = x_prompt.shape
    ns, dec_seq, _ = x_sample.shape
    assert depth == 1 and bsz == 1 and dec_seq == 1
    page = cache_k.shape[2]
    past = page_table.shape[1] * page
    assert t % MOBA_BLOCK == 0 and past % MOBA_BLOCK == 0 and MOBA_BLOCK % page == 0
    assert t % (M_CHUNK * MLSTM_CHUNKS_PER_STEP) == 0 and ns % MLSTM_SAMPLE_SEQS == 0
    n = t + ns

    xp = x_prompt[0]
    xs = x_sample[:, 0]
    w = w_in[0]
    fw = (g_attn[0], w, b_i[0], b_f[0], g_q[0], g_k[0])

    qkv_p, gcol_p, grow_p, sg_p, q32_p, k32_p, k16_p, v32_p, v16_p = _mixer_inputs(
        xp, jnp.arange(t, dtype=jnp.int32), *fw, BF16)
    hm_p, ct_p, n_p, m_p = _mlstm_prompt(qkv_p, gcol_p, grow_p)
    ao_p = _moba_prompt(q32_p, k16_p, v16_p, _block_kmean(k32_p))

    qkv_s, gcol_s, _, sg_s, q32_s, k32_s, _, v32_s, _ = _mixer_inputs(
        xs, jnp.full((ns,), past, jnp.int32), *fw, F32)
    hm_s, c_s, n_s, m_s = _mlstm_sample(qkv_s, gcol_s, state_mlstm_C[0].astype(F32),
                                        state_mlstm_n[0].astype(F32), state_mlstm_m[0].astype(F32))
    ao_s = _moba_sample(q32_s, k32_s, v32_s, cache_k, cache_v, page_table)

    wbm, wba, wout = w_bm[0].astype(BF16), w_ba[0].astype(BF16), w_out[0].astype(BF16)
    mix_p = _mix(hm_p, sg_p, ao_p, g_mh[0], wbm, wba)
    mix_s = _mix(hm_s, sg_s, ao_s, g_mh[0], wbm, wba)
    wr = jnp.pad(w_router[0], ((0, 0), (0, LANES - N_EXPERTS)))
    wr_hi = wr.astype(BF16)
    wr_lo = (wr - wr_hi.astype(F32)).astype(BF16)
    b_r = jnp.pad(b_router[0].astype(F32), (0, LANES - N_EXPERTS)).reshape(1, LANES)
    routed = _resid_router(xp, mix_p, wout, g_ffn[0], wr_hi, wr_lo, b_r, n, 0, None)
    x1, x1t, gate, eid = _resid_router(xs, mix_s, wout, g_ffn[0], wr_hi, wr_lo, b_r, n, t, routed)

    n_slots = n * TOP_K
    n_chunks = (n_slots + N_EXPERTS * (MOE_ROWS - 1)) // MOE_ROWS
    dest, src_tok, n_valid, chunk_expert, next_expert = _moe_plan(eid, n_chunks)
    xs_sorted = _dispatch(x1t, g_ffn[0], src_tok, n_valid, n_chunks)
    act = _moe_up(xs_sorted, w_up[0], b_up[0], chunk_expert, next_expert, n_valid)
    yt = _moe_down(act, w_down[0], b_down[0], chunk_expert, next_expert, n_valid)
    x2, h3 = _combine(x1, gate, dest, yt, g_ple[0])

    wpg, wple = w_pg[0].astype(BF16), w_ple[0].astype(BF16)
    y_p = _ple(x2, h3, p_prompt[0, 0], wpg, wple, 0, t)
    y_s = _ple(x2, h3, p_sample[0, :, 0], wpg, wple, t, ns)

    def heads(a):
        return a.reshape(a.shape[0], A_HEADS, A_HEAD_DIM)

    return (y_p[None], y_s[:, None],
            heads(k32_p)[None, None], heads(v32_p)[None, None],
            jnp.swapaxes(ct_p, 1, 2)[None, None].astype(state_mlstm_C.dtype),
            n_p[:M_HEADS][None, None].astype(state_mlstm_n.dtype),
            m_p[:M_HEADS, 0][None, None].astype(state_mlstm_m.dtype),
            heads(k32_s)[None, :, None], heads(v32_s)[None, :, None],
            c_s[None].astype(state_mlstm_C.dtype), n_s[None].astype(state_mlstm_n.dtype),
            m_s[None].astype(state_mlstm_m.dtype))
```

```python
import functools

import jax
import jax.numpy as jnp
import numpy as np
from jax import lax
from jax.experimental import pallas as pl
from jax.experimental.pallas import tpu as pltpu

F32 = jnp.float32
BF16 = jnp.bfloat16

M_HEADS = 4
M_QK = 128
M_V = 256
M_CHUNK = 64
M_WIDTH = M_HEADS * M_V
A_HEADS = 8
A_HEAD_DIM = 128
A_WIDTH = A_HEADS * A_HEAD_DIM
MOBA_BLOCK = 256
MOBA_TOPK = 3
ROT_DIM = A_HEAD_DIM // 4
ROPE_THETA = 500000.0
N_EXPERTS = 32
TOP_K = 4
SWIGLU_LIMIT = 7.0
SWIGLU_ALPHA = 1.702
EPS = 1e-6
NEG = -1e30

LANES = 128
SUBLANES = 8
VMEM_LIMIT_BYTES = 56 * 1024 * 1024

MOE_ROWS = 256
MLSTM_CHUNKS_PER_STEP = 2
MLSTM_SAMPLE_SEQS = 8
COMBINE_ROWS = 128
MOBA_HEADS_PER_STEP = 4
GATHER_UNROLL = 8


def _pick(n, candidates):
    for c in candidates:
        if n % c == 0:
            return c
    raise ValueError(f"no tile in {candidates} divides {n}")


def _params(*sem):
    return pltpu.CompilerParams(dimension_semantics=sem, vmem_limit_bytes=VMEM_LIMIT_BYTES)


def _dot(a, b):
    return jnp.dot(a, b, preferred_element_type=F32)


def _dot_nt(a, b):
    return lax.dot_general(a, b, (((1,), (1,)), ((), ())), preferred_element_type=F32)


def _dot_tn(a, b):
    return lax.dot_general(a, b, (((0,), (0,)), ((), ())), preferred_element_type=F32)


def _split_bf16(x):
    hi = x.astype(BF16)
    lo = (x - hi.astype(F32)).astype(BF16)
    return hi, lo


def _rms(x, g):
    return x * lax.rsqrt(jnp.mean(x * x, axis=-1, keepdims=True) + EPS) * g


def _log_sigmoid(x):
    return -(jnp.maximum(-x, 0.0) + jnp.log1p(jnp.exp(-jnp.abs(x))))


def _first_argmax(work, lane_f):
    m = jnp.max(work, axis=-1, keepdims=True)
    idx = jnp.min(jnp.where(work == m, lane_f, float(4 * LANES)), axis=-1, keepdims=True)
    return m, idx


def _rmsnorm_kernel(x_ref, g_ref, o_ref):
    o_ref[...] = _rms(x_ref[...], g_ref[...]).astype(o_ref.dtype)


def _rmsnorm_bf16(x, g):
    r, d = x.shape
    tm = _pick(r, (512, 256, 128, 64, 32, 16))
    return pl.pallas_call(
        _rmsnorm_kernel,
        grid=(r // tm,),
        in_specs=[pl.BlockSpec((tm, d), lambda i: (i, 0)), pl.BlockSpec((1, d), lambda i: (0, 0))],
        out_specs=pl.BlockSpec((tm, d), lambda i: (i, 0)),
        out_shape=jax.ShapeDtypeStruct((r, d), BF16),
        compiler_params=_params("parallel"),
        name="rmsnorm_rows",
    )(x, g.reshape(1, d))


def _proj_scale_kernel(h_ref, w_ref, s_ref, o_ref):
    o_ref[...] = (_dot(h_ref[...], w_ref[...]) * s_ref[...]).astype(o_ref.dtype)


def _proj_sigmoid_kernel(h_ref, w_ref, o_ref):
    o_ref[...] = jax.nn.sigmoid(_dot(h_ref[...], w_ref[...])).astype(o_ref.dtype)


def _proj_plain2_kernel(h_ref, w_ref, o32_ref, o16_ref):
    acc = _dot(h_ref[...], w_ref[...])
    o32_ref[...] = acc
    o16_ref[...] = acc.astype(BF16)


def _proj_qk_kernel(h_ref, w_ref, g_ref, c_ref, s1_ref, s2_ref, o32_ref, o16_ref):
    acc = _dot(h_ref[...], w_ref[...])
    g = g_ref[...]
    c, s1, s2 = c_ref[...], s1_ref[...], s2_ref[...]
    for hh in range(acc.shape[1] // A_HEAD_DIM):
        sl = slice(hh * A_HEAD_DIM, (hh + 1) * A_HEAD_DIM)
        y = _rms(acc[:, sl], g)
        up = pltpu.roll(y, A_HEAD_DIM - ROT_DIM // 2, 1)
        dn = pltpu.roll(y, ROT_DIM // 2, 1)
        r = y * c + up * s1 + dn * s2
        o32_ref[:, sl] = r
        o16_ref[:, sl] = r.astype(BF16)


def _proj_gates_kernel(h_ref, wc_ref, wr_ref, bc_ref, br_ref, gc_ref, gr_ref):
    h = h_ref[...]
    zc = _dot(h, wc_ref[...]) + bc_ref[...]
    zr = _dot_nt(wr_ref[...], h) + br_ref[...]
    lane = lax.broadcasted_iota(jnp.int32, zc.shape, 1)
    gc_ref[...] = jnp.where((lane >= M_HEADS) & (lane < 2 * M_HEADS), _log_sigmoid(zc), zc)
    row = lax.broadcasted_iota(jnp.int32, zr.shape, 0)
    gr_ref[...] = jnp.where(row >= M_HEADS, _log_sigmoid(zr), zr)


def _proj_tiles(r, n):
    tm = _pick(r, (1024, 512, 256, 128, 64, 32, 16))
    tn = _pick(n, (1024, 512, 256, 128))
    return tm, tn


def _proj_scale(h, w, colscale, out_dtype):
    r, d = h.shape
    n = w.shape[1]
    tm, tn = _proj_tiles(r, n)
    return pl.pallas_call(
        _proj_scale_kernel,
        grid=(n // tn, r // tm),
        in_specs=[pl.BlockSpec((tm, d), lambda j, i: (i, 0)),
                  pl.BlockSpec((d, tn), lambda j, i: (0, j)),
                  pl.BlockSpec((1, tn), lambda j, i: (0, j))],
        out_specs=pl.BlockSpec((tm, tn), lambda j, i: (i, j)),
        out_shape=jax.ShapeDtypeStruct((r, n), out_dtype),
        compiler_params=_params("parallel", "parallel"),
        name="proj_scale",
    )(h, w, colscale)


def _proj_sigmoid(h, w):
    r, d = h.shape
    n = w.shape[1]
    tm, tn = _proj_tiles(r, n)
    return pl.pallas_call(
        _proj_sigmoid_kernel,
        grid=(n // tn, r // tm),
        in_specs=[pl.BlockSpec((tm, d), lambda j, i: (i, 0)),
                  pl.BlockSpec((d, tn), lambda j, i: (0, j))],
        out_specs=pl.BlockSpec((tm, tn), lambda j, i: (i, j)),
        out_shape=jax.ShapeDtypeStruct((r, n), BF16),
        compiler_params=_params("parallel", "parallel"),
        name="proj_sigmoid",
    )(h, w)


def _proj_plain2(h, w):
    r, d = h.shape
    n = w.shape[1]
    tm, tn = _proj_tiles(r, n)
    return pl.pallas_call(
        _proj_plain2_kernel,
        grid=(n // tn, r // tm),
        in_specs=[pl.BlockSpec((tm, d), lambda j, i: (i, 0)),
                  pl.BlockSpec((d, tn), lambda j, i: (0, j))],
        out_specs=[pl.BlockSpec((tm, tn), lambda j, i: (i, j)),
                   pl.BlockSpec((tm, tn), lambda j, i: (i, j))],
        out_shape=[jax.ShapeDtypeStruct((r, n), F32), jax.ShapeDtypeStruct((r, n), BF16)],
        compiler_params=_params("parallel", "parallel"),
        name="proj_plain",
    )(h, w)


def _proj_qk(h, w, g, rope_c, rope_s1, rope_s2):
    r, d = h.shape
    n = w.shape[1]
    tm, tn = _proj_tiles(r, n)
    hd = A_HEAD_DIM
    return pl.pallas_call(
        _proj_qk_kernel,
        grid=(n // tn, r // tm),
        in_specs=[pl.BlockSpec((tm, d), lambda j, i: (i, 0)),
                  pl.BlockSpec((d, tn), lambda j, i: (0, j)),
                  pl.BlockSpec((1, hd), lambda j, i: (0, 0)),
                  pl.BlockSpec((tm, hd), lambda j, i: (i, 0)),
                  pl.BlockSpec((tm, hd), lambda j, i: (i, 0)),
                  pl.BlockSpec((tm, hd), lambda j, i: (i, 0))],
        out_specs=[pl.BlockSpec((tm, tn), lambda j, i: (i, j)),
                   pl.BlockSpec((tm, tn), lambda j, i: (i, j))],
        out_shape=[jax.ShapeDtypeStruct((r, n), F32), jax.ShapeDtypeStruct((r, n), BF16)],
        compiler_params=_params("parallel", "parallel"),
        name="proj_qk_norm_rope",
    )(h, w, g.reshape(1, hd), rope_c, rope_s1, rope_s2)


def _proj_gates(h, wc, wr, bc, br):
    r, d = h.shape
    tm = _pick(r, (1024, 512, 256, 128)) if r % LANES == 0 else r
    g2 = 2 * M_HEADS
    return pl.pallas_call(
        _proj_gates_kernel,
        grid=(r // tm,),
        in_specs=[pl.BlockSpec((tm, d), lambda i: (i, 0)),
                  pl.BlockSpec((d, LANES), lambda i: (0, 0)),
                  pl.BlockSpec((g2, d), lambda i: (0, 0)),
                  pl.BlockSpec((1, LANES), lambda i: (0, 0)),
                  pl.BlockSpec((g2, 1), lambda i: (0, 0))],
        out_specs=[pl.BlockSpec((tm, LANES), lambda i: (i, 0)),
                   pl.BlockSpec((g2, tm), lambda i: (0, i))],
        out_shape=[jax.ShapeDtypeStruct((r, LANES), F32), jax.ShapeDtypeStruct((g2, r), F32)],
        compiler_params=_params("parallel"),
        name="proj_gates",
    )(h, wc, wr, bc, br)


def _mlstm_prompt_kernel(q_ref, k_ref, v_ref, gc_ref, gr_ref, h_ref, ct_out, n_out, m_out,
                         ct_s, n_s, m_s):
    step = pl.program_id(0)

    @pl.when(step == 0)
    def _():
        ct_s[...] = jnp.zeros_like(ct_s)
        n_s[...] = jnp.zeros_like(n_s)
        m_s[...] = jnp.zeros_like(m_s)

    ln = M_CHUNK
    row = lax.broadcasted_iota(jnp.int32, (ln, ln), 0)
    col = lax.broadcasted_iota(jnp.int32, (ln, ln), 1)
    causal = col <= row
    upto = row <= col
    pre = {}
    for cc in range(MLSTM_CHUNKS_PER_STEP):
        rs = slice(cc * ln, (cc + 1) * ln)
        for hh in range(M_HEADS):
            ig_c = gc_ref[rs, hh:hh + 1]
            lf_c = gc_ref[rs, M_HEADS + hh:M_HEADS + hh + 1]
            ig_r = gr_ref[hh:hh + 1, rs]
            lf_r = gr_ref[M_HEADS + hh:M_HEADS + hh + 1, rs]
            bcum_c = jnp.sum(jnp.where(causal, lf_r, 0.0), axis=1, keepdims=True)
            bcum_r = jnp.sum(jnp.where(upto, lf_c, 0.0), axis=0, keepdims=True)
            dmat = jnp.where(causal, bcum_c - bcum_r + ig_r, NEG)
            qh = q_ref[rs, hh * M_QK:(hh + 1) * M_QK]
            kh = k_ref[rs, hh * M_QK:(hh + 1) * M_QK]
            b_last = bcum_c[ln - 1:ln, :]
            pre[cc, hh] = dict(
                bcum_c=bcum_c, dmat=dmat, dmax=jnp.max(dmat, axis=1, keepdims=True),
                qk=_dot_nt(qh, kh), b_last=b_last,
                d_last_c=b_last - bcum_c + ig_c,
                d_last_max=jnp.max(b_last - bcum_r + ig_r, axis=1, keepdims=True))

    for hh in range(M_HEADS):
        ct = ct_s[hh]
        n_row = n_s[hh:hh + 1, :]
        m_prev = m_s[hh:hh + 1, 0:1]
        for cc in range(MLSTM_CHUNKS_PER_STEP):
            rs = slice(cc * ln, (cc + 1) * ln)
            u = pre[cc, hh]
            qh = q_ref[rs, hh * M_QK:(hh + 1) * M_QK]
            kh = k_ref[rs, hh * M_QK:(hh + 1) * M_QK]
            vh = v_ref[rs, hh * M_V:(hh + 1) * M_V]
            a = u["bcum_c"] + m_prev
            m_t = jnp.maximum(a, u["dmax"])
            w_inter = jnp.exp(a - m_t)
            s = u["qk"] * jnp.exp(u["dmat"] - m_t)
            num = w_inter * _dot(qh, ct.astype(BF16)) + _dot(s.astype(BF16), vh)
            den = (w_inter * jnp.sum(qh.astype(F32) * n_row, axis=1, keepdims=True)
                   + jnp.sum(s, axis=1, keepdims=True))
            h_ref[rs, hh * M_V:(hh + 1) * M_V] = num / jnp.maximum(jnp.abs(den), jnp.exp(-m_t))
            a_last = u["b_last"] + m_prev
            m_new = jnp.maximum(a_last, u["d_last_max"])
            w_c = jnp.exp(a_last - m_new)
            w_j = jnp.exp(u["d_last_c"] - m_new)
            vw = (vh.astype(F32) * w_j).astype(BF16)
            ct = w_c * ct + _dot_tn(kh, vw)
            n_row = w_c * n_row + jnp.sum(kh.astype(F32) * w_j, axis=0, keepdims=True)
            m_prev = m_new
        ct_s[hh] = ct
        n_s[hh:hh + 1, :] = n_row
        m_s[hh:hh + 1, :] = jnp.broadcast_to(m_prev, (1, LANES))

    @pl.when(step == pl.num_programs(0) - 1)
    def _():
        ct_out[...] = ct_s[...]
        n_out[...] = n_s[...]
        m_out[...] = m_s[...]


def _mlstm_prompt(qkv, gcol, grow):
    t = qkv.shape[0]
    rows = M_CHUNK * MLSTM_CHUNKS_PER_STEP
    kq = M_HEADS * M_QK
    return pl.pallas_call(
        _mlstm_prompt_kernel,
        grid=(t // rows,),
        in_specs=[pl.BlockSpec((rows, kq), lambda c: (c, 0)),
                  pl.BlockSpec((rows, kq), lambda c: (c, 1)),
                  pl.BlockSpec((rows, M_WIDTH), lambda c: (c, (2 * kq) // M_WIDTH)),
                  pl.BlockSpec((rows, LANES), lambda c: (c, 0)),
                  pl.BlockSpec((2 * M_HEADS, rows), lambda c: (0, c))],
        out_specs=[pl.BlockSpec((rows, M_WIDTH), lambda c: (c, 0)),
                   pl.BlockSpec((M_HEADS, M_QK, M_V), lambda c: (0, 0, 0)),
                   pl.BlockSpec((SUBLANES, LANES), lambda c: (0, 0)),
                   pl.BlockSpec((SUBLANES, LANES), lambda c: (0, 0))],
        out_shape=[jax.ShapeDtypeStruct((t, M_WIDTH), F32),
                   jax.ShapeDtypeStruct((M_HEADS, M_QK, M_V), F32),
                   jax.ShapeDtypeStruct((SUBLANES, LANES), F32),
                   jax.ShapeDtypeStruct((SUBLANES, LANES), F32)],
        scratch_shapes=[pltpu.VMEM((M_HEADS, M_QK, M_V), F32),
                        pltpu.VMEM((SUBLANES, LANES), F32),
                        pltpu.VMEM((SUBLANES, LANES), F32)],
        compiler_params=_params("arbitrary"),
        name="mlstm_prompt_scan",
    )(qkv, qkv, qkv, gcol, grow)


def _mlstm_sample_kernel(qkv_ref, gc_ref, c_ref, n_ref, m_ref, h_ref, co_ref, no_ref, mo_ref):
    kq = M_HEADS * M_QK
    row8 = lax.broadcasted_iota(jnp.int32, (SUBLANES, 1), 0)
    mrow = lax.broadcasted_iota(jnp.int32, mo_ref.shape, 0)
    mcol = lax.broadcasted_iota(jnp.int32, mo_ref.shape, 1)
    m_all = m_ref[...]
    for s in range(MLSTM_SAMPLE_SEQS):
        for hh in range(M_HEADS):
            q = qkv_ref[s:s + 1, hh * M_QK:(hh + 1) * M_QK]
            k = qkv_ref[s:s + 1, kq + hh * M_QK:kq + (hh + 1) * M_QK]
            v = qkv_ref[s:s + 1, 2 * kq + hh * M_V:2 * kq + (hh + 1) * M_V]
            ig = gc_ref[s:s + 1, hh:hh + 1]
            lf = gc_ref[s:s + 1, M_HEADS + hh:M_HEADS + hh + 1]
            m_prev = m_ref[s:s + 1, hh:hh + 1]
            c = c_ref[s, hh]
            n_row = n_ref[s, hh:hh + 1, :]
            a = lf + m_prev
            m_t = jnp.maximum(a, ig)
            w_c = jnp.exp(a - m_t)
            w_j = jnp.exp(ig - m_t)
            sc = jnp.sum(q * k, axis=1, keepdims=True) * w_j
            q8 = jnp.broadcast_to(q, (SUBLANES, M_QK)).astype(BF16)
            cq = _dot_nt(q8, c.astype(BF16))[0:1, :]
            num = w_c * cq + sc * v
            den = w_c * jnp.sum(n_row * q, axis=1, keepdims=True) + sc
            h_ref[s:s + 1, hh * M_V:(hh + 1) * M_V] = num / jnp.maximum(jnp.abs(den), jnp.exp(-m_t))
            vw8 = jnp.where(row8 == 0, v * w_j, 0.0).astype(BF16)
            k8 = jnp.where(row8 == 0, k, 0.0).astype(BF16)
            co_ref[s, hh] = w_c * c + _dot_tn(vw8, k8)
            no_ref[s, hh:hh + 1, :] = w_c * n_row + w_j * k
            m_all = jnp.where((mrow == s) & (mcol == hh), m_t, m_all)
    mo_ref[...] = m_all


def _mlstm_sample(qkv, gcol, c0, n0, m0):
    ns = qkv.shape[0]
    sb = MLSTM_SAMPLE_SEQS
    wq = qkv.shape[1]
    return pl.pallas_call(
        _mlstm_sample_kernel,
        grid=(ns // sb,),
        in_specs=[pl.BlockSpec((sb, wq), lambda i: (i, 0)),
                  pl.BlockSpec((sb, LANES), lambda i: (i, 0)),
                  pl.BlockSpec((sb, M_HEADS, M_V, M_QK), lambda i: (i, 0, 0, 0)),
                  pl.BlockSpec((sb, M_HEADS, M_QK), lambda i: (i, 0, 0)),
                  pl.BlockSpec((sb, M_HEADS), lambda i: (i, 0))],
        out_specs=[pl.BlockSpec((sb, M_WIDTH), lambda i: (i, 0)),
                   pl.BlockSpec((sb, M_HEADS, M_V, M_QK), lambda i: (i, 0, 0, 0)),
                   pl.BlockSpec((sb, M_HEADS, M_QK), lambda i: (i, 0, 0)),
                   pl.BlockSpec((sb, M_HEADS), lambda i: (i, 0))],
        out_shape=[jax.ShapeDtypeStruct((ns, M_WIDTH), F32),
                   jax.ShapeDtypeStruct(c0.shape, F32),
                   jax.ShapeDtypeStruct(n0.shape, F32),
                   jax.ShapeDtypeStruct(m0.shape, F32)],
        compiler_params=_params("parallel"),
        name="mlstm_sample_step",
    )(qkv, gcol, c0, n0, m0)


def _kmean_kernel(k_ref, o_ref):
    o_ref[0] = jnp.mean(k_ref[...], axis=0, keepdims=True)


def _block_kmean(k32):
    t, w = k32.shape
    nb = t // MOBA_BLOCK
    out = pl.pallas_call(
        _kmean_kernel,
        grid=(nb,),
        in_specs=[pl.BlockSpec((MOBA_BLOCK, w), lambda b: (b, 0))],
        out_specs=pl.BlockSpec((1, 1, w), lambda b: (b, 0, 0)),
        out_shape=jax.ShapeDtypeStruct((nb, 1, w), F32),
        compiler_params=_params("parallel"),
        name="moba_block_kmean",
    )(k32)
    return out.reshape(nb, w)


def _moba_select(sc, own, lane_i, lane_f):
    work = jnp.where(lane_i < own, sc, NEG)
    sel = jnp.zeros(sc.shape, dtype=jnp.bool_)
    for _ in range(MOBA_TOPK):
        _, idx = _first_argmax(work, lane_f)
        hit = lane_f == idx
        sel = sel | hit
        work = jnp.where(hit, -jnp.inf, work)
    return sel & (lane_i < own)


def _moba_prompt_kernel(q_ref, k_ref, v_ref, e_ref, km_ref, o_ref):
    i = pl.program_id(1)
    bs = MOBA_BLOCK
    pw = 2 * bs
    hd = A_HEAD_DIM
    heads = range(MOBA_HEADS_PER_STEP)
    ones = jnp.ones((pw, hd), BF16)
    lane_i = lax.broadcasted_iota(jnp.int32, (bs, LANES), 1)
    lane_f = lane_i.astype(F32)

    def head_slice(hh):
        return slice(hh * hd, (hh + 1) * hd)

    q_aug = []
    for hh in heads:
        q = q_ref[:, head_slice(hh)]
        qh, ql = _split_bf16(q)
        kmh, kml = _split_bf16(km_ref[:, head_slice(hh)])
        sc = _dot_nt(qh, kmh) + (_dot_nt(qh, kml) + _dot_nt(ql, kmh))
        sel = _moba_select(sc, i, lane_i, lane_f)
        bias = jnp.where(sel | (lane_i >= i), 0.0, NEG)
        q_aug.append(jnp.concatenate([(q * (hd ** -0.5)).astype(BF16), bias.astype(BF16)], axis=1))

    def pair_scores(p, hh, onehot):
        off = pl.multiple_of(p * pw, pw)
        k_aug = jnp.concatenate([k_ref[pl.ds(off, pw), head_slice(hh)], onehot], axis=1)
        v_aug = jnp.concatenate([v_ref[pl.ds(off, pw), head_slice(hh)], ones], axis=1)
        return _dot_nt(q_aug[hh], k_aug), v_aug

    p_own = i // 2
    row = lax.broadcasted_iota(jnp.int32, (bs, pw), 0)
    col = lax.broadcasted_iota(jnp.int32, (bs, pw), 1)
    causal = col + (p_own * pw - i * bs) <= row
    onehot_own = e_ref[pl.ds(pl.multiple_of(p_own * pw, pw), pw), :]
    init = []
    for hh in heads:
        s, v_aug = pair_scores(p_own, hh, onehot_own)
        s = jnp.where(causal, s, NEG)
        m0 = jnp.max(s, axis=1, keepdims=True)
        init += [m0, _dot(jnp.exp(s - m0).astype(BF16), v_aug)]

    def body(p, carry):
        onehot = e_ref[pl.ds(pl.multiple_of(p * pw, pw), pw), :]
        out = []
        for hh in heads:
            m_i, acc = carry[2 * hh], carry[2 * hh + 1]
            sj, vj = pair_scores(p, hh, onehot)
            m_n = jnp.maximum(m_i, jnp.max(sj, axis=1, keepdims=True))
            out += [m_n, jnp.exp(m_i - m_n) * acc + _dot(jnp.exp(sj - m_n).astype(BF16), vj)]
        return tuple(out)

    final = lax.fori_loop(0, p_own, body, tuple(init))
    for hh in heads:
        acc = final[2 * hh + 1]
        o_ref[:, head_slice(hh)] = (acc[:, :hd] / acc[:, hd:]).astype(o_ref.dtype)


def _moba_prompt(q32, k16, v16, kmean):
    t = q32.shape[0]
    nb = t // MOBA_BLOCK
    hd = MOBA_HEADS_PER_STEP * A_HEAD_DIM
    assert nb <= LANES and nb % 2 == 0 and A_HEADS % MOBA_HEADS_PER_STEP == 0
    block_onehot = (jnp.arange(t, dtype=jnp.int32)[:, None] // MOBA_BLOCK
                    == jnp.arange(LANES, dtype=jnp.int32)[None, :]).astype(BF16)
    kmean_pad = jnp.pad(kmean, ((0, LANES - nb), (0, 0)))
    return pl.pallas_call(
        _moba_prompt_kernel,
        grid=(A_HEADS // MOBA_HEADS_PER_STEP, nb),
        in_specs=[pl.BlockSpec((MOBA_BLOCK, hd), lambda h, i: (i, h)),
                  pl.BlockSpec((t, hd), lambda h, i: (0, h)),
                  pl.BlockSpec((t, hd), lambda h, i: (0, h)),
                  pl.BlockSpec((t, LANES), lambda h, i: (0, 0)),
                  pl.BlockSpec((LANES, hd), lambda h, i: (0, h))],
        out_specs=pl.BlockSpec((MOBA_BLOCK, hd), lambda h, i: (i, h)),
        out_shape=jax.ShapeDtypeStruct((t, A_WIDTH), BF16),
        compiler_params=_params("parallel", "arbitrary"),
        name="moba_prompt_attention",
    )(q32, k16, v16, block_onehot, kmean_pad)


def _moba_sample_kernel(n_pages, pages_per_block, pt_ref, q_ref, kn_ref, vn_ref, *refs):
    del pt_ref
    kp = refs[:n_pages]
    vp = refs[n_pages:2 * n_pages]
    o_ref = refs[2 * n_pages]
    nbp = n_pages // pages_per_block
    hd = A_HEAD_DIM
    rows = kp[0].shape[2]
    page = rows // A_HEADS
    q = q_ref[0]
    lane_i = lax.broadcasted_iota(jnp.int32, (A_HEADS, LANES), 1)

    sc = jnp.zeros((A_HEADS, LANES), F32)
    k16 = []
    for b in range(nbp):
        ksum = jnp.zeros((A_HEADS, hd), F32)
        for pp in range(pages_per_block):
            kf = kp[b * pages_per_block + pp][0, 0]
            ksum = ksum + jnp.sum(kf.reshape(page, A_HEADS, hd), axis=0)
            k16.append(kf.astype(BF16))
        kmean = ksum / float(MOBA_BLOCK)
        sc = jnp.where(lane_i == b, jnp.sum(q * kmean, axis=1, keepdims=True), sc)
    sel = _moba_select(sc, nbp, lane_i, lane_i.astype(F32)).astype(F32)

    qs = q * (hd ** -0.5)
    own = jnp.sum(qs * kn_ref[0], axis=1, keepdims=True)
    s = _dot_nt(qs.astype(BF16), jnp.concatenate(k16, axis=0))
    head_row = lax.broadcasted_iota(jnp.int32, (A_HEADS, rows), 0)
    row_head = lax.broadcasted_iota(jnp.int32, (A_HEADS, rows), 1) % A_HEADS
    mine = head_row == row_head
    s = jnp.concatenate(
        [jnp.where(mine & (sel[:, pg // pages_per_block:pg // pages_per_block + 1] > 0.5),
                   s[:, pg * rows:(pg + 1) * rows], NEG) for pg in range(n_pages)], axis=1)
    m = jnp.maximum(own, jnp.max(s, axis=1, keepdims=True))
    p = jnp.exp(s - m)
    p_own = jnp.exp(own - m)
    l = jnp.sum(p, axis=1, keepdims=True) + p_own
    v16 = jnp.concatenate([vp[pg][0, 0].astype(BF16) for pg in range(n_pages)], axis=0)
    o_ref[0] = (_dot(p.astype(BF16), v16) + p_own * vn_ref[0]) / l


def _moba_sample(q32, kn32, vn32, cache_k, cache_v, page_table):
    ns = q32.shape[0]
    page = cache_k.shape[2]
    n_pages = page_table.shape[1]
    ppb = MOBA_BLOCK // page
    head_spec = pl.BlockSpec((1, A_HEADS, A_HEAD_DIM), lambda s, pt: (s, 0, 0))

    def page_spec(p):
        return pl.BlockSpec((1, 1, page * A_HEADS, A_HEAD_DIM), lambda s, pt: (0, pt[s, p], 0, 0))

    def rows(cache):
        return cache.reshape(cache.shape[0], cache.shape[1], page * A_HEADS, A_HEAD_DIM)

    def heads(a):
        return a.reshape(ns, A_HEADS, A_HEAD_DIM)

    grid_spec = pltpu.PrefetchScalarGridSpec(
        num_scalar_prefetch=1,
        grid=(ns,),
        in_specs=[head_spec, head_spec, head_spec] + [page_spec(p) for p in range(n_pages)] * 2,
        out_specs=head_spec,
    )
    out = pl.pallas_call(
        functools.partial(_moba_sample_kernel, n_pages, ppb),
        grid_spec=grid_spec,
        out_shape=jax.ShapeDtypeStruct((ns, A_HEADS, A_HEAD_DIM), F32),
        compiler_params=_params("parallel"),
        name="moba_sample_attention",
    )(page_table, heads(q32), heads(kn32), heads(vn32), *([rows(cache_k)] * n_pages), *([rows(cache_v)] * n_pages))
    return out.reshape(ns, A_WIDTH)


def _mix_kernel(hm_ref, om_ref, ao_ref, gm_ref, ga_ref, gmh_ref, wbm_ref, wba_ref, o_ref, hs_ref):
    @pl.when(pl.program_id(1) == 0)
    def _():
        for hh in range(M_HEADS):
            sl = slice(hh * M_V, (hh + 1) * M_V)
            y = _rms(hm_ref[:, sl], gmh_ref[:, sl])
            hs_ref[:, sl] = (y * om_ref[:, sl].astype(F32)).astype(BF16)

    t1 = _dot(hs_ref[...], wbm_ref[...])
    t2 = _dot(ao_ref[...].astype(BF16), wba_ref[...])
    o_ref[...] = (gm_ref[...].astype(F32) * t1 + ga_ref[...].astype(F32) * t2).astype(o_ref.dtype)


def _mix(hm, sg, ao, g_mh, w_bm, w_ba):
    r = hm.shape[0]
    d = w_bm.shape[1]
    tm = _pick(r, (512, 256, 128, 64, 32, 16))
    tn = _pick(d, (512, 256, 128))
    gm0 = M_WIDTH // tn
    ga0 = (M_WIDTH + d) // tn
    return pl.pallas_call(
        _mix_kernel,
        grid=(r // tm, d // tn),
        in_specs=[pl.BlockSpec((tm, M_WIDTH), lambda i, j: (i, 0)),
                  pl.BlockSpec((tm, M_WIDTH), lambda i, j: (i, 0)),
                  pl.BlockSpec((tm, A_WIDTH), lambda i, j: (i, 0)),
                  pl.BlockSpec((tm, tn), lambda i, j: (i, gm0 + j)),
                  pl.BlockSpec((tm, tn), lambda i, j: (i, ga0 + j)),
                  pl.BlockSpec((1, M_WIDTH), lambda i, j: (0, 0)),
                  pl.BlockSpec((M_WIDTH, tn), lambda i, j: (0, j)),
                  pl.BlockSpec((A_WIDTH, tn), lambda i, j: (0, j))],
        out_specs=pl.BlockSpec((tm, tn), lambda i, j: (i, j)),
        out_shape=jax.ShapeDtypeStruct((r, d), BF16),
        scratch_shapes=[pltpu.VMEM((tm, M_WIDTH), BF16)],
        compiler_params=_params("parallel", "arbitrary"),
        name="mixer_merge",
    )(hm, sg, ao, sg, sg, g_mh.reshape(1, M_WIDTH), w_bm, w_ba)


def _store_row_tiles(o_ref, x, rows):
    s = x.shape[1] // LANES
    for c in range(s):
        o_ref[pl.ds(c, rows, stride=s), :] = x[:, c * LANES:(c + 1) * LANES]


def _resid_router_kernel(x_ref, mix_ref, wout_ref, gffn_ref, wrh_ref, wrl_ref, br_ref,
                         x1_ref, x1t_ref, gate_ref, eid_ref):
    x1 = x_ref[...] + _dot(mix_ref[...], wout_ref[...])
    x1_ref[...] = x1
    _store_row_tiles(x1t_ref, x1, x1.shape[0])
    hh, hl = _split_bf16(_rms(x1, gffn_ref[...]))
    wrh = wrh_ref[...]
    logits = _dot(hh, wrh) + (_dot(hh, wrl_ref[...]) + _dot(hl, wrh)) + br_ref[...]
    lane_i = lax.broadcasted_iota(jnp.int32, logits.shape, 1)
    lane_f = lane_i.astype(F32)
    work = jnp.where(lane_i < N_EXPERTS, logits, -jnp.inf)
    vals, ids = [], []
    for _ in range(TOP_K):
        m, idx = _first_argmax(work, lane_f)
        vals.append(m)
        ids.append(idx)
        work = jnp.where(lane_f == idx, -jnp.inf, work)
    es = [jnp.exp(v - vals[0]) for v in vals]
    den = es[0]
    for e in es[1:]:
        den = den + e
    gate = jnp.zeros(logits.shape, F32)
    eid = jnp.zeros(logits.shape, F32)
    for k in range(TOP_K):
        gate = jnp.where(lane_i == k, es[k] / den, gate)
        eid = jnp.where(lane_i == k, ids[k], eid)
    gate_ref[...] = gate
    eid_ref[...] = eid.astype(jnp.int32)


def _resid_router(x, mix, w_out, g_ffn, wr_hi, wr_lo, b_r, n_total, row0, prev):
    r, d = x.shape
    s = d // LANES
    tm = _pick(r, (256, 128, 64, 32, 16))
    assert row0 % tm == 0
    b0 = row0 // tm
    out_shape = [jax.ShapeDtypeStruct((n_total, d), F32),
                 jax.ShapeDtypeStruct((n_total * s, LANES), F32),
                 jax.ShapeDtypeStruct((n_total, LANES), F32),
                 jax.ShapeDtypeStruct((n_total, LANES), jnp.int32)]
    out_specs = [pl.BlockSpec((tm, d), lambda i: (b0 + i, 0)),
                 pl.BlockSpec((tm * s, LANES), lambda i: (b0 + i, 0)),
                 pl.BlockSpec((tm, LANES), lambda i: (b0 + i, 0)),
                 pl.BlockSpec((tm, LANES), lambda i: (b0 + i, 0))]
    in_specs = [pl.BlockSpec((tm, d), lambda i: (i, 0)),
                pl.BlockSpec((tm, d), lambda i: (i, 0)),
                pl.BlockSpec((d, d), lambda i: (0, 0)),
                pl.BlockSpec((1, d), lambda i: (0, 0)),
                pl.BlockSpec((d, LANES), lambda i: (0, 0)),
                pl.BlockSpec((d, LANES), lambda i: (0, 0)),
                pl.BlockSpec((1, LANES), lambda i: (0, 0))]
    args = [x, mix, w_out, g_ffn.reshape(1, d), wr_hi, wr_lo, b_r]
    n_in = len(args)
    kern = _resid_router_kernel
    aliases = {}
    if prev is not None:
        in_specs = in_specs + [pl.BlockSpec(memory_space=pl.ANY)] * len(prev)
        args = args + list(prev)
        aliases = {n_in + k: k for k in range(len(prev))}

        def kern(*refs):
            _resid_router_kernel(*refs[:n_in], *refs[n_in + len(prev):])

    return pl.pallas_call(
        kern,
        grid=(r // tm,),
        in_specs=in_specs,
        out_specs=out_specs,
        out_shape=out_shape,
        input_output_aliases=aliases,
        compiler_params=_params("parallel"),
        name="residual_router_topk",
    )(*args)


def _row_pitch(s):
    return s + 1 if s % 2 == 0 else s


def _gathered_rows(buf, rows, s):
    return jnp.concatenate([buf[pl.ds(c, rows, stride=_row_pitch(s)), :] for c in range(s)], axis=1)


def _start_row_gathers(src_hbm, idx_ref, idx_base, idx_step, dst, rows, s, sem):
    pitch = _row_pitch(s)
    trips = rows // GATHER_UNROLL

    def trip(i, carry):
        for u in range(GATHER_UNROLL):
            r = u * trips + i
            idx = idx_ref[idx_base + r * idx_step]
            pltpu.make_async_copy(src_hbm.at[pl.ds(idx * s, s)], dst.at[pl.ds(r * pitch, s)],
                                  sem).start(priority=u % 2)
        return carry

    lax.fori_loop(0, trips, trip, 0)


def _wait_row_gathers(dst, rows, s, sem):
    landed = dst.at[pl.ds(0, rows * s)]
    pltpu.make_async_copy(landed, landed, sem).wait()


def _expert_tile_schedule(ce_ref, nx_ref, nv_ref, c, j, n_pass):
    first = (c == 0) | (ce_ref[c] != ce_ref[jnp.maximum(c - 1, 0)])
    more_in_pass = nx_ref[c] >= 0
    has_next = more_in_pass | (j + 1 < n_pass)
    next_e = jnp.where(more_in_pass, nx_ref[c], ce_ref[0])
    next_j = jnp.where(more_in_pass, j, j + 1)
    return (c < nv_ref[0]) & first, has_next, next_j, next_e


def _moe_up_kernel(f_lo, nf, gather, ce_ref, nx_ref, nv_ref, *refs):
    if gather:
        tok_ref, tok_next_ref, g_ref, bg_ref, bl_ref, xt_hbm, w_hbm, o_ref, xs_ref, gbuf, gsem, raw, w16, sem = refs
    else:
        x_ref, bg_ref, bl_ref, w_hbm, _, o_ref, raw, w16, sem = refs
    j = pl.program_id(0)
    c = pl.program_id(1)
    tf = raw.shape[2]
    n_pass = pl.num_programs(0)
    dff = nf * tf

    def fetch(jj, e, k):
        col = pl.multiple_of(k * dff + (f_lo + jj) * tf, LANES)
        return pltpu.make_async_copy(w_hbm.at[e, :, pl.ds(col, tf)], raw.at[k], sem.at[k])

    first, has_next, next_j, next_e = _expert_tile_schedule(ce_ref, nx_ref, nv_ref, c, j, n_pass)

    if gather:
        rows, d = xs_ref.shape
        s = d // LANES
        slot = c % 2

        @pl.when(c == 0)
        def _():
            _start_row_gathers(xt_hbm, tok_ref, 0, 1, gbuf.at[0], rows, s, gsem.at[0])

        @pl.when(c + 1 < nv_ref[0])
        def _():
            _start_row_gathers(xt_hbm, tok_next_ref, 0, 1, gbuf.at[1 - slot], rows, s, gsem.at[1 - slot])

    @pl.when(first & (j == 0) & (c == 0))
    def _():
        for k in range(2):
            fetch(0, ce_ref[0], k).start()

    @pl.when(first)
    def _():
        for k in range(2):
            fetch(j, ce_ref[c], k).wait()
            w16[k] = raw[k].astype(BF16)

    @pl.when(first & has_next)
    def _():
        for k in range(2):
            fetch(next_j, next_e, k).start()

    @pl.when(c < nv_ref[0])
    def _():
        if gather:
            _wait_row_gathers(gbuf.at[slot], rows, s, gsem.at[slot])
            x = _rms(_gathered_rows(gbuf.at[slot], rows, s), g_ref[...]).astype(BF16)
            xs_ref[...] = x
        else:
            x = x_ref[...]
        gate = jnp.minimum(_dot(x, w16[0]) + bg_ref[0], SWIGLU_LIMIT)
        lin = jnp.clip(_dot(x, w16[1]) + bl_ref[0], -SWIGLU_LIMIT, SWIGLU_LIMIT)
        o_ref[...] = (gate * jax.nn.sigmoid(SWIGLU_ALPHA * gate) * (lin + 1.0)).astype(o_ref.dtype)


def _moe_up(x1t, g_ffn, src_tok, w_up, b_up, chunk_expert, next_expert, n_valid, n_chunks):
    rows = MOE_ROWS
    e, d, f2 = w_up.shape
    s = d // LANES
    dff = f2 // 2
    tf = _pick(dff, (1024, 512, 256, 128))
    nf = dff // tf
    assert rows % GATHER_UNROLL == 0
    b3 = b_up.reshape(e, 1, f2)
    weight_scratch = [pltpu.VMEM((2, d, tf), F32), pltpu.VMEM((2, d, tf), BF16), pltpu.SemaphoreType.DMA((2,))]
    act_shape = jax.ShapeDtypeStruct((n_chunks * rows, dff), BF16)

    def last(c, nv):
        return jnp.minimum(c, nv[0] - 1)

    def nxt(c, nv):
        return jnp.minimum(c + 1, nv[0] - 1)

    def bias_spec(off):
        return pl.BlockSpec((1, 1, tf), lambda j, c, ce, nx, nv: (ce[c], 0, off + j))

    def act_spec(f_lo):
        return pl.BlockSpec((rows, tf), lambda j, c, ce, nx, nv: (last(c, nv), f_lo + j))

    first_spec = pltpu.PrefetchScalarGridSpec(
        num_scalar_prefetch=3,
        grid=(1, n_chunks),
        in_specs=[pl.BlockSpec((rows,), lambda j, c, ce, nx, nv: (last(c, nv),), memory_space=pltpu.SMEM),
                  pl.BlockSpec((rows,), lambda j, c, ce, nx, nv: (nxt(c, nv),), memory_space=pltpu.SMEM),
                  pl.BlockSpec((1, d), lambda j, c, ce, nx, nv: (0, 0)),
                  bias_spec(0), bias_spec(nf),
                  pl.BlockSpec(memory_space=pl.ANY),
                  pl.BlockSpec(memory_space=pl.ANY)],
        out_specs=[act_spec(0),
                   pl.BlockSpec((rows, d), lambda j, c, ce, nx, nv: (last(c, nv), 0))],
        scratch_shapes=[pltpu.VMEM((2, rows * _row_pitch(s), LANES), F32), pltpu.SemaphoreType.DMA((2,))]
        + weight_scratch,
    )
    act, xs = pl.pallas_call(
        functools.partial(_moe_up_kernel, 0, nf, True),
        grid_spec=first_spec,
        out_shape=[act_shape, jax.ShapeDtypeStruct((n_chunks * rows, d), BF16)],
        compiler_params=_params("arbitrary", "arbitrary"),
        name="moe_gather_up_swiglu",
    )(chunk_expert, next_expert, n_valid, src_tok, src_tok, g_ffn.reshape(1, d), b3, b3, x1t, w_up)
    if nf == 1:
        return act
    rest_spec = pltpu.PrefetchScalarGridSpec(
        num_scalar_prefetch=3,
        grid=(nf - 1, n_chunks),
        in_specs=[pl.BlockSpec((rows, d), lambda j, c, ce, nx, nv: (last(c, nv), 0)),
                  bias_spec(1), bias_spec(nf + 1),
                  pl.BlockSpec(memory_space=pl.ANY),
                  pl.BlockSpec(memory_space=pl.ANY)],
        out_specs=act_spec(1),
        scratch_shapes=weight_scratch,
    )
    return pl.pallas_call(
        functools.partial(_moe_up_kernel, 1, nf, False),
        grid_spec=rest_spec,
        out_shape=act_shape,
        input_output_aliases={7: 0},
        compiler_params=_params("arbitrary", "arbitrary"),
        name="moe_up_swiglu",
    )(chunk_expert, next_expert, n_valid, xs, b3, b3, w_up, act)


def _moe_down_kernel(ce_ref, nx_ref, nv_ref, a_ref, b_ref, w_hbm, o_ref, raw, w16, sem):
    c = pl.program_id(0)

    def fetch(e):
        return pltpu.make_async_copy(w_hbm.at[e], raw, sem)

    first, has_next, _, next_e = _expert_tile_schedule(ce_ref, nx_ref, nv_ref, c, 0, 1)

    @pl.when(first & (c == 0))
    def _():
        fetch(ce_ref[0]).start()

    @pl.when(first)
    def _():
        fetch(ce_ref[c]).wait()
        w16[...] = raw[...].astype(BF16)

    @pl.when(first & has_next)
    def _():
        fetch(next_e).start()

    @pl.when(c < nv_ref[0])
    def _():
        y = _dot(a_ref[...], w16[...]) + b_ref[0]
        _store_row_tiles(o_ref, y, y.shape[0])


def _moe_down(act, w_down, b_down, chunk_expert, next_expert, n_valid):
    rows = MOE_ROWS
    n_chunks = act.shape[0] // rows
    e, dff, d = w_down.shape
    s = d // LANES

    def last(c, nv):
        return jnp.minimum(c, nv[0] - 1)

    grid_spec = pltpu.PrefetchScalarGridSpec(
        num_scalar_prefetch=3,
        grid=(n_chunks,),
        in_specs=[pl.BlockSpec((rows, dff), lambda c, ce, nx, nv: (last(c, nv), 0)),
                  pl.BlockSpec((1, 1, d), lambda c, ce, nx, nv: (ce[c], 0, 0)),
                  pl.BlockSpec(memory_space=pl.ANY)],
        out_specs=pl.BlockSpec((rows * s, LANES), lambda c, ce, nx, nv: (last(c, nv), 0)),
        scratch_shapes=[pltpu.VMEM((dff, d), F32), pltpu.VMEM((dff, d), BF16),
                        pltpu.SemaphoreType.DMA(())],
    )
    return pl.pallas_call(
        _moe_down_kernel,
        grid_spec=grid_spec,
        out_shape=jax.ShapeDtypeStruct((n_chunks * rows * s, LANES), F32),
        compiler_params=_params("arbitrary"),
        name="moe_down",
    )(chunk_expert, next_expert, n_valid, act, b_down.reshape(e, 1, d), w_down)


def _combine_kernel(dest_ref, dest_next_ref, x1_ref, gate_ref, gple_ref, yt_hbm, x2_ref, h3_ref, buf, sem):
    i = pl.program_id(0)
    rows = x1_ref.shape[0]
    s = x1_ref.shape[1] // LANES
    slot = i % 2

    def start(idx_ref, to):
        for k in range(TOP_K):
            _start_row_gathers(yt_hbm, idx_ref, k, TOP_K, buf.at[to, k], rows, s, sem.at[to])

    @pl.when(i == 0)
    def _():
        start(dest_ref, 0)

    @pl.when(i + 1 < pl.num_programs(0))
    def _():
        start(dest_next_ref, 1 - slot)

    for k in range(TOP_K):
        _wait_row_gathers(buf.at[slot, k], rows, s, sem.at[slot])
    gate = gate_ref[...]
    acc = x1_ref[...]
    for k in range(TOP_K):
        acc = acc + gate[:, k:k + 1] * _gathered_rows(buf.at[slot, k], rows, s)
    x2_ref[...] = acc
    h3_ref[...] = _rms(acc, gple_ref[...]).astype(h3_ref.dtype)


def _combine(x1, gate, dest, yt, g_ple):
    n, d = x1.shape
    s = d // LANES
    rows = _pick(n, (COMBINE_ROWS, 64, 32, 16))
    steps = n // rows
    assert rows % GATHER_UNROLL == 0
    return pl.pallas_call(
        _combine_kernel,
        grid=(steps,),
        in_specs=[pl.BlockSpec((rows * TOP_K,), lambda i: (i,), memory_space=pltpu.SMEM),
                  pl.BlockSpec((rows * TOP_K,), lambda i: (jnp.minimum(i + 1, steps - 1),),
                               memory_space=pltpu.SMEM),
                  pl.BlockSpec((rows, d), lambda i: (i, 0)),
                  pl.BlockSpec((rows, LANES), lambda i: (i, 0)),
                  pl.BlockSpec((1, d), lambda i: (0, 0)),
                  pl.BlockSpec(memory_space=pl.ANY)],
        out_specs=[pl.BlockSpec((rows, d), lambda i: (i, 0)),
                   pl.BlockSpec((rows, d), lambda i: (i, 0))],
        out_shape=[jax.ShapeDtypeStruct((n, d), F32), jax.ShapeDtypeStruct((n, d), BF16)],
        scratch_shapes=[pltpu.VMEM((2, TOP_K, rows * _row_pitch(s), LANES), F32),
                        pltpu.SemaphoreType.DMA((2,))],
        compiler_params=_params("arbitrary"),
        name="moe_combine_gather",
    )(dest, dest, x1, gate, g_ple.reshape(1, d), yt)


def _ple_kernel(x2_ref, h3_ref, p_ref, wpg_ref, wple_ref, o_ref):
    gate = jax.nn.sigmoid(_dot(h3_ref[...], wpg_ref[...]))
    o_ref[...] = x2_ref[...] + gate * _dot(p_ref[...].astype(BF16), wple_ref[...])


def _ple(x2, h3, p, w_pg, w_ple, row0, r):
    d = x2.shape[1]
    pd = p.shape[1]
    tm = _pick(r, (512, 256, 128, 64, 32, 16))
    tn = _pick(d, (1024, 512, 256, 128))
    assert row0 % tm == 0
    b0 = row0 // tm
    return pl.pallas_call(
        _ple_kernel,
        grid=(d // tn, r // tm),
        in_specs=[pl.BlockSpec((tm, tn), lambda j, i: (b0 + i, j)),
                  pl.BlockSpec((tm, d), lambda j, i: (b0 + i, 0)),
                  pl.BlockSpec((tm, pd), lambda j, i: (i, 0)),
                  pl.BlockSpec((d, tn), lambda j, i: (0, j)),
                  pl.BlockSpec((pd, tn), lambda j, i: (0, j))],
        out_specs=pl.BlockSpec((tm, tn), lambda j, i: (i, j)),
        out_shape=jax.ShapeDtypeStruct((r, d), F32),
        compiler_params=_params("parallel", "parallel"),
        name="ple_gate",
    )(x2, h3, p, w_pg, w_ple)


def _rope_tables(pos):
    half = ROT_DIM // 2
    inv = jnp.power(ROPE_THETA, -jnp.arange(half, dtype=F32) * 2.0 / ROT_DIM)
    ang = pos.astype(F32)[:, None] * inv[None, :]
    cos, sin = jnp.cos(ang), jnp.sin(ang)
    r = pos.shape[0]
    rest = A_HEAD_DIM - ROT_DIM
    c = jnp.concatenate([cos, cos, jnp.ones((r, rest), F32)], axis=1)
    s1 = jnp.concatenate([-sin, jnp.zeros((r, half + rest), F32)], axis=1)
    s2 = jnp.concatenate([jnp.zeros((r, half), F32), sin, jnp.zeros((r, rest), F32)], axis=1)
    return c, s1, s2


def _mixer_inputs(x, pos, g_attn, w, b_i, b_f, g_q, g_k, mlstm_dtype):
    d = x.shape[1]
    kq = M_HEADS * M_QK
    o_ig = 2 * kq + M_WIDTH
    o_om = o_ig + 2 * M_HEADS
    o_qa = o_om + M_WIDTH
    o_gm = o_qa + 3 * A_WIDTH
    h = _rmsnorm_bf16(x, g_attn)
    colscale = jnp.concatenate([jnp.full((1, kq), M_QK ** -0.5, F32), jnp.ones((1, kq + M_WIDTH), F32)], axis=1)
    qkv_m = _proj_scale(h, w[:, :o_ig].astype(BF16), colscale, mlstm_dtype)
    wg = w[:, o_ig:o_om]
    wc = jnp.pad(wg, ((0, 0), (0, LANES - 2 * M_HEADS))).astype(BF16)
    bias = jnp.concatenate([b_i, b_f]).astype(F32)
    bc = jnp.pad(bias, (0, LANES - 2 * M_HEADS)).reshape(1, LANES)
    gcol, grow = _proj_gates(h, wc, wg.T.astype(BF16), bc, bias.reshape(2 * M_HEADS, 1))
    sg = _proj_sigmoid(h, jnp.concatenate([w[:, o_om:o_qa], w[:, o_gm:]], axis=1).astype(BF16))
    tabs = _rope_tables(pos)
    q32, _ = _proj_qk(h, w[:, o_qa:o_qa + A_WIDTH].astype(BF16), g_q, *tabs)
    k32, k16 = _proj_qk(h, w[:, o_qa + A_WIDTH:o_qa + 2 * A_WIDTH].astype(BF16), g_k, *tabs)
    v32, v16 = _proj_plain2(h, w[:, o_qa + 2 * A_WIDTH:o_gm].astype(BF16))
    return qkv_m, gcol, grow, sg, q32, k32, k16, v32, v16


def _moe_plan(eid, n_chunks):
    n = eid.shape[0]
    e_flat = eid[:, :TOP_K].reshape(-1)
    onehot = (e_flat[:, None] == jnp.arange(N_EXPERTS, dtype=jnp.int32)[None, :]).astype(jnp.int32)
    before = jnp.cumsum(onehot, axis=0) - onehot
    rank = jnp.sum(before * onehot, axis=1)
    counts = jnp.sum(onehot, axis=0)
    padded = (counts + MOE_ROWS - 1) // MOE_ROWS * MOE_ROWS
    p_ends = jnp.cumsum(padded)
    p_starts = p_ends - padded
    dest = (p_starts[e_flat] + rank).astype(jnp.int32)
    tok = jnp.repeat(jnp.arange(n, dtype=jnp.int32), TOP_K)
    src_tok = jnp.zeros((n_chunks * MOE_ROWS,), jnp.int32).at[dest].set(tok)
    n_valid = (p_ends[-1] // MOE_ROWS).astype(jnp.int32).reshape(1)
    chunk0 = jnp.minimum(jnp.arange(n_chunks, dtype=jnp.int32), n_valid[0] - 1) * MOE_ROWS
    chunk_expert = jnp.minimum(jnp.sum((p_ends[None, :] <= chunk0[:, None]).astype(jnp.int32), axis=1),
                               N_EXPERTS - 1)
    ids = jnp.arange(N_EXPERTS, dtype=jnp.int32)
    later = (ids[None, :] > chunk_expert[:, None]) & (counts[None, :] > 0)
    next_expert = jnp.min(jnp.where(later, ids[None, :], N_EXPERTS), axis=1)
    next_expert = jnp.where(next_expert < N_EXPERTS, next_expert, -1).astype(jnp.int32)
    return dest, src_tok, n_valid, chunk_expert, next_expert


def kernel(x_prompt, x_sample, cache_k, cache_v, state_mlstm_C, state_mlstm_n, state_mlstm_m, page_table, p_prompt, p_sample, g_attn, w_in, b_i, b_f, g_q, g_k, g_mh, w_bm, w_ba, w_out, g_ffn, w_router, b_router, w_up, b_up, w_down, b_down, g_ple, w_pg, w_ple):
    depth = w_in.shape[0]
    bsz, t, d = x_prompt.shape
    ns, dec_seq, _ = x_sample.shape
    assert depth == 1 and bsz == 1 and dec_seq == 1
    page = cache_k.shape[2]
    past = page_table.shape[1] * page
    assert t % MOBA_BLOCK == 0 and past % MOBA_BLOCK == 0 and MOBA_BLOCK % page == 0
    assert t % (M_CHUNK * MLSTM_CHUNKS_PER_STEP) == 0 and ns % MLSTM_SAMPLE_SEQS == 0
    n = t + ns

    xp = x_prompt[0]
    xs = x_sample[:, 0]
    w = w_in[0]
    fw = (g_attn[0], w, b_i[0], b_f[0], g_q[0], g_k[0])

    qkv_p, gcol_p, grow_p, sg_p, q32_p, k32_p, k16_p, v32_p, v16_p = _mixer_inputs(
        xp, jnp.arange(t, dtype=jnp.int32), *fw, BF16)
    hm_p, ct_p, n_p, m_p = _mlstm_prompt(qkv_p, gcol_p, grow_p)
    ao_p = _moba_prompt(q32_p, k16_p, v16_p, _block_kmean(k32_p))

    qkv_s, gcol_s, _, sg_s, q32_s, k32_s, _, v32_s, _ = _mixer_inputs(
        xs, jnp.full((ns,), past, jnp.int32), *fw, F32)
    hm_s, c_s, n_s, m_s = _mlstm_sample(qkv_s, gcol_s, state_mlstm_C[0].astype(F32),
                                        state_mlstm_n[0].astype(F32), state_mlstm_m[0].astype(F32))
    ao_s = _moba_sample(q32_s, k32_s, v32_s, cache_k, cache_v, page_table)

    wbm, wba, wout = w_bm[0].astype(BF16), w_ba[0].astype(BF16), w_out[0].astype(BF16)
    mix_p = _mix(hm_p, sg_p, ao_p, g_mh[0], wbm, wba)
    mix_s = _mix(hm_s, sg_s, ao_s, g_mh[0], wbm, wba)
    wr = jnp.pad(w_router[0], ((0, 0), (0, LANES - N_EXPERTS)))
    wr_hi = wr.astype(BF16)
    wr_lo = (wr - wr_hi.astype(F32)).astype(BF16)
    b_r = jnp.pad(b_router[0].astype(F32), (0, LANES - N_EXPERTS)).reshape(1, LANES)
    routed = _resid_router(xp, mix_p, wout, g_ffn[0], wr_hi, wr_lo, b_r, n, 0, None)
    x1, x1t, gate, eid = _resid_router(xs, mix_s, wout, g_ffn[0], wr_hi, wr_lo, b_r, n, t, routed)

    n_slots = n * TOP_K
    n_chunks = (n_slots + N_EXPERTS * (MOE_ROWS - 1)) // MOE_ROWS
    dest, src_tok, n_valid, chunk_expert, next_expert = _moe_plan(eid, n_chunks)
    act = _moe_up(x1t, g_ffn[0], src_tok, w_up[0], b_up[0], chunk_expert, next_expert, n_valid, n_chunks)
    yt = _moe_down(act, w_down[0], b_down[0], chunk_expert, next_expert, n_valid)
    x2, h3 = _combine(x1, gate, dest, yt, g_ple[0])

    wpg, wple = w_pg[0].astype(BF16), w_ple[0].astype(BF16)
    y_p = _ple(x2, h3, p_prompt[0, 0], wpg, wple, 0, t)
    y_s = _ple(x2, h3, p_sample[0, :, 0], wpg, wple, t, ns)

    def heads(a):
        return a.reshape(a.shape[0], A_HEADS, A_HEAD_DIM)

    return (y_p[None], y_s[:, None],
            heads(k32_p)[None, None], heads(v32_p)[None, None],
            jnp.swapaxes(ct_p, 1, 2)[None, None].astype(state_mlstm_C.dtype),
            n_p[:M_HEADS][None, None].astype(state_mlstm_n.dtype),
            m_p[:M_HEADS, 0][None, None].astype(state_mlstm_m.dtype),
            heads(k32_s)[None, :, None], heads(v32_s)[None, :, None],
            c_s[None].astype(state_mlstm_C.dtype), n_s[None].astype(state_mlstm_n.dtype),
            m_s[None].astype(state_mlstm_m.dtype))
```

```python
import functools

import jax
import jax.numpy as jnp
import numpy as np
from jax import lax
from jax.experimental import pallas as pl
from jax.experimental.pallas import tpu as pltpu

F32 = jnp.float32
BF16 = jnp.bfloat16

M_HEADS = 4
M_QK = 128
M_V = 256
M_CHUNK = 64
M_WIDTH = M_HEADS * M_V
A_HEADS = 8
A_HEAD_DIM = 128
A_WIDTH = A_HEADS * A_HEAD_DIM
MOBA_BLOCK = 256
MOBA_TOPK = 3
ROT_DIM = A_HEAD_DIM // 4
ROPE_THETA = 500000.0
N_EXPERTS = 32
TOP_K = 4
SWIGLU_LIMIT = 7.0
SWIGLU_ALPHA = 1.702
EPS = 1e-6
NEG = -1e30

LANES = 128
SUBLANES = 8
VMEM_LIMIT_BYTES = 56 * 1024 * 1024

MOE_ROWS = 256
MLSTM_CHUNKS_PER_STEP = 2
MLSTM_SAMPLE_SEQS = 8
COMBINE_ROWS = 128
MOBA_HEADS_PER_STEP = 4
MOBA_BLOCKS_PER_STEP = 4
GATHER_UNROLL = 8


def _pick(n, candidates):
    for c in candidates:
        if n % c == 0:
            return c
    raise ValueError(f"no tile in {candidates} divides {n}")


def _params(*sem):
    return pltpu.CompilerParams(dimension_semantics=sem, vmem_limit_bytes=VMEM_LIMIT_BYTES)


def _dot(a, b):
    return jnp.dot(a, b, preferred_element_type=F32)


def _dot_nt(a, b):
    return lax.dot_general(a, b, (((1,), (1,)), ((), ())), preferred_element_type=F32)


def _dot_tn(a, b):
    return lax.dot_general(a, b, (((0,), (0,)), ((), ())), preferred_element_type=F32)


def _split_bf16(x):
    hi = x.astype(BF16)
    lo = (x - hi.astype(F32)).astype(BF16)
    return hi, lo


def _rms(x, g):
    return x * lax.rsqrt(jnp.mean(x * x, axis=-1, keepdims=True) + EPS) * g


def _log_sigmoid(x):
    return -(jnp.maximum(-x, 0.0) + jnp.log1p(jnp.exp(-jnp.abs(x))))


def _first_argmax(work, lane_f):
    m = jnp.max(work, axis=-1, keepdims=True)
    idx = jnp.min(jnp.where(work == m, lane_f, float(4 * LANES)), axis=-1, keepdims=True)
    return m, idx


def _rmsnorm_kernel(x_ref, g_ref, o_ref):
    o_ref[...] = _rms(x_ref[...], g_ref[...]).astype(o_ref.dtype)


def _rmsnorm_bf16(x, g):
    r, d = x.shape
    tm = _pick(r, (512, 256, 128, 64, 32, 16))
    return pl.pallas_call(
        _rmsnorm_kernel,
        grid=(r // tm,),
        in_specs=[pl.BlockSpec((tm, d), lambda i: (i, 0)), pl.BlockSpec((1, d), lambda i: (0, 0))],
        out_specs=pl.BlockSpec((tm, d), lambda i: (i, 0)),
        out_shape=jax.ShapeDtypeStruct((r, d), BF16),
        compiler_params=_params("parallel"),
        name="rmsnorm_rows",
    )(x, g.reshape(1, d))


def _proj_scale_kernel(h_ref, w_ref, s_ref, o_ref):
    o_ref[...] = (_dot(h_ref[...], w_ref[...]) * s_ref[...]).astype(o_ref.dtype)


def _proj_sigmoid_kernel(h_ref, w_ref, o_ref):
    o_ref[...] = jax.nn.sigmoid(_dot(h_ref[...], w_ref[...])).astype(o_ref.dtype)


def _proj_plain2_kernel(h_ref, w_ref, o32_ref, o16_ref):
    acc = _dot(h_ref[...], w_ref[...])
    o32_ref[...] = acc
    o16_ref[...] = acc.astype(BF16)


def _proj_qk_kernel(h_ref, w_ref, g_ref, c_ref, s1_ref, s2_ref, o32_ref, o16_ref):
    acc = _dot(h_ref[...], w_ref[...])
    g = g_ref[...]
    c, s1, s2 = c_ref[...], s1_ref[...], s2_ref[...]
    for hh in range(acc.shape[1] // A_HEAD_DIM):
        sl = slice(hh * A_HEAD_DIM, (hh + 1) * A_HEAD_DIM)
        y = _rms(acc[:, sl], g)
        up = pltpu.roll(y, A_HEAD_DIM - ROT_DIM // 2, 1)
        dn = pltpu.roll(y, ROT_DIM // 2, 1)
        r = y * c + up * s1 + dn * s2
        o32_ref[:, sl] = r
        o16_ref[:, sl] = r.astype(BF16)


def _proj_gates_kernel(h_ref, wc_ref, wr_ref, bc_ref, br_ref, gc_ref, gr_ref):
    h = h_ref[...]
    zc = _dot(h, wc_ref[...]) + bc_ref[...]
    zr = _dot_nt(wr_ref[...], h) + br_ref[...]
    lane = lax.broadcasted_iota(jnp.int32, zc.shape, 1)
    gc_ref[...] = jnp.where((lane >= M_HEADS) & (lane < 2 * M_HEADS), _log_sigmoid(zc), zc)
    row = lax.broadcasted_iota(jnp.int32, zr.shape, 0)
    gr_ref[...] = jnp.where(row >= M_HEADS, _log_sigmoid(zr), zr)


def _proj_tiles(r, n):
    tm = _pick(r, (1024, 512, 256, 128, 64, 32, 16))
    tn = _pick(n, (1024, 512, 256, 128))
    return tm, tn


def _proj_scale(h, w, colscale, out_dtype):
    r, d = h.shape
    n = w.shape[1]
    tm, tn = _proj_tiles(r, n)
    return pl.pallas_call(
        _proj_scale_kernel,
        grid=(n // tn, r // tm),
        in_specs=[pl.BlockSpec((tm, d), lambda j, i: (i, 0)),
                  pl.BlockSpec((d, tn), lambda j, i: (0, j)),
                  pl.BlockSpec((1, tn), lambda j, i: (0, j))],
        out_specs=pl.BlockSpec((tm, tn), lambda j, i: (i, j)),
        out_shape=jax.ShapeDtypeStruct((r, n), out_dtype),
        compiler_params=_params("parallel", "parallel"),
        name="proj_scale",
    )(h, w, colscale)


def _proj_sigmoid(h, w):
    r, d = h.shape
    n = w.shape[1]
    tm, tn = _proj_tiles(r, n)
    return pl.pallas_call(
        _proj_sigmoid_kernel,
        grid=(n // tn, r // tm),
        in_specs=[pl.BlockSpec((tm, d), lambda j, i: (i, 0)),
                  pl.BlockSpec((d, tn), lambda j, i: (0, j))],
        out_specs=pl.BlockSpec((tm, tn), lambda j, i: (i, j)),
        out_shape=jax.ShapeDtypeStruct((r, n), BF16),
        compiler_params=_params("parallel", "parallel"),
        name="proj_sigmoid",
    )(h, w)


def _proj_plain2(h, w):
    r, d = h.shape
    n = w.shape[1]
    tm, tn = _proj_tiles(r, n)
    return pl.pallas_call(
        _proj_plain2_kernel,
        grid=(n // tn, r // tm),
        in_specs=[pl.BlockSpec((tm, d), lambda j, i: (i, 0)),
                  pl.BlockSpec((d, tn), lambda j, i: (0, j))],
        out_specs=[pl.BlockSpec((tm, tn), lambda j, i: (i, j)),
                   pl.BlockSpec((tm, tn), lambda j, i: (i, j))],
        out_shape=[jax.ShapeDtypeStruct((r, n), F32), jax.ShapeDtypeStruct((r, n), BF16)],
        compiler_params=_params("parallel", "parallel"),
        name="proj_plain",
    )(h, w)


def _proj_qk(h, w, g, rope_c, rope_s1, rope_s2):
    r, d = h.shape
    n = w.shape[1]
    tm, tn = _proj_tiles(r, n)
    hd = A_HEAD_DIM
    return pl.pallas_call(
        _proj_qk_kernel,
        grid=(n // tn, r // tm),
        in_specs=[pl.BlockSpec((tm, d), lambda j, i: (i, 0)),
                  pl.BlockSpec((d, tn), lambda j, i: (0, j)),
                  pl.BlockSpec((1, hd), lambda j, i: (0, 0)),
                  pl.BlockSpec((tm, hd), lambda j, i: (i, 0)),
                  pl.BlockSpec((tm, hd), lambda j, i: (i, 0)),
                  pl.BlockSpec((tm, hd), lambda j, i: (i, 0))],
        out_specs=[pl.BlockSpec((tm, tn), lambda j, i: (i, j)),
                   pl.BlockSpec((tm, tn), lambda j, i: (i, j))],
        out_shape=[jax.ShapeDtypeStruct((r, n), F32), jax.ShapeDtypeStruct((r, n), BF16)],
        compiler_params=_params("parallel", "parallel"),
        name="proj_qk_norm_rope",
    )(h, w, g.reshape(1, hd), rope_c, rope_s1, rope_s2)


def _proj_gates(h, wc, wr, bc, br):
    r, d = h.shape
    tm = _pick(r, (1024, 512, 256, 128)) if r % LANES == 0 else r
    g2 = 2 * M_HEADS
    return pl.pallas_call(
        _proj_gates_kernel,
        grid=(r // tm,),
        in_specs=[pl.BlockSpec((tm, d), lambda i: (i, 0)),
                  pl.BlockSpec((d, LANES), lambda i: (0, 0)),
                  pl.BlockSpec((g2, d), lambda i: (0, 0)),
                  pl.BlockSpec((1, LANES), lambda i: (0, 0)),
                  pl.BlockSpec((g2, 1), lambda i: (0, 0))],
        out_specs=[pl.BlockSpec((tm, LANES), lambda i: (i, 0)),
                   pl.BlockSpec((g2, tm), lambda i: (0, i))],
        out_shape=[jax.ShapeDtypeStruct((r, LANES), F32), jax.ShapeDtypeStruct((g2, r), F32)],
        compiler_params=_params("parallel"),
        name="proj_gates",
    )(h, wc, wr, bc, br)


def _mlstm_prompt_kernel(q_ref, k_ref, v_ref, gc_ref, gr_ref, h_ref, ct_out, n_out, m_out,
                         ct_s, n_s, m_s):
    step = pl.program_id(0)

    @pl.when(step == 0)
    def _():
        ct_s[...] = jnp.zeros_like(ct_s)
        n_s[...] = jnp.zeros_like(n_s)
        m_s[...] = jnp.zeros_like(m_s)

    ln = M_CHUNK
    row = lax.broadcasted_iota(jnp.int32, (ln, ln), 0)
    col = lax.broadcasted_iota(jnp.int32, (ln, ln), 1)
    causal = col <= row
    upto = row <= col
    pre = {}
    for cc in range(MLSTM_CHUNKS_PER_STEP):
        rs = slice(cc * ln, (cc + 1) * ln)
        for hh in range(M_HEADS):
            ig_c = gc_ref[rs, hh:hh + 1]
            lf_c = gc_ref[rs, M_HEADS + hh:M_HEADS + hh + 1]
            ig_r = gr_ref[hh:hh + 1, rs]
            lf_r = gr_ref[M_HEADS + hh:M_HEADS + hh + 1, rs]
            bcum_c = jnp.sum(jnp.where(causal, lf_r, 0.0), axis=1, keepdims=True)
            bcum_r = jnp.sum(jnp.where(upto, lf_c, 0.0), axis=0, keepdims=True)
            dmat = jnp.where(causal, bcum_c - bcum_r + ig_r, NEG)
            qh = q_ref[rs, hh * M_QK:(hh + 1) * M_QK]
            kh = k_ref[rs, hh * M_QK:(hh + 1) * M_QK]
            b_last = bcum_c[ln - 1:ln, :]
            pre[cc, hh] = dict(
                bcum_c=bcum_c, dmat=dmat, dmax=jnp.max(dmat, axis=1, keepdims=True),
                qk=_dot_nt(qh, kh), b_last=b_last,
                d_last_c=b_last - bcum_c + ig_c,
                d_last_max=jnp.max(b_last - bcum_r + ig_r, axis=1, keepdims=True))

    for hh in range(M_HEADS):
        ct = ct_s[hh]
        n_row = n_s[hh:hh + 1, :]
        m_prev = m_s[hh:hh + 1, 0:1]
        for cc in range(MLSTM_CHUNKS_PER_STEP):
            rs = slice(cc * ln, (cc + 1) * ln)
            u = pre[cc, hh]
            qh = q_ref[rs, hh * M_QK:(hh + 1) * M_QK]
            kh = k_ref[rs, hh * M_QK:(hh + 1) * M_QK]
            vh = v_ref[rs, hh * M_V:(hh + 1) * M_V]
            a = u["bcum_c"] + m_prev
            m_t = jnp.maximum(a, u["dmax"])
            w_inter = jnp.exp(a - m_t)
            s = u["qk"] * jnp.exp(u["dmat"] - m_t)
            num = w_inter * _dot(qh, ct.astype(BF16)) + _dot(s.astype(BF16), vh)
            den = (w_inter * jnp.sum(qh.astype(F32) * n_row, axis=1, keepdims=True)
                   + jnp.sum(s, axis=1, keepdims=True))
            h_ref[rs, hh * M_V:(hh + 1) * M_V] = num / jnp.maximum(jnp.abs(den), jnp.exp(-m_t))
            a_last = u["b_last"] + m_prev
            m_new = jnp.maximum(a_last, u["d_last_max"])
            w_c = jnp.exp(a_last - m_new)
            w_j = jnp.exp(u["d_last_c"] - m_new)
            vw = (vh.astype(F32) * w_j).astype(BF16)
            ct = w_c * ct + _dot_tn(kh, vw)
            n_row = w_c * n_row + jnp.sum(kh.astype(F32) * w_j, axis=0, keepdims=True)
            m_prev = m_new
        ct_s[hh] = ct
        n_s[hh:hh + 1, :] = n_row
        m_s[hh:hh + 1, :] = jnp.broadcast_to(m_prev, (1, LANES))

    @pl.when(step == pl.num_programs(0) - 1)
    def _():
        ct_out[...] = ct_s[...]
        n_out[...] = n_s[...]
        m_out[...] = m_s[...]


def _mlstm_prompt(qkv, gcol, grow):
    t = qkv.shape[0]
    rows = M_CHUNK * MLSTM_CHUNKS_PER_STEP
    kq = M_HEADS * M_QK
    return pl.pallas_call(
        _mlstm_prompt_kernel,
        grid=(t // rows,),
        in_specs=[pl.BlockSpec((rows, kq), lambda c: (c, 0)),
                  pl.BlockSpec((rows, kq), lambda c: (c, 1)),
                  pl.BlockSpec((rows, M_WIDTH), lambda c: (c, (2 * kq) // M_WIDTH)),
                  pl.BlockSpec((rows, LANES), lambda c: (c, 0)),
                  pl.BlockSpec((2 * M_HEADS, rows), lambda c: (0, c))],
        out_specs=[pl.BlockSpec((rows, M_WIDTH), lambda c: (c, 0)),
                   pl.BlockSpec((M_HEADS, M_QK, M_V), lambda c: (0, 0, 0)),
                   pl.BlockSpec((SUBLANES, LANES), lambda c: (0, 0)),
                   pl.BlockSpec((SUBLANES, LANES), lambda c: (0, 0))],
        out_shape=[jax.ShapeDtypeStruct((t, M_WIDTH), F32),
                   jax.ShapeDtypeStruct((M_HEADS, M_QK, M_V), F32),
                   jax.ShapeDtypeStruct((SUBLANES, LANES), F32),
                   jax.ShapeDtypeStruct((SUBLANES, LANES), F32)],
        scratch_shapes=[pltpu.VMEM((M_HEADS, M_QK, M_V), F32),
                        pltpu.VMEM((SUBLANES, LANES), F32),
                        pltpu.VMEM((SUBLANES, LANES), F32)],
        compiler_params=_params("arbitrary"),
        name="mlstm_prompt_scan",
    )(qkv, qkv, qkv, gcol, grow)


def _mlstm_sample_kernel(qkv_ref, gc_ref, c_ref, n_ref, m_ref, h_ref, co_ref, no_ref, mo_ref):
    kq = M_HEADS * M_QK
    row8 = lax.broadcasted_iota(jnp.int32, (SUBLANES, 1), 0)
    mrow = lax.broadcasted_iota(jnp.int32, mo_ref.shape, 0)
    mcol = lax.broadcasted_iota(jnp.int32, mo_ref.shape, 1)
    m_all = m_ref[...]
    for s in range(MLSTM_SAMPLE_SEQS):
        for hh in range(M_HEADS):
            q = qkv_ref[s:s + 1, hh * M_QK:(hh + 1) * M_QK]
            k = qkv_ref[s:s + 1, kq + hh * M_QK:kq + (hh + 1) * M_QK]
            v = qkv_ref[s:s + 1, 2 * kq + hh * M_V:2 * kq + (hh + 1) * M_V]
            ig = gc_ref[s:s + 1, hh:hh + 1]
            lf = gc_ref[s:s + 1, M_HEADS + hh:M_HEADS + hh + 1]
            m_prev = m_ref[s:s + 1, hh:hh + 1]
            c = c_ref[s, hh]
            n_row = n_ref[s, hh:hh + 1, :]
            a = lf + m_prev
            m_t = jnp.maximum(a, ig)
            w_c = jnp.exp(a - m_t)
            w_j = jnp.exp(ig - m_t)
            sc = jnp.sum(q * k, axis=1, keepdims=True) * w_j
            q8 = jnp.broadcast_to(q, (SUBLANES, M_QK)).astype(BF16)
            cq = _dot_nt(q8, c.astype(BF16))[0:1, :]
            num = w_c * cq + sc * v
            den = w_c * jnp.sum(n_row * q, axis=1, keepdims=True) + sc
            h_ref[s:s + 1, hh * M_V:(hh + 1) * M_V] = num / jnp.maximum(jnp.abs(den), jnp.exp(-m_t))
            vw8 = jnp.where(row8 == 0, v * w_j, 0.0).astype(BF16)
            k8 = jnp.where(row8 == 0, k, 0.0).astype(BF16)
            co_ref[s, hh] = w_c * c + _dot_tn(vw8, k8)
            no_ref[s, hh:hh + 1, :] = w_c * n_row + w_j * k
            m_all = jnp.where((mrow == s) & (mcol == hh), m_t, m_all)
    mo_ref[...] = m_all


def _mlstm_sample(qkv, gcol, c0, n0, m0):
    ns = qkv.shape[0]
    sb = MLSTM_SAMPLE_SEQS
    wq = qkv.shape[1]
    return pl.pallas_call(
        _mlstm_sample_kernel,
        grid=(ns // sb,),
        in_specs=[pl.BlockSpec((sb, wq), lambda i: (i, 0)),
                  pl.BlockSpec((sb, LANES), lambda i: (i, 0)),
                  pl.BlockSpec((sb, M_HEADS, M_V, M_QK), lambda i: (i, 0, 0, 0)),
                  pl.BlockSpec((sb, M_HEADS, M_QK), lambda i: (i, 0, 0)),
                  pl.BlockSpec((sb, M_HEADS), lambda i: (i, 0))],
        out_specs=[pl.BlockSpec((sb, M_WIDTH), lambda i: (i, 0)),
                   pl.BlockSpec((sb, M_HEADS, M_V, M_QK), lambda i: (i, 0, 0, 0)),
                   pl.BlockSpec((sb, M_HEADS, M_QK), lambda i: (i, 0, 0)),
                   pl.BlockSpec((sb, M_HEADS), lambda i: (i, 0))],
        out_shape=[jax.ShapeDtypeStruct((ns, M_WIDTH), F32),
                   jax.ShapeDtypeStruct(c0.shape, F32),
                   jax.ShapeDtypeStruct(n0.shape, F32),
                   jax.ShapeDtypeStruct(m0.shape, F32)],
        compiler_params=_params("parallel"),
        name="mlstm_sample_step",
    )(qkv, gcol, c0, n0, m0)


def _kmean_kernel(k_ref, o_ref):
    o_ref[0] = jnp.mean(k_ref[...], axis=0, keepdims=True)


def _block_kmean(k32):
    t, w = k32.shape
    nb = t // MOBA_BLOCK
    out = pl.pallas_call(
        _kmean_kernel,
        grid=(nb,),
        in_specs=[pl.BlockSpec((MOBA_BLOCK, w), lambda b: (b, 0))],
        out_specs=pl.BlockSpec((1, 1, w), lambda b: (b, 0, 0)),
        out_shape=jax.ShapeDtypeStruct((nb, 1, w), F32),
        compiler_params=_params("parallel"),
        name="moba_block_kmean",
    )(k32)
    return out.reshape(nb, w)


def _moba_select(sc, own, lane_i, lane_f):
    work = jnp.where(lane_i < own, sc, NEG)
    sel = jnp.zeros(sc.shape, dtype=jnp.bool_)
    for _ in range(MOBA_TOPK):
        _, idx = _first_argmax(work, lane_f)
        hit = lane_f == idx
        sel = sel | hit
        work = jnp.where(hit, -jnp.inf, work)
    return sel & (lane_i < own)


def _moba_prompt_kernel(q_ref, k_ref, v_ref, e_ref, km_ref, o_ref):
    i = pl.program_id(1)
    bs = MOBA_BLOCK
    pw = MOBA_BLOCKS_PER_STEP * bs
    hd = A_HEAD_DIM
    heads = range(MOBA_HEADS_PER_STEP)
    ones = jnp.ones((pw, hd), BF16)
    lane_i = lax.broadcasted_iota(jnp.int32, (bs, LANES), 1)
    lane_f = lane_i.astype(F32)

    def head_slice(hh):
        return slice(hh * hd, (hh + 1) * hd)

    q_aug = []
    for hh in heads:
        q = q_ref[:, head_slice(hh)]
        qh, ql = _split_bf16(q)
        kmh, kml = _split_bf16(km_ref[:, head_slice(hh)])
        sc = _dot_nt(qh, kmh) + (_dot_nt(qh, kml) + _dot_nt(ql, kmh))
        sel = _moba_select(sc, i, lane_i, lane_f)
        bias = jnp.where(sel | (lane_i >= i), 0.0, NEG)
        q_aug.append(jnp.concatenate([(q * (hd ** -0.5)).astype(BF16), bias.astype(BF16)], axis=1))

    def pair_scores(p, hh, onehot):
        off = pl.multiple_of(p * pw, pw)
        k_aug = jnp.concatenate([k_ref[pl.ds(off, pw), head_slice(hh)], onehot], axis=1)
        v_aug = jnp.concatenate([v_ref[pl.ds(off, pw), head_slice(hh)], ones], axis=1)
        return _dot_nt(q_aug[hh], k_aug), v_aug

    p_own = i // MOBA_BLOCKS_PER_STEP
    row = lax.broadcasted_iota(jnp.int32, (bs, pw), 0)
    col = lax.broadcasted_iota(jnp.int32, (bs, pw), 1)
    causal = col + (p_own * pw - i * bs) <= row
    onehot_own = e_ref[pl.ds(pl.multiple_of(p_own * pw, pw), pw), :]
    init = []
    for hh in heads:
        s, v_aug = pair_scores(p_own, hh, onehot_own)
        s = jnp.where(causal, s, NEG)
        m0 = jnp.max(s, axis=1, keepdims=True)
        init += [m0, _dot(jnp.exp(s - m0).astype(BF16), v_aug)]

    def body(p, carry):
        onehot = e_ref[pl.ds(pl.multiple_of(p * pw, pw), pw), :]
        out = []
        for hh in heads:
            m_i, acc = carry[2 * hh], carry[2 * hh + 1]
            sj, vj = pair_scores(p, hh, onehot)
            m_n = jnp.maximum(m_i, jnp.max(sj, axis=1, keepdims=True))
            out += [m_n, jnp.exp(m_i - m_n) * acc + _dot(jnp.exp(sj - m_n).astype(BF16), vj)]
        return tuple(out)

    final = lax.fori_loop(0, p_own, body, tuple(init))
    for hh in heads:
        acc = final[2 * hh + 1]
        o_ref[:, head_slice(hh)] = (acc[:, :hd] / acc[:, hd:]).astype(o_ref.dtype)


def _moba_prompt(q32, k16, v16, kmean):
    t = q32.shape[0]
    nb = t // MOBA_BLOCK
    hd = MOBA_HEADS_PER_STEP * A_HEAD_DIM
    assert nb <= LANES and nb % MOBA_BLOCKS_PER_STEP == 0 and A_HEADS % MOBA_HEADS_PER_STEP == 0
    block_onehot = (jnp.arange(t, dtype=jnp.int32)[:, None] // MOBA_BLOCK
                    == jnp.arange(LANES, dtype=jnp.int32)[None, :]).astype(BF16)
    kmean_pad = jnp.pad(kmean, ((0, LANES - nb), (0, 0)))
    return pl.pallas_call(
        _moba_prompt_kernel,
        grid=(A_HEADS // MOBA_HEADS_PER_STEP, nb),
        in_specs=[pl.BlockSpec((MOBA_BLOCK, hd), lambda h, i: (i, h)),
                  pl.BlockSpec((t, hd), lambda h, i: (0, h)),
                  pl.BlockSpec((t, hd), lambda h, i: (0, h)),
                  pl.BlockSpec((t, LANES), lambda h, i: (0, 0)),
                  pl.BlockSpec((LANES, hd), lambda h, i: (0, h))],
        out_specs=pl.BlockSpec((MOBA_BLOCK, hd), lambda h, i: (i, h)),
        out_shape=jax.ShapeDtypeStruct((t, A_WIDTH), BF16),
        compiler_params=_params("parallel", "arbitrary"),
        name="moba_prompt_attention",
    )(q32, k16, v16, block_onehot, kmean_pad)


def _moba_sample_kernel(n_pages, pages_per_block, pt_ref, q_ref, kn_ref, vn_ref, *refs):
    del pt_ref
    kp = refs[:n_pages]
    vp = refs[n_pages:2 * n_pages]
    o_ref = refs[2 * n_pages]
    nbp = n_pages // pages_per_block
    hd = A_HEAD_DIM
    rows = kp[0].shape[2]
    page = rows // A_HEADS
    q = q_ref[0]
    lane_i = lax.broadcasted_iota(jnp.int32, (A_HEADS, LANES), 1)

    sc = jnp.zeros((A_HEADS, LANES), F32)
    k16 = []
    for b in range(nbp):
        ksum = jnp.zeros((A_HEADS, hd), F32)
        for pp in range(pages_per_block):
            kf = kp[b * pages_per_block + pp][0, 0]
            ksum = ksum + jnp.sum(kf.reshape(page, A_HEADS, hd), axis=0)
            k16.append(kf.astype(BF16))
        kmean = ksum / float(MOBA_BLOCK)
        sc = jnp.where(lane_i == b, jnp.sum(q * kmean, axis=1, keepdims=True), sc)
    sel = _moba_select(sc, nbp, lane_i, lane_i.astype(F32)).astype(F32)

    qs = q * (hd ** -0.5)
    own = jnp.sum(qs * kn_ref[0], axis=1, keepdims=True)
    s = _dot_nt(qs.astype(BF16), jnp.concatenate(k16, axis=0))
    head_row = lax.broadcasted_iota(jnp.int32, (A_HEADS, rows), 0)
    row_head = lax.broadcasted_iota(jnp.int32, (A_HEADS, rows), 1) % A_HEADS
    mine = head_row == row_head
    s = jnp.concatenate(
        [jnp.where(mine & (sel[:, pg // pages_per_block:pg // pages_per_block + 1] > 0.5),
                   s[:, pg * rows:(pg + 1) * rows], NEG) for pg in range(n_pages)], axis=1)
    m = jnp.maximum(own, jnp.max(s, axis=1, keepdims=True))
    p = jnp.exp(s - m)
    p_own = jnp.exp(own - m)
    l = jnp.sum(p, axis=1, keepdims=True) + p_own
    v16 = jnp.concatenate([vp[pg][0, 0].astype(BF16) for pg in range(n_pages)], axis=0)
    o_ref[0] = (_dot(p.astype(BF16), v16) + p_own * vn_ref[0]) / l


def _moba_sample(q32, kn32, vn32, cache_k, cache_v, page_table):
    ns = q32.shape[0]
    page = cache_k.shape[2]
    n_pages = page_table.shape[1]
    ppb = MOBA_BLOCK // page
    head_spec = pl.BlockSpec((1, A_HEADS, A_HEAD_DIM), lambda s, pt: (s, 0, 0))

    def page_spec(p):
        return pl.BlockSpec((1, 1, page * A_HEADS, A_HEAD_DIM), lambda s, pt: (0, pt[s, p], 0, 0))

    def rows(cache):
        return cache.reshape(cache.shape[0], cache.shape[1], page * A_HEADS, A_HEAD_DIM)

    def heads(a):
        return a.reshape(ns, A_HEADS, A_HEAD_DIM)

    grid_spec = pltpu.PrefetchScalarGridSpec(
        num_scalar_prefetch=1,
        grid=(ns,),
        in_specs=[head_spec, head_spec, head_spec] + [page_spec(p) for p in range(n_pages)] * 2,
        out_specs=head_spec,
    )
    out = pl.pallas_call(
        functools.partial(_moba_sample_kernel, n_pages, ppb),
        grid_spec=grid_spec,
        out_shape=jax.ShapeDtypeStruct((ns, A_HEADS, A_HEAD_DIM), F32),
        compiler_params=_params("parallel"),
        name="moba_sample_attention",
    )(page_table, heads(q32), heads(kn32), heads(vn32), *([rows(cache_k)] * n_pages), *([rows(cache_v)] * n_pages))
    return out.reshape(ns, A_WIDTH)


def _mix_kernel(hm_ref, om_ref, ao_ref, gm_ref, ga_ref, gmh_ref, wbm_ref, wba_ref, o_ref, hs_ref):
    @pl.when(pl.program_id(1) == 0)
    def _():
        for hh in range(M_HEADS):
            sl = slice(hh * M_V, (hh + 1) * M_V)
            y = _rms(hm_ref[:, sl], gmh_ref[:, sl])
            hs_ref[:, sl] = (y * om_ref[:, sl].astype(F32)).astype(BF16)

    t1 = _dot(hs_ref[...], wbm_ref[...])
    t2 = _dot(ao_ref[...].astype(BF16), wba_ref[...])
    o_ref[...] = (gm_ref[...].astype(F32) * t1 + ga_ref[...].astype(F32) * t2).astype(o_ref.dtype)


def _mix(hm, sg, ao, g_mh, w_bm, w_ba):
    r = hm.shape[0]
    d = w_bm.shape[1]
    tm = _pick(r, (512, 256, 128, 64, 32, 16))
    tn = _pick(d, (1024, 512, 256, 128))
    assert M_WIDTH % tn == 0
    gm0 = M_WIDTH // tn
    ga0 = (M_WIDTH + d) // tn
    return pl.pallas_call(
        _mix_kernel,
        grid=(r // tm, d // tn),
        in_specs=[pl.BlockSpec((tm, M_WIDTH), lambda i, j: (i, 0)),
                  pl.BlockSpec((tm, M_WIDTH), lambda i, j: (i, 0)),
                  pl.BlockSpec((tm, A_WIDTH), lambda i, j: (i, 0)),
                  pl.BlockSpec((tm, tn), lambda i, j: (i, gm0 + j)),
                  pl.BlockSpec((tm, tn), lambda i, j: (i, ga0 + j)),
                  pl.BlockSpec((1, M_WIDTH), lambda i, j: (0, 0)),
                  pl.BlockSpec((M_WIDTH, tn), lambda i, j: (0, j)),
                  pl.BlockSpec((A_WIDTH, tn), lambda i, j: (0, j))],
        out_specs=pl.BlockSpec((tm, tn), lambda i, j: (i, j)),
        out_shape=jax.ShapeDtypeStruct((r, d), BF16),
        scratch_shapes=[pltpu.VMEM((tm, M_WIDTH), BF16)],
        compiler_params=_params("parallel", "arbitrary"),
        name="mixer_merge",
    )(hm, sg, ao, sg, sg, g_mh.reshape(1, M_WIDTH), w_bm, w_ba)


def _store_row_tiles(o_ref, x, rows):
    s = x.shape[1] // LANES
    for c in range(s):
        o_ref[pl.ds(c, rows, stride=s), :] = x[:, c * LANES:(c + 1) * LANES]


def _resid_router_kernel(x_ref, mix_ref, wout_ref, gffn_ref, wrh_ref, wrl_ref, br_ref,
                         x1_ref, x1t_ref, gate_ref, eid_ref):
    x1 = x_ref[...] + _dot(mix_ref[...], wout_ref[...])
    x1_ref[...] = x1
    _store_row_tiles(x1t_ref, x1, x1.shape[0])
    hh, hl = _split_bf16(_rms(x1, gffn_ref[...]))
    wrh = wrh_ref[...]
    logits = _dot(hh, wrh) + (_dot(hh, wrl_ref[...]) + _dot(hl, wrh)) + br_ref[...]
    lane_i = lax.broadcasted_iota(jnp.int32, logits.shape, 1)
    lane_f = lane_i.astype(F32)
    work = jnp.where(lane_i < N_EXPERTS, logits, -jnp.inf)
    vals, ids = [], []
    for _ in range(TOP_K):
        m, idx = _first_argmax(work, lane_f)
        vals.append(m)
        ids.append(idx)
        work = jnp.where(lane_f == idx, -jnp.inf, work)
    es = [jnp.exp(v - vals[0]) for v in vals]
    den = es[0]
    for e in es[1:]:
        den = den + e
    gate = jnp.zeros(logits.shape, F32)
    eid = jnp.zeros(logits.shape, F32)
    for k in range(TOP_K):
        gate = jnp.where(lane_i == k, es[k] / den, gate)
        eid = jnp.where(lane_i == k, ids[k], eid)
    gate_ref[...] = gate
    eid_ref[...] = eid.astype(jnp.int32)


def _resid_router(x, mix, w_out, g_ffn, wr_hi, wr_lo, b_r, n_total, row0, prev):
    r, d = x.shape
    s = d // LANES
    tm = _pick(r, (256, 128, 64, 32, 16))
    assert row0 % tm == 0
    b0 = row0 // tm
    out_shape = [jax.ShapeDtypeStruct((n_total, d), F32),
                 jax.ShapeDtypeStruct((n_total * s, LANES), F32),
                 jax.ShapeDtypeStruct((n_total, LANES), F32),
                 jax.ShapeDtypeStruct((n_total, LANES), jnp.int32)]
    out_specs = [pl.BlockSpec((tm, d), lambda i: (b0 + i, 0)),
                 pl.BlockSpec((tm * s, LANES), lambda i: (b0 + i, 0)),
                 pl.BlockSpec((tm, LANES), lambda i: (b0 + i, 0)),
                 pl.BlockSpec((tm, LANES), lambda i: (b0 + i, 0))]
    in_specs = [pl.BlockSpec((tm, d), lambda i: (i, 0)),
                pl.BlockSpec((tm, d), lambda i: (i, 0)),
                pl.BlockSpec((d, d), lambda i: (0, 0)),
                pl.BlockSpec((1, d), lambda i: (0, 0)),
                pl.BlockSpec((d, LANES), lambda i: (0, 0)),
                pl.BlockSpec((d, LANES), lambda i: (0, 0)),
                pl.BlockSpec((1, LANES), lambda i: (0, 0))]
    args = [x, mix, w_out, g_ffn.reshape(1, d), wr_hi, wr_lo, b_r]
    n_in = len(args)
    kern = _resid_router_kernel
    aliases = {}
    if prev is not None:
        in_specs = in_specs + [pl.BlockSpec(memory_space=pl.ANY)] * len(prev)
        args = args + list(prev)
        aliases = {n_in + k: k for k in range(len(prev))}

        def kern(*refs):
            _resid_router_kernel(*refs[:n_in], *refs[n_in + len(prev):])

    return pl.pallas_call(
        kern,
        grid=(r // tm,),
        in_specs=in_specs,
        out_specs=out_specs,
        out_shape=out_shape,
        input_output_aliases=aliases,
        compiler_params=_params("parallel"),
        name="residual_router_topk",
    )(*args)


def _row_pitch(s):
    return s + 1 if s % 2 == 0 else s


def _gathered_rows(buf, rows, s):
    return jnp.concatenate([buf[pl.ds(c, rows, stride=_row_pitch(s)), :] for c in range(s)], axis=1)


def _start_row_gathers(src_hbm, idx_ref, idx_base, idx_step, dst, rows, s, sem):
    pitch = _row_pitch(s)
    trips = rows // GATHER_UNROLL

    def trip(i, carry):
        for u in range(GATHER_UNROLL):
            r = u * trips + i
            idx = idx_ref[idx_base + r * idx_step]
            pltpu.make_async_copy(src_hbm.at[pl.ds(idx * s, s)], dst.at[pl.ds(r * pitch, s)],
                                  sem).start(priority=u % 2)
        return carry

    lax.fori_loop(0, trips, trip, 0)


def _wait_row_gathers(dst, rows, s, sem):
    landed = dst.at[pl.ds(0, rows * s)]
    pltpu.make_async_copy(landed, landed, sem).wait()


def _expert_tile_schedule(ce_ref, nx_ref, nv_ref, c, j, n_pass):
    first = (c == 0) | (ce_ref[c] != ce_ref[jnp.maximum(c - 1, 0)])
    more_in_pass = nx_ref[c] >= 0
    has_next = more_in_pass | (j + 1 < n_pass)
    next_e = jnp.where(more_in_pass, nx_ref[c], ce_ref[0])
    next_j = jnp.where(more_in_pass, j, j + 1)
    return (c < nv_ref[0]) & first, has_next, next_j, next_e


def _moe_up_kernel(f_lo, nf, gather, ce_ref, nx_ref, nv_ref, *refs):
    if gather:
        tok_ref, tok_next_ref, g_ref, bg_ref, bl_ref, xt_hbm, w_hbm, o_ref, xs_ref, gbuf, gsem, raw, w16, sem = refs
    else:
        x_ref, bg_ref, bl_ref, w_hbm, _, o_ref, raw, w16, sem = refs
    j = pl.program_id(0)
    c = pl.program_id(1)
    tf = raw.shape[2]
    n_pass = pl.num_programs(0)
    dff = nf * tf

    def fetch(jj, e, k):
        col = pl.multiple_of(k * dff + (f_lo + jj) * tf, LANES)
        return pltpu.make_async_copy(w_hbm.at[e, :, pl.ds(col, tf)], raw.at[k], sem.at[k])

    first, has_next, next_j, next_e = _expert_tile_schedule(ce_ref, nx_ref, nv_ref, c, j, n_pass)

    if gather:
        rows, d = xs_ref.shape
        s = d // LANES
        slot = c % 2

        @pl.when(c == 0)
        def _():
            _start_row_gathers(xt_hbm, tok_ref, 0, 1, gbuf.at[0], rows, s, gsem.at[0])

        @pl.when(c + 1 < nv_ref[0])
        def _():
            _start_row_gathers(xt_hbm, tok_next_ref, 0, 1, gbuf.at[1 - slot], rows, s, gsem.at[1 - slot])

    @pl.when(first & (j == 0) & (c == 0))
    def _():
        for k in range(2):
            fetch(0, ce_ref[0], k).start()

    @pl.when(first)
    def _():
        for k in range(2):
            fetch(j, ce_ref[c], k).wait()
            w16[k] = raw[k].astype(BF16)

    @pl.when(first & has_next)
    def _():
        for k in range(2):
            fetch(next_j, next_e, k).start()

    @pl.when(c < nv_ref[0])
    def _():
        if gather:
            _wait_row_gathers(gbuf.at[slot], rows, s, gsem.at[slot])
            x = _rms(_gathered_rows(gbuf.at[slot], rows, s), g_ref[...]).astype(BF16)
            xs_ref[...] = x
        else:
            x = x_ref[...]
        gate = jnp.minimum(_dot(x, w16[0]) + bg_ref[0], SWIGLU_LIMIT)
        lin = jnp.clip(_dot(x, w16[1]) + bl_ref[0], -SWIGLU_LIMIT, SWIGLU_LIMIT)
        o_ref[...] = (gate * jax.nn.sigmoid(SWIGLU_ALPHA * gate) * (lin + 1.0)).astype(o_ref.dtype)


def _moe_up(x1t, g_ffn, src_tok, w_up, b_up, chunk_expert, next_expert, n_valid, n_chunks):
    rows = MOE_ROWS
    e, d, f2 = w_up.shape
    s = d // LANES
    dff = f2 // 2
    tf = _pick(dff, (1024, 512, 256, 128))
    nf = dff // tf
    assert rows % GATHER_UNROLL == 0
    b3 = b_up.reshape(e, 1, f2)
    weight_scratch = [pltpu.VMEM((2, d, tf), F32), pltpu.VMEM((2, d, tf), BF16), pltpu.SemaphoreType.DMA((2,))]
    act_shape = jax.ShapeDtypeStruct((n_chunks * rows, dff), BF16)

    def last(c, nv):
        return jnp.minimum(c, nv[0] - 1)

    def nxt(c, nv):
        return jnp.minimum(c + 1, nv[0] - 1)

    def bias_spec(off):
        return pl.BlockSpec((1, 1, tf), lambda j, c, ce, nx, nv: (ce[c], 0, off + j))

    def act_spec(f_lo):
        return pl.BlockSpec((rows, tf), lambda j, c, ce, nx, nv: (last(c, nv), f_lo + j))

    first_spec = pltpu.PrefetchScalarGridSpec(
        num_scalar_prefetch=3,
        grid=(1, n_chunks),
        in_specs=[pl.BlockSpec((rows,), lambda j, c, ce, nx, nv: (last(c, nv),), memory_space=pltpu.SMEM),
                  pl.BlockSpec((rows,), lambda j, c, ce, nx, nv: (nxt(c, nv),), memory_space=pltpu.SMEM),
                  pl.BlockSpec((1, d), lambda j, c, ce, nx, nv: (0, 0)),
                  bias_spec(0), bias_spec(nf),
                  pl.BlockSpec(memory_space=pl.ANY),
                  pl.BlockSpec(memory_space=pl.ANY)],
        out_specs=[act_spec(0),
                   pl.BlockSpec((rows, d), lambda j, c, ce, nx, nv: (last(c, nv), 0))],
        scratch_shapes=[pltpu.VMEM((2, rows * _row_pitch(s), LANES), F32), pltpu.SemaphoreType.DMA((2,))]
        + weight_scratch,
    )
    act, xs = pl.pallas_call(
        functools.partial(_moe_up_kernel, 0, nf, True),
        grid_spec=first_spec,
        out_shape=[act_shape, jax.ShapeDtypeStruct((n_chunks * rows, d), BF16)],
        compiler_params=_params("arbitrary", "arbitrary"),
        name="moe_gather_up_swiglu",
    )(chunk_expert, next_expert, n_valid, src_tok, src_tok, g_ffn.reshape(1, d), b3, b3, x1t, w_up)
    if nf == 1:
        return act
    rest_spec = pltpu.PrefetchScalarGridSpec(
        num_scalar_prefetch=3,
        grid=(nf - 1, n_chunks),
        in_specs=[pl.BlockSpec((rows, d), lambda j, c, ce, nx, nv: (last(c, nv), 0)),
                  bias_spec(1), bias_spec(nf + 1),
                  pl.BlockSpec(memory_space=pl.ANY),
                  pl.BlockSpec(memory_space=pl.ANY)],
        out_specs=act_spec(1),
        scratch_shapes=weight_scratch,
    )
    return pl.pallas_call(
        functools.partial(_moe_up_kernel, 1, nf, False),
        grid_spec=rest_spec,
        out_shape=act_shape,
        input_output_aliases={7: 0},
        compiler_params=_params("arbitrary", "arbitrary"),
        name="moe_up_swiglu",
    )(chunk_expert, next_expert, n_valid, xs, b3, b3, w_up, act)


def _moe_down_kernel(ce_ref, nx_ref, nv_ref, a_ref, b_ref, w_hbm, o_ref, raw, w16, sem):
    c = pl.program_id(0)

    def fetch(e):
        return pltpu.make_async_copy(w_hbm.at[e], raw, sem)

    first, has_next, _, next_e = _expert_tile_schedule(ce_ref, nx_ref, nv_ref, c, 0, 1)

    @pl.when(first & (c == 0))
    def _():
        fetch(ce_ref[0]).start()

    @pl.when(first)
    def _():
        fetch(ce_ref[c]).wait()
        w16[...] = raw[...].astype(BF16)

    @pl.when(first & has_next)
    def _():
        fetch(next_e).start()

    @pl.when(c < nv_ref[0])
    def _():
        y = _dot(a_ref[...], w16[...]) + b_ref[0]
        _store_row_tiles(o_ref, y, y.shape[0])


def _moe_down(act, w_down, b_down, chunk_expert, next_expert, n_valid):
    rows = MOE_ROWS
    n_chunks = act.shape[0] // rows
    e, dff, d = w_down.shape
    s = d // LANES

    def last(c, nv):
        return jnp.minimum(c, nv[0] - 1)

    grid_spec = pltpu.PrefetchScalarGridSpec(
        num_scalar_prefetch=3,
        grid=(n_chunks,),
        in_specs=[pl.BlockSpec((rows, dff), lambda c, ce, nx, nv: (last(c, nv), 0)),
                  pl.BlockSpec((1, 1, d), lambda c, ce, nx, nv: (ce[c], 0, 0)),
                  pl.BlockSpec(memory_space=pl.ANY)],
        out_specs=pl.BlockSpec((rows * s, LANES), lambda c, ce, nx, nv: (last(c, nv), 0)),
        scratch_shapes=[pltpu.VMEM((dff, d), F32), pltpu.VMEM((dff, d), BF16),
                        pltpu.SemaphoreType.DMA(())],
    )
    return pl.pallas_call(
        _moe_down_kernel,
        grid_spec=grid_spec,
        out_shape=jax.ShapeDtypeStruct((n_chunks * rows * s, LANES), F32),
        compiler_params=_params("arbitrary"),
        name="moe_down",
    )(chunk_expert, next_expert, n_valid, act, b_down.reshape(e, 1, d), w_down)


def _combine_kernel(dest_ref, dest_next_ref, x1_ref, gate_ref, gple_ref, yt_hbm, x2_ref, h3_ref, buf, sem):
    i = pl.program_id(0)
    rows = x1_ref.shape[0]
    s = x1_ref.shape[1] // LANES
    slot = i % 2

    def start(idx_ref, to):
        for k in range(TOP_K):
            _start_row_gathers(yt_hbm, idx_ref, k, TOP_K, buf.at[to, k], rows, s, sem.at[to])

    @pl.when(i == 0)
    def _():
        start(dest_ref, 0)

    @pl.when(i + 1 < pl.num_programs(0))
    def _():
        start(dest_next_ref, 1 - slot)

    for k in range(TOP_K):
        _wait_row_gathers(buf.at[slot, k], rows, s, sem.at[slot])
    gate = gate_ref[...]
    acc = x1_ref[...]
    for k in range(TOP_K):
        acc = acc + gate[:, k:k + 1] * _gathered_rows(buf.at[slot, k], rows, s)
    x2_ref[...] = acc
    h3_ref[...] = _rms(acc, gple_ref[...]).astype(h3_ref.dtype)


def _combine(x1, gate, dest, yt, g_ple):
    n, d = x1.shape
    s = d // LANES
    rows = _pick(n, (COMBINE_ROWS, 64, 32, 16))
    steps = n // rows
    assert rows % GATHER_UNROLL == 0
    return pl.pallas_call(
        _combine_kernel,
        grid=(steps,),
        in_specs=[pl.BlockSpec((rows * TOP_K,), lambda i: (i,), memory_space=pltpu.SMEM),
                  pl.BlockSpec((rows * TOP_K,), lambda i: (jnp.minimum(i + 1, steps - 1),),
                               memory_space=pltpu.SMEM),
                  pl.BlockSpec((rows, d), lambda i: (i, 0)),
                  pl.BlockSpec((rows, LANES), lambda i: (i, 0)),
                  pl.BlockSpec((1, d), lambda i: (0, 0)),
                  pl.BlockSpec(memory_space=pl.ANY)],
        out_specs=[pl.BlockSpec((rows, d), lambda i: (i, 0)),
                   pl.BlockSpec((rows, d), lambda i: (i, 0))],
        out_shape=[jax.ShapeDtypeStruct((n, d), F32), jax.ShapeDtypeStruct((n, d), BF16)],
        scratch_shapes=[pltpu.VMEM((2, TOP_K, rows * _row_pitch(s), LANES), F32),
                        pltpu.SemaphoreType.DMA((2,))],
        compiler_params=_params("arbitrary"),
        name="moe_combine_gather",
    )(dest, dest, x1, gate, g_ple.reshape(1, d), yt)


def _ple_kernel(x2_ref, h3_ref, p_ref, wpg_ref, wple_ref, o_ref):
    gate = jax.nn.sigmoid(_dot(h3_ref[...], wpg_ref[...]))
    o_ref[...] = x2_ref[...] + gate * _dot(p_ref[...].astype(BF16), wple_ref[...])


def _ple(x2, h3, p, w_pg, w_ple, row0, r):
    d = x2.shape[1]
    pd = p.shape[1]
    tm = _pick(r, (512, 256, 128, 64, 32, 16))
    tn = _pick(d, (1024, 512, 256, 128))
    assert row0 % tm == 0
    b0 = row0 // tm
    return pl.pallas_call(
        _ple_kernel,
        grid=(d // tn, r // tm),
        in_specs=[pl.BlockSpec((tm, tn), lambda j, i: (b0 + i, j)),
                  pl.BlockSpec((tm, d), lambda j, i: (b0 + i, 0)),
                  pl.BlockSpec((tm, pd), lambda j, i: (i, 0)),
                  pl.BlockSpec((d, tn), lambda j, i: (0, j)),
                  pl.BlockSpec((pd, tn), lambda j, i: (0, j))],
        out_specs=pl.BlockSpec((tm, tn), lambda j, i: (i, j)),
        out_shape=jax.ShapeDtypeStruct((r, d), F32),
        compiler_params=_params("parallel", "parallel"),
        name="ple_gate",
    )(x2, h3, p, w_pg, w_ple)


def _rope_tables(pos):
    half = ROT_DIM // 2
    inv = jnp.power(ROPE_THETA, -jnp.arange(half, dtype=F32) * 2.0 / ROT_DIM)
    ang = pos.astype(F32)[:, None] * inv[None, :]
    cos, sin = jnp.cos(ang), jnp.sin(ang)
    r = pos.shape[0]
    rest = A_HEAD_DIM - ROT_DIM
    c = jnp.concatenate([cos, cos, jnp.ones((r, rest), F32)], axis=1)
    s1 = jnp.concatenate([-sin, jnp.zeros((r, half + rest), F32)], axis=1)
    s2 = jnp.concatenate([jnp.zeros((r, half), F32), sin, jnp.zeros((r, rest), F32)], axis=1)
    return c, s1, s2


def _mixer_inputs(x, pos, g_attn, w, b_i, b_f, g_q, g_k, mlstm_dtype):
    d = x.shape[1]
    kq = M_HEADS * M_QK
    o_ig = 2 * kq + M_WIDTH
    o_om = o_ig + 2 * M_HEADS
    o_qa = o_om + M_WIDTH
    o_gm = o_qa + 3 * A_WIDTH
    h = _rmsnorm_bf16(x, g_attn)
    colscale = jnp.concatenate([jnp.full((1, kq), M_QK ** -0.5, F32), jnp.ones((1, kq + M_WIDTH), F32)], axis=1)
    qkv_m = _proj_scale(h, w[:, :o_ig].astype(BF16), colscale, mlstm_dtype)
    wg = w[:, o_ig:o_om]
    wc = jnp.pad(wg, ((0, 0), (0, LANES - 2 * M_HEADS))).astype(BF16)
    bias = jnp.concatenate([b_i, b_f]).astype(F32)
    bc = jnp.pad(bias, (0, LANES - 2 * M_HEADS)).reshape(1, LANES)
    gcol, grow = _proj_gates(h, wc, wg.T.astype(BF16), bc, bias.reshape(2 * M_HEADS, 1))
    sg = _proj_sigmoid(h, jnp.concatenate([w[:, o_om:o_qa], w[:, o_gm:]], axis=1).astype(BF16))
    tabs = _rope_tables(pos)
    q32, _ = _proj_qk(h, w[:, o_qa:o_qa + A_WIDTH].astype(BF16), g_q, *tabs)
    k32, k16 = _proj_qk(h, w[:, o_qa + A_WIDTH:o_qa + 2 * A_WIDTH].astype(BF16), g_k, *tabs)
    v32, v16 = _proj_plain2(h, w[:, o_qa + 2 * A_WIDTH:o_gm].astype(BF16))
    return qkv_m, gcol, grow, sg, q32, k32, k16, v32, v16


def _moe_plan(eid, n_chunks):
    n = eid.shape[0]
    e_flat = eid[:, :TOP_K].reshape(-1)
    onehot = (e_flat[:, None] == jnp.arange(N_EXPERTS, dtype=jnp.int32)[None, :]).astype(jnp.int32)
    before = jnp.cumsum(onehot, axis=0) - onehot
    rank = jnp.sum(before * onehot, axis=1)
    counts = jnp.sum(onehot, axis=0)
    padded = (counts + MOE_ROWS - 1) // MOE_ROWS * MOE_ROWS
    p_ends = jnp.cumsum(padded)
    p_starts = p_ends - padded
    dest = (p_starts[e_flat] + rank).astype(jnp.int32)
    tok = jnp.repeat(jnp.arange(n, dtype=jnp.int32), TOP_K)
    src_tok = jnp.zeros((n_chunks * MOE_ROWS,), jnp.int32).at[dest].set(tok)
    n_valid = (p_ends[-1] // MOE_ROWS).astype(jnp.int32).reshape(1)
    chunk0 = jnp.minimum(jnp.arange(n_chunks, dtype=jnp.int32), n_valid[0] - 1) * MOE_ROWS
    chunk_expert = jnp.minimum(jnp.sum((p_ends[None, :] <= chunk0[:, None]).astype(jnp.int32), axis=1),
                               N_EXPERTS - 1)
    ids = jnp.arange(N_EXPERTS, dtype=jnp.int32)
    later = (ids[None, :] > chunk_expert[:, None]) & (counts[None, :] > 0)
    next_expert = jnp.min(jnp.where(later, ids[None, :], N_EXPERTS), axis=1)
    next_expert = jnp.where(next_expert < N_EXPERTS, next_expert, -1).astype(jnp.int32)
    return dest, src_tok, n_valid, chunk_expert, next_expert


def kernel(x_prompt, x_sample, cache_k, cache_v, state_mlstm_C, state_mlstm_n, state_mlstm_m, page_table, p_prompt, p_sample, g_attn, w_in, b_i, b_f, g_q, g_k, g_mh, w_bm, w_ba, w_out, g_ffn, w_router, b_router, w_up, b_up, w_down, b_down, g_ple, w_pg, w_ple):
    depth = w_in.shape[0]
    bsz, t, d = x_prompt.shape
    ns, dec_seq, _ = x_sample.shape
    assert depth == 1 and bsz == 1 and dec_seq == 1
    page = cache_k.shape[2]
    past = page_table.shape[1] * page
    assert t % MOBA_BLOCK == 0 and past % MOBA_BLOCK == 0 and MOBA_BLOCK % page == 0
    assert t % (M_CHUNK * MLSTM_CHUNKS_PER_STEP) == 0 and ns % MLSTM_SAMPLE_SEQS == 0
    n = t + ns

    xp = x_prompt[0]
    xs = x_sample[:, 0]
    w = w_in[0]
    fw = (g_attn[0], w, b_i[0], b_f[0], g_q[0], g_k[0])

    qkv_p, gcol_p, grow_p, sg_p, q32_p, k32_p, k16_p, v32_p, v16_p = _mixer_inputs(
        xp, jnp.arange(t, dtype=jnp.int32), *fw, BF16)
    hm_p, ct_p, n_p, m_p = _mlstm_prompt(qkv_p, gcol_p, grow_p)
    ao_p = _moba_prompt(q32_p, k16_p, v16_p, _block_kmean(k32_p))

    qkv_s, gcol_s, _, sg_s, q32_s, k32_s, _, v32_s, _ = _mixer_inputs(
        xs, jnp.full((ns,), past, jnp.int32), *fw, F32)
    hm_s, c_s, n_s, m_s = _mlstm_sample(qkv_s, gcol_s, state_mlstm_C[0].astype(F32),
                                        state_mlstm_n[0].astype(F32), state_mlstm_m[0].astype(F32))
    ao_s = _moba_sample(q32_s, k32_s, v32_s, cache_k, cache_v, page_table)

    wbm, wba, wout = w_bm[0].astype(BF16), w_ba[0].astype(BF16), w_out[0].astype(BF16)
    mix_p = _mix(hm_p, sg_p, ao_p, g_mh[0], wbm, wba)
    mix_s = _mix(hm_s, sg_s, ao_s, g_mh[0], wbm, wba)
    wr = jnp.pad(w_router[0], ((0, 0), (0, LANES - N_EXPERTS)))
    wr_hi = wr.astype(BF16)
    wr_lo = (wr - wr_hi.astype(F32)).astype(BF16)
    b_r = jnp.pad(b_router[0].astype(F32), (0, LANES - N_EXPERTS)).reshape(1, LANES)
    routed = _resid_router(xp, mix_p, wout, g_ffn[0], wr_hi, wr_lo, b_r, n, 0, None)
    x1, x1t, gate, eid = _resid_router(xs, mix_s, wout, g_ffn[0], wr_hi, wr_lo, b_r, n, t, routed)

    n_slots = n * TOP_K
    n_chunks = (n_slots + N_EXPERTS * (MOE_ROWS - 1)) // MOE_ROWS
    dest, src_tok, n_valid, chunk_expert, next_expert = _moe_plan(eid, n_chunks)
    act = _moe_up(x1t, g_ffn[0], src_tok, w_up[0], b_up[0], chunk_expert, next_expert, n_valid, n_chunks)
    yt = _moe_down(act, w_down[0], b_down[0], chunk_expert, next_expert, n_valid)
    x2, h3 = _combine(x1, gate, dest, yt, g_ple[0])

    wpg, wple = w_pg[0].astype(BF16), w_ple[0].astype(BF16)
    y_p = _ple(x2, h3, p_prompt[0, 0], wpg, wple, 0, t)
    y_s = _ple(x2, h3, p_sample[0, :, 0], wpg, wple, t, ns)

    def heads(a):
        return a.reshape(a.shape[0], A_HEADS, A_HEAD_DIM)

    return (y_p[None], y_s[:, None],
            heads(k32_p)[None, None], heads(v32_p)[None, None],
            jnp.swapaxes(ct_p, 1, 2)[None, None].astype(state_mlstm_C.dtype),
            n_p[:M_HEADS][None, None].astype(state_mlstm_n.dtype),
            m_p[:M_HEADS, 0][None, None].astype(state_mlstm_m.dtype),
            heads(k32_s)[None, :, None], heads(v32_s)[None, :, None],
            c_s[None].astype(state_mlstm_C.dtype), n_s[None].astype(state_mlstm_n.dtype),
            m_s[None].astype(state_mlstm_m.dtype))
```

```python
import functools

import jax
import jax.numpy as jnp
import numpy as np
from jax import lax
from jax.experimental import pallas as pl
from jax.experimental.pallas import tpu as pltpu

F32 = jnp.float32
BF16 = jnp.bfloat16

M_HEADS = 4
M_QK = 128
M_V = 256
M_CHUNK = 64
M_WIDTH = M_HEADS * M_V
A_HEADS = 8
A_HEAD_DIM = 128
A_WIDTH = A_HEADS * A_HEAD_DIM
MOBA_BLOCK = 256
MOBA_TOPK = 3
ROT_DIM = A_HEAD_DIM // 4
ROPE_THETA = 500000.0
N_EXPERTS = 32
TOP_K = 4
SWIGLU_LIMIT = 7.0
SWIGLU_ALPHA = 1.702
EPS = 1e-6
NEG = -1e30

LANES = 128
SUBLANES = 8
VMEM_LIMIT_BYTES = 56 * 1024 * 1024

MOE_ROWS = 256
MLSTM_CHUNKS_PER_STEP = 2
MLSTM_SAMPLE_SEQS = 8
COMBINE_ROWS = 128
MOBA_HEADS_PER_STEP = 4
MOBA_BLOCKS_PER_STEP = 4
GATHER_UNROLL = 8
GATHER_ORDER_STEP = 101


def _pick(n, candidates):
    for c in candidates:
        if n % c == 0:
            return c
    raise ValueError(f"no tile in {candidates} divides {n}")


def _params(*sem):
    return pltpu.CompilerParams(dimension_semantics=sem, vmem_limit_bytes=VMEM_LIMIT_BYTES)


def _dot(a, b):
    return jnp.dot(a, b, preferred_element_type=F32)


def _dot_nt(a, b):
    return lax.dot_general(a, b, (((1,), (1,)), ((), ())), preferred_element_type=F32)


def _dot_tn(a, b):
    return lax.dot_general(a, b, (((0,), (0,)), ((), ())), preferred_element_type=F32)


def _split_bf16(x):
    hi = x.astype(BF16)
    lo = (x - hi.astype(F32)).astype(BF16)
    return hi, lo


def _rms(x, g):
    return x * lax.rsqrt(jnp.mean(x * x, axis=-1, keepdims=True) + EPS) * g


def _log_sigmoid(x):
    return -(jnp.maximum(-x, 0.0) + jnp.log1p(jnp.exp(-jnp.abs(x))))


def _first_argmax(work, lane_f):
    m = jnp.max(work, axis=-1, keepdims=True)
    idx = jnp.min(jnp.where(work == m, lane_f, float(4 * LANES)), axis=-1, keepdims=True)
    return m, idx


def _rmsnorm_kernel(x_ref, g_ref, o_ref):
    o_ref[...] = _rms(x_ref[...], g_ref[...]).astype(o_ref.dtype)


def _rmsnorm_bf16(x, g):
    r, d = x.shape
    tm = _pick(r, (512, 256, 128, 64, 32, 16))
    return pl.pallas_call(
        _rmsnorm_kernel,
        grid=(r // tm,),
        in_specs=[pl.BlockSpec((tm, d), lambda i: (i, 0)), pl.BlockSpec((1, d), lambda i: (0, 0))],
        out_specs=pl.BlockSpec((tm, d), lambda i: (i, 0)),
        out_shape=jax.ShapeDtypeStruct((r, d), BF16),
        compiler_params=_params("parallel"),
        name="rmsnorm_rows",
    )(x, g.reshape(1, d))


def _proj_scale_kernel(h_ref, w_ref, s_ref, o_ref):
    o_ref[...] = (_dot(h_ref[...], w_ref[...]) * s_ref[...]).astype(o_ref.dtype)


def _proj_sigmoid_kernel(h_ref, w_ref, o_ref):
    o_ref[...] = jax.nn.sigmoid(_dot(h_ref[...], w_ref[...])).astype(o_ref.dtype)


def _proj_plain2_kernel(h_ref, w_ref, o32_ref, o16_ref):
    acc = _dot(h_ref[...], w_ref[...])
    o32_ref[...] = acc
    o16_ref[...] = acc.astype(BF16)


def _proj_qk_kernel(h_ref, w_ref, g_ref, c_ref, s1_ref, s2_ref, o32_ref, o16_ref):
    h = h_ref[...]
    g = g_ref[...]
    c, s1, s2 = c_ref[...], s1_ref[...], s2_ref[...]
    pair = 2 * A_HEAD_DIM
    for p in range(w_ref.shape[1] // pair):
        acc = _dot(h, w_ref[:, p * pair:(p + 1) * pair])
        for hh in range(2):
            y = _rms(acc[:, hh * A_HEAD_DIM:(hh + 1) * A_HEAD_DIM], g)
            up = pltpu.roll(y, A_HEAD_DIM - ROT_DIM // 2, 1)
            dn = pltpu.roll(y, ROT_DIM // 2, 1)
            r = y * c + up * s1 + dn * s2
            sl = slice(p * pair + hh * A_HEAD_DIM, p * pair + (hh + 1) * A_HEAD_DIM)
            o32_ref[:, sl] = r
            o16_ref[:, sl] = r.astype(BF16)


def _proj_gates_kernel(h_ref, wc_ref, wr_ref, bc_ref, br_ref, gc_ref, gr_ref):
    h = h_ref[...]
    zc = _dot(h, wc_ref[...]) + bc_ref[...]
    zr = _dot_nt(wr_ref[...], h) + br_ref[...]
    lane = lax.broadcasted_iota(jnp.int32, zc.shape, 1)
    gc_ref[...] = jnp.where((lane >= M_HEADS) & (lane < 2 * M_HEADS), _log_sigmoid(zc), zc)
    row = lax.broadcasted_iota(jnp.int32, zr.shape, 0)
    gr_ref[...] = jnp.where(row >= M_HEADS, _log_sigmoid(zr), zr)


def _proj_tiles(r, n):
    tm = _pick(r, (1024, 512, 256, 128, 64, 32, 16))
    tn = _pick(n, (1024, 512, 256, 128))
    return tm, tn


def _proj_scale(h, w, colscale, out_dtype):
    r, d = h.shape
    n = w.shape[1]
    tm, tn = _proj_tiles(r, n)
    return pl.pallas_call(
        _proj_scale_kernel,
        grid=(n // tn, r // tm),
        in_specs=[pl.BlockSpec((tm, d), lambda j, i: (i, 0)),
                  pl.BlockSpec((d, tn), lambda j, i: (0, j)),
                  pl.BlockSpec((1, tn), lambda j, i: (0, j))],
        out_specs=pl.BlockSpec((tm, tn), lambda j, i: (i, j)),
        out_shape=jax.ShapeDtypeStruct((r, n), out_dtype),
        compiler_params=_params("parallel", "parallel"),
        name="proj_scale",
    )(h, w, colscale)


def _proj_sigmoid(h, w):
    r, d = h.shape
    n = w.shape[1]
    tm, tn = _proj_tiles(r, n)
    return pl.pallas_call(
        _proj_sigmoid_kernel,
        grid=(n // tn, r // tm),
        in_specs=[pl.BlockSpec((tm, d), lambda j, i: (i, 0)),
                  pl.BlockSpec((d, tn), lambda j, i: (0, j))],
        out_specs=pl.BlockSpec((tm, tn), lambda j, i: (i, j)),
        out_shape=jax.ShapeDtypeStruct((r, n), BF16),
        compiler_params=_params("parallel", "parallel"),
        name="proj_sigmoid",
    )(h, w)


def _proj_plain2(h, w):
    r, d = h.shape
    n = w.shape[1]
    tm, tn = _proj_tiles(r, n)
    return pl.pallas_call(
        _proj_plain2_kernel,
        grid=(n // tn, r // tm),
        in_specs=[pl.BlockSpec((tm, d), lambda j, i: (i, 0)),
                  pl.BlockSpec((d, tn), lambda j, i: (0, j))],
        out_specs=[pl.BlockSpec((tm, tn), lambda j, i: (i, j)),
                   pl.BlockSpec((tm, tn), lambda j, i: (i, j))],
        out_shape=[jax.ShapeDtypeStruct((r, n), F32), jax.ShapeDtypeStruct((r, n), BF16)],
        compiler_params=_params("parallel", "parallel"),
        name="proj_plain",
    )(h, w)


def _proj_qk(h, w, g, rope_c, rope_s1, rope_s2):
    r, d = h.shape
    n = w.shape[1]
    tm, tn = _proj_tiles(r, n)
    hd = A_HEAD_DIM
    return pl.pallas_call(
        _proj_qk_kernel,
        grid=(n // tn, r // tm),
        in_specs=[pl.BlockSpec((tm, d), lambda j, i: (i, 0)),
                  pl.BlockSpec((d, tn), lambda j, i: (0, j)),
                  pl.BlockSpec((1, hd), lambda j, i: (0, 0)),
                  pl.BlockSpec((tm, hd), lambda j, i: (i, 0)),
                  pl.BlockSpec((tm, hd), lambda j, i: (i, 0)),
                  pl.BlockSpec((tm, hd), lambda j, i: (i, 0))],
        out_specs=[pl.BlockSpec((tm, tn), lambda j, i: (i, j)),
                   pl.BlockSpec((tm, tn), lambda j, i: (i, j))],
        out_shape=[jax.ShapeDtypeStruct((r, n), F32), jax.ShapeDtypeStruct((r, n), BF16)],
        compiler_params=_params("parallel", "parallel"),
        name="proj_qk_norm_rope",
    )(h, w, g.reshape(1, hd), rope_c, rope_s1, rope_s2)


def _proj_gates(h, wc, wr, bc, br):
    r, d = h.shape
    tm = _pick(r, (1024, 512, 256, 128)) if r % LANES == 0 else r
    g2 = 2 * M_HEADS
    return pl.pallas_call(
        _proj_gates_kernel,
        grid=(r // tm,),
        in_specs=[pl.BlockSpec((tm, d), lambda i: (i, 0)),
                  pl.BlockSpec((d, LANES), lambda i: (0, 0)),
                  pl.BlockSpec((g2, d), lambda i: (0, 0)),
                  pl.BlockSpec((1, LANES), lambda i: (0, 0)),
                  pl.BlockSpec((g2, 1), lambda i: (0, 0))],
        out_specs=[pl.BlockSpec((tm, LANES), lambda i: (i, 0)),
                   pl.BlockSpec((g2, tm), lambda i: (0, i))],
        out_shape=[jax.ShapeDtypeStruct((r, LANES), F32), jax.ShapeDtypeStruct((g2, r), F32)],
        compiler_params=_params("parallel"),
        name="proj_gates",
    )(h, wc, wr, bc, br)


def _mlstm_prompt_kernel(q_ref, k_ref, v_ref, gc_ref, gr_ref, h_ref, ct_out, n_out, m_out,
                         ct_s, n_s, m_s):
    step = pl.program_id(0)

    @pl.when(step == 0)
    def _():
        ct_s[...] = jnp.zeros_like(ct_s)
        n_s[...] = jnp.zeros_like(n_s)
        m_s[...] = jnp.zeros_like(m_s)

    ln = M_CHUNK
    row = lax.broadcasted_iota(jnp.int32, (ln, ln), 0)
    col = lax.broadcasted_iota(jnp.int32, (ln, ln), 1)
    causal = col <= row
    upto = row <= col
    pre = {}
    for cc in range(MLSTM_CHUNKS_PER_STEP):
        rs = slice(cc * ln, (cc + 1) * ln)
        for hh in range(M_HEADS):
            ig_c = gc_ref[rs, hh:hh + 1]
            lf_c = gc_ref[rs, M_HEADS + hh:M_HEADS + hh + 1]
            ig_r = gr_ref[hh:hh + 1, rs]
            lf_r = gr_ref[M_HEADS + hh:M_HEADS + hh + 1, rs]
            bcum_c = jnp.sum(jnp.where(causal, lf_r, 0.0), axis=1, keepdims=True)
            bcum_r = jnp.sum(jnp.where(upto, lf_c, 0.0), axis=0, keepdims=True)
            dmat = jnp.where(causal, bcum_c - bcum_r + ig_r, NEG)
            qh = q_ref[rs, hh * M_QK:(hh + 1) * M_QK]
            kh = k_ref[rs, hh * M_QK:(hh + 1) * M_QK]
            b_last = bcum_c[ln - 1:ln, :]
            pre[cc, hh] = dict(
                bcum_c=bcum_c, dmat=dmat, dmax=jnp.max(dmat, axis=1, keepdims=True),
                qk=_dot_nt(qh, kh), b_last=b_last,
                d_last_c=b_last - bcum_c + ig_c,
                d_last_max=jnp.max(b_last - bcum_r + ig_r, axis=1, keepdims=True))

    for hh in range(M_HEADS):
        ct = ct_s[hh]
        n_row = n_s[hh:hh + 1, :]
        m_prev = m_s[hh:hh + 1, 0:1]
        for cc in range(MLSTM_CHUNKS_PER_STEP):
            rs = slice(cc * ln, (cc + 1) * ln)
            u = pre[cc, hh]
            qh = q_ref[rs, hh * M_QK:(hh + 1) * M_QK]
            kh = k_ref[rs, hh * M_QK:(hh + 1) * M_QK]
            vh = v_ref[rs, hh * M_V:(hh + 1) * M_V]
            a = u["bcum_c"] + m_prev
            m_t = jnp.maximum(a, u["dmax"])
            w_inter = jnp.exp(a - m_t)
            s = u["qk"] * jnp.exp(u["dmat"] - m_t)
            num = w_inter * _dot(qh, ct.astype(BF16)) + _dot(s.astype(BF16), vh)
            den = (w_inter * jnp.sum(qh.astype(F32) * n_row, axis=1, keepdims=True)
                   + jnp.sum(s, axis=1, keepdims=True))
            h_ref[rs, hh * M_V:(hh + 1) * M_V] = num / jnp.maximum(jnp.abs(den), jnp.exp(-m_t))
            a_last = u["b_last"] + m_prev
            m_new = jnp.maximum(a_last, u["d_last_max"])
            w_c = jnp.exp(a_last - m_new)
            w_j = jnp.exp(u["d_last_c"] - m_new)
            vw = (vh.astype(F32) * w_j).astype(BF16)
            ct = w_c * ct + _dot_tn(kh, vw)
            n_row = w_c * n_row + jnp.sum(kh.astype(F32) * w_j, axis=0, keepdims=True)
            m_prev = m_new
        ct_s[hh] = ct
        n_s[hh:hh + 1, :] = n_row
        m_s[hh:hh + 1, :] = jnp.broadcast_to(m_prev, (1, LANES))

    @pl.when(step == pl.num_programs(0) - 1)
    def _():
        ct_out[...] = ct_s[...]
        n_out[...] = n_s[...]
        m_out[...] = m_s[...]


def _mlstm_prompt(qkv, gcol, grow):
    t = qkv.shape[0]
    rows = M_CHUNK * MLSTM_CHUNKS_PER_STEP
    kq = M_HEADS * M_QK
    return pl.pallas_call(
        _mlstm_prompt_kernel,
        grid=(t // rows,),
        in_specs=[pl.BlockSpec((rows, kq), lambda c: (c, 0)),
                  pl.BlockSpec((rows, kq), lambda c: (c, 1)),
                  pl.BlockSpec((rows, M_WIDTH), lambda c: (c, (2 * kq) // M_WIDTH)),
                  pl.BlockSpec((rows, LANES), lambda c: (c, 0)),
                  pl.BlockSpec((2 * M_HEADS, rows), lambda c: (0, c))],
        out_specs=[pl.BlockSpec((rows, M_WIDTH), lambda c: (c, 0)),
                   pl.BlockSpec((M_HEADS, M_QK, M_V), lambda c: (0, 0, 0)),
                   pl.BlockSpec((SUBLANES, LANES), lambda c: (0, 0)),
                   pl.BlockSpec((SUBLANES, LANES), lambda c: (0, 0))],
        out_shape=[jax.ShapeDtypeStruct((t, M_WIDTH), F32),
                   jax.ShapeDtypeStruct((M_HEADS, M_QK, M_V), F32),
                   jax.ShapeDtypeStruct((SUBLANES, LANES), F32),
                   jax.ShapeDtypeStruct((SUBLANES, LANES), F32)],
        scratch_shapes=[pltpu.VMEM((M_HEADS, M_QK, M_V), F32),
                        pltpu.VMEM((SUBLANES, LANES), F32),
                        pltpu.VMEM((SUBLANES, LANES), F32)],
        compiler_params=_params("arbitrary"),
        name="mlstm_prompt_scan",
    )(qkv, qkv, qkv, gcol, grow)


def _mlstm_sample_kernel(qkv_ref, gc_ref, c_ref, n_ref, m_ref, h_ref, co_ref, no_ref, mo_ref):
    kq = M_HEADS * M_QK
    row8 = lax.broadcasted_iota(jnp.int32, (SUBLANES, 1), 0)
    mrow = lax.broadcasted_iota(jnp.int32, mo_ref.shape, 0)
    mcol = lax.broadcasted_iota(jnp.int32, mo_ref.shape, 1)
    m_all = m_ref[...]
    for s in range(MLSTM_SAMPLE_SEQS):
        for hh in range(M_HEADS):
            q = qkv_ref[s:s + 1, hh * M_QK:(hh + 1) * M_QK]
            k = qkv_ref[s:s + 1, kq + hh * M_QK:kq + (hh + 1) * M_QK]
            v = qkv_ref[s:s + 1, 2 * kq + hh * M_V:2 * kq + (hh + 1) * M_V]
            ig = gc_ref[s:s + 1, hh:hh + 1]
            lf = gc_ref[s:s + 1, M_HEADS + hh:M_HEADS + hh + 1]
            m_prev = m_ref[s:s + 1, hh:hh + 1]
            c = c_ref[s, hh]
            n_row = n_ref[s, hh:hh + 1, :]
            a = lf + m_prev
            m_t = jnp.maximum(a, ig)
            w_c = jnp.exp(a - m_t)
            w_j = jnp.exp(ig - m_t)
            sc = jnp.sum(q * k, axis=1, keepdims=True) * w_j
            q8 = jnp.broadcast_to(q, (SUBLANES, M_QK)).astype(BF16)
            cq = _dot_nt(q8, c.astype(BF16))[0:1, :]
            num = w_c * cq + sc * v
            den = w_c * jnp.sum(n_row * q, axis=1, keepdims=True) + sc
            h_ref[s:s + 1, hh * M_V:(hh + 1) * M_V] = num / jnp.maximum(jnp.abs(den), jnp.exp(-m_t))
            vw8 = jnp.where(row8 == 0, v * w_j, 0.0).astype(BF16)
            k8 = jnp.where(row8 == 0, k, 0.0).astype(BF16)
            co_ref[s, hh] = w_c * c + _dot_tn(vw8, k8)
            no_ref[s, hh:hh + 1, :] = w_c * n_row + w_j * k
            m_all = jnp.where((mrow == s) & (mcol == hh), m_t, m_all)
    mo_ref[...] = m_all


def _mlstm_sample(qkv, gcol, c0, n0, m0):
    ns = qkv.shape[0]
    sb = MLSTM_SAMPLE_SEQS
    wq = qkv.shape[1]
    return pl.pallas_call(
        _mlstm_sample_kernel,
        grid=(ns // sb,),
        in_specs=[pl.BlockSpec((sb, wq), lambda i: (i, 0)),
                  pl.BlockSpec((sb, LANES), lambda i: (i, 0)),
                  pl.BlockSpec((sb, M_HEADS, M_V, M_QK), lambda i: (i, 0, 0, 0)),
                  pl.BlockSpec((sb, M_HEADS, M_QK), lambda i: (i, 0, 0)),
                  pl.BlockSpec((sb, M_HEADS), lambda i: (i, 0))],
        out_specs=[pl.BlockSpec((sb, M_WIDTH), lambda i: (i, 0)),
                   pl.BlockSpec((sb, M_HEADS, M_V, M_QK), lambda i: (i, 0, 0, 0)),
                   pl.BlockSpec((sb, M_HEADS, M_QK), lambda i: (i, 0, 0)),
                   pl.BlockSpec((sb, M_HEADS), lambda i: (i, 0))],
        out_shape=[jax.ShapeDtypeStruct((ns, M_WIDTH), F32),
                   jax.ShapeDtypeStruct(c0.shape, F32),
                   jax.ShapeDtypeStruct(n0.shape, F32),
                   jax.ShapeDtypeStruct(m0.shape, F32)],
        compiler_params=_params("parallel"),
        name="mlstm_sample_step",
    )(qkv, gcol, c0, n0, m0)


def _kmean_kernel(k_ref, o_ref):
    o_ref[0] = jnp.mean(k_ref[...], axis=0, keepdims=True)


def _block_kmean(k32):
    t, w = k32.shape
    nb = t // MOBA_BLOCK
    out = pl.pallas_call(
        _kmean_kernel,
        grid=(nb,),
        in_specs=[pl.BlockSpec((MOBA_BLOCK, w), lambda b: (b, 0))],
        out_specs=pl.BlockSpec((1, 1, w), lambda b: (b, 0, 0)),
        out_shape=jax.ShapeDtypeStruct((nb, 1, w), F32),
        compiler_params=_params("parallel"),
        name="moba_block_kmean",
    )(k32)
    return out.reshape(nb, w)


def _moba_select(sc, own, lane_i, lane_f):
    work = jnp.where(lane_i < own, sc, NEG)
    sel = jnp.zeros(sc.shape, dtype=jnp.bool_)
    for _ in range(MOBA_TOPK):
        _, idx = _first_argmax(work, lane_f)
        hit = lane_f == idx
        sel = sel | hit
        work = jnp.where(hit, -jnp.inf, work)
    return sel & (lane_i < own)


def _moba_prompt_kernel(q_ref, k_ref, v_ref, e_ref, km_ref, o_ref):
    i = pl.program_id(1)
    bs = MOBA_BLOCK
    pw = MOBA_BLOCKS_PER_STEP * bs
    hd = A_HEAD_DIM
    heads = range(MOBA_HEADS_PER_STEP)
    ones = jnp.ones((pw, hd), BF16)
    lane_i = lax.broadcasted_iota(jnp.int32, (bs, LANES), 1)
    lane_f = lane_i.astype(F32)

    def head_slice(hh):
        return slice(hh * hd, (hh + 1) * hd)

    q_aug = []
    for hh in heads:
        q = q_ref[:, head_slice(hh)]
        qh, ql = _split_bf16(q)
        kmh, kml = _split_bf16(km_ref[:, head_slice(hh)])
        sc = _dot_nt(qh, kmh) + (_dot_nt(qh, kml) + _dot_nt(ql, kmh))
        sel = _moba_select(sc, i, lane_i, lane_f)
        bias = jnp.where(sel | (lane_i >= i), 0.0, NEG)
        q_aug.append(jnp.concatenate([(q * (hd ** -0.5)).astype(BF16), bias.astype(BF16)], axis=1))

    def pair_scores(p, hh, onehot):
        off = pl.multiple_of(p * pw, pw)
        k_aug = jnp.concatenate([k_ref[pl.ds(off, pw), head_slice(hh)], onehot], axis=1)
        v_aug = jnp.concatenate([v_ref[pl.ds(off, pw), head_slice(hh)], ones], axis=1)
        return _dot_nt(q_aug[hh], k_aug), v_aug

    p_own = i // MOBA_BLOCKS_PER_STEP
    row = lax.broadcasted_iota(jnp.int32, (bs, pw), 0)
    col = lax.broadcasted_iota(jnp.int32, (bs, pw), 1)
    causal = col + (p_own * pw - i * bs) <= row
    onehot_own = e_ref[pl.ds(pl.multiple_of(p_own * pw, pw), pw), :]
    init = []
    for hh in heads:
        s, v_aug = pair_scores(p_own, hh, onehot_own)
        s = jnp.where(causal, s, NEG)
        m0 = jnp.max(s, axis=1, keepdims=True)
        init += [m0, _dot(jnp.exp(s - m0).astype(BF16), v_aug)]

    def body(p, carry):
        onehot = e_ref[pl.ds(pl.multiple_of(p * pw, pw), pw), :]
        out = []
        for hh in heads:
            m_i, acc = carry[2 * hh], carry[2 * hh + 1]
            sj, vj = pair_scores(p, hh, onehot)
            m_n = jnp.maximum(m_i, jnp.max(sj, axis=1, keepdims=True))
            out += [m_n, jnp.exp(m_i - m_n) * acc + _dot(jnp.exp(sj - m_n).astype(BF16), vj)]
        return tuple(out)

    final = lax.fori_loop(0, p_own, body, tuple(init))
    for hh in heads:
        acc = final[2 * hh + 1]
        o_ref[:, head_slice(hh)] = (acc[:, :hd] / acc[:, hd:]).astype(o_ref.dtype)


def _moba_prompt(q32, k16, v16, kmean):
    t = q32.shape[0]
    nb = t // MOBA_BLOCK
    hd = MOBA_HEADS_PER_STEP * A_HEAD_DIM
    assert nb <= LANES and nb % MOBA_BLOCKS_PER_STEP == 0 and A_HEADS % MOBA_HEADS_PER_STEP == 0
    block_onehot = (jnp.arange(t, dtype=jnp.int32)[:, None] // MOBA_BLOCK
                    == jnp.arange(LANES, dtype=jnp.int32)[None, :]).astype(BF16)
    kmean_pad = jnp.pad(kmean, ((0, LANES - nb), (0, 0)))
    return pl.pallas_call(
        _moba_prompt_kernel,
        grid=(A_HEADS // MOBA_HEADS_PER_STEP, nb),
        in_specs=[pl.BlockSpec((MOBA_BLOCK, hd), lambda h, i: (i, h)),
                  pl.BlockSpec((t, hd), lambda h, i: (0, h)),
                  pl.BlockSpec((t, hd), lambda h, i: (0, h)),
                  pl.BlockSpec((t, LANES), lambda h, i: (0, 0)),
                  pl.BlockSpec((LANES, hd), lambda h, i: (0, h))],
        out_specs=pl.BlockSpec((MOBA_BLOCK, hd), lambda h, i: (i, h)),
        out_shape=jax.ShapeDtypeStruct((t, A_WIDTH), BF16),
        compiler_params=_params("parallel", "arbitrary"),
        name="moba_prompt_attention",
    )(q32, k16, v16, block_onehot, kmean_pad)


def _moba_sample_kernel(n_pages, pages_per_block, pt_ref, q_ref, kn_ref, vn_ref, *refs):
    del pt_ref
    kp = refs[:n_pages]
    vp = refs[n_pages:2 * n_pages]
    o_ref = refs[2 * n_pages]
    nbp = n_pages // pages_per_block
    hd = A_HEAD_DIM
    rows = kp[0].shape[2]
    page = rows // A_HEADS
    q = q_ref[0]
    lane_i = lax.broadcasted_iota(jnp.int32, (A_HEADS, LANES), 1)

    sc = jnp.zeros((A_HEADS, LANES), F32)
    k16 = []
    for b in range(nbp):
        ksum = jnp.zeros((A_HEADS, hd), F32)
        for pp in range(pages_per_block):
            kf = kp[b * pages_per_block + pp][0, 0]
            ksum = ksum + jnp.sum(kf.reshape(page, A_HEADS, hd), axis=0)
            k16.append(kf.astype(BF16))
        kmean = ksum / float(MOBA_BLOCK)
        sc = jnp.where(lane_i == b, jnp.sum(q * kmean, axis=1, keepdims=True), sc)
    sel = _moba_select(sc, nbp, lane_i, lane_i.astype(F32)).astype(F32)

    qs = q * (hd ** -0.5)
    own = jnp.sum(qs * kn_ref[0], axis=1, keepdims=True)
    s = _dot_nt(qs.astype(BF16), jnp.concatenate(k16, axis=0))
    head_row = lax.broadcasted_iota(jnp.int32, (A_HEADS, rows), 0)
    row_head = lax.broadcasted_iota(jnp.int32, (A_HEADS, rows), 1) % A_HEADS
    mine = head_row == row_head
    s = jnp.concatenate(
        [jnp.where(mine & (sel[:, pg // pages_per_block:pg // pages_per_block + 1] > 0.5),
                   s[:, pg * rows:(pg + 1) * rows], NEG) for pg in range(n_pages)], axis=1)
    m = jnp.maximum(own, jnp.max(s, axis=1, keepdims=True))
    p = jnp.exp(s - m)
    p_own = jnp.exp(own - m)
    l = jnp.sum(p, axis=1, keepdims=True) + p_own
    v16 = jnp.concatenate([vp[pg][0, 0].astype(BF16) for pg in range(n_pages)], axis=0)
    o_ref[0] = (_dot(p.astype(BF16), v16) + p_own * vn_ref[0]) / l


def _moba_sample(q32, kn32, vn32, cache_k, cache_v, page_table):
    ns = q32.shape[0]
    page = cache_k.shape[2]
    n_pages = page_table.shape[1]
    ppb = MOBA_BLOCK // page
    head_spec = pl.BlockSpec((1, A_HEADS, A_HEAD_DIM), lambda s, pt: (s, 0, 0))

    def page_spec(p):
        return pl.BlockSpec((1, 1, page * A_HEADS, A_HEAD_DIM), lambda s, pt: (0, pt[s, p], 0, 0))

    def rows(cache):
        return cache.reshape(cache.shape[0], cache.shape[1], page * A_HEADS, A_HEAD_DIM)

    def heads(a):
        return a.reshape(ns, A_HEADS, A_HEAD_DIM)

    grid_spec = pltpu.PrefetchScalarGridSpec(
        num_scalar_prefetch=1,
        grid=(ns,),
        in_specs=[head_spec, head_spec, head_spec] + [page_spec(p) for p in range(n_pages)] * 2,
        out_specs=head_spec,
    )
    out = pl.pallas_call(
        functools.partial(_moba_sample_kernel, n_pages, ppb),
        grid_spec=grid_spec,
        out_shape=jax.ShapeDtypeStruct((ns, A_HEADS, A_HEAD_DIM), F32),
        compiler_params=_params("parallel"),
        name="moba_sample_attention",
    )(page_table, heads(q32), heads(kn32), heads(vn32), *([rows(cache_k)] * n_pages), *([rows(cache_v)] * n_pages))
    return out.reshape(ns, A_WIDTH)


def _mix_kernel(hm_ref, om_ref, ao_ref, gm_ref, ga_ref, gmh_ref, wbm_ref, wba_ref, o_ref, hs_ref):
    @pl.when(pl.program_id(1) == 0)
    def _():
        for hh in range(M_HEADS):
            sl = slice(hh * M_V, (hh + 1) * M_V)
            y = _rms(hm_ref[:, sl], gmh_ref[:, sl])
            hs_ref[:, sl] = (y * om_ref[:, sl].astype(F32)).astype(BF16)

    t1 = _dot(hs_ref[...], wbm_ref[...])
    t2 = _dot(ao_ref[...].astype(BF16), wba_ref[...])
    o_ref[...] = (gm_ref[...].astype(F32) * t1 + ga_ref[...].astype(F32) * t2).astype(o_ref.dtype)


def _mix(hm, sg, ao, g_mh, w_bm, w_ba):
    r = hm.shape[0]
    d = w_bm.shape[1]
    tm = _pick(r, (512, 256, 128, 64, 32, 16))
    tn = _pick(d, (1024, 512, 256, 128))
    assert M_WIDTH % tn == 0
    gm0 = M_WIDTH // tn
    ga0 = (M_WIDTH + d) // tn
    return pl.pallas_call(
        _mix_kernel,
        grid=(r // tm, d // tn),
        in_specs=[pl.BlockSpec((tm, M_WIDTH), lambda i, j: (i, 0)),
                  pl.BlockSpec((tm, M_WIDTH), lambda i, j: (i, 0)),
                  pl.BlockSpec((tm, A_WIDTH), lambda i, j: (i, 0)),
                  pl.BlockSpec((tm, tn), lambda i, j: (i, gm0 + j)),
                  pl.BlockSpec((tm, tn), lambda i, j: (i, ga0 + j)),
                  pl.BlockSpec((1, M_WIDTH), lambda i, j: (0, 0)),
                  pl.BlockSpec((M_WIDTH, tn), lambda i, j: (0, j)),
                  pl.BlockSpec((A_WIDTH, tn), lambda i, j: (0, j))],
        out_specs=pl.BlockSpec((tm, tn), lambda i, j: (i, j)),
        out_shape=jax.ShapeDtypeStruct((r, d), BF16),
        scratch_shapes=[pltpu.VMEM((tm, M_WIDTH), BF16)],
        compiler_params=_params("parallel", "arbitrary"),
        name="mixer_merge",
    )(hm, sg, ao, sg, sg, g_mh.reshape(1, M_WIDTH), w_bm, w_ba)


def _store_row_tiles(o_ref, x, rows):
    s = x.shape[1] // LANES
    for c in range(s):
        o_ref[pl.ds(c, rows, stride=s), :] = x[:, c * LANES:(c + 1) * LANES]


def _resid_router_kernel(x_ref, mix_ref, wout_ref, gffn_ref, wrh_ref, wrl_ref, br_ref,
                         x1_ref, x1t_ref, gate_ref, eid_ref):
    x1 = x_ref[...] + _dot(mix_ref[...], wout_ref[...])
    x1_ref[...] = x1
    _store_row_tiles(x1t_ref, x1, x1.shape[0])
    hh, hl = _split_bf16(_rms(x1, gffn_ref[...]))
    wrh = wrh_ref[...]
    logits = _dot(hh, wrh) + (_dot(hh, wrl_ref[...]) + _dot(hl, wrh)) + br_ref[...]
    lane_i = lax.broadcasted_iota(jnp.int32, logits.shape, 1)
    lane_f = lane_i.astype(F32)
    work = jnp.where(lane_i < N_EXPERTS, logits, -jnp.inf)
    vals, ids = [], []
    for _ in range(TOP_K):
        m, idx = _first_argmax(work, lane_f)
        vals.append(m)
        ids.append(idx)
        work = jnp.where(lane_f == idx, -jnp.inf, work)
    es = [jnp.exp(v - vals[0]) for v in vals]
    den = es[0]
    for e in es[1:]:
        den = den + e
    gate = jnp.zeros(logits.shape, F32)
    eid = jnp.zeros(logits.shape, F32)
    for k in range(TOP_K):
        gate = jnp.where(lane_i == k, es[k] / den, gate)
        eid = jnp.where(lane_i == k, ids[k], eid)
    gate_ref[...] = gate
    eid_ref[...] = eid.astype(jnp.int32)


def _resid_router(x, mix, w_out, g_ffn, wr_hi, wr_lo, b_r, n_total, row0, prev):
    r, d = x.shape
    s = d // LANES
    tm = _pick(r, (256, 128, 64, 32, 16))
    assert row0 % tm == 0
    b0 = row0 // tm
    out_shape = [jax.ShapeDtypeStruct((n_total, d), F32),
                 jax.ShapeDtypeStruct((n_total * s, LANES), F32),
                 jax.ShapeDtypeStruct((n_total, LANES), F32),
                 jax.ShapeDtypeStruct((n_total, LANES), jnp.int32)]
    out_specs = [pl.BlockSpec((tm, d), lambda i: (b0 + i, 0)),
                 pl.BlockSpec((tm * s, LANES), lambda i: (b0 + i, 0)),
                 pl.BlockSpec((tm, LANES), lambda i: (b0 + i, 0)),
                 pl.BlockSpec((tm, LANES), lambda i: (b0 + i, 0))]
    in_specs = [pl.BlockSpec((tm, d), lambda i: (i, 0)),
                pl.BlockSpec((tm, d), lambda i: (i, 0)),
                pl.BlockSpec((d, d), lambda i: (0, 0)),
                pl.BlockSpec((1, d), lambda i: (0, 0)),
                pl.BlockSpec((d, LANES), lambda i: (0, 0)),
                pl.BlockSpec((d, LANES), lambda i: (0, 0)),
                pl.BlockSpec((1, LANES), lambda i: (0, 0))]
    args = [x, mix, w_out, g_ffn.reshape(1, d), wr_hi, wr_lo, b_r]
    n_in = len(args)
    kern = _resid_router_kernel
    aliases = {}
    if prev is not None:
        in_specs = in_specs + [pl.BlockSpec(memory_space=pl.ANY)] * len(prev)
        args = args + list(prev)
        aliases = {n_in + k: k for k in range(len(prev))}

        def kern(*refs):
            _resid_router_kernel(*refs[:n_in], *refs[n_in + len(prev):])

    return pl.pallas_call(
        kern,
        grid=(r // tm,),
        in_specs=in_specs,
        out_specs=out_specs,
        out_shape=out_shape,
        input_output_aliases=aliases,
        compiler_params=_params("parallel"),
        name="residual_router_topk",
    )(*args)


def _row_pitch(s):
    return s + 1 if s % 2 == 0 else s


def _gathered_rows(buf, rows, s):
    return jnp.concatenate([buf[pl.ds(c, rows, stride=_row_pitch(s)), :] for c in range(s)], axis=1)


def _start_row_gathers(src_hbm, idx_ref, idx_base, idx_step, dst, rows, s, sem):
    pitch = _row_pitch(s)
    trips = rows // GATHER_UNROLL

    def trip(i, carry):
        for u in range(GATHER_UNROLL):
            r = ((i * GATHER_UNROLL + u) * GATHER_ORDER_STEP) & (rows - 1)
            idx = idx_ref[idx_base + r * idx_step]
            pltpu.make_async_copy(src_hbm.at[pl.ds(idx * s, s)], dst.at[pl.ds(r * pitch, s)],
                                  sem).start(priority=u % 2)
        return carry

    lax.fori_loop(0, trips, trip, 0)


def _wait_row_gathers(dst, rows, s, sem):
    landed = dst.at[pl.ds(0, rows * s)]
    pltpu.make_async_copy(landed, landed, sem).wait()


def _expert_tile_schedule(ce_ref, nx_ref, nv_ref, c, j, n_pass):
    first = (c == 0) | (ce_ref[c] != ce_ref[jnp.maximum(c - 1, 0)])
    more_in_pass = nx_ref[c] >= 0
    has_next = more_in_pass | (j + 1 < n_pass)
    next_e = jnp.where(more_in_pass, nx_ref[c], ce_ref[0])
    next_j = jnp.where(more_in_pass, j, j + 1)
    return (c < nv_ref[0]) & first, has_next, next_j, next_e


def _moe_up_kernel(f_lo, nf, gather, ce_ref, nx_ref, nv_ref, *refs):
    if gather:
        tok_ref, tok_next_ref, g_ref, bg_ref, bl_ref, xt_hbm, w_hbm, o_ref, xs_ref, gbuf, gsem, raw, w16, sem = refs
    else:
        x_ref, bg_ref, bl_ref, w_hbm, _, o_ref, raw, w16, sem = refs
    j = pl.program_id(0)
    c = pl.program_id(1)
    tf = raw.shape[2]
    n_pass = pl.num_programs(0)
    dff = nf * tf

    def fetch(jj, e, k):
        col = pl.multiple_of(k * dff + (f_lo + jj) * tf, LANES)
        return pltpu.make_async_copy(w_hbm.at[e, :, pl.ds(col, tf)], raw.at[k], sem.at[k])

    first, has_next, next_j, next_e = _expert_tile_schedule(ce_ref, nx_ref, nv_ref, c, j, n_pass)

    if gather:
        rows, d = xs_ref.shape
        s = d // LANES
        slot = c % 2

        @pl.when(c == 0)
        def _():
            _start_row_gathers(xt_hbm, tok_ref, 0, 1, gbuf.at[0], rows, s, gsem.at[0])

        @pl.when(c + 1 < nv_ref[0])
        def _():
            _start_row_gathers(xt_hbm, tok_next_ref, 0, 1, gbuf.at[1 - slot], rows, s, gsem.at[1 - slot])

    @pl.when(first & (j == 0) & (c == 0))
    def _():
        for k in range(2):
            fetch(0, ce_ref[0], k).start()

    @pl.when(first)
    def _():
        for k in range(2):
            fetch(j, ce_ref[c], k).wait()
            w16[k] = raw[k].astype(BF16)

    @pl.when(first & has_next)
    def _():
        for k in range(2):
            fetch(next_j, next_e, k).start()

    @pl.when(c < nv_ref[0])
    def _():
        if gather:
            _wait_row_gathers(gbuf.at[slot], rows, s, gsem.at[slot])
            x = _rms(_gathered_rows(gbuf.at[slot], rows, s), g_ref[...]).astype(BF16)
            xs_ref[...] = x
        else:
            x = x_ref[...]
        gate = jnp.minimum(_dot(x, w16[0]) + bg_ref[0], SWIGLU_LIMIT)
        lin = jnp.clip(_dot(x, w16[1]) + bl_ref[0], -SWIGLU_LIMIT, SWIGLU_LIMIT)
        o_ref[...] = (gate * jax.nn.sigmoid(SWIGLU_ALPHA * gate) * (lin + 1.0)).astype(o_ref.dtype)


def _moe_up(x1t, g_ffn, src_tok, w_up, b_up, chunk_expert, next_expert, n_valid, n_chunks):
    rows = MOE_ROWS
    e, d, f2 = w_up.shape
    s = d // LANES
    dff = f2 // 2
    tf = _pick(dff, (1024, 512, 256, 128))
    nf = dff // tf
    assert rows % GATHER_UNROLL == 0 and rows & (rows - 1) == 0 and GATHER_ORDER_STEP % 2 == 1
    b3 = b_up.reshape(e, 1, f2)
    weight_scratch = [pltpu.VMEM((2, d, tf), F32), pltpu.VMEM((2, d, tf), BF16), pltpu.SemaphoreType.DMA((2,))]
    act_shape = jax.ShapeDtypeStruct((n_chunks * rows, dff), BF16)

    def last(c, nv):
        return jnp.minimum(c, nv[0] - 1)

    def nxt(c, nv):
        return jnp.minimum(c + 1, nv[0] - 1)

    def bias_spec(off):
        return pl.BlockSpec((1, 1, tf), lambda j, c, ce, nx, nv: (ce[c], 0, off + j))

    def act_spec(f_lo):
        return pl.BlockSpec((rows, tf), lambda j, c, ce, nx, nv: (last(c, nv), f_lo + j))

    first_spec = pltpu.PrefetchScalarGridSpec(
        num_scalar_prefetch=3,
        grid=(1, n_chunks),
        in_specs=[pl.BlockSpec((rows,), lambda j, c, ce, nx, nv: (last(c, nv),), memory_space=pltpu.SMEM),
                  pl.BlockSpec((rows,), lambda j, c, ce, nx, nv: (nxt(c, nv),), memory_space=pltpu.SMEM),
                  pl.BlockSpec((1, d), lambda j, c, ce, nx, nv: (0, 0)),
                  bias_spec(0), bias_spec(nf),
                  pl.BlockSpec(memory_space=pl.ANY),
                  pl.BlockSpec(memory_space=pl.ANY)],
        out_specs=[act_spec(0),
                   pl.BlockSpec((rows, d), lambda j, c, ce, nx, nv: (last(c, nv), 0))],
        scratch_shapes=[pltpu.VMEM((2, rows * _row_pitch(s), LANES), F32), pltpu.SemaphoreType.DMA((2,))]
        + weight_scratch,
    )
    act, xs = pl.pallas_call(
        functools.partial(_moe_up_kernel, 0, nf, True),
        grid_spec=first_spec,
        out_shape=[act_shape, jax.ShapeDtypeStruct((n_chunks * rows, d), BF16)],
        compiler_params=_params("arbitrary", "arbitrary"),
        name="moe_gather_up_swiglu",
    )(chunk_expert, next_expert, n_valid, src_tok, src_tok, g_ffn.reshape(1, d), b3, b3, x1t, w_up)
    if nf == 1:
        return act
    rest_spec = pltpu.PrefetchScalarGridSpec(
        num_scalar_prefetch=3,
        grid=(nf - 1, n_chunks),
        in_specs=[pl.BlockSpec((rows, d), lambda j, c, ce, nx, nv: (last(c, nv), 0)),
                  bias_spec(1), bias_spec(nf + 1),
                  pl.BlockSpec(memory_space=pl.ANY),
                  pl.BlockSpec(memory_space=pl.ANY)],
        out_specs=act_spec(1),
        scratch_shapes=weight_scratch,
    )
    return pl.pallas_call(
        functools.partial(_moe_up_kernel, 1, nf, False),
        grid_spec=rest_spec,
        out_shape=act_shape,
        input_output_aliases={7: 0},
        compiler_params=_params("arbitrary", "arbitrary"),
        name="moe_up_swiglu",
    )(chunk_expert, next_expert, n_valid, xs, b3, b3, w_up, act)


def _moe_down_kernel(ce_ref, nx_ref, nv_ref, a_ref, b_ref, w_hbm, o_ref, raw, w16, sem):
    c = pl.program_id(0)

    def fetch(e):
        return pltpu.make_async_copy(w_hbm.at[e], raw, sem)

    first, has_next, _, next_e = _expert_tile_schedule(ce_ref, nx_ref, nv_ref, c, 0, 1)

    @pl.when(first & (c == 0))
    def _():
        fetch(ce_ref[0]).start()

    @pl.when(first)
    def _():
        fetch(ce_ref[c]).wait()
        w16[...] = raw[...].astype(BF16)

    @pl.when(first & has_next)
    def _():
        fetch(next_e).start()

    @pl.when(c < nv_ref[0])
    def _():
        y = _dot(a_ref[...], w16[...]) + b_ref[0]
        _store_row_tiles(o_ref, y, y.shape[0])


def _moe_down(act, w_down, b_down, chunk_expert, next_expert, n_valid):
    rows = MOE_ROWS
    n_chunks = act.shape[0] // rows
    e, dff, d = w_down.shape
    s = d // LANES

    def last(c, nv):
        return jnp.minimum(c, nv[0] - 1)

    grid_spec = pltpu.PrefetchScalarGridSpec(
        num_scalar_prefetch=3,
        grid=(n_chunks,),
        in_specs=[pl.BlockSpec((rows, dff), lambda c, ce, nx, nv: (last(c, nv), 0)),
                  pl.BlockSpec((1, 1, d), lambda c, ce, nx, nv: (ce[c], 0, 0)),
                  pl.BlockSpec(memory_space=pl.ANY)],
        out_specs=pl.BlockSpec((rows * s, LANES), lambda c, ce, nx, nv: (last(c, nv), 0)),
        scratch_shapes=[pltpu.VMEM((dff, d), F32), pltpu.VMEM((dff, d), BF16),
                        pltpu.SemaphoreType.DMA(())],
    )
    return pl.pallas_call(
        _moe_down_kernel,
        grid_spec=grid_spec,
        out_shape=jax.ShapeDtypeStruct((n_chunks * rows * s, LANES), F32),
        compiler_params=_params("arbitrary"),
        name="moe_down",
    )(chunk_expert, next_expert, n_valid, act, b_down.reshape(e, 1, d), w_down)


def _combine_kernel(dest_ref, dest_next_ref, x1_ref, gate_ref, gple_ref, yt_hbm, x2_ref, h3_ref, buf, sem):
    i = pl.program_id(0)
    rows = x1_ref.shape[0]
    s = x1_ref.shape[1] // LANES
    slot = i % 2

    def start(idx_ref, to):
        for k in range(TOP_K):
            _start_row_gathers(yt_hbm, idx_ref, k, TOP_K, buf.at[to, k], rows, s, sem.at[to])

    @pl.when(i == 0)
    def _():
        start(dest_ref, 0)

    @pl.when(i + 1 < pl.num_programs(0))
    def _():
        start(dest_next_ref, 1 - slot)

    for k in range(TOP_K):
        _wait_row_gathers(buf.at[slot, k], rows, s, sem.at[slot])
    gate = gate_ref[...]
    acc = x1_ref[...]
    for k in range(TOP_K):
        acc = acc + gate[:, k:k + 1] * _gathered_rows(buf.at[slot, k], rows, s)
    x2_ref[...] = acc
    h3_ref[...] = _rms(acc, gple_ref[...]).astype(h3_ref.dtype)


def _combine(x1, gate, dest, yt, g_ple):
    n, d = x1.shape
    s = d // LANES
    rows = _pick(n, (COMBINE_ROWS, 64, 32, 16))
    steps = n // rows
    assert rows % GATHER_UNROLL == 0 and rows & (rows - 1) == 0 and GATHER_ORDER_STEP % 2 == 1
    return pl.pallas_call(
        _combine_kernel,
        grid=(steps,),
        in_specs=[pl.BlockSpec((rows * TOP_K,), lambda i: (i,), memory_space=pltpu.SMEM),
                  pl.BlockSpec((rows * TOP_K,), lambda i: (jnp.minimum(i + 1, steps - 1),),
                               memory_space=pltpu.SMEM),
                  pl.BlockSpec((rows, d), lambda i: (i, 0)),
                  pl.BlockSpec((rows, LANES), lambda i: (i, 0)),
                  pl.BlockSpec((1, d), lambda i: (0, 0)),
                  pl.BlockSpec(memory_space=pl.ANY)],
        out_specs=[pl.BlockSpec((rows, d), lambda i: (i, 0)),
                   pl.BlockSpec((rows, d), lambda i: (i, 0))],
        out_shape=[jax.ShapeDtypeStruct((n, d), F32), jax.ShapeDtypeStruct((n, d), BF16)],
        scratch_shapes=[pltpu.VMEM((2, TOP_K, rows * _row_pitch(s), LANES), F32),
                        pltpu.SemaphoreType.DMA((2,))],
        compiler_params=_params("arbitrary"),
        name="moe_combine_gather",
    )(dest, dest, x1, gate, g_ple.reshape(1, d), yt)


def _ple_kernel(x2_ref, h3_ref, p_ref, wpg_ref, wple_ref, o_ref):
    gate = jax.nn.sigmoid(_dot(h3_ref[...], wpg_ref[...]))
    o_ref[...] = x2_ref[...] + gate * _dot(p_ref[...].astype(BF16), wple_ref[...])


def _ple(x2, h3, p, w_pg, w_ple, row0, r):
    d = x2.shape[1]
    pd = p.shape[1]
    tm = _pick(r, (512, 256, 128, 64, 32, 16))
    tn = _pick(d, (1024, 512, 256, 128))
    assert row0 % tm == 0
    b0 = row0 // tm
    return pl.pallas_call(
        _ple_kernel,
        grid=(d // tn, r // tm),
        in_specs=[pl.BlockSpec((tm, tn), lambda j, i: (b0 + i, j)),
                  pl.BlockSpec((tm, d), lambda j, i: (b0 + i, 0)),
                  pl.BlockSpec((tm, pd), lambda j, i: (i, 0)),
                  pl.BlockSpec((d, tn), lambda j, i: (0, j)),
                  pl.BlockSpec((pd, tn), lambda j, i: (0, j))],
        out_specs=pl.BlockSpec((tm, tn), lambda j, i: (i, j)),
        out_shape=jax.ShapeDtypeStruct((r, d), F32),
        compiler_params=_params("parallel", "parallel"),
        name="ple_gate",
    )(x2, h3, p, w_pg, w_ple)


def _rope_tables(pos):
    half = ROT_DIM // 2
    inv = jnp.power(ROPE_THETA, -jnp.arange(half, dtype=F32) * 2.0 / ROT_DIM)
    ang = pos.astype(F32)[:, None] * inv[None, :]
    cos, sin = jnp.cos(ang), jnp.sin(ang)
    r = pos.shape[0]
    rest = A_HEAD_DIM - ROT_DIM
    c = jnp.concatenate([cos, cos, jnp.ones((r, rest), F32)], axis=1)
    s1 = jnp.concatenate([-sin, jnp.zeros((r, half + rest), F32)], axis=1)
    s2 = jnp.concatenate([jnp.zeros((r, half), F32), sin, jnp.zeros((r, rest), F32)], axis=1)
    return c, s1, s2


def _mixer_inputs(x, pos, g_attn, w, b_i, b_f, g_q, g_k, mlstm_dtype):
    d = x.shape[1]
    kq = M_HEADS * M_QK
    o_ig = 2 * kq + M_WIDTH
    o_om = o_ig + 2 * M_HEADS
    o_qa = o_om + M_WIDTH
    o_gm = o_qa + 3 * A_WIDTH
    h = _rmsnorm_bf16(x, g_attn)
    colscale = jnp.concatenate([jnp.full((1, kq), M_QK ** -0.5, F32), jnp.ones((1, kq + M_WIDTH), F32)], axis=1)
    qkv_m = _proj_scale(h, w[:, :o_ig].astype(BF16), colscale, mlstm_dtype)
    wg = w[:, o_ig:o_om]
    wc = jnp.pad(wg, ((0, 0), (0, LANES - 2 * M_HEADS))).astype(BF16)
    bias = jnp.concatenate([b_i, b_f]).astype(F32)
    bc = jnp.pad(bias, (0, LANES - 2 * M_HEADS)).reshape(1, LANES)
    gcol, grow = _proj_gates(h, wc, wg.T.astype(BF16), bc, bias.reshape(2 * M_HEADS, 1))
    sg = _proj_sigmoid(h, jnp.concatenate([w[:, o_om:o_qa], w[:, o_gm:]], axis=1).astype(BF16))
    tabs = _rope_tables(pos)
    q32, _ = _proj_qk(h, w[:, o_qa:o_qa + A_WIDTH].astype(BF16), g_q, *tabs)
    k32, k16 = _proj_qk(h, w[:, o_qa + A_WIDTH:o_qa + 2 * A_WIDTH].astype(BF16), g_k, *tabs)
    v32, v16 = _proj_plain2(h, w[:, o_qa + 2 * A_WIDTH:o_gm].astype(BF16))
    return qkv_m, gcol, grow, sg, q32, k32, k16, v32, v16


def _moe_plan(eid, n_chunks):
    n = eid.shape[0]
    e_flat = eid[:, :TOP_K].reshape(-1)
    onehot = (e_flat[:, None] == jnp.arange(N_EXPERTS, dtype=jnp.int32)[None, :]).astype(jnp.int32)
    before = jnp.cumsum(onehot, axis=0) - onehot
    rank = jnp.sum(before * onehot, axis=1)
    counts = jnp.sum(onehot, axis=0)
    padded = (counts + MOE_ROWS - 1) // MOE_ROWS * MOE_ROWS
    p_ends = jnp.cumsum(padded)
    p_starts = p_ends - padded
    dest = (p_starts[e_flat] + rank).astype(jnp.int32)
    tok = jnp.repeat(jnp.arange(n, dtype=jnp.int32), TOP_K)
    src_tok = jnp.zeros((n_chunks * MOE_ROWS,), jnp.int32).at[dest].set(tok)
    n_valid = (p_ends[-1] // MOE_ROWS).astype(jnp.int32).reshape(1)
    chunk0 = jnp.minimum(jnp.arange(n_chunks, dtype=jnp.int32), n_valid[0] - 1) * MOE_ROWS
    chunk_expert = jnp.minimum(jnp.sum((p_ends[None, :] <= chunk0[:, None]).astype(jnp.int32), axis=1),
                               N_EXPERTS - 1)
    ids = jnp.arange(N_EXPERTS, dtype=jnp.int32)
    later = (ids[None, :] > chunk_expert[:, None]) & (counts[None, :] > 0)
    next_expert = jnp.min(jnp.where(later, ids[None, :], N_EXPERTS), axis=1)
    next_expert = jnp.where(next_expert < N_EXPERTS, next_expert, -1).astype(jnp.int32)
    return dest, src_tok, n_valid, chunk_expert, next_expert


def kernel(x_prompt, x_sample, cache_k, cache_v, state_mlstm_C, state_mlstm_n, state_mlstm_m, page_table, p_prompt, p_sample, g_attn, w_in, b_i, b_f, g_q, g_k, g_mh, w_bm, w_ba, w_out, g_ffn, w_router, b_router, w_up, b_up, w_down, b_down, g_ple, w_pg, w_ple):
    depth = w_in.shape[0]
    bsz, t, d = x_prompt.shape
    ns, dec_seq, _ = x_sample.shape
    assert depth == 1 and bsz == 1 and dec_seq == 1
    page = cache_k.shape[2]
    past = page_table.shape[1] * page
    assert t % MOBA_BLOCK == 0 and past % MOBA_BLOCK == 0 and MOBA_BLOCK % page == 0
    assert t % (M_CHUNK * MLSTM_CHUNKS_PER_STEP) == 0 and ns % MLSTM_SAMPLE_SEQS == 0
    n = t + ns

    xp = x_prompt[0]
    xs = x_sample[:, 0]
    w = w_in[0]
    fw = (g_attn[0], w, b_i[0], b_f[0], g_q[0], g_k[0])

    qkv_p, gcol_p, grow_p, sg_p, q32_p, k32_p, k16_p, v32_p, v16_p = _mixer_inputs(
        xp, jnp.arange(t, dtype=jnp.int32), *fw, BF16)
    hm_p, ct_p, n_p, m_p = _mlstm_prompt(qkv_p, gcol_p, grow_p)
    ao_p = _moba_prompt(q32_p, k16_p, v16_p, _block_kmean(k32_p))

    qkv_s, gcol_s, _, sg_s, q32_s, k32_s, _, v32_s, _ = _mixer_inputs(
        xs, jnp.full((ns,), past, jnp.int32), *fw, F32)
    hm_s, c_s, n_s, m_s = _mlstm_sample(qkv_s, gcol_s, state_mlstm_C[0].astype(F32),
                                        state_mlstm_n[0].astype(F32), state_mlstm_m[0].astype(F32))
    ao_s = _moba_sample(q32_s, k32_s, v32_s, cache_k, cache_v, page_table)

    wbm, wba, wout = w_bm[0].astype(BF16), w_ba[0].astype(BF16), w_out[0].astype(BF16)
    mix_p = _mix(hm_p, sg_p, ao_p, g_mh[0], wbm, wba)
    mix_s = _mix(hm_s, sg_s, ao_s, g_mh[0], wbm, wba)
    wr = jnp.pad(w_router[0], ((0, 0), (0, LANES - N_EXPERTS)))
    wr_hi = wr.astype(BF16)
    wr_lo = (wr - wr_hi.astype(F32)).astype(BF16)
    b_r = jnp.pad(b_router[0].astype(F32), (0, LANES - N_EXPERTS)).reshape(1, LANES)
    routed = _resid_router(xp, mix_p, wout, g_ffn[0], wr_hi, wr_lo, b_r, n, 0, None)
    x1, x1t, gate, eid = _resid_router(xs, mix_s, wout, g_ffn[0], wr_hi, wr_lo, b_r, n, t, routed)

    n_slots = n * TOP_K
    n_chunks = (n_slots + N_EXPERTS * (MOE_ROWS - 1)) // MOE_ROWS
    dest, src_tok, n_valid, chunk_expert, next_expert = _moe_plan(eid, n_chunks)
    act = _moe_up(x1t, g_ffn[0], src_tok, w_up[0], b_up[0], chunk_expert, next_expert, n_valid, n_chunks)
    yt = _moe_down(act, w_down[0], b_down[0], chunk_expert, next_expert, n_valid)
    x2, h3 = _combine(x1, gate, dest, yt, g_ple[0])

    wpg, wple = w_pg[0].astype(BF16), w_ple[0].astype(BF16)
    y_p = _ple(x2, h3, p_prompt[0, 0], wpg, wple, 0, t)
    y_s = _ple(x2, h3, p_sample[0, :, 0], wpg, wple, t, ns)

    def heads(a):
        return a.reshape(a.shape[0], A_HEADS, A_HEAD_DIM)

    return (y_p[None], y_s[:, None],
            heads(k32_p)[None, None], heads(v32_p)[None, None],
            jnp.swapaxes(ct_p, 1, 2)[None, None].astype(state_mlstm_C.dtype),
            n_p[:M_HEADS][None, None].astype(state_mlstm_n.dtype),
            m_p[:M_HEADS, 0][None, None].astype(state_mlstm_m.dtype),
            heads(k32_s)[None, :, None], heads(v32_s)[None, :, None],
            c_s[None].astype(state_mlstm_C.dtype), n_s[None].astype(state_mlstm_n.dtype),
            m_s[None].astype(state_mlstm_m.dtype))
```

```python
import functools

import jax
import jax.numpy as jnp
import numpy as np
from jax import lax
from jax.experimental import pallas as pl
from jax.experimental.pallas import tpu as pltpu

F32 = jnp.float32
BF16 = jnp.bfloat16

M_HEADS = 4
M_QK = 128
M_V = 256
M_CHUNK = 64
M_WIDTH = M_HEADS * M_V
A_HEADS = 8
A_HEAD_DIM = 128
A_WIDTH = A_HEADS * A_HEAD_DIM
MOBA_BLOCK = 256
MOBA_TOPK = 3
ROT_DIM = A_HEAD_DIM // 4
ROPE_THETA = 500000.0
N_EXPERTS = 32
TOP_K = 4
SWIGLU_LIMIT = 7.0
SWIGLU_ALPHA = 1.702
EPS = 1e-6
NEG = -1e30

LANES = 128
SUBLANES = 8
VMEM_LIMIT_BYTES = 56 * 1024 * 1024

MOE_ROWS = 256
MLSTM_CHUNKS_PER_STEP = 2
MLSTM_SAMPLE_SEQS = 8
COMBINE_ROWS = 128
MOBA_HEADS_PER_STEP = 8
MOBA_BLOCKS_PER_STEP = 4
GATHER_UNROLL = 8


def _pick(n, candidates):
    for c in candidates:
        if n % c == 0:
            return c
    raise ValueError(f"no tile in {candidates} divides {n}")


def _params(*sem):
    return pltpu.CompilerParams(dimension_semantics=sem, vmem_limit_bytes=VMEM_LIMIT_BYTES)


def _dot(a, b):
    return jnp.dot(a, b, preferred_element_type=F32)


def _dot_nt(a, b):
    return lax.dot_general(a, b, (((1,), (1,)), ((), ())), preferred_element_type=F32)


def _dot_tn(a, b):
    return lax.dot_general(a, b, (((0,), (0,)), ((), ())), preferred_element_type=F32)


def _split_bf16(x):
    hi = x.astype(BF16)
    lo = (x - hi.astype(F32)).astype(BF16)
    return hi, lo


def _rms(x, g):
    return x * lax.rsqrt(jnp.mean(x * x, axis=-1, keepdims=True) + EPS) * g


def _log_sigmoid(x):
    return -(jnp.maximum(-x, 0.0) + jnp.log1p(jnp.exp(-jnp.abs(x))))


def _first_argmax(work, lane_f):
    m = jnp.max(work, axis=-1, keepdims=True)
    idx = jnp.min(jnp.where(work == m, lane_f, float(4 * LANES)), axis=-1, keepdims=True)
    return m, idx


def _rmsnorm_kernel(x_ref, g_ref, o_ref):
    o_ref[...] = _rms(x_ref[...], g_ref[...]).astype(o_ref.dtype)


def _rmsnorm_bf16(x, g):
    r, d = x.shape
    tm = _pick(r, (512, 256, 128, 64, 32, 16))
    return pl.pallas_call(
        _rmsnorm_kernel,
        grid=(r // tm,),
        in_specs=[pl.BlockSpec((tm, d), lambda i: (i, 0)), pl.BlockSpec((1, d), lambda i: (0, 0))],
        out_specs=pl.BlockSpec((tm, d), lambda i: (i, 0)),
        out_shape=jax.ShapeDtypeStruct((r, d), BF16),
        compiler_params=_params("parallel"),
        name="rmsnorm_rows",
    )(x, g.reshape(1, d))


def _proj_scale_kernel(h_ref, w_ref, s_ref, o_ref):
    o_ref[...] = (_dot(h_ref[...], w_ref[...]) * s_ref[...]).astype(o_ref.dtype)


def _proj_sigmoid_kernel(h_ref, w_ref, o_ref):
    o_ref[...] = jax.nn.sigmoid(_dot(h_ref[...], w_ref[...])).astype(o_ref.dtype)


def _proj_plain2_kernel(h_ref, w_ref, o32_ref, o16_ref):
    acc = _dot(h_ref[...], w_ref[...])
    o32_ref[...] = acc
    o16_ref[...] = acc.astype(BF16)


def _proj_qk_kernel(h_ref, w_ref, g_ref, c_ref, s1_ref, s2_ref, o32_ref, o16_ref):
    acc = _dot(h_ref[...], w_ref[...])
    g = g_ref[...]
    c, s1, s2 = c_ref[...], s1_ref[...], s2_ref[...]
    for hh in range(acc.shape[1] // A_HEAD_DIM):
        sl = slice(hh * A_HEAD_DIM, (hh + 1) * A_HEAD_DIM)
        y = _rms(acc[:, sl], g)
        up = pltpu.roll(y, A_HEAD_DIM - ROT_DIM // 2, 1)
        dn = pltpu.roll(y, ROT_DIM // 2, 1)
        r = y * c + up * s1 + dn * s2
        o32_ref[:, sl] = r
        o16_ref[:, sl] = r.astype(BF16)


def _proj_gates_kernel(h_ref, wc_ref, wr_ref, bc_ref, br_ref, gc_ref, gr_ref):
    h = h_ref[...]
    zc = _dot(h, wc_ref[...]) + bc_ref[...]
    zr = _dot_nt(wr_ref[...], h) + br_ref[...]
    lane = lax.broadcasted_iota(jnp.int32, zc.shape, 1)
    gc_ref[...] = jnp.where((lane >= M_HEADS) & (lane < 2 * M_HEADS), _log_sigmoid(zc), zc)
    row = lax.broadcasted_iota(jnp.int32, zr.shape, 0)
    gr_ref[...] = jnp.where(row >= M_HEADS, _log_sigmoid(zr), zr)


def _proj_tiles(r, n):
    tm = _pick(r, (1024, 512, 256, 128, 64, 32, 16))
    tn = _pick(n, (1024, 512, 256, 128))
    return tm, tn


def _proj_scale(h, w, colscale, out_dtype):
    r, d = h.shape
    n = w.shape[1]
    tm, tn = _proj_tiles(r, n)
    return pl.pallas_call(
        _proj_scale_kernel,
        grid=(n // tn, r // tm),
        in_specs=[pl.BlockSpec((tm, d), lambda j, i: (i, 0)),
                  pl.BlockSpec((d, tn), lambda j, i: (0, j)),
                  pl.BlockSpec((1, tn), lambda j, i: (0, j))],
        out_specs=pl.BlockSpec((tm, tn), lambda j, i: (i, j)),
        out_shape=jax.ShapeDtypeStruct((r, n), out_dtype),
        compiler_params=_params("parallel", "parallel"),
        name="proj_scale",
    )(h, w, colscale)


def _proj_sigmoid(h, w):
    r, d = h.shape
    n = w.shape[1]
    tm, tn = _proj_tiles(r, n)
    return pl.pallas_call(
        _proj_sigmoid_kernel,
        grid=(n // tn, r // tm),
        in_specs=[pl.BlockSpec((tm, d), lambda j, i: (i, 0)),
                  pl.BlockSpec((d, tn), lambda j, i: (0, j))],
        out_specs=pl.BlockSpec((tm, tn), lambda j, i: (i, j)),
        out_shape=jax.ShapeDtypeStruct((r, n), BF16),
        compiler_params=_params("parallel", "parallel"),
        name="proj_sigmoid",
    )(h, w)


def _proj_plain2(h, w):
    r, d = h.shape
    n = w.shape[1]
    tm, tn = _proj_tiles(r, n)
    return pl.pallas_call(
        _proj_plain2_kernel,
        grid=(n // tn, r // tm),
        in_specs=[pl.BlockSpec((tm, d), lambda j, i: (i, 0)),
                  pl.BlockSpec((d, tn), lambda j, i: (0, j))],
        out_specs=[pl.BlockSpec((tm, tn), lambda j, i: (i, j)),
                   pl.BlockSpec((tm, tn), lambda j, i: (i, j))],
        out_shape=[jax.ShapeDtypeStruct((r, n), F32), jax.ShapeDtypeStruct((r, n), BF16)],
        compiler_params=_params("parallel", "parallel"),
        name="proj_plain",
    )(h, w)


def _proj_qk(h, w, g, rope_c, rope_s1, rope_s2):
    r, d = h.shape
    n = w.shape[1]
    tm, tn = _proj_tiles(r, n)
    hd = A_HEAD_DIM
    return pl.pallas_call(
        _proj_qk_kernel,
        grid=(n // tn, r // tm),
        in_specs=[pl.BlockSpec((tm, d), lambda j, i: (i, 0)),
                  pl.BlockSpec((d, tn), lambda j, i: (0, j)),
                  pl.BlockSpec((1, hd), lambda j, i: (0, 0)),
                  pl.BlockSpec((tm, hd), lambda j, i: (i, 0)),
                  pl.BlockSpec((tm, hd), lambda j, i: (i, 0)),
                  pl.BlockSpec((tm, hd), lambda j, i: (i, 0))],
        out_specs=[pl.BlockSpec((tm, tn), lambda j, i: (i, j)),
                   pl.BlockSpec((tm, tn), lambda j, i: (i, j))],
        out_shape=[jax.ShapeDtypeStruct((r, n), F32), jax.ShapeDtypeStruct((r, n), BF16)],
        compiler_params=_params("parallel", "parallel"),
        name="proj_qk_norm_rope",
    )(h, w, g.reshape(1, hd), rope_c, rope_s1, rope_s2)


def _proj_gates(h, wc, wr, bc, br):
    r, d = h.shape
    tm = _pick(r, (1024, 512, 256, 128)) if r % LANES == 0 else r
    g2 = 2 * M_HEADS
    return pl.pallas_call(
        _proj_gates_kernel,
        grid=(r // tm,),
        in_specs=[pl.BlockSpec((tm, d), lambda i: (i, 0)),
                  pl.BlockSpec((d, LANES), lambda i: (0, 0)),
                  pl.BlockSpec((g2, d), lambda i: (0, 0)),
                  pl.BlockSpec((1, LANES), lambda i: (0, 0)),
                  pl.BlockSpec((g2, 1), lambda i: (0, 0))],
        out_specs=[pl.BlockSpec((tm, LANES), lambda i: (i, 0)),
                   pl.BlockSpec((g2, tm), lambda i: (0, i))],
        out_shape=[jax.ShapeDtypeStruct((r, LANES), F32), jax.ShapeDtypeStruct((g2, r), F32)],
        compiler_params=_params("parallel"),
        name="proj_gates",
    )(h, wc, wr, bc, br)


def _mlstm_prompt_kernel(q_ref, k_ref, v_ref, gc_ref, gr_ref, h_ref, ct_out, n_out, m_out,
                         ct_s, n_s, m_s):
    step = pl.program_id(0)

    @pl.when(step == 0)
    def _():
        ct_s[...] = jnp.zeros_like(ct_s)
        n_s[...] = jnp.zeros_like(n_s)
        m_s[...] = jnp.zeros_like(m_s)

    ln = M_CHUNK
    row = lax.broadcasted_iota(jnp.int32, (ln, ln), 0)
    col = lax.broadcasted_iota(jnp.int32, (ln, ln), 1)
    causal = col <= row
    upto = row <= col
    pre = {}
    for cc in range(MLSTM_CHUNKS_PER_STEP):
        rs = slice(cc * ln, (cc + 1) * ln)
        for hh in range(M_HEADS):
            ig_c = gc_ref[rs, hh:hh + 1]
            lf_c = gc_ref[rs, M_HEADS + hh:M_HEADS + hh + 1]
            ig_r = gr_ref[hh:hh + 1, rs]
            lf_r = gr_ref[M_HEADS + hh:M_HEADS + hh + 1, rs]
            bcum_c = jnp.sum(jnp.where(causal, lf_r, 0.0), axis=1, keepdims=True)
            bcum_r = jnp.sum(jnp.where(upto, lf_c, 0.0), axis=0, keepdims=True)
            dmat = jnp.where(causal, bcum_c - bcum_r + ig_r, NEG)
            qh = q_ref[rs, hh * M_QK:(hh + 1) * M_QK]
            kh = k_ref[rs, hh * M_QK:(hh + 1) * M_QK]
            b_last = bcum_c[ln - 1:ln, :]
            pre[cc, hh] = dict(
                bcum_c=bcum_c, dmat=dmat, dmax=jnp.max(dmat, axis=1, keepdims=True),
                qk=_dot_nt(qh, kh), b_last=b_last,
                d_last_c=b_last - bcum_c + ig_c,
                d_last_max=jnp.max(b_last - bcum_r + ig_r, axis=1, keepdims=True))

    for hh in range(M_HEADS):
        ct = ct_s[hh]
        n_row = n_s[hh:hh + 1, :]
        m_prev = m_s[hh:hh + 1, 0:1]
        for cc in range(MLSTM_CHUNKS_PER_STEP):
            rs = slice(cc * ln, (cc + 1) * ln)
            u = pre[cc, hh]
            qh = q_ref[rs, hh * M_QK:(hh + 1) * M_QK]
            kh = k_ref[rs, hh * M_QK:(hh + 1) * M_QK]
            vh = v_ref[rs, hh * M_V:(hh + 1) * M_V]
            a = u["bcum_c"] + m_prev
            m_t = jnp.maximum(a, u["dmax"])
            w_inter = jnp.exp(a - m_t)
            s = u["qk"] * jnp.exp(u["dmat"] - m_t)
            num = w_inter * _dot(qh, ct.astype(BF16)) + _dot(s.astype(BF16), vh)
            den = (w_inter * jnp.sum(qh.astype(F32) * n_row, axis=1, keepdims=True)
                   + jnp.sum(s, axis=1, keepdims=True))
            h_ref[rs, hh * M_V:(hh + 1) * M_V] = num / jnp.maximum(jnp.abs(den), jnp.exp(-m_t))
            a_last = u["b_last"] + m_prev
            m_new = jnp.maximum(a_last, u["d_last_max"])
            w_c = jnp.exp(a_last - m_new)
            w_j = jnp.exp(u["d_last_c"] - m_new)
            vw = (vh.astype(F32) * w_j).astype(BF16)
            ct = w_c * ct + _dot_tn(kh, vw)
            n_row = w_c * n_row + jnp.sum(kh.astype(F32) * w_j, axis=0, keepdims=True)
            m_prev = m_new
        ct_s[hh] = ct
        n_s[hh:hh + 1, :] = n_row
        m_s[hh:hh + 1, :] = jnp.broadcast_to(m_prev, (1, LANES))

    @pl.when(step == pl.num_programs(0) - 1)
    def _():
        ct_out[...] = ct_s[...]
        n_out[...] = n_s[...]
        m_out[...] = m_s[...]


def _mlstm_prompt(qkv, gcol, grow):
    t = qkv.shape[0]
    rows = M_CHUNK * MLSTM_CHUNKS_PER_STEP
    kq = M_HEADS * M_QK
    return pl.pallas_call(
        _mlstm_prompt_kernel,
        grid=(t // rows,),
        in_specs=[pl.BlockSpec((rows, kq), lambda c: (c, 0)),
                  pl.BlockSpec((rows, kq), lambda c: (c, 1)),
                  pl.BlockSpec((rows, M_WIDTH), lambda c: (c, (2 * kq) // M_WIDTH)),
                  pl.BlockSpec((rows, LANES), lambda c: (c, 0)),
                  pl.BlockSpec((2 * M_HEADS, rows), lambda c: (0, c))],
        out_specs=[pl.BlockSpec((rows, M_WIDTH), lambda c: (c, 0)),
                   pl.BlockSpec((M_HEADS, M_QK, M_V), lambda c: (0, 0, 0)),
                   pl.BlockSpec((SUBLANES, LANES), lambda c: (0, 0)),
                   pl.BlockSpec((SUBLANES, LANES), lambda c: (0, 0))],
        out_shape=[jax.ShapeDtypeStruct((t, M_WIDTH), F32),
                   jax.ShapeDtypeStruct((M_HEADS, M_QK, M_V), F32),
                   jax.ShapeDtypeStruct((SUBLANES, LANES), F32),
                   jax.ShapeDtypeStruct((SUBLANES, LANES), F32)],
        scratch_shapes=[pltpu.VMEM((M_HEADS, M_QK, M_V), F32),
                        pltpu.VMEM((SUBLANES, LANES), F32),
                        pltpu.VMEM((SUBLANES, LANES), F32)],
        compiler_params=_params("arbitrary"),
        name="mlstm_prompt_scan",
    )(qkv, qkv, qkv, gcol, grow)


def _mlstm_sample_kernel(qkv_ref, gc_ref, c_ref, n_ref, m_ref, h_ref, co_ref, no_ref, mo_ref):
    kq = M_HEADS * M_QK
    row8 = lax.broadcasted_iota(jnp.int32, (SUBLANES, 1), 0)
    mrow = lax.broadcasted_iota(jnp.int32, mo_ref.shape, 0)
    mcol = lax.broadcasted_iota(jnp.int32, mo_ref.shape, 1)
    m_all = m_ref[...]
    for s in range(MLSTM_SAMPLE_SEQS):
        for hh in range(M_HEADS):
            q = qkv_ref[s:s + 1, hh * M_QK:(hh + 1) * M_QK]
            k = qkv_ref[s:s + 1, kq + hh * M_QK:kq + (hh + 1) * M_QK]
            v = qkv_ref[s:s + 1, 2 * kq + hh * M_V:2 * kq + (hh + 1) * M_V]
            ig = gc_ref[s:s + 1, hh:hh + 1]
            lf = gc_ref[s:s + 1, M_HEADS + hh:M_HEADS + hh + 1]
            m_prev = m_ref[s:s + 1, hh:hh + 1]
            c = c_ref[s, hh]
            n_row = n_ref[s, hh:hh + 1, :]
            a = lf + m_prev
            m_t = jnp.maximum(a, ig)
            w_c = jnp.exp(a - m_t)
            w_j = jnp.exp(ig - m_t)
            sc = jnp.sum(q * k, axis=1, keepdims=True) * w_j
            q8 = jnp.broadcast_to(q, (SUBLANES, M_QK)).astype(BF16)
            cq = _dot_nt(q8, c.astype(BF16))[0:1, :]
            num = w_c * cq + sc * v
            den = w_c * jnp.sum(n_row * q, axis=1, keepdims=True) + sc
            h_ref[s:s + 1, hh * M_V:(hh + 1) * M_V] = num / jnp.maximum(jnp.abs(den), jnp.exp(-m_t))
            vw8 = jnp.where(row8 == 0, v * w_j, 0.0).astype(BF16)
            k8 = jnp.where(row8 == 0, k, 0.0).astype(BF16)
            co_ref[s, hh] = w_c * c + _dot_tn(vw8, k8)
            no_ref[s, hh:hh + 1, :] = w_c * n_row + w_j * k
            m_all = jnp.where((mrow == s) & (mcol == hh), m_t, m_all)
    mo_ref[...] = m_all


def _mlstm_sample(qkv, gcol, c0, n0, m0):
    ns = qkv.shape[0]
    sb = MLSTM_SAMPLE_SEQS
    wq = qkv.shape[1]
    return pl.pallas_call(
        _mlstm_sample_kernel,
        grid=(ns // sb,),
        in_specs=[pl.BlockSpec((sb, wq), lambda i: (i, 0)),
                  pl.BlockSpec((sb, LANES), lambda i: (i, 0)),
                  pl.BlockSpec((sb, M_HEADS, M_V, M_QK), lambda i: (i, 0, 0, 0)),
                  pl.BlockSpec((sb, M_HEADS, M_QK), lambda i: (i, 0, 0)),
                  pl.BlockSpec((sb, M_HEADS), lambda i: (i, 0))],
        out_specs=[pl.BlockSpec((sb, M_WIDTH), lambda i: (i, 0)),
                   pl.BlockSpec((sb, M_HEADS, M_V, M_QK), lambda i: (i, 0, 0, 0)),
                   pl.BlockSpec((sb, M_HEADS, M_QK), lambda i: (i, 0, 0)),
                   pl.BlockSpec((sb, M_HEADS), lambda i: (i, 0))],
        out_shape=[jax.ShapeDtypeStruct((ns, M_WIDTH), F32),
                   jax.ShapeDtypeStruct(c0.shape, F32),
                   jax.ShapeDtypeStruct(n0.shape, F32),
                   jax.ShapeDtypeStruct(m0.shape, F32)],
        compiler_params=_params("parallel"),
        name="mlstm_sample_step",
    )(qkv, gcol, c0, n0, m0)


def _kmean_kernel(k_ref, o_ref):
    o_ref[0] = jnp.mean(k_ref[...], axis=0, keepdims=True)


def _block_kmean(k32):
    t, w = k32.shape
    nb = t // MOBA_BLOCK
    out = pl.pallas_call(
        _kmean_kernel,
        grid=(nb,),
        in_specs=[pl.BlockSpec((MOBA_BLOCK, w), lambda b: (b, 0))],
        out_specs=pl.BlockSpec((1, 1, w), lambda b: (b, 0, 0)),
        out_shape=jax.ShapeDtypeStruct((nb, 1, w), F32),
        compiler_params=_params("parallel"),
        name="moba_block_kmean",
    )(k32)
    return out.reshape(nb, w)


def _moba_select(sc, own, lane_i, lane_f):
    work = jnp.where(lane_i < own, sc, NEG)
    sel = jnp.zeros(sc.shape, dtype=jnp.bool_)
    for _ in range(MOBA_TOPK):
        _, idx = _first_argmax(work, lane_f)
        hit = lane_f == idx
        sel = sel | hit
        work = jnp.where(hit, -jnp.inf, work)
    return sel & (lane_i < own)


def _moba_prompt_kernel(q_ref, k_ref, v_ref, e_ref, km_ref, o_ref):
    i = pl.program_id(1)
    bs = MOBA_BLOCK
    pw = MOBA_BLOCKS_PER_STEP * bs
    hd = A_HEAD_DIM
    heads = range(MOBA_HEADS_PER_STEP)
    ones = jnp.ones((pw, hd), BF16)
    lane_i = lax.broadcasted_iota(jnp.int32, (bs, LANES), 1)
    lane_f = lane_i.astype(F32)

    def head_slice(hh):
        return slice(hh * hd, (hh + 1) * hd)

    q_aug = []
    for hh in heads:
        q = q_ref[:, head_slice(hh)]
        qh, ql = _split_bf16(q)
        kmh, kml = _split_bf16(km_ref[:, head_slice(hh)])
        sc = _dot_nt(qh, kmh) + (_dot_nt(qh, kml) + _dot_nt(ql, kmh))
        sel = _moba_select(sc, i, lane_i, lane_f)
        bias = jnp.where(sel | (lane_i >= i), 0.0, NEG)
        q_aug.append(jnp.concatenate([(q * (hd ** -0.5)).astype(BF16), bias.astype(BF16)], axis=1))

    def pair_scores(p, hh, onehot):
        off = pl.multiple_of(p * pw, pw)
        k_aug = jnp.concatenate([k_ref[pl.ds(off, pw), head_slice(hh)], onehot], axis=1)
        v_aug = jnp.concatenate([v_ref[pl.ds(off, pw), head_slice(hh)], ones], axis=1)
        return _dot_nt(q_aug[hh], k_aug), v_aug

    p_own = i // MOBA_BLOCKS_PER_STEP
    row = lax.broadcasted_iota(jnp.int32, (bs, pw), 0)
    col = lax.broadcasted_iota(jnp.int32, (bs, pw), 1)
    causal = col + (p_own * pw - i * bs) <= row
    onehot_own = e_ref[pl.ds(pl.multiple_of(p_own * pw, pw), pw), :]
    init = []
    for hh in heads:
        s, v_aug = pair_scores(p_own, hh, onehot_own)
        s = jnp.where(causal, s, NEG)
        m0 = jnp.max(s, axis=1, keepdims=True)
        init += [m0, _dot(jnp.exp(s - m0).astype(BF16), v_aug)]

    def body(p, carry):
        onehot = e_ref[pl.ds(pl.multiple_of(p * pw, pw), pw), :]
        out = []
        for hh in heads:
            m_i, acc = carry[2 * hh], carry[2 * hh + 1]
            sj, vj = pair_scores(p, hh, onehot)
            m_n = jnp.maximum(m_i, jnp.max(sj, axis=1, keepdims=True))
            out += [m_n, jnp.exp(m_i - m_n) * acc + _dot(jnp.exp(sj - m_n).astype(BF16), vj)]
        return tuple(out)

    final = lax.fori_loop(0, p_own, body, tuple(init))
    for hh in heads:
        acc = final[2 * hh + 1]
        o_ref[:, head_slice(hh)] = (acc[:, :hd] / acc[:, hd:]).astype(o_ref.dtype)


def _moba_prompt(q32, k16, v16, kmean):
    t = q32.shape[0]
    nb = t // MOBA_BLOCK
    hd = MOBA_HEADS_PER_STEP * A_HEAD_DIM
    assert nb <= LANES and nb % MOBA_BLOCKS_PER_STEP == 0 and A_HEADS % MOBA_HEADS_PER_STEP == 0
    block_onehot = (jnp.arange(t, dtype=jnp.int32)[:, None] // MOBA_BLOCK
                    == jnp.arange(LANES, dtype=jnp.int32)[None, :]).astype(BF16)
    kmean_pad = jnp.pad(kmean, ((0, LANES - nb), (0, 0)))
    return pl.pallas_call(
        _moba_prompt_kernel,
        grid=(A_HEADS // MOBA_HEADS_PER_STEP, nb),
        in_specs=[pl.BlockSpec((MOBA_BLOCK, hd), lambda h, i: (i, h)),
                  pl.BlockSpec((t, hd), lambda h, i: (0, h), pipeline_mode=pl.Buffered(1)),
                  pl.BlockSpec((t, hd), lambda h, i: (0, h), pipeline_mode=pl.Buffered(1)),
                  pl.BlockSpec((t, LANES), lambda h, i: (0, 0), pipeline_mode=pl.Buffered(1)),
                  pl.BlockSpec((LANES, hd), lambda h, i: (0, h))],
        out_specs=pl.BlockSpec((MOBA_BLOCK, hd), lambda h, i: (i, h)),
        out_shape=jax.ShapeDtypeStruct((t, A_WIDTH), BF16),
        compiler_params=_params("parallel", "arbitrary"),
        name="moba_prompt_attention",
    )(q32, k16, v16, block_onehot, kmean_pad)


def _moba_sample_kernel(n_pages, pages_per_block, pt_ref, q_ref, kn_ref, vn_ref, *refs):
    del pt_ref
    kp = refs[:n_pages]
    vp = refs[n_pages:2 * n_pages]
    o_ref = refs[2 * n_pages]
    nbp = n_pages // pages_per_block
    hd = A_HEAD_DIM
    rows = kp[0].shape[2]
    page = rows // A_HEADS
    q = q_ref[0]
    lane_i = lax.broadcasted_iota(jnp.int32, (A_HEADS, LANES), 1)

    sc = jnp.zeros((A_HEADS, LANES), F32)
    k16 = []
    for b in range(nbp):
        ksum = jnp.zeros((A_HEADS, hd), F32)
        for pp in range(pages_per_block):
            kf = kp[b * pages_per_block + pp][0, 0]
            ksum = ksum + jnp.sum(kf.reshape(page, A_HEADS, hd), axis=0)
            k16.append(kf.astype(BF16))
        kmean = ksum / float(MOBA_BLOCK)
        sc = jnp.where(lane_i == b, jnp.sum(q * kmean, axis=1, keepdims=True), sc)
    sel = _moba_select(sc, nbp, lane_i, lane_i.astype(F32)).astype(F32)

    qs = q * (hd ** -0.5)
    own = jnp.sum(qs * kn_ref[0], axis=1, keepdims=True)
    s = _dot_nt(qs.astype(BF16), jnp.concatenate(k16, axis=0))
    head_row = lax.broadcasted_iota(jnp.int32, (A_HEADS, rows), 0)
    row_head = lax.broadcasted_iota(jnp.int32, (A_HEADS, rows), 1) % A_HEADS
    mine = head_row == row_head
    s = jnp.concatenate(
        [jnp.where(mine & (sel[:, pg // pages_per_block:pg // pages_per_block + 1] > 0.5),
                   s[:, pg * rows:(pg + 1) * rows], NEG) for pg in range(n_pages)], axis=1)
    m = jnp.maximum(own, jnp.max(s, axis=1, keepdims=True))
    p = jnp.exp(s - m)
    p_own = jnp.exp(own - m)
    l = jnp.sum(p, axis=1, keepdims=True) + p_own
    v16 = jnp.concatenate([vp[pg][0, 0].astype(BF16) for pg in range(n_pages)], axis=0)
    o_ref[0] = (_dot(p.astype(BF16), v16) + p_own * vn_ref[0]) / l


def _moba_sample(q32, kn32, vn32, cache_k, cache_v, page_table):
    ns = q32.shape[0]
    page = cache_k.shape[2]
    n_pages = page_table.shape[1]
    ppb = MOBA_BLOCK // page
    head_spec = pl.BlockSpec((1, A_HEADS, A_HEAD_DIM), lambda s, pt: (s, 0, 0))

    def page_spec(p):
        return pl.BlockSpec((1, 1, page * A_HEADS, A_HEAD_DIM), lambda s, pt: (0, pt[s, p], 0, 0))

    def rows(cache):
        return cache.reshape(cache.shape[0], cache.shape[1], page * A_HEADS, A_HEAD_DIM)

    def heads(a):
        return a.reshape(ns, A_HEADS, A_HEAD_DIM)

    grid_spec = pltpu.PrefetchScalarGridSpec(
        num_scalar_prefetch=1,
        grid=(ns,),
        in_specs=[head_spec, head_spec, head_spec] + [page_spec(p) for p in range(n_pages)] * 2,
        out_specs=head_spec,
    )
    out = pl.pallas_call(
        functools.partial(_moba_sample_kernel, n_pages, ppb),
        grid_spec=grid_spec,
        out_shape=jax.ShapeDtypeStruct((ns, A_HEADS, A_HEAD_DIM), F32),
        compiler_params=_params("parallel"),
        name="moba_sample_attention",
    )(page_table, heads(q32), heads(kn32), heads(vn32), *([rows(cache_k)] * n_pages), *([rows(cache_v)] * n_pages))
    return out.reshape(ns, A_WIDTH)


def _mix_kernel(hm_ref, om_ref, ao_ref, gm_ref, ga_ref, gmh_ref, wbm_ref, wba_ref, o_ref, hs_ref):
    @pl.when(pl.program_id(1) == 0)
    def _():
        for hh in range(M_HEADS):
            sl = slice(hh * M_V, (hh + 1) * M_V)
            y = _rms(hm_ref[:, sl], gmh_ref[:, sl])
            hs_ref[:, sl] = (y * om_ref[:, sl].astype(F32)).astype(BF16)

    t1 = _dot(hs_ref[...], wbm_ref[...])
    t2 = _dot(ao_ref[...].astype(BF16), wba_ref[...])
    o_ref[...] = (gm_ref[...].astype(F32) * t1 + ga_ref[...].astype(F32) * t2).astype(o_ref.dtype)


def _mix(hm, sg, ao, g_mh, w_bm, w_ba):
    r = hm.shape[0]
    d = w_bm.shape[1]
    tm = _pick(r, (512, 256, 128, 64, 32, 16))
    tn = _pick(d, (1024, 512, 256, 128))
    assert M_WIDTH % tn == 0
    gm0 = M_WIDTH // tn
    ga0 = (M_WIDTH + d) // tn
    return pl.pallas_call(
        _mix_kernel,
        grid=(r // tm, d // tn),
        in_specs=[pl.BlockSpec((tm, M_WIDTH), lambda i, j: (i, 0)),
                  pl.BlockSpec((tm, M_WIDTH), lambda i, j: (i, 0)),
                  pl.BlockSpec((tm, A_WIDTH), lambda i, j: (i, 0)),
                  pl.BlockSpec((tm, tn), lambda i, j: (i, gm0 + j)),
                  pl.BlockSpec((tm, tn), lambda i, j: (i, ga0 + j)),
                  pl.BlockSpec((1, M_WIDTH), lambda i, j: (0, 0)),
                  pl.BlockSpec((M_WIDTH, tn), lambda i, j: (0, j)),
                  pl.BlockSpec((A_WIDTH, tn), lambda i, j: (0, j))],
        out_specs=pl.BlockSpec((tm, tn), lambda i, j: (i, j)),
        out_shape=jax.ShapeDtypeStruct((r, d), BF16),
        scratch_shapes=[pltpu.VMEM((tm, M_WIDTH), BF16)],
        compiler_params=_params("parallel", "arbitrary"),
        name="mixer_merge",
    )(hm, sg, ao, sg, sg, g_mh.reshape(1, M_WIDTH), w_bm, w_ba)


def _store_row_tiles(o_ref, x, rows):
    s = x.shape[1] // LANES
    for c in range(s):
        o_ref[pl.ds(c, rows, stride=s), :] = x[:, c * LANES:(c + 1) * LANES]


def _resid_router_kernel(x_ref, mix_ref, wout_ref, gffn_ref, wrh_ref, wrl_ref, br_ref,
                         x1_ref, x1t_ref, gate_ref, eid_ref):
    x1 = x_ref[...] + _dot(mix_ref[...], wout_ref[...])
    x1_ref[...] = x1
    _store_row_tiles(x1t_ref, x1, x1.shape[0])
    hh, hl = _split_bf16(_rms(x1, gffn_ref[...]))
    wrh = wrh_ref[...]
    logits = _dot(hh, wrh) + (_dot(hh, wrl_ref[...]) + _dot(hl, wrh)) + br_ref[...]
    lane_i = lax.broadcasted_iota(jnp.int32, logits.shape, 1)
    lane_f = lane_i.astype(F32)
    work = jnp.where(lane_i < N_EXPERTS, logits, -jnp.inf)
    vals, ids = [], []
    for _ in range(TOP_K):
        m, idx = _first_argmax(work, lane_f)
        vals.append(m)
        ids.append(idx)
        work = jnp.where(lane_f == idx, -jnp.inf, work)
    es = [jnp.exp(v - vals[0]) for v in vals]
    den = es[0]
    for e in es[1:]:
        den = den + e
    gate = jnp.zeros(logits.shape, F32)
    eid = jnp.zeros(logits.shape, F32)
    for k in range(TOP_K):
        gate = jnp.where(lane_i == k, es[k] / den, gate)
        eid = jnp.where(lane_i == k, ids[k], eid)
    gate_ref[...] = gate
    eid_ref[...] = eid.astype(jnp.int32)


def _resid_router(x, mix, w_out, g_ffn, wr_hi, wr_lo, b_r, n_total, row0, prev):
    r, d = x.shape
    s = d // LANES
    tm = _pick(r, (256, 128, 64, 32, 16))
    assert row0 % tm == 0
    b0 = row0 // tm
    out_shape = [jax.ShapeDtypeStruct((n_total, d), F32),
                 jax.ShapeDtypeStruct((n_total * s, LANES), F32),
                 jax.ShapeDtypeStruct((n_total, LANES), F32),
                 jax.ShapeDtypeStruct((n_total, LANES), jnp.int32)]
    out_specs = [pl.BlockSpec((tm, d), lambda i: (b0 + i, 0)),
                 pl.BlockSpec((tm * s, LANES), lambda i: (b0 + i, 0)),
                 pl.BlockSpec((tm, LANES), lambda i: (b0 + i, 0)),
                 pl.BlockSpec((tm, LANES), lambda i: (b0 + i, 0))]
    in_specs = [pl.BlockSpec((tm, d), lambda i: (i, 0)),
                pl.BlockSpec((tm, d), lambda i: (i, 0)),
                pl.BlockSpec((d, d), lambda i: (0, 0)),
                pl.BlockSpec((1, d), lambda i: (0, 0)),
                pl.BlockSpec((d, LANES), lambda i: (0, 0)),
                pl.BlockSpec((d, LANES), lambda i: (0, 0)),
                pl.BlockSpec((1, LANES), lambda i: (0, 0))]
    args = [x, mix, w_out, g_ffn.reshape(1, d), wr_hi, wr_lo, b_r]
    n_in = len(args)
    kern = _resid_router_kernel
    aliases = {}
    if prev is not None:
        in_specs = in_specs + [pl.BlockSpec(memory_space=pl.ANY)] * len(prev)
        args = args + list(prev)
        aliases = {n_in + k: k for k in range(len(prev))}

        def kern(*refs):
            _resid_router_kernel(*refs[:n_in], *refs[n_in + len(prev):])

    return pl.pallas_call(
        kern,
        grid=(r // tm,),
        in_specs=in_specs,
        out_specs=out_specs,
        out_shape=out_shape,
        input_output_aliases=aliases,
        compiler_params=_params("parallel"),
        name="residual_router_topk",
    )(*args)


def _row_pitch(s):
    return s + 1 if s % 2 == 0 else s


def _gathered_rows(buf, rows, s):
    return jnp.concatenate([buf[pl.ds(c, rows, stride=_row_pitch(s)), :] for c in range(s)], axis=1)


def _start_row_gathers(src_hbm, idx_ref, idx_base, idx_step, dst, rows, s, sem):
    pitch = _row_pitch(s)
    trips = rows // GATHER_UNROLL

    def trip(i, carry):
        for u in range(GATHER_UNROLL):
            r = u * trips + i
            idx = idx_ref[idx_base + r * idx_step]
            pltpu.make_async_copy(src_hbm.at[pl.ds(idx * s, s)], dst.at[pl.ds(r * pitch, s)],
                                  sem).start(priority=u % 2)
        return carry

    lax.fori_loop(0, trips, trip, 0)


def _wait_row_gathers(dst, rows, s, sem):
    landed = dst.at[pl.ds(0, rows * s)]
    pltpu.make_async_copy(landed, landed, sem).wait()


def _expert_tile_schedule(ce_ref, nx_ref, nv_ref, c, j, n_pass):
    first = (c == 0) | (ce_ref[c] != ce_ref[jnp.maximum(c - 1, 0)])
    more_in_pass = nx_ref[c] >= 0
    has_next = more_in_pass | (j + 1 < n_pass)
    next_e = jnp.where(more_in_pass, nx_ref[c], ce_ref[0])
    next_j = jnp.where(more_in_pass, j, j + 1)
    return (c < nv_ref[0]) & first, has_next, next_j, next_e


def _moe_up_kernel(f_lo, nf, gather, ce_ref, nx_ref, nv_ref, *refs):
    if gather:
        tok_ref, tok_next_ref, g_ref, bg_ref, bl_ref, xt_hbm, w_hbm, o_ref, xs_ref, gbuf, gsem, raw, w16, sem = refs
    else:
        x_ref, bg_ref, bl_ref, w_hbm, _, o_ref, raw, w16, sem = refs
    j = pl.program_id(0)
    c = pl.program_id(1)
    tf = raw.shape[2]
    n_pass = pl.num_programs(0)
    dff = nf * tf

    def fetch(jj, e, k):
        col = pl.multiple_of(k * dff + (f_lo + jj) * tf, LANES)
        return pltpu.make_async_copy(w_hbm.at[e, :, pl.ds(col, tf)], raw.at[k], sem.at[k])

    first, has_next, next_j, next_e = _expert_tile_schedule(ce_ref, nx_ref, nv_ref, c, j, n_pass)

    if gather:
        rows, d = xs_ref.shape
        s = d // LANES
        slot = c % 2

        @pl.when(c == 0)
        def _():
            _start_row_gathers(xt_hbm, tok_ref, 0, 1, gbuf.at[0], rows, s, gsem.at[0])

        @pl.when(c + 1 < nv_ref[0])
        def _():
            _start_row_gathers(xt_hbm, tok_next_ref, 0, 1, gbuf.at[1 - slot], rows, s, gsem.at[1 - slot])

    @pl.when(first & (j == 0) & (c == 0))
    def _():
        for k in range(2):
            fetch(0, ce_ref[0], k).start()

    @pl.when(first)
    def _():
        for k in range(2):
            fetch(j, ce_ref[c], k).wait()
            w16[k] = raw[k].astype(BF16)

    @pl.when(first & has_next)
    def _():
        for k in range(2):
            fetch(next_j, next_e, k).start()

    @pl.when(c < nv_ref[0])
    def _():
        if gather:
            _wait_row_gathers(gbuf.at[slot], rows, s, gsem.at[slot])
            x = _rms(_gathered_rows(gbuf.at[slot], rows, s), g_ref[...]).astype(BF16)
            xs_ref[...] = x
        else:
            x = x_ref[...]
        gate = jnp.minimum(_dot(x, w16[0]) + bg_ref[0], SWIGLU_LIMIT)
        lin = jnp.clip(_dot(x, w16[1]) + bl_ref[0], -SWIGLU_LIMIT, SWIGLU_LIMIT)
        o_ref[...] = (gate * jax.nn.sigmoid(SWIGLU_ALPHA * gate) * (lin + 1.0)).astype(o_ref.dtype)


def _moe_up(x1t, g_ffn, src_tok, w_up, b_up, chunk_expert, next_expert, n_valid, n_chunks):
    rows = MOE_ROWS
    e, d, f2 = w_up.shape
    s = d // LANES
    dff = f2 // 2
    tf = _pick(dff, (1024, 512, 256, 128))
    nf = dff // tf
    assert rows % GATHER_UNROLL == 0
    b3 = b_up.reshape(e, 1, f2)
    weight_scratch = [pltpu.VMEM((2, d, tf), F32), pltpu.VMEM((2, d, tf), BF16), pltpu.SemaphoreType.DMA((2,))]
    act_shape = jax.ShapeDtypeStruct((n_chunks * rows, dff), BF16)

    def last(c, nv):
        return jnp.minimum(c, nv[0] - 1)

    def nxt(c, nv):
        return jnp.minimum(c + 1, nv[0] - 1)

    def bias_spec(off):
        return pl.BlockSpec((1, 1, tf), lambda j, c, ce, nx, nv: (ce[c], 0, off + j))

    def act_spec(f_lo):
        return pl.BlockSpec((rows, tf), lambda j, c, ce, nx, nv: (last(c, nv), f_lo + j))

    first_spec = pltpu.PrefetchScalarGridSpec(
        num_scalar_prefetch=3,
        grid=(1, n_chunks),
        in_specs=[pl.BlockSpec((rows,), lambda j, c, ce, nx, nv: (last(c, nv),), memory_space=pltpu.SMEM),
                  pl.BlockSpec((rows,), lambda j, c, ce, nx, nv: (nxt(c, nv),), memory_space=pltpu.SMEM),
                  pl.BlockSpec((1, d), lambda j, c, ce, nx, nv: (0, 0)),
                  bias_spec(0), bias_spec(nf),
                  pl.BlockSpec(memory_space=pl.ANY),
                  pl.BlockSpec(memory_space=pl.ANY)],
        out_specs=[act_spec(0),
                   pl.BlockSpec((rows, d), lambda j, c, ce, nx, nv: (last(c, nv), 0))],
        scratch_shapes=[pltpu.VMEM((2, rows * _row_pitch(s), LANES), F32), pltpu.SemaphoreType.DMA((2,))]
        + weight_scratch,
    )
    act, xs = pl.pallas_call(
        functools.partial(_moe_up_kernel, 0, nf, True),
        grid_spec=first_spec,
        out_shape=[act_shape, jax.ShapeDtypeStruct((n_chunks * rows, d), BF16)],
        compiler_params=_params("arbitrary", "arbitrary"),
        name="moe_gather_up_swiglu",
    )(chunk_expert, next_expert, n_valid, src_tok, src_tok, g_ffn.reshape(1, d), b3, b3, x1t, w_up)
    if nf == 1:
        return act
    rest_spec = pltpu.PrefetchScalarGridSpec(
        num_scalar_prefetch=3,
        grid=(nf - 1, n_chunks),
        in_specs=[pl.BlockSpec((rows, d), lambda j, c, ce, nx, nv: (last(c, nv), 0)),
                  bias_spec(1), bias_spec(nf + 1),
                  pl.BlockSpec(memory_space=pl.ANY),
                  pl.BlockSpec(memory_space=pl.ANY)],
        out_specs=act_spec(1),
        scratch_shapes=weight_scratch,
    )
    return pl.pallas_call(
        functools.partial(_moe_up_kernel, 1, nf, False),
        grid_spec=rest_spec,
        out_shape=act_shape,
        input_output_aliases={7: 0},
        compiler_params=_params("arbitrary", "arbitrary"),
        name="moe_up_swiglu",
    )(chunk_expert, next_expert, n_valid, xs, b3, b3, w_up, act)


def _moe_down_kernel(ce_ref, nx_ref, nv_ref, a_ref, b_ref, w_hbm, o_ref, raw, w16, sem):
    c = pl.program_id(0)

    def fetch(e):
        return pltpu.make_async_copy(w_hbm.at[e], raw, sem)

    first, has_next, _, next_e = _expert_tile_schedule(ce_ref, nx_ref, nv_ref, c, 0, 1)

    @pl.when(first & (c == 0))
    def _():
        fetch(ce_ref[0]).start()

    @pl.when(first)
    def _():
        fetch(ce_ref[c]).wait()
        w16[...] = raw[...].astype(BF16)

    @pl.when(first & has_next)
    def _():
        fetch(next_e).start()

    @pl.when(c < nv_ref[0])
    def _():
        y = _dot(a_ref[...], w16[...]) + b_ref[0]
        _store_row_tiles(o_ref, y, y.shape[0])


def _moe_down(act, w_down, b_down, chunk_expert, next_expert, n_valid):
    rows = MOE_ROWS
    n_chunks = act.shape[0] // rows
    e, dff, d = w_down.shape
    s = d // LANES

    def last(c, nv):
        return jnp.minimum(c, nv[0] - 1)

    grid_spec = pltpu.PrefetchScalarGridSpec(
        num_scalar_prefetch=3,
        grid=(n_chunks,),
        in_specs=[pl.BlockSpec((rows, dff), lambda c, ce, nx, nv: (last(c, nv), 0)),
                  pl.BlockSpec((1, 1, d), lambda c, ce, nx, nv: (ce[c], 0, 0)),
                  pl.BlockSpec(memory_space=pl.ANY)],
        out_specs=pl.BlockSpec((rows * s, LANES), lambda c, ce, nx, nv: (last(c, nv), 0)),
        scratch_shapes=[pltpu.VMEM((dff, d), F32), pltpu.VMEM((dff, d), BF16),
                        pltpu.SemaphoreType.DMA(())],
    )
    return pl.pallas_call(
        _moe_down_kernel,
        grid_spec=grid_spec,
        out_shape=jax.ShapeDtypeStruct((n_chunks * rows * s, LANES), F32),
        compiler_params=_params("arbitrary"),
        name="moe_down",
    )(chunk_expert, next_expert, n_valid, act, b_down.reshape(e, 1, d), w_down)


def _combine_kernel(dest_ref, dest_next_ref, x1_ref, gate_ref, gple_ref, yt_hbm, x2_ref, h3_ref, buf, sem):
    i = pl.program_id(0)
    rows = x1_ref.shape[0]
    s = x1_ref.shape[1] // LANES
    slot = i % 2

    def start(idx_ref, to):
        for k in range(TOP_K):
            _start_row_gathers(yt_hbm, idx_ref, k, TOP_K, buf.at[to, k], rows, s, sem.at[to])

    @pl.when(i == 0)
    def _():
        start(dest_ref, 0)

    @pl.when(i + 1 < pl.num_programs(0))
    def _():
        start(dest_next_ref, 1 - slot)

    for k in range(TOP_K):
        _wait_row_gathers(buf.at[slot, k], rows, s, sem.at[slot])
    gate = gate_ref[...]
    acc = x1_ref[...]
    for k in range(TOP_K):
        acc = acc + gate[:, k:k + 1] * _gathered_rows(buf.at[slot, k], rows, s)
    x2_ref[...] = acc
    h3_ref[...] = _rms(acc, gple_ref[...]).astype(h3_ref.dtype)


def _combine(x1, gate, dest, yt, g_ple):
    n, d = x1.shape
    s = d // LANES
    rows = _pick(n, (COMBINE_ROWS, 64, 32, 16))
    steps = n // rows
    assert rows % GATHER_UNROLL == 0
    return pl.pallas_call(
        _combine_kernel,
        grid=(steps,),
        in_specs=[pl.BlockSpec((rows * TOP_K,), lambda i: (i,), memory_space=pltpu.SMEM),
                  pl.BlockSpec((rows * TOP_K,), lambda i: (jnp.minimum(i + 1, steps - 1),),
                               memory_space=pltpu.SMEM),
                  pl.BlockSpec((rows, d), lambda i: (i, 0)),
                  pl.BlockSpec((rows, LANES), lambda i: (i, 0)),
                  pl.BlockSpec((1, d), lambda i: (0, 0)),
                  pl.BlockSpec(memory_space=pl.ANY)],
        out_specs=[pl.BlockSpec((rows, d), lambda i: (i, 0)),
                   pl.BlockSpec((rows, d), lambda i: (i, 0))],
        out_shape=[jax.ShapeDtypeStruct((n, d), F32), jax.ShapeDtypeStruct((n, d), BF16)],
        scratch_shapes=[pltpu.VMEM((2, TOP_K, rows * _row_pitch(s), LANES), F32),
                        pltpu.SemaphoreType.DMA((2,))],
        compiler_params=_params("arbitrary"),
        name="moe_combine_gather",
    )(dest, dest, x1, gate, g_ple.reshape(1, d), yt)


def _ple_kernel(x2_ref, h3_ref, p_ref, wpg_ref, wple_ref, o_ref):
    gate = jax.nn.sigmoid(_dot(h3_ref[...], wpg_ref[...]))
    o_ref[...] = x2_ref[...] + gate * _dot(p_ref[...].astype(BF16), wple_ref[...])


def _ple(x2, h3, p, w_pg, w_ple, row0, r):
    d = x2.shape[1]
    pd = p.shape[1]
    tm = _pick(r, (512, 256, 128, 64, 32, 16))
    tn = _pick(d, (1024, 512, 256, 128))
    assert row0 % tm == 0
    b0 = row0 // tm
    return pl.pallas_call(
        _ple_kernel,
        grid=(d // tn, r // tm),
        in_specs=[pl.BlockSpec((tm, tn), lambda j, i: (b0 + i, j)),
                  pl.BlockSpec((tm, d), lambda j, i: (b0 + i, 0)),
                  pl.BlockSpec((tm, pd), lambda j, i: (i, 0)),
                  pl.BlockSpec((d, tn), lambda j, i: (0, j)),
                  pl.BlockSpec((pd, tn), lambda j, i: (0, j))],
        out_specs=pl.BlockSpec((tm, tn), lambda j, i: (i, j)),
        out_shape=jax.ShapeDtypeStruct((r, d), F32),
        compiler_params=_params("parallel", "parallel"),
        name="ple_gate",
    )(x2, h3, p, w_pg, w_ple)


def _rope_tables(pos):
    half = ROT_DIM // 2
    inv = jnp.power(ROPE_THETA, -jnp.arange(half, dtype=F32) * 2.0 / ROT_DIM)
    ang = pos.astype(F32)[:, None] * inv[None, :]
    cos, sin = jnp.cos(ang), jnp.sin(ang)
    r = pos.shape[0]
    rest = A_HEAD_DIM - ROT_DIM
    c = jnp.concatenate([cos, cos, jnp.ones((r, rest), F32)], axis=1)
    s1 = jnp.concatenate([-sin, jnp.zeros((r, half + rest), F32)], axis=1)
    s2 = jnp.concatenate([jnp.zeros((r, half), F32), sin, jnp.zeros((r, rest), F32)], axis=1)
    return c, s1, s2


def _mixer_inputs(x, pos, g_attn, w, b_i, b_f, g_q, g_k, mlstm_dtype):
    d = x.shape[1]
    kq = M_HEADS * M_QK
    o_ig = 2 * kq + M_WIDTH
    o_om = o_ig + 2 * M_HEADS
    o_qa = o_om + M_WIDTH
    o_gm = o_qa + 3 * A_WIDTH
    h = _rmsnorm_bf16(x, g_attn)
    colscale = jnp.concatenate([jnp.full((1, kq), M_QK ** -0.5, F32), jnp.ones((1, kq + M_WIDTH), F32)], axis=1)
    qkv_m = _proj_scale(h, w[:, :o_ig].astype(BF16), colscale, mlstm_dtype)
    wg = w[:, o_ig:o_om]
    wc = jnp.pad(wg, ((0, 0), (0, LANES - 2 * M_HEADS))).astype(BF16)
    bias = jnp.concatenate([b_i, b_f]).astype(F32)
    bc = jnp.pad(bias, (0, LANES - 2 * M_HEADS)).reshape(1, LANES)
    gcol, grow = _proj_gates(h, wc, wg.T.astype(BF16), bc, bias.reshape(2 * M_HEADS, 1))
    sg = _proj_sigmoid(h, jnp.concatenate([w[:, o_om:o_qa], w[:, o_gm:]], axis=1).astype(BF16))
    tabs = _rope_tables(pos)
    q32, _ = _proj_qk(h, w[:, o_qa:o_qa + A_WIDTH].astype(BF16), g_q, *tabs)
    k32, k16 = _proj_qk(h, w[:, o_qa + A_WIDTH:o_qa + 2 * A_WIDTH].astype(BF16), g_k, *tabs)
    v32, v16 = _proj_plain2(h, w[:, o_qa + 2 * A_WIDTH:o_gm].astype(BF16))
    return qkv_m, gcol, grow, sg, q32, k32, k16, v32, v16


def _moe_plan(eid, n_chunks):
    n = eid.shape[0]
    e_flat = eid[:, :TOP_K].reshape(-1)
    onehot = (e_flat[:, None] == jnp.arange(N_EXPERTS, dtype=jnp.int32)[None, :]).astype(jnp.int32)
    before = jnp.cumsum(onehot, axis=0) - onehot
    rank = jnp.sum(before * onehot, axis=1)
    counts = jnp.sum(onehot, axis=0)
    padded = (counts + MOE_ROWS - 1) // MOE_ROWS * MOE_ROWS
    p_ends = jnp.cumsum(padded)
    p_starts = p_ends - padded
    dest = (p_starts[e_flat] + rank).astype(jnp.int32)
    tok = jnp.repeat(jnp.arange(n, dtype=jnp.int32), TOP_K)
    src_tok = jnp.zeros((n_chunks * MOE_ROWS,), jnp.int32).at[dest].set(tok)
    n_valid = (p_ends[-1] // MOE_ROWS).astype(jnp.int32).reshape(1)
    chunk0 = jnp.minimum(jnp.arange(n_chunks, dtype=jnp.int32), n_valid[0] - 1) * MOE_ROWS
    chunk_expert = jnp.minimum(jnp.sum((p_ends[None, :] <= chunk0[:, None]).astype(jnp.int32), axis=1),
                               N_EXPERTS - 1)
    ids = jnp.arange(N_EXPERTS, dtype=jnp.int32)
    later = (ids[None, :] > chunk_expert[:, None]) & (counts[None, :] > 0)
    next_expert = jnp.min(jnp.where(later, ids[None, :], N_EXPERTS), axis=1)
    next_expert = jnp.where(next_expert < N_EXPERTS, next_expert, -1).astype(jnp.int32)
    return dest, src_tok, n_valid, chunk_expert, next_expert


def kernel(x_prompt, x_sample, cache_k, cache_v, state_mlstm_C, state_mlstm_n, state_mlstm_m, page_table, p_prompt, p_sample, g_attn, w_in, b_i, b_f, g_q, g_k, g_mh, w_bm, w_ba, w_out, g_ffn, w_router, b_router, w_up, b_up, w_down, b_down, g_ple, w_pg, w_ple):
    depth = w_in.shape[0]
    bsz, t, d = x_prompt.shape
    ns, dec_seq, _ = x_sample.shape
    assert depth == 1 and bsz == 1 and dec_seq == 1
    page = cache_k.shape[2]
    past = page_table.shape[1] * page
    assert t % MOBA_BLOCK == 0 and past % MOBA_BLOCK == 0 and MOBA_BLOCK % page == 0
    assert t % (M_CHUNK * MLSTM_CHUNKS_PER_STEP) == 0 and ns % MLSTM_SAMPLE_SEQS == 0
    n = t + ns

    xp = x_prompt[0]
    xs = x_sample[:, 0]
    w = w_in[0]
    fw = (g_attn[0], w, b_i[0], b_f[0], g_q[0], g_k[0])

    qkv_p, gcol_p, grow_p, sg_p, q32_p, k32_p, k16_p, v32_p, v16_p = _mixer_inputs(
        xp, jnp.arange(t, dtype=jnp.int32), *fw, BF16)
    hm_p, ct_p, n_p, m_p = _mlstm_prompt(qkv_p, gcol_p, grow_p)
    ao_p = _moba_prompt(q32_p, k16_p, v16_p, _block_kmean(k32_p))

    qkv_s, gcol_s, _, sg_s, q32_s, k32_s, _, v32_s, _ = _mixer_inputs(
        xs, jnp.full((ns,), past, jnp.int32), *fw, F32)
    hm_s, c_s, n_s, m_s = _mlstm_sample(qkv_s, gcol_s, state_mlstm_C[0].astype(F32),
                                        state_mlstm_n[0].astype(F32), state_mlstm_m[0].astype(F32))
    ao_s = _moba_sample(q32_s, k32_s, v32_s, cache_k, cache_v, page_table)

    wbm, wba, wout = w_bm[0].astype(BF16), w_ba[0].astype(BF16), w_out[0].astype(BF16)
    mix_p = _mix(hm_p, sg_p, ao_p, g_mh[0], wbm, wba)
    mix_s = _mix(hm_s, sg_s, ao_s, g_mh[0], wbm, wba)
    wr = jnp.pad(w_router[0], ((0, 0), (0, LANES - N_EXPERTS)))
    wr_hi = wr.astype(BF16)
    wr_lo = (wr - wr_hi.astype(F32)).astype(BF16)
    b_r = jnp.pad(b_router[0].astype(F32), (0, LANES - N_EXPERTS)).reshape(1, LANES)
    routed = _resid_router(xp, mix_p, wout, g_ffn[0], wr_hi, wr_lo, b_r, n, 0, None)
    x1, x1t, gate, eid = _resid_router(xs, mix_s, wout, g_ffn[0], wr_hi, wr_lo, b_r, n, t, routed)

    n_slots = n * TOP_K
    n_chunks = (n_slots + N_EXPERTS * (MOE_ROWS - 1)) // MOE_ROWS
    dest, src_tok, n_valid, chunk_expert, next_expert = _moe_plan(eid, n_chunks)
    act = _moe_up(x1t, g_ffn[0], src_tok, w_up[0], b_up[0], chunk_expert, next_expert, n_valid, n_chunks)
    yt = _moe_down(act, w_down[0], b_down[0], chunk_expert, next_expert, n_valid)
    x2, h3 = _combine(x1, gate, dest, yt, g_ple[0])

    wpg, wple = w_pg[0].astype(BF16), w_ple[0].astype(BF16)
    y_p = _ple(x2, h3, p_prompt[0, 0], wpg, wple, 0, t)
    y_s = _ple(x2, h3, p_sample[0, :, 0], wpg, wple, t, ns)

    def heads(a):
        return a.reshape(a.shape[0], A_HEADS, A_HEAD_DIM)

    return (y_p[None], y_s[:, None],
            heads(k32_p)[None, None], heads(v32_p)[None, None],
            jnp.swapaxes(ct_p, 1, 2)[None, None].astype(state_mlstm_C.dtype),
            n_p[:M_HEADS][None, None].astype(state_mlstm_n.dtype),
            m_p[:M_HEADS, 0][None, None].astype(state_mlstm_m.dtype),
            heads(k32_s)[None, :, None], heads(v32_s)[None, :, None],
            c_s[None].astype(state_mlstm_C.dtype), n_s[None].astype(state_mlstm_n.dtype),
            m_s[None].astype(state_mlstm_m.dtype))
```

```python
import functools

import jax
import jax.numpy as jnp
import numpy as np
from jax import lax
from jax.experimental import pallas as pl
from jax.experimental.pallas import tpu as pltpu

F32 = jnp.float32
BF16 = jnp.bfloat16

M_HEADS = 4
M_QK = 128
M_V = 256
M_CHUNK = 64
M_WIDTH = M_HEADS * M_V
A_HEADS = 8
A_HEAD_DIM = 128
A_WIDTH = A_HEADS * A_HEAD_DIM
MOBA_BLOCK = 256
MOBA_TOPK = 3
ROT_DIM = A_HEAD_DIM // 4
ROPE_THETA = 500000.0
N_EXPERTS = 32
TOP_K = 4
SWIGLU_LIMIT = 7.0
SWIGLU_ALPHA = 1.702
EPS = 1e-6
NEG = -1e30

LANES = 128
SUBLANES = 8
VMEM_LIMIT_BYTES = 56 * 1024 * 1024

MOE_ROWS = 256
MLSTM_CHUNKS_PER_STEP = 4
MLSTM_SAMPLE_SEQS = 8
COMBINE_ROWS = 128
MOBA_HEADS_PER_STEP = 8
MOBA_BLOCKS_PER_STEP = 4
GATHER_UNROLL = 8


def _pick(n, candidates):
    for c in candidates:
        if n % c == 0:
            return c
    raise ValueError(f"no tile in {candidates} divides {n}")


def _params(*sem):
    return pltpu.CompilerParams(dimension_semantics=sem, vmem_limit_bytes=VMEM_LIMIT_BYTES)


def _dot(a, b):
    return jnp.dot(a, b, preferred_element_type=F32)


def _dot_nt(a, b):
    return lax.dot_general(a, b, (((1,), (1,)), ((), ())), preferred_element_type=F32)


def _dot_tn(a, b):
    return lax.dot_general(a, b, (((0,), (0,)), ((), ())), preferred_element_type=F32)


def _split_bf16(x):
    hi = x.astype(BF16)
    lo = (x - hi.astype(F32)).astype(BF16)
    return hi, lo


def _rms(x, g):
    return x * lax.rsqrt(jnp.mean(x * x, axis=-1, keepdims=True) + EPS) * g


def _log_sigmoid(x):
    return -(jnp.maximum(-x, 0.0) + jnp.log1p(jnp.exp(-jnp.abs(x))))


def _first_argmax(work, lane_f):
    m = jnp.max(work, axis=-1, keepdims=True)
    idx = jnp.min(jnp.where(work == m, lane_f, float(4 * LANES)), axis=-1, keepdims=True)
    return m, idx


def _rmsnorm_kernel(x_ref, g_ref, o_ref):
    o_ref[...] = _rms(x_ref[...], g_ref[...]).astype(o_ref.dtype)


def _rmsnorm_bf16(x, g):
    r, d = x.shape
    tm = _pick(r, (512, 256, 128, 64, 32, 16))
    return pl.pallas_call(
        _rmsnorm_kernel,
        grid=(r // tm,),
        in_specs=[pl.BlockSpec((tm, d), lambda i: (i, 0)), pl.BlockSpec((1, d), lambda i: (0, 0))],
        out_specs=pl.BlockSpec((tm, d), lambda i: (i, 0)),
        out_shape=jax.ShapeDtypeStruct((r, d), BF16),
        compiler_params=_params("parallel"),
        name="rmsnorm_rows",
    )(x, g.reshape(1, d))


def _proj_scale_kernel(h_ref, w_ref, s_ref, o_ref):
    o_ref[...] = (_dot(h_ref[...], w_ref[...]) * s_ref[...]).astype(o_ref.dtype)


def _proj_sigmoid_kernel(h_ref, w_ref, o_ref):
    o_ref[...] = jax.nn.sigmoid(_dot(h_ref[...], w_ref[...])).astype(o_ref.dtype)


def _proj_plain2_kernel(h_ref, w_ref, o32_ref, o16_ref):
    acc = _dot(h_ref[...], w_ref[...])
    o32_ref[...] = acc
    o16_ref[...] = acc.astype(BF16)


def _proj_qk_kernel(h_ref, w_ref, g_ref, c_ref, s1_ref, s2_ref, o32_ref, o16_ref):
    acc = _dot(h_ref[...], w_ref[...])
    g = g_ref[...]
    c, s1, s2 = c_ref[...], s1_ref[...], s2_ref[...]
    for hh in range(acc.shape[1] // A_HEAD_DIM):
        sl = slice(hh * A_HEAD_DIM, (hh + 1) * A_HEAD_DIM)
        y = _rms(acc[:, sl], g)
        up = pltpu.roll(y, A_HEAD_DIM - ROT_DIM // 2, 1)
        dn = pltpu.roll(y, ROT_DIM // 2, 1)
        r = y * c + up * s1 + dn * s2
        o32_ref[:, sl] = r
        o16_ref[:, sl] = r.astype(BF16)


def _proj_gates_kernel(h_ref, wc_ref, wr_ref, bc_ref, br_ref, gc_ref, gr_ref):
    h = h_ref[...]
    zc = _dot(h, wc_ref[...]) + bc_ref[...]
    zr = _dot_nt(wr_ref[...], h) + br_ref[...]
    lane = lax.broadcasted_iota(jnp.int32, zc.shape, 1)
    gc_ref[...] = jnp.where((lane >= M_HEADS) & (lane < 2 * M_HEADS), _log_sigmoid(zc), zc)
    row = lax.broadcasted_iota(jnp.int32, zr.shape, 0)
    gr_ref[...] = jnp.where(row >= M_HEADS, _log_sigmoid(zr), zr)


def _proj_tiles(r, n):
    tm = _pick(r, (1024, 512, 256, 128, 64, 32, 16))
    tn = _pick(n, (1024, 512, 256, 128))
    return tm, tn


def _proj_scale(h, w, colscale, out_dtype):
    r, d = h.shape
    n = w.shape[1]
    tm, tn = _proj_tiles(r, n)
    return pl.pallas_call(
        _proj_scale_kernel,
        grid=(n // tn, r // tm),
        in_specs=[pl.BlockSpec((tm, d), lambda j, i: (i, 0)),
                  pl.BlockSpec((d, tn), lambda j, i: (0, j)),
                  pl.BlockSpec((1, tn), lambda j, i: (0, j))],
        out_specs=pl.BlockSpec((tm, tn), lambda j, i: (i, j)),
        out_shape=jax.ShapeDtypeStruct((r, n), out_dtype),
        compiler_params=_params("parallel", "parallel"),
        name="proj_scale",
    )(h, w, colscale)


def _proj_sigmoid(h, w):
    r, d = h.shape
    n = w.shape[1]
    tm, tn = _proj_tiles(r, n)
    return pl.pallas_call(
        _proj_sigmoid_kernel,
        grid=(n // tn, r // tm),
        in_specs=[pl.BlockSpec((tm, d), lambda j, i: (i, 0)),
                  pl.BlockSpec((d, tn), lambda j, i: (0, j))],
        out_specs=pl.BlockSpec((tm, tn), lambda j, i: (i, j)),
        out_shape=jax.ShapeDtypeStruct((r, n), BF16),
        compiler_params=_params("parallel", "parallel"),
        name="proj_sigmoid",
    )(h, w)


def _proj_plain2(h, w):
    r, d = h.shape
    n = w.shape[1]
    tm, tn = _proj_tiles(r, n)
    return pl.pallas_call(
        _proj_plain2_kernel,
        grid=(n // tn, r // tm),
        in_specs=[pl.BlockSpec((tm, d), lambda j, i: (i, 0)),
                  pl.BlockSpec((d, tn), lambda j, i: (0, j))],
        out_specs=[pl.BlockSpec((tm, tn), lambda j, i: (i, j)),
                   pl.BlockSpec((tm, tn), lambda j, i: (i, j))],
        out_shape=[jax.ShapeDtypeStruct((r, n), F32), jax.ShapeDtypeStruct((r, n), BF16)],
        compiler_params=_params("parallel", "parallel"),
        name="proj_plain",
    )(h, w)


def _proj_qk(h, w, g, rope_c, rope_s1, rope_s2):
    r, d = h.shape
    n = w.shape[1]
    tm, tn = _proj_tiles(r, n)
    hd = A_HEAD_DIM
    return pl.pallas_call(
        _proj_qk_kernel,
        grid=(n // tn, r // tm),
        in_specs=[pl.BlockSpec((tm, d), lambda j, i: (i, 0)),
                  pl.BlockSpec((d, tn), lambda j, i: (0, j)),
                  pl.BlockSpec((1, hd), lambda j, i: (0, 0)),
                  pl.BlockSpec((tm, hd), lambda j, i: (i, 0)),
                  pl.BlockSpec((tm, hd), lambda j, i: (i, 0)),
                  pl.BlockSpec((tm, hd), lambda j, i: (i, 0))],
        out_specs=[pl.BlockSpec((tm, tn), lambda j, i: (i, j)),
                   pl.BlockSpec((tm, tn), lambda j, i: (i, j))],
        out_shape=[jax.ShapeDtypeStruct((r, n), F32), jax.ShapeDtypeStruct((r, n), BF16)],
        compiler_params=_params("parallel", "parallel"),
        name="proj_qk_norm_rope",
    )(h, w, g.reshape(1, hd), rope_c, rope_s1, rope_s2)


def _proj_gates(h, wc, wr, bc, br):
    r, d = h.shape
    tm = _pick(r, (1024, 512, 256, 128)) if r % LANES == 0 else r
    g2 = 2 * M_HEADS
    return pl.pallas_call(
        _proj_gates_kernel,
        grid=(r // tm,),
        in_specs=[pl.BlockSpec((tm, d), lambda i: (i, 0)),
                  pl.BlockSpec((d, LANES), lambda i: (0, 0)),
                  pl.BlockSpec((g2, d), lambda i: (0, 0)),
                  pl.BlockSpec((1, LANES), lambda i: (0, 0)),
                  pl.BlockSpec((g2, 1), lambda i: (0, 0))],
        out_specs=[pl.BlockSpec((tm, LANES), lambda i: (i, 0)),
                   pl.BlockSpec((g2, tm), lambda i: (0, i))],
        out_shape=[jax.ShapeDtypeStruct((r, LANES), F32), jax.ShapeDtypeStruct((g2, r), F32)],
        compiler_params=_params("parallel"),
        name="proj_gates",
    )(h, wc, wr, bc, br)


def _mlstm_prompt_kernel(q_ref, k_ref, v_ref, gc_ref, gr_ref, h_ref, ct_out, n_out, m_out,
                         ct_s, n_s, m_s):
    step = pl.program_id(0)

    @pl.when(step == 0)
    def _():
        ct_s[...] = jnp.zeros_like(ct_s)
        n_s[...] = jnp.zeros_like(n_s)
        m_s[...] = jnp.zeros_like(m_s)

    ln = M_CHUNK
    row = lax.broadcasted_iota(jnp.int32, (ln, ln), 0)
    col = lax.broadcasted_iota(jnp.int32, (ln, ln), 1)
    causal = col <= row
    upto = row <= col
    pre = {}
    for cc in range(MLSTM_CHUNKS_PER_STEP):
        rs = slice(cc * ln, (cc + 1) * ln)
        for hh in range(M_HEADS):
            ig_c = gc_ref[rs, hh:hh + 1]
            lf_c = gc_ref[rs, M_HEADS + hh:M_HEADS + hh + 1]
            ig_r = gr_ref[hh:hh + 1, rs]
            lf_r = gr_ref[M_HEADS + hh:M_HEADS + hh + 1, rs]
            bcum_c = jnp.sum(jnp.where(causal, lf_r, 0.0), axis=1, keepdims=True)
            bcum_r = jnp.sum(jnp.where(upto, lf_c, 0.0), axis=0, keepdims=True)
            dmat = jnp.where(causal, bcum_c - bcum_r + ig_r, NEG)
            qh = q_ref[rs, hh * M_QK:(hh + 1) * M_QK]
            kh = k_ref[rs, hh * M_QK:(hh + 1) * M_QK]
            b_last = bcum_c[ln - 1:ln, :]
            pre[cc, hh] = dict(
                bcum_c=bcum_c, dmat=dmat, dmax=jnp.max(dmat, axis=1, keepdims=True),
                qk=_dot_nt(qh, kh), b_last=b_last,
                d_last_c=b_last - bcum_c + ig_c,
                d_last_max=jnp.max(b_last - bcum_r + ig_r, axis=1, keepdims=True))

    for hh in range(M_HEADS):
        ct = ct_s[hh]
        n_row = n_s[hh:hh + 1, :]
        m_prev = m_s[hh:hh + 1, 0:1]
        for cc in range(MLSTM_CHUNKS_PER_STEP):
            rs = slice(cc * ln, (cc + 1) * ln)
            u = pre[cc, hh]
            qh = q_ref[rs, hh * M_QK:(hh + 1) * M_QK]
            kh = k_ref[rs, hh * M_QK:(hh + 1) * M_QK]
            vh = v_ref[rs, hh * M_V:(hh + 1) * M_V]
            a = u["bcum_c"] + m_prev
            m_t = jnp.maximum(a, u["dmax"])
            w_inter = jnp.exp(a - m_t)
            s = u["qk"] * jnp.exp(u["dmat"] - m_t)
            num = w_inter * _dot(qh, ct.astype(BF16)) + _dot(s.astype(BF16), vh)
            den = (w_inter * jnp.sum(qh.astype(F32) * n_row, axis=1, keepdims=True)
                   + jnp.sum(s, axis=1, keepdims=True))
            h_ref[rs, hh * M_V:(hh + 1) * M_V] = num / jnp.maximum(jnp.abs(den), jnp.exp(-m_t))
            a_last = u["b_last"] + m_prev
            m_new = jnp.maximum(a_last, u["d_last_max"])
            w_c = jnp.exp(a_last - m_new)
            w_j = jnp.exp(u["d_last_c"] - m_new)
            vw = (vh.astype(F32) * w_j).astype(BF16)
            ct = w_c * ct + _dot_tn(kh, vw)
            n_row = w_c * n_row + jnp.sum(kh.astype(F32) * w_j, axis=0, keepdims=True)
            m_prev = m_new
        ct_s[hh] = ct
        n_s[hh:hh + 1, :] = n_row
        m_s[hh:hh + 1, :] = jnp.broadcast_to(m_prev, (1, LANES))

    @pl.when(step == pl.num_programs(0) - 1)
    def _():
        ct_out[...] = ct_s[...]
        n_out[...] = n_s[...]
        m_out[...] = m_s[...]


def _mlstm_prompt(qkv, gcol, grow):
    t = qkv.shape[0]
    rows = M_CHUNK * MLSTM_CHUNKS_PER_STEP
    kq = M_HEADS * M_QK
    return pl.pallas_call(
        _mlstm_prompt_kernel,
        grid=(t // rows,),
        in_specs=[pl.BlockSpec((rows, kq), lambda c: (c, 0)),
                  pl.BlockSpec((rows, kq), lambda c: (c, 1)),
                  pl.BlockSpec((rows, M_WIDTH), lambda c: (c, (2 * kq) // M_WIDTH)),
                  pl.BlockSpec((rows, LANES), lambda c: (c, 0)),
                  pl.BlockSpec((2 * M_HEADS, rows), lambda c: (0, c))],
        out_specs=[pl.BlockSpec((rows, M_WIDTH), lambda c: (c, 0)),
                   pl.BlockSpec((M_HEADS, M_QK, M_V), lambda c: (0, 0, 0)),
                   pl.BlockSpec((SUBLANES, LANES), lambda c: (0, 0)),
                   pl.BlockSpec((SUBLANES, LANES), lambda c: (0, 0))],
        out_shape=[jax.ShapeDtypeStruct((t, M_WIDTH), F32),
                   jax.ShapeDtypeStruct((M_HEADS, M_QK, M_V), F32),
                   jax.ShapeDtypeStruct((SUBLANES, LANES), F32),
                   jax.ShapeDtypeStruct((SUBLANES, LANES), F32)],
        scratch_shapes=[pltpu.VMEM((M_HEADS, M_QK, M_V), F32),
                        pltpu.VMEM((SUBLANES, LANES), F32),
                        pltpu.VMEM((SUBLANES, LANES), F32)],
        compiler_params=_params("arbitrary"),
        name="mlstm_prompt_scan",
    )(qkv, qkv, qkv, gcol, grow)


def _mlstm_sample_kernel(qkv_ref, gc_ref, c_ref, n_ref, m_ref, h_ref, co_ref, no_ref, mo_ref):
    kq = M_HEADS * M_QK
    row8 = lax.broadcasted_iota(jnp.int32, (SUBLANES, 1), 0)
    mrow = lax.broadcasted_iota(jnp.int32, mo_ref.shape, 0)
    mcol = lax.broadcasted_iota(jnp.int32, mo_ref.shape, 1)
    m_all = m_ref[...]
    for s in range(MLSTM_SAMPLE_SEQS):
        for hh in range(M_HEADS):
            q = qkv_ref[s:s + 1, hh * M_QK:(hh + 1) * M_QK]
            k = qkv_ref[s:s + 1, kq + hh * M_QK:kq + (hh + 1) * M_QK]
            v = qkv_ref[s:s + 1, 2 * kq + hh * M_V:2 * kq + (hh + 1) * M_V]
            ig = gc_ref[s:s + 1, hh:hh + 1]
            lf = gc_ref[s:s + 1, M_HEADS + hh:M_HEADS + hh + 1]
            m_prev = m_ref[s:s + 1, hh:hh + 1]
            c = c_ref[s, hh]
            n_row = n_ref[s, hh:hh + 1, :]
            a = lf + m_prev
            m_t = jnp.maximum(a, ig)
            w_c = jnp.exp(a - m_t)
            w_j = jnp.exp(ig - m_t)
            sc = jnp.sum(q * k, axis=1, keepdims=True) * w_j
            q8 = jnp.broadcast_to(q, (SUBLANES, M_QK)).astype(BF16)
            cq = _dot_nt(q8, c.astype(BF16))[0:1, :]
            num = w_c * cq + sc * v
            den = w_c * jnp.sum(n_row * q, axis=1, keepdims=True) + sc
            h_ref[s:s + 1, hh * M_V:(hh + 1) * M_V] = num / jnp.maximum(jnp.abs(den), jnp.exp(-m_t))
            vw8 = jnp.where(row8 == 0, v * w_j, 0.0).astype(BF16)
            k8 = jnp.where(row8 == 0, k, 0.0).astype(BF16)
            co_ref[s, hh] = w_c * c + _dot_tn(vw8, k8)
            no_ref[s, hh:hh + 1, :] = w_c * n_row + w_j * k
            m_all = jnp.where((mrow == s) & (mcol == hh), m_t, m_all)
    mo_ref[...] = m_all


def _mlstm_sample(qkv, gcol, c0, n0, m0):
    ns = qkv.shape[0]
    sb = MLSTM_SAMPLE_SEQS
    wq = qkv.shape[1]
    return pl.pallas_call(
        _mlstm_sample_kernel,
        grid=(ns // sb,),
        in_specs=[pl.BlockSpec((sb, wq), lambda i: (i, 0)),
                  pl.BlockSpec((sb, LANES), lambda i: (i, 0)),
                  pl.BlockSpec((sb, M_HEADS, M_V, M_QK), lambda i: (i, 0, 0, 0)),
                  pl.BlockSpec((sb, M_HEADS, M_QK), lambda i: (i, 0, 0)),
                  pl.BlockSpec((sb, M_HEADS), lambda i: (i, 0))],
        out_specs=[pl.BlockSpec((sb, M_WIDTH), lambda i: (i, 0)),
                   pl.BlockSpec((sb, M_HEADS, M_V, M_QK), lambda i: (i, 0, 0, 0)),
                   pl.BlockSpec((sb, M_HEADS, M_QK), lambda i: (i, 0, 0)),
                   pl.BlockSpec((sb, M_HEADS), lambda i: (i, 0))],
        out_shape=[jax.ShapeDtypeStruct((ns, M_WIDTH), F32),
                   jax.ShapeDtypeStruct(c0.shape, F32),
                   jax.ShapeDtypeStruct(n0.shape, F32),
                   jax.ShapeDtypeStruct(m0.shape, F32)],
        compiler_params=_params("parallel"),
        name="mlstm_sample_step",
    )(qkv, gcol, c0, n0, m0)


def _kmean_kernel(k_ref, o_ref):
    o_ref[0] = jnp.mean(k_ref[...], axis=0, keepdims=True)


def _block_kmean(k32):
    t, w = k32.shape
    nb = t // MOBA_BLOCK
    out = pl.pallas_call(
        _kmean_kernel,
        grid=(nb,),
        in_specs=[pl.BlockSpec((MOBA_BLOCK, w), lambda b: (b, 0))],
        out_specs=pl.BlockSpec((1, 1, w), lambda b: (b, 0, 0)),
        out_shape=jax.ShapeDtypeStruct((nb, 1, w), F32),
        compiler_params=_params("parallel"),
        name="moba_block_kmean",
    )(k32)
    return out.reshape(nb, w)


def _moba_select(sc, own, lane_i, lane_f):
    work = jnp.where(lane_i < own, sc, NEG)
    sel = jnp.zeros(sc.shape, dtype=jnp.bool_)
    for _ in range(MOBA_TOPK):
        _, idx = _first_argmax(work, lane_f)
        hit = lane_f == idx
        sel = sel | hit
        work = jnp.where(hit, -jnp.inf, work)
    return sel & (lane_i < own)


def _moba_prompt_kernel(q_ref, k_ref, v_ref, e_ref, km_ref, o_ref):
    i = pl.program_id(1)
    bs = MOBA_BLOCK
    pw = MOBA_BLOCKS_PER_STEP * bs
    hd = A_HEAD_DIM
    heads = range(MOBA_HEADS_PER_STEP)
    ones = jnp.ones((pw, hd), BF16)
    lane_i = lax.broadcasted_iota(jnp.int32, (bs, LANES), 1)
    lane_f = lane_i.astype(F32)

    def head_slice(hh):
        return slice(hh * hd, (hh + 1) * hd)

    q_aug = []
    for hh in heads:
        q = q_ref[:, head_slice(hh)]
        qh, ql = _split_bf16(q)
        kmh, kml = _split_bf16(km_ref[:, head_slice(hh)])
        sc = _dot_nt(qh, kmh) + (_dot_nt(qh, kml) + _dot_nt(ql, kmh))
        sel = _moba_select(sc, i, lane_i, lane_f)
        bias = jnp.where(sel | (lane_i >= i), 0.0, NEG)
        q_aug.append(jnp.concatenate([(q * (hd ** -0.5)).astype(BF16), bias.astype(BF16)], axis=1))

    def pair_scores(p, hh, onehot):
        off = pl.multiple_of(p * pw, pw)
        k_aug = jnp.concatenate([k_ref[pl.ds(off, pw), head_slice(hh)], onehot], axis=1)
        v_aug = jnp.concatenate([v_ref[pl.ds(off, pw), head_slice(hh)], ones], axis=1)
        return _dot_nt(q_aug[hh], k_aug), v_aug

    p_own = i // MOBA_BLOCKS_PER_STEP
    row = lax.broadcasted_iota(jnp.int32, (bs, pw), 0)
    col = lax.broadcasted_iota(jnp.int32, (bs, pw), 1)
    causal = col + (p_own * pw - i * bs) <= row
    onehot_own = e_ref[pl.ds(pl.multiple_of(p_own * pw, pw), pw), :]
    init = []
    for hh in heads:
        s, v_aug = pair_scores(p_own, hh, onehot_own)
        s = jnp.where(causal, s, NEG)
        m0 = jnp.max(s, axis=1, keepdims=True)
        init += [m0, _dot(jnp.exp(s - m0).astype(BF16), v_aug)]

    def body(p, carry):
        onehot = e_ref[pl.ds(pl.multiple_of(p * pw, pw), pw), :]
        out = []
        for hh in heads:
            m_i, acc = carry[2 * hh], carry[2 * hh + 1]
            sj, vj = pair_scores(p, hh, onehot)
            m_n = jnp.maximum(m_i, jnp.max(sj, axis=1, keepdims=True))
            out += [m_n, jnp.exp(m_i - m_n) * acc + _dot(jnp.exp(sj - m_n).astype(BF16), vj)]
        return tuple(out)

    final = lax.fori_loop(0, p_own, body, tuple(init))
    for hh in heads:
        acc = final[2 * hh + 1]
        o_ref[:, head_slice(hh)] = (acc[:, :hd] / acc[:, hd:]).astype(o_ref.dtype)


def _moba_prompt(q32, k16, v16, kmean):
    t = q32.shape[0]
    nb = t // MOBA_BLOCK
    hd = MOBA_HEADS_PER_STEP * A_HEAD_DIM
    assert nb <= LANES and nb % MOBA_BLOCKS_PER_STEP == 0 and A_HEADS % MOBA_HEADS_PER_STEP == 0
    block_onehot = (jnp.arange(t, dtype=jnp.int32)[:, None] // MOBA_BLOCK
                    == jnp.arange(LANES, dtype=jnp.int32)[None, :]).astype(BF16)
    kmean_pad = jnp.pad(kmean, ((0, LANES - nb), (0, 0)))
    return pl.pallas_call(
        _moba_prompt_kernel,
        grid=(A_HEADS // MOBA_HEADS_PER_STEP, nb),
        in_specs=[pl.BlockSpec((MOBA_BLOCK, hd), lambda h, i: (i, h)),
                  pl.BlockSpec((t, hd), lambda h, i: (0, h), pipeline_mode=pl.Buffered(1)),
                  pl.BlockSpec((t, hd), lambda h, i: (0, h), pipeline_mode=pl.Buffered(1)),
                  pl.BlockSpec((t, LANES), lambda h, i: (0, 0), pipeline_mode=pl.Buffered(1)),
                  pl.BlockSpec((LANES, hd), lambda h, i: (0, h))],
        out_specs=pl.BlockSpec((MOBA_BLOCK, hd), lambda h, i: (i, h)),
        out_shape=jax.ShapeDtypeStruct((t, A_WIDTH), BF16),
        compiler_params=_params("parallel", "arbitrary"),
        name="moba_prompt_attention",
    )(q32, k16, v16, block_onehot, kmean_pad)


def _moba_sample_kernel(n_pages, pages_per_block, pt_ref, q_ref, kn_ref, vn_ref, *refs):
    del pt_ref
    kp = refs[:n_pages]
    vp = refs[n_pages:2 * n_pages]
    o_ref = refs[2 * n_pages]
    nbp = n_pages // pages_per_block
    hd = A_HEAD_DIM
    rows = kp[0].shape[2]
    page = rows // A_HEADS
    q = q_ref[0]
    lane_i = lax.broadcasted_iota(jnp.int32, (A_HEADS, LANES), 1)

    sc = jnp.zeros((A_HEADS, LANES), F32)
    k16 = []
    for b in range(nbp):
        ksum = jnp.zeros((A_HEADS, hd), F32)
        for pp in range(pages_per_block):
            kf = kp[b * pages_per_block + pp][0, 0]
            ksum = ksum + jnp.sum(kf.reshape(page, A_HEADS, hd), axis=0)
            k16.append(kf.astype(BF16))
        kmean = ksum / float(MOBA_BLOCK)
        sc = jnp.where(lane_i == b, jnp.sum(q * kmean, axis=1, keepdims=True), sc)
    sel = _moba_select(sc, nbp, lane_i, lane_i.astype(F32)).astype(F32)

    qs = q * (hd ** -0.5)
    own = jnp.sum(qs * kn_ref[0], axis=1, keepdims=True)
    s = _dot_nt(qs.astype(BF16), jnp.concatenate(k16, axis=0))
    head_row = lax.broadcasted_iota(jnp.int32, (A_HEADS, rows), 0)
    row_head = lax.broadcasted_iota(jnp.int32, (A_HEADS, rows), 1) % A_HEADS
    mine = head_row == row_head
    s = jnp.concatenate(
        [jnp.where(mine & (sel[:, pg // pages_per_block:pg // pages_per_block + 1] > 0.5),
                   s[:, pg * rows:(pg + 1) * rows], NEG) for pg in range(n_pages)], axis=1)
    m = jnp.maximum(own, jnp.max(s, axis=1, keepdims=True))
    p = jnp.exp(s - m)
    p_own = jnp.exp(own - m)
    l = jnp.sum(p, axis=1, keepdims=True) + p_own
    v16 = jnp.concatenate([vp[pg][0, 0].astype(BF16) for pg in range(n_pages)], axis=0)
    o_ref[0] = (_dot(p.astype(BF16), v16) + p_own * vn_ref[0]) / l


def _moba_sample(q32, kn32, vn32, cache_k, cache_v, page_table):
    ns = q32.shape[0]
    page = cache_k.shape[2]
    n_pages = page_table.shape[1]
    ppb = MOBA_BLOCK // page
    head_spec = pl.BlockSpec((1, A_HEADS, A_HEAD_DIM), lambda s, pt: (s, 0, 0))

    def page_spec(p):
        return pl.BlockSpec((1, 1, page * A_HEADS, A_HEAD_DIM), lambda s, pt: (0, pt[s, p], 0, 0))

    def rows(cache):
        return cache.reshape(cache.shape[0], cache.shape[1], page * A_HEADS, A_HEAD_DIM)

    def heads(a):
        return a.reshape(ns, A_HEADS, A_HEAD_DIM)

    grid_spec = pltpu.PrefetchScalarGridSpec(
        num_scalar_prefetch=1,
        grid=(ns,),
        in_specs=[head_spec, head_spec, head_spec] + [page_spec(p) for p in range(n_pages)] * 2,
        out_specs=head_spec,
    )
    out = pl.pallas_call(
        functools.partial(_moba_sample_kernel, n_pages, ppb),
        grid_spec=grid_spec,
        out_shape=jax.ShapeDtypeStruct((ns, A_HEADS, A_HEAD_DIM), F32),
        compiler_params=_params("parallel"),
        name="moba_sample_attention",
    )(page_table, heads(q32), heads(kn32), heads(vn32), *([rows(cache_k)] * n_pages), *([rows(cache_v)] * n_pages))
    return out.reshape(ns, A_WIDTH)


def _mix_kernel(hm_ref, om_ref, ao_ref, gm_ref, ga_ref, gmh_ref, wbm_ref, wba_ref, o_ref, hs_ref):
    @pl.when(pl.program_id(1) == 0)
    def _():
        for hh in range(M_HEADS):
            sl = slice(hh * M_V, (hh + 1) * M_V)
            y = _rms(hm_ref[:, sl], gmh_ref[:, sl])
            hs_ref[:, sl] = (y * om_ref[:, sl].astype(F32)).astype(BF16)

    t1 = _dot(hs_ref[...], wbm_ref[...])
    t2 = _dot(ao_ref[...].astype(BF16), wba_ref[...])
    o_ref[...] = (gm_ref[...].astype(F32) * t1 + ga_ref[...].astype(F32) * t2).astype(o_ref.dtype)


def _mix(hm, sg, ao, g_mh, w_bm, w_ba):
    r = hm.shape[0]
    d = w_bm.shape[1]
    tm = _pick(r, (512, 256, 128, 64, 32, 16))
    tn = _pick(d, (1024, 512, 256, 128))
    assert M_WIDTH % tn == 0
    gm0 = M_WIDTH // tn
    ga0 = (M_WIDTH + d) // tn
    return pl.pallas_call(
        _mix_kernel,
        grid=(r // tm, d // tn),
        in_specs=[pl.BlockSpec((tm, M_WIDTH), lambda i, j: (i, 0)),
                  pl.BlockSpec((tm, M_WIDTH), lambda i, j: (i, 0)),
                  pl.BlockSpec((tm, A_WIDTH), lambda i, j: (i, 0)),
                  pl.BlockSpec((tm, tn), lambda i, j: (i, gm0 + j)),
                  pl.BlockSpec((tm, tn), lambda i, j: (i, ga0 + j)),
                  pl.BlockSpec((1, M_WIDTH), lambda i, j: (0, 0)),
                  pl.BlockSpec((M_WIDTH, tn), lambda i, j: (0, j)),
                  pl.BlockSpec((A_WIDTH, tn), lambda i, j: (0, j))],
        out_specs=pl.BlockSpec((tm, tn), lambda i, j: (i, j)),
        out_shape=jax.ShapeDtypeStruct((r, d), BF16),
        scratch_shapes=[pltpu.VMEM((tm, M_WIDTH), BF16)],
        compiler_params=_params("parallel", "arbitrary"),
        name="mixer_merge",
    )(hm, sg, ao, sg, sg, g_mh.reshape(1, M_WIDTH), w_bm, w_ba)


def _store_row_tiles(o_ref, x, rows):
    s = x.shape[1] // LANES
    for c in range(s):
        o_ref[pl.ds(c, rows, stride=s), :] = x[:, c * LANES:(c + 1) * LANES]


def _resid_router_kernel(x_ref, mix_ref, wout_ref, gffn_ref, wrh_ref, wrl_ref, br_ref,
                         x1_ref, x1t_ref, gate_ref, eid_ref):
    x1 = x_ref[...] + _dot(mix_ref[...], wout_ref[...])
    x1_ref[...] = x1
    _store_row_tiles(x1t_ref, x1, x1.shape[0])
    hh, hl = _split_bf16(_rms(x1, gffn_ref[...]))
    wrh = wrh_ref[...]
    logits = _dot(hh, wrh) + (_dot(hh, wrl_ref[...]) + _dot(hl, wrh)) + br_ref[...]
    lane_i = lax.broadcasted_iota(jnp.int32, logits.shape, 1)
    lane_f = lane_i.astype(F32)
    work = jnp.where(lane_i < N_EXPERTS, logits, -jnp.inf)
    vals, ids = [], []
    for _ in range(TOP_K):
        m, idx = _first_argmax(work, lane_f)
        vals.append(m)
        ids.append(idx)
        work = jnp.where(lane_f == idx, -jnp.inf, work)
    es = [jnp.exp(v - vals[0]) for v in vals]
    den = es[0]
    for e in es[1:]:
        den = den + e
    gate = jnp.zeros(logits.shape, F32)
    eid = jnp.zeros(logits.shape, F32)
    for k in range(TOP_K):
        gate = jnp.where(lane_i == k, es[k] / den, gate)
        eid = jnp.where(lane_i == k, ids[k], eid)
    gate_ref[...] = gate
    eid_ref[...] = eid.astype(jnp.int32)


def _resid_router(x, mix, w_out, g_ffn, wr_hi, wr_lo, b_r, n_total, row0, prev):
    r, d = x.shape
    s = d // LANES
    tm = _pick(r, (256, 128, 64, 32, 16))
    assert row0 % tm == 0
    b0 = row0 // tm
    out_shape = [jax.ShapeDtypeStruct((n_total, d), F32),
                 jax.ShapeDtypeStruct((n_total * s, LANES), F32),
                 jax.ShapeDtypeStruct((n_total, LANES), F32),
                 jax.ShapeDtypeStruct((n_total, LANES), jnp.int32)]
    out_specs = [pl.BlockSpec((tm, d), lambda i: (b0 + i, 0)),
                 pl.BlockSpec((tm * s, LANES), lambda i: (b0 + i, 0)),
                 pl.BlockSpec((tm, LANES), lambda i: (b0 + i, 0)),
                 pl.BlockSpec((tm, LANES), lambda i: (b0 + i, 0))]
    in_specs = [pl.BlockSpec((tm, d), lambda i: (i, 0)),
                pl.BlockSpec((tm, d), lambda i: (i, 0)),
                pl.BlockSpec((d, d), lambda i: (0, 0)),
                pl.BlockSpec((1, d), lambda i: (0, 0)),
                pl.BlockSpec((d, LANES), lambda i: (0, 0)),
                pl.BlockSpec((d, LANES), lambda i: (0, 0)),
                pl.BlockSpec((1, LANES), lambda i: (0, 0))]
    args = [x, mix, w_out, g_ffn.reshape(1, d), wr_hi, wr_lo, b_r]
    n_in = len(args)
    kern = _resid_router_kernel
    aliases = {}
    if prev is not None:
        in_specs = in_specs + [pl.BlockSpec(memory_space=pl.ANY)] * len(prev)
        args = args + list(prev)
        aliases = {n_in + k: k for k in range(len(prev))}

        def kern(*refs):
            _resid_router_kernel(*refs[:n_in], *refs[n_in + len(prev):])

    return pl.pallas_call(
        kern,
        grid=(r // tm,),
        in_specs=in_specs,
        out_specs=out_specs,
        out_shape=out_shape,
        input_output_aliases=aliases,
        compiler_params=_params("parallel"),
        name="residual_router_topk",
    )(*args)


def _row_pitch(s):
    return s + 1 if s % 2 == 0 else s


def _gathered_rows(buf, rows, s):
    return jnp.concatenate([buf[pl.ds(c, rows, stride=_row_pitch(s)), :] for c in range(s)], axis=1)


def _start_row_gathers(src_hbm, idx_ref, idx_base, idx_step, dst, rows, s, sem):
    pitch = _row_pitch(s)
    trips = rows // GATHER_UNROLL

    def trip(i, carry):
        for u in range(GATHER_UNROLL):
            r = u * trips + i
            idx = idx_ref[idx_base + r * idx_step]
            pltpu.make_async_copy(src_hbm.at[pl.ds(idx * s, s)], dst.at[pl.ds(r * pitch, s)],
                                  sem).start(priority=u % 2)
        return carry

    lax.fori_loop(0, trips, trip, 0)


def _wait_row_gathers(dst, rows, s, sem):
    landed = dst.at[pl.ds(0, rows * s)]
    pltpu.make_async_copy(landed, landed, sem).wait()


def _expert_tile_schedule(ce_ref, nx_ref, nv_ref, c, j, n_pass):
    first = (c == 0) | (ce_ref[c] != ce_ref[jnp.maximum(c - 1, 0)])
    more_in_pass = nx_ref[c] >= 0
    has_next = more_in_pass | (j + 1 < n_pass)
    next_e = jnp.where(more_in_pass, nx_ref[c], ce_ref[0])
    next_j = jnp.where(more_in_pass, j, j + 1)
    return (c < nv_ref[0]) & first, has_next, next_j, next_e


def _moe_up_kernel(f_lo, nf, gather, ce_ref, nx_ref, nv_ref, *refs):
    if gather:
        tok_ref, tok_next_ref, g_ref, bg_ref, bl_ref, xt_hbm, w_hbm, o_ref, xs_ref, gbuf, gsem, raw, w16, sem = refs
    else:
        x_ref, bg_ref, bl_ref, w_hbm, _, o_ref, raw, w16, sem = refs
    j = pl.program_id(0)
    c = pl.program_id(1)
    tf = raw.shape[2]
    n_pass = pl.num_programs(0)
    dff = nf * tf

    def fetch(jj, e, k):
        col = pl.multiple_of(k * dff + (f_lo + jj) * tf, LANES)
        return pltpu.make_async_copy(w_hbm.at[e, :, pl.ds(col, tf)], raw.at[k], sem.at[k])

    first, has_next, next_j, next_e = _expert_tile_schedule(ce_ref, nx_ref, nv_ref, c, j, n_pass)

    if gather:
        rows, d = xs_ref.shape
        s = d // LANES
        slot = c % 2

        @pl.when(c == 0)
        def _():
            _start_row_gathers(xt_hbm, tok_ref, 0, 1, gbuf.at[0], rows, s, gsem.at[0])

        @pl.when(c + 1 < nv_ref[0])
        def _():
            _start_row_gathers(xt_hbm, tok_next_ref, 0, 1, gbuf.at[1 - slot], rows, s, gsem.at[1 - slot])

    @pl.when(first & (j == 0) & (c == 0))
    def _():
        for k in range(2):
            fetch(0, ce_ref[0], k).start()

    @pl.when(first)
    def _():
        for k in range(2):
            fetch(j, ce_ref[c], k).wait()
            w16[k] = raw[k].astype(BF16)

    @pl.when(first & has_next)
    def _():
        for k in range(2):
            fetch(next_j, next_e, k).start()

    @pl.when(c < nv_ref[0])
    def _():
        if gather:
            _wait_row_gathers(gbuf.at[slot], rows, s, gsem.at[slot])
            x = _rms(_gathered_rows(gbuf.at[slot], rows, s), g_ref[...]).astype(BF16)
            xs_ref[...] = x
        else:
            x = x_ref[...]
        gate = jnp.minimum(_dot(x, w16[0]) + bg_ref[0], SWIGLU_LIMIT)
        lin = jnp.clip(_dot(x, w16[1]) + bl_ref[0], -SWIGLU_LIMIT, SWIGLU_LIMIT)
        o_ref[...] = (gate * jax.nn.sigmoid(SWIGLU_ALPHA * gate) * (lin + 1.0)).astype(o_ref.dtype)


def _moe_up(x1t, g_ffn, src_tok, w_up, b_up, chunk_expert, next_expert, n_valid, n_chunks):
    rows = MOE_ROWS
    e, d, f2 = w_up.shape
    s = d // LANES
    dff = f2 // 2
    tf = _pick(dff, (1024, 512, 256, 128))
    nf = dff // tf
    assert rows % GATHER_UNROLL == 0
    b3 = b_up.reshape(e, 1, f2)
    weight_scratch = [pltpu.VMEM((2, d, tf), F32), pltpu.VMEM((2, d, tf), BF16), pltpu.SemaphoreType.DMA((2,))]
    act_shape = jax.ShapeDtypeStruct((n_chunks * rows, dff), BF16)

    def last(c, nv):
        return jnp.minimum(c, nv[0] - 1)

    def nxt(c, nv):
        return jnp.minimum(c + 1, nv[0] - 1)

    def bias_spec(off):
        return pl.BlockSpec((1, 1, tf), lambda j, c, ce, nx, nv: (ce[c], 0, off + j))

    def act_spec(f_lo):
        return pl.BlockSpec((rows, tf), lambda j, c, ce, nx, nv: (last(c, nv), f_lo + j))

    first_spec = pltpu.PrefetchScalarGridSpec(
        num_scalar_prefetch=3,
        grid=(1, n_chunks),
        in_specs=[pl.BlockSpec((rows,), lambda j, c, ce, nx, nv: (last(c, nv),), memory_space=pltpu.SMEM),
                  pl.BlockSpec((rows,), lambda j, c, ce, nx, nv: (nxt(c, nv),), memory_space=pltpu.SMEM),
                  pl.BlockSpec((1, d), lambda j, c, ce, nx, nv: (0, 0)),
                  bias_spec(0), bias_spec(nf),
                  pl.BlockSpec(memory_space=pl.ANY),
                  pl.BlockSpec(memory_space=pl.ANY)],
        out_specs=[act_spec(0),
                   pl.BlockSpec((rows, d), lambda j, c, ce, nx, nv: (last(c, nv), 0))],
        scratch_shapes=[pltpu.VMEM((2, rows * _row_pitch(s), LANES), F32), pltpu.SemaphoreType.DMA((2,))]
        + weight_scratch,
    )
    act, xs = pl.pallas_call(
        functools.partial(_moe_up_kernel, 0, nf, True),
        grid_spec=first_spec,
        out_shape=[act_shape, jax.ShapeDtypeStruct((n_chunks * rows, d), BF16)],
        compiler_params=_params("arbitrary", "arbitrary"),
        name="moe_gather_up_swiglu",
    )(chunk_expert, next_expert, n_valid, src_tok, src_tok, g_ffn.reshape(1, d), b3, b3, x1t, w_up)
    if nf == 1:
        return act
    rest_spec = pltpu.PrefetchScalarGridSpec(
        num_scalar_prefetch=3,
        grid=(nf - 1, n_chunks),
        in_specs=[pl.BlockSpec((rows, d), lambda j, c, ce, nx, nv: (last(c, nv), 0)),
                  bias_spec(1), bias_spec(nf + 1),
                  pl.BlockSpec(memory_space=pl.ANY),
                  pl.BlockSpec(memory_space=pl.ANY)],
        out_specs=act_spec(1),
        scratch_shapes=weight_scratch,
    )
    return pl.pallas_call(
        functools.partial(_moe_up_kernel, 1, nf, False),
        grid_spec=rest_spec,
        out_shape=act_shape,
        input_output_aliases={7: 0},
        compiler_params=_params("arbitrary", "arbitrary"),
        name="moe_up_swiglu",
    )(chunk_expert, next_expert, n_valid, xs, b3, b3, w_up, act)


def _moe_down_kernel(ce_ref, nx_ref, nv_ref, a_ref, b_ref, w_hbm, o_ref, raw, w16, sem):
    c = pl.program_id(0)

    def fetch(e):
        return pltpu.make_async_copy(w_hbm.at[e], raw, sem)

    first, has_next, _, next_e = _expert_tile_schedule(ce_ref, nx_ref, nv_ref, c, 0, 1)

    @pl.when(first & (c == 0))
    def _():
        fetch(ce_ref[0]).start()

    @pl.when(first)
    def _():
        fetch(ce_ref[c]).wait()
        w16[...] = raw[...].astype(BF16)

    @pl.when(first & has_next)
    def _():
        fetch(next_e).start()

    @pl.when(c < nv_ref[0])
    def _():
        y = _dot(a_ref[...], w16[...]) + b_ref[0]
        _store_row_tiles(o_ref, y, y.shape[0])


def _moe_down(act, w_down, b_down, chunk_expert, next_expert, n_valid):
    rows = MOE_ROWS
    n_chunks = act.shape[0] // rows
    e, dff, d = w_down.shape
    s = d // LANES

    def last(c, nv):
        return jnp.minimum(c, nv[0] - 1)

    grid_spec = pltpu.PrefetchScalarGridSpec(
        num_scalar_prefetch=3,
        grid=(n_chunks,),
        in_specs=[pl.BlockSpec((rows, dff), lambda c, ce, nx, nv: (last(c, nv), 0)),
                  pl.BlockSpec((1, 1, d), lambda c, ce, nx, nv: (ce[c], 0, 0)),
                  pl.BlockSpec(memory_space=pl.ANY)],
        out_specs=pl.BlockSpec((rows * s, LANES), lambda c, ce, nx, nv: (last(c, nv), 0)),
        scratch_shapes=[pltpu.VMEM((dff, d), F32), pltpu.VMEM((dff, d), BF16),
                        pltpu.SemaphoreType.DMA(())],
    )
    return pl.pallas_call(
        _moe_down_kernel,
        grid_spec=grid_spec,
        out_shape=jax.ShapeDtypeStruct((n_chunks * rows * s, LANES), F32),
        compiler_params=_params("arbitrary"),
        name="moe_down",
    )(chunk_expert, next_expert, n_valid, act, b_down.reshape(e, 1, d), w_down)


def _combine_kernel(dest_ref, dest_next_ref, x1_ref, gate_ref, gple_ref, yt_hbm, x2_ref, h3_ref, buf, sem):
    i = pl.program_id(0)
    rows = x1_ref.shape[0]
    s = x1_ref.shape[1] // LANES
    slot = i % 2

    def start(idx_ref, to):
        for k in range(TOP_K):
            _start_row_gathers(yt_hbm, idx_ref, k, TOP_K, buf.at[to, k], rows, s, sem.at[to])

    @pl.when(i == 0)
    def _():
        start(dest_ref, 0)

    @pl.when(i + 1 < pl.num_programs(0))
    def _():
        start(dest_next_ref, 1 - slot)

    for k in range(TOP_K):
        _wait_row_gathers(buf.at[slot, k], rows, s, sem.at[slot])
    gate = gate_ref[...]
    acc = x1_ref[...]
    for k in range(TOP_K):
        acc = acc + gate[:, k:k + 1] * _gathered_rows(buf.at[slot, k], rows, s)
    x2_ref[...] = acc
    h3_ref[...] = _rms(acc, gple_ref[...]).astype(h3_ref.dtype)


def _combine(x1, gate, dest, yt, g_ple):
    n, d = x1.shape
    s = d // LANES
    rows = _pick(n, (COMBINE_ROWS, 64, 32, 16))
    steps = n // rows
    assert rows % GATHER_UNROLL == 0
    return pl.pallas_call(
        _combine_kernel,
        grid=(steps,),
        in_specs=[pl.BlockSpec((rows * TOP_K,), lambda i: (i,), memory_space=pltpu.SMEM),
                  pl.BlockSpec((rows * TOP_K,), lambda i: (jnp.minimum(i + 1, steps - 1),),
                               memory_space=pltpu.SMEM),
                  pl.BlockSpec((rows, d), lambda i: (i, 0)),
                  pl.BlockSpec((rows, LANES), lambda i: (i, 0)),
                  pl.BlockSpec((1, d), lambda i: (0, 0)),
                  pl.BlockSpec(memory_space=pl.ANY)],
        out_specs=[pl.BlockSpec((rows, d), lambda i: (i, 0)),
                   pl.BlockSpec((rows, d), lambda i: (i, 0))],
        out_shape=[jax.ShapeDtypeStruct((n, d), F32), jax.ShapeDtypeStruct((n, d), BF16)],
        scratch_shapes=[pltpu.VMEM((2, TOP_K, rows * _row_pitch(s), LANES), F32),
                        pltpu.SemaphoreType.DMA((2,))],
        compiler_params=_params("arbitrary"),
        name="moe_combine_gather",
    )(dest, dest, x1, gate, g_ple.reshape(1, d), yt)


def _ple_kernel(x2_ref, h3_ref, p_ref, wpg_ref, wple_ref, o_ref):
    gate = jax.nn.sigmoid(_dot(h3_ref[...], wpg_ref[...]))
    o_ref[...] = x2_ref[...] + gate * _dot(p_ref[...].astype(BF16), wple_ref[...])


def _ple(x2, h3, p, w_pg, w_ple, row0, r):
    d = x2.shape[1]
    pd = p.shape[1]
    tm = _pick(r, (512, 256, 128, 64, 32, 16))
    tn = _pick(d, (1024, 512, 256, 128))
    assert row0 % tm == 0
    b0 = row0 // tm
    return pl.pallas_call(
        _ple_kernel,
        grid=(d // tn, r // tm),
        in_specs=[pl.BlockSpec((tm, tn), lambda j, i: (b0 + i, j)),
                  pl.BlockSpec((tm, d), lambda j, i: (b0 + i, 0)),
                  pl.BlockSpec((tm, pd), lambda j, i: (i, 0)),
                  pl.BlockSpec((d, tn), lambda j, i: (0, j)),
                  pl.BlockSpec((pd, tn), lambda j, i: (0, j))],
        out_specs=pl.BlockSpec((tm, tn), lambda j, i: (i, j)),
        out_shape=jax.ShapeDtypeStruct((r, d), F32),
        compiler_params=_params("parallel", "parallel"),
        name="ple_gate",
    )(x2, h3, p, w_pg, w_ple)


def _rope_tables(pos):
    half = ROT_DIM // 2
    inv = jnp.power(ROPE_THETA, -jnp.arange(half, dtype=F32) * 2.0 / ROT_DIM)
    ang = pos.astype(F32)[:, None] * inv[None, :]
    cos, sin = jnp.cos(ang), jnp.sin(ang)
    r = pos.shape[0]
    rest = A_HEAD_DIM - ROT_DIM
    c = jnp.concatenate([cos, cos, jnp.ones((r, rest), F32)], axis=1)
    s1 = jnp.concatenate([-sin, jnp.zeros((r, half + rest), F32)], axis=1)
    s2 = jnp.concatenate([jnp.zeros((r, half), F32), sin, jnp.zeros((r, rest), F32)], axis=1)
    return c, s1, s2


def _mixer_inputs(x, pos, g_attn, w, b_i, b_f, g_q, g_k, mlstm_dtype):
    d = x.shape[1]
    kq = M_HEADS * M_QK
    o_ig = 2 * kq + M_WIDTH
    o_om = o_ig + 2 * M_HEADS
    o_qa = o_om + M_WIDTH
    o_gm = o_qa + 3 * A_WIDTH
    h = _rmsnorm_bf16(x, g_attn)
    colscale = jnp.concatenate([jnp.full((1, kq), M_QK ** -0.5, F32), jnp.ones((1, kq + M_WIDTH), F32)], axis=1)
    qkv_m = _proj_scale(h, w[:, :o_ig].astype(BF16), colscale, mlstm_dtype)
    wg = w[:, o_ig:o_om]
    wc = jnp.pad(wg, ((0, 0), (0, LANES - 2 * M_HEADS))).astype(BF16)
    bias = jnp.concatenate([b_i, b_f]).astype(F32)
    bc = jnp.pad(bias, (0, LANES - 2 * M_HEADS)).reshape(1, LANES)
    gcol, grow = _proj_gates(h, wc, wg.T.astype(BF16), bc, bias.reshape(2 * M_HEADS, 1))
    sg = _proj_sigmoid(h, jnp.concatenate([w[:, o_om:o_qa], w[:, o_gm:]], axis=1).astype(BF16))
    tabs = _rope_tables(pos)
    q32, _ = _proj_qk(h, w[:, o_qa:o_qa + A_WIDTH].astype(BF16), g_q, *tabs)
    k32, k16 = _proj_qk(h, w[:, o_qa + A_WIDTH:o_qa + 2 * A_WIDTH].astype(BF16), g_k, *tabs)
    v32, v16 = _proj_plain2(h, w[:, o_qa + 2 * A_WIDTH:o_gm].astype(BF16))
    return qkv_m, gcol, grow, sg, q32, k32, k16, v32, v16


def _moe_plan(eid, n_chunks):
    n = eid.shape[0]
    e_flat = eid[:, :TOP_K].reshape(-1)
    onehot = (e_flat[:, None] == jnp.arange(N_EXPERTS, dtype=jnp.int32)[None, :]).astype(jnp.int32)
    before = jnp.cumsum(onehot, axis=0) - onehot
    rank = jnp.sum(before * onehot, axis=1)
    counts = jnp.sum(onehot, axis=0)
    padded = (counts + MOE_ROWS - 1) // MOE_ROWS * MOE_ROWS
    p_ends = jnp.cumsum(padded)
    p_starts = p_ends - padded
    dest = (p_starts[e_flat] + rank).astype(jnp.int32)
    tok = jnp.repeat(jnp.arange(n, dtype=jnp.int32), TOP_K)
    src_tok = jnp.zeros((n_chunks * MOE_ROWS,), jnp.int32).at[dest].set(tok)
    n_valid = (p_ends[-1] // MOE_ROWS).astype(jnp.int32).reshape(1)
    chunk0 = jnp.minimum(jnp.arange(n_chunks, dtype=jnp.int32), n_valid[0] - 1) * MOE_ROWS
    chunk_expert = jnp.minimum(jnp.sum((p_ends[None, :] <= chunk0[:, None]).astype(jnp.int32), axis=1),
                               N_EXPERTS - 1)
    ids = jnp.arange(N_EXPERTS, dtype=jnp.int32)
    later = (ids[None, :] > chunk_expert[:, None]) & (counts[None, :] > 0)
    next_expert = jnp.min(jnp.where(later, ids[None, :], N_EXPERTS), axis=1)
    next_expert = jnp.where(next_expert < N_EXPERTS, next_expert, -1).astype(jnp.int32)
    return dest, src_tok, n_valid, chunk_expert, next_expert


def kernel(x_prompt, x_sample, cache_k, cache_v, state_mlstm_C, state_mlstm_n, state_mlstm_m, page_table, p_prompt, p_sample, g_attn, w_in, b_i, b_f, g_q, g_k, g_mh, w_bm, w_ba, w_out, g_ffn, w_router, b_router, w_up, b_up, w_down, b_down, g_ple, w_pg, w_ple):
    depth = w_in.shape[0]
    bsz, t, d = x_prompt.shape
    ns, dec_seq, _ = x_sample.shape
    assert depth == 1 and bsz == 1 and dec_seq == 1
    page = cache_k.shape[2]
    past = page_table.shape[1] * page
    assert t % MOBA_BLOCK == 0 and past % MOBA_BLOCK == 0 and MOBA_BLOCK % page == 0
    assert t % (M_CHUNK * MLSTM_CHUNKS_PER_STEP) == 0 and ns % MLSTM_SAMPLE_SEQS == 0
    n = t + ns

    xp = x_prompt[0]
    xs = x_sample[:, 0]
    w = w_in[0]
    fw = (g_attn[0], w, b_i[0], b_f[0], g_q[0], g_k[0])

    qkv_p, gcol_p, grow_p, sg_p, q32_p, k32_p, k16_p, v32_p, v16_p = _mixer_inputs(
        xp, jnp.arange(t, dtype=jnp.int32), *fw, BF16)
    hm_p, ct_p, n_p, m_p = _mlstm_prompt(qkv_p, gcol_p, grow_p)
    ao_p = _moba_prompt(q32_p, k16_p, v16_p, _block_kmean(k32_p))

    qkv_s, gcol_s, _, sg_s, q32_s, k32_s, _, v32_s, _ = _mixer_inputs(
        xs, jnp.full((ns,), past, jnp.int32), *fw, F32)
    hm_s, c_s, n_s, m_s = _mlstm_sample(qkv_s, gcol_s, state_mlstm_C[0].astype(F32),
                                        state_mlstm_n[0].astype(F32), state_mlstm_m[0].astype(F32))
    ao_s = _moba_sample(q32_s, k32_s, v32_s, cache_k, cache_v, page_table)

    wbm, wba, wout = w_bm[0].astype(BF16), w_ba[0].astype(BF16), w_out[0].astype(BF16)
    mix_p = _mix(hm_p, sg_p, ao_p, g_mh[0], wbm, wba)
    mix_s = _mix(hm_s, sg_s, ao_s, g_mh[0], wbm, wba)
    wr = jnp.pad(w_router[0], ((0, 0), (0, LANES - N_EXPERTS)))
    wr_hi = wr.astype(BF16)
    wr_lo = (wr - wr_hi.astype(F32)).astype(BF16)
    b_r = jnp.pad(b_router[0].astype(F32), (0, LANES - N_EXPERTS)).reshape(1, LANES)
    routed = _resid_router(xp, mix_p, wout, g_ffn[0], wr_hi, wr_lo, b_r, n, 0, None)
    x1, x1t, gate, eid = _resid_router(xs, mix_s, wout, g_ffn[0], wr_hi, wr_lo, b_r, n, t, routed)

    n_slots = n * TOP_K
    n_chunks = (n_slots + N_EXPERTS * (MOE_ROWS - 1)) // MOE_ROWS
    dest, src_tok, n_valid, chunk_expert, next_expert = _moe_plan(eid, n_chunks)
    act = _moe_up(x1t, g_ffn[0], src_tok, w_up[0], b_up[0], chunk_expert, next_expert, n_valid, n_chunks)
    yt = _moe_down(act, w_down[0], b_down[0], chunk_expert, next_expert, n_valid)
    x2, h3 = _combine(x1, gate, dest, yt, g_ple[0])

    wpg, wple = w_pg[0].astype(BF16), w_ple[0].astype(BF16)
    y_p = _ple(x2, h3, p_prompt[0, 0], wpg, wple, 0, t)
    y_s = _ple(x2, h3, p_sample[0, :, 0], wpg, wple, t, ns)

    def heads(a):
        return a.reshape(a.shape[0], A_HEADS, A_HEAD_DIM)

    return (y_p[None], y_s[:, None],
            heads(k32_p)[None, None], heads(v32_p)[None, None],
            jnp.swapaxes(ct_p, 1, 2)[None, None].astype(state_mlstm_C.dtype),
            n_p[:M_HEADS][None, None].astype(state_mlstm_n.dtype),
            m_p[:M_HEADS, 0][None, None].astype(state_mlstm_m.dtype),
            heads(k32_s)[None, :, None], heads(v32_s)[None, :, None],
            c_s[None].astype(state_mlstm_C.dtype), n_s[None].astype(state_mlstm_n.dtype),
            m_s[None].astype(state_mlstm_m.dtype))
```
